```python
import jax
import jax.numpy as jnp
from jax import lax
import numpy as np

D_MODEL = 1024
BATCH = 4
SEQ = 4096
DEPTH = 2
DEC_BATCH = 16
DEC_SEQ = 64
PAST_LEN = 4096

CHUNK = 64
Q_BLOCK = 128
N_AB_LAYERS = (DEPTH + 1) // 2
N_MLA_LAYERS = DEPTH // 2

HEAD_DIM = 64
FOX_HEADS = 8
SB_HEADS = 8
FOX_W = FOX_HEADS * HEAD_DIM
SB_W = SB_HEADS * HEAD_DIM
AB_IN = 3 * FOX_W + FOX_HEADS + 3 * SB_W
AB_MIX = FOX_W + SB_W
FORGET_BIAS_INIT = 3.0

MLA_HEADS = 16
QK_NOPE = 64
QK_ROPE = 32
V_DIM = 64
Q_LORA = 384
KV_LORA = 256
MLA_DOWN = Q_LORA + KV_LORA + QK_ROPE
ROPE_BASE = 10000.0

N_GROUPS = 4
EXPERTS_PER_GROUP = 8
TOP_K_IN_GROUP = 2
EXPERT_HIDDEN = 256

DEEPNORM_ALPHA = (2 * DEPTH) ** 0.25
DEEPNORM_BETA = (8 * DEPTH) ** -0.25
LN_EPS = 1e-5
RMS_EPS = 1e-6
NEG_INF = -1e30

kernel_name = 'hybrid_fox_stickbreak_mla_hmoe_stream_step'


def layer_norm(x, g, b):
    xf = x.astype(jnp.float32)
    mu = jnp.mean(xf, axis=-1, keepdims=True)
    xc = xf - mu
    var = jnp.mean(xc * xc, axis=-1, keepdims=True)
    return (xc * lax.rsqrt(var + LN_EPS) * g.astype(jnp.float32) + b.astype(jnp.float32)).astype(x.dtype)


def rms_norm(x, g):
    xf = x.astype(jnp.float32)
    inv = lax.rsqrt(jnp.mean(xf * xf, axis=-1, keepdims=True) + RMS_EPS)
    return (xf * inv * g.astype(jnp.float32)).astype(x.dtype)


def rope(x, pos):
    half = x.shape[-1] // 2
    inv_freq = ROPE_BASE ** (-jnp.arange(half, dtype=jnp.float32) / half)
    ang = pos.astype(jnp.float32)[:, None] * inv_freq[None, :]
    shape = (1, pos.shape[0]) + (1,) * (x.ndim - 3) + (half,)
    cos = jnp.cos(ang).reshape(shape)
    sin = jnp.sin(ang).reshape(shape)
    xf = x.astype(jnp.float32)
    x1, x2 = xf[..., :half], xf[..., half:]
    return jnp.concatenate([x1 * cos - x2 * sin, x1 * sin + x2 * cos], axis=-1).astype(x.dtype)


def to_blocks(a, nb):
    return jnp.moveaxis(a.reshape((a.shape[0], nb, Q_BLOCK) + a.shape[2:]), 1, 0)


def from_blocks(a):
    a = jnp.moveaxis(a, 0, 1)
    return a.reshape((a.shape[0], a.shape[1] * a.shape[2]) + a.shape[3:])


def chunk_visible(pos_q, pos_k):
    return (pos_k[None, :] // CHUNK) <= (pos_q[:, None] // CHUNK)


def fox_attend(q, k, v, cum_q, cum_k, pos_q, pos_k):
    s = jnp.einsum('bqhd,bkhd->bhqk', q, k).astype(jnp.float32) * (HEAD_DIM ** -0.5)
    decay = jnp.transpose(cum_q, (0, 2, 1))[..., :, None] - jnp.transpose(cum_k, (0, 2, 1))[..., None, :]
    visible = pos_k[None, :] <= pos_q[:, None]
    p = jax.nn.softmax(jnp.where(visible, s + decay, NEG_INF), axis=-1)
    return jnp.einsum('bhqk,bkhd->bqhd', p.astype(v.dtype), v)


def sb_attend(q, k, v, pos_q, pos_k):
    z = jnp.einsum('bqhd,bkhd->bhqk', q, k).astype(jnp.float32) * (HEAD_DIM ** -0.5)
    before = pos_k[None, :] < pos_q[:, None]
    log_beta = jnp.where(before, jax.nn.log_sigmoid(z), NEG_INF)
    log_rest = jnp.where(before, jax.nn.log_sigmoid(-z), 0.0)
    later = lax.cumsum(log_rest, axis=log_rest.ndim - 1, reverse=True) - log_rest
    a = jnp.exp(log_beta + later)
    return jnp.einsum('bhqk,bkhd->bqhd', a.astype(v.dtype), v)


def ab_project(h, w_in, b_f):
    B, T, _ = h.shape
    z = h @ w_in
    cuts = [FOX_W, 2 * FOX_W, 3 * FOX_W, 3 * FOX_W + FOX_HEADS,
            3 * FOX_W + FOX_HEADS + SB_W, 3 * FOX_W + FOX_HEADS + 2 * SB_W]
    qa, ka, va, f_logit, qb, kb, vb = jnp.split(z, cuts, axis=-1)
    fh = lambda a: a.reshape(B, T, FOX_HEADS, HEAD_DIM)
    sh = lambda a: a.reshape(B, T, SB_HEADS, HEAD_DIM)
    logf = jax.nn.log_sigmoid(f_logit.astype(jnp.float32) + b_f.astype(jnp.float32))
    return fh(qa), fh(ka), fh(va), logf, sh(qb), sh(kb), sh(vb)


def ab_merge(oa, ob, w_out):
    B, T = oa.shape[0], oa.shape[1]
    return jnp.concatenate([oa.reshape(B, T, FOX_W), ob.reshape(B, T, SB_W)], axis=-1) @ w_out


def ab_prompt(h, w_in, b_f, w_out):
    B, S, _ = h.shape
    nb = S // Q_BLOCK
    qa, ka, va, logf, qb, kb, vb = ab_project(h, w_in, b_f)
    cum = jnp.cumsum(logf, axis=1)
    pos = jnp.arange(S, dtype=jnp.int32)

    def step(blk):
        qa_b, qb_b, cum_b, pos_b = blk
        return (fox_attend(qa_b, ka, va, cum_b, cum, pos_b, pos),
                sb_attend(qb_b, kb, vb, pos_b, pos))

    oa, ob = lax.map(step, (to_blocks(qa, nb), to_blocks(qb, nb), to_blocks(cum, nb),
                            pos.reshape(nb, Q_BLOCK)))
    y = ab_merge(from_blocks(oa), from_blocks(ob), w_out)
    return y, ka, va, logf.astype(h.dtype), kb, vb


def ab_sample(h, ka_past, va_past, logf_past, kb_past, vb_past, w_in, b_f, w_out):
    B, n, _ = h.shape
    P = ka_past.shape[1]
    qa, ka, va, logf, qb, kb, vb = ab_project(h, w_in, b_f)
    ka_all = jnp.concatenate([ka_past, ka.astype(ka_past.dtype)], axis=1)
    va_all = jnp.concatenate([va_past, va.astype(va_past.dtype)], axis=1)
    kb_all = jnp.concatenate([kb_past, kb.astype(kb_past.dtype)], axis=1)
    vb_all = jnp.concatenate([vb_past, vb.astype(vb_past.dtype)], axis=1)
    cum = jnp.cumsum(jnp.concatenate([logf_past.astype(jnp.float32), logf], axis=1), axis=1)
    pos_k = jnp.arange(P + n, dtype=jnp.int32)
    pos_q = P + jnp.arange(n, dtype=jnp.int32)
    oa = fox_attend(qa, ka_all, va_all, cum[:, P:], cum, pos_q, pos_k)
    ob = sb_attend(qb, kb_all, vb_all, pos_q, pos_k)
    y = ab_merge(oa, ob, w_out)
    return y, ka, va, logf.astype(h.dtype), kb, vb


def mla_project(h, w_down, g_q, g_kv, w_uq, pos):
    B, T, _ = h.shape
    z = h @ w_down
    cq, ckv, kr = jnp.split(z, [Q_LORA, Q_LORA + KV_LORA], axis=-1)
    cq = rms_norm(cq, g_q)
    ckv = rms_norm(ckv, g_kv)
    q = (cq @ w_uq).reshape(B, T, MLA_HEADS, QK_NOPE + QK_ROPE)
    q_nope, q_rope = q[..., :QK_NOPE], rope(q[..., QK_NOPE:], pos)
    kr = rope(kr, pos)
    return q_nope, q_rope, ckv, kr


def split_ukv(w_ukv):
    w = w_ukv.reshape(KV_LORA, MLA_HEADS, QK_NOPE + V_DIM)
    return w[..., :QK_NOPE], w[..., QK_NOPE:]


def mla_attend_expanded(q_nope, q_rope, k_nope, k_rope, v, pos_q, pos_k):
    s = (jnp.einsum('bqhd,bkhd->bhqk', q_nope, k_nope).astype(jnp.float32)
         + jnp.einsum('bqhr,bkr->bhqk', q_rope, k_rope).astype(jnp.float32)) * ((QK_NOPE + QK_ROPE) ** -0.5)
    p = jax.nn.softmax(jnp.where(chunk_visible(pos_q, pos_k), s, NEG_INF), axis=-1)
    return jnp.einsum('bhqk,bkhd->bqhd', p.astype(v.dtype), v)


def mla_attend_latent(q_nope, q_rope, ckv, k_rope, w_uk, w_uv, pos_q, pos_k):
    q_lat = jnp.einsum('bqhd,chd->bqhc', q_nope, w_uk)
    s = (jnp.einsum('bqhc,bkc->bhqk', q_lat, ckv).astype(jnp.float32)
         + jnp.einsum('bqhr,bkr->bhqk', q_rope, k_rope).astype(jnp.float32)) * ((QK_NOPE + QK_ROPE) ** -0.5)
    p = jax.nn.softmax(jnp.where(chunk_visible(pos_q, pos_k), s, NEG_INF), axis=-1)
    o_lat = jnp.einsum('bhqk,bkc->bqhc', p.astype(ckv.dtype), ckv)
    return jnp.einsum('bqhc,chd->bqhd', o_lat, w_uv)


def mla_prompt(h, w_down, g_q, g_kv, w_uq, w_ukv, w_out):
    B, S, _ = h.shape
    nb = S // Q_BLOCK
    pos = jnp.arange(S, dtype=jnp.int32)
    q_nope, q_rope, ckv, kr = mla_project(h, w_down, g_q, g_kv, w_uq, pos)
    w_uk, w_uv = split_ukv(w_ukv)
    k_nope = jnp.einsum('bsc,chd->bshd', ckv, w_uk)
    v = jnp.einsum('bsc,chd->bshd', ckv, w_uv)

    def step(blk):
        qn_b, qr_b, pos_b = blk
        return mla_attend_expanded(qn_b, qr_b, k_nope, kr, v, pos_b, pos)

    o = from_blocks(lax.map(step, (to_blocks(q_nope, nb), to_blocks(q_rope, nb), pos.reshape(nb, Q_BLOCK))))
    y = o.reshape(B, S, MLA_HEADS * V_DIM) @ w_out
    return y, ckv, kr


def mla_sample(h, ckv_past, kr_past, w_down, g_q, g_kv, w_uq, w_ukv, w_out):
    B, n, _ = h.shape
    P = ckv_past.shape[1]
    pos_q = P + jnp.arange(n, dtype=jnp.int32)
    pos_k = jnp.arange(P + n, dtype=jnp.int32)
    q_nope, q_rope, ckv, kr = mla_project(h, w_down, g_q, g_kv, w_uq, pos_q)
    w_uk, w_uv = split_ukv(w_ukv)
    ckv_all = jnp.concatenate([ckv_past, ckv.astype(ckv_past.dtype)], axis=1)
    kr_all = jnp.concatenate([kr_past, kr.astype(kr_past.dtype)], axis=1)
    o = mla_attend_latent(q_nope, q_rope, ckv_all, kr_all, w_uk, w_uv, pos_q, pos_k)
    y = o.reshape(B, n, MLA_HEADS * V_DIM) @ w_out
    return y, ckv, kr


def hier_moe(h, w_group, b_group, w_router, b_router, w_gate, w_up, w_down):
    x = h.reshape(-1, D_MODEL)
    g_logits = (x @ w_group).astype(jnp.float32) + b_group.astype(jnp.float32)
    g_prob = jax.nn.softmax(g_logits, axis=-1)
    g_idx = jnp.argmax(g_logits, axis=-1)
    g_gate = jnp.take_along_axis(g_prob, g_idx[:, None], axis=1)[:, 0]
    e_logits = jnp.einsum('td,gde->tge', x, w_router).astype(jnp.float32) + b_router.astype(jnp.float32)
    e_sel = jnp.take_along_axis(e_logits, g_idx[:, None, None], axis=1)[:, 0]
    top_v, top_i = lax.top_k(e_sel, TOP_K_IN_GROUP)
    top_w = jax.nn.softmax(top_v, axis=-1)
    w_e = jnp.sum(jax.nn.one_hot(top_i, EXPERTS_PER_GROUP, dtype=jnp.float32) * top_w[..., None], axis=1)
    comb = (jax.nn.one_hot(g_idx, N_GROUPS, dtype=jnp.float32)[:, :, None]
            * (g_gate[:, None] * w_e)[:, None, :]).astype(x.dtype)
    y = jnp.zeros_like(x)
    for g in range(N_GROUPS):
        a = jnp.einsum('td,edf->tef', x, w_gate[g])
        u = jnp.einsum('td,edf->tef', x, w_up[g])
        hid = jax.nn.silu(a) * u * comb[:, g, :, None]
        y = y + jnp.einsum('tef,efd->td', hid, w_down[g])
    return y.reshape(h.shape)


def setup_inputs(seed: int = 0) -> dict:
    key = jax.random.key(seed)
    ks = jax.random.split(key, 32)
    nrm = lambda k, shape, scale: jax.random.normal(k, shape, jnp.float32) * scale
    D = D_MODEL
    return {
        'x_prompt': nrm(ks[0], (BATCH, SEQ, D), 1.0),
        'x_sample': nrm(ks[1], (DEC_BATCH, DEC_SEQ, D), 1.0),
        'cache_fox_k': nrm(ks[2], (N_AB_LAYERS, DEC_BATCH, PAST_LEN, FOX_HEADS, HEAD_DIM), 1.0),
        'cache_fox_v': nrm(ks[3], (N_AB_LAYERS, DEC_BATCH, PAST_LEN, FOX_HEADS, HEAD_DIM), 1.0),
        'cache_fox_logf': jax.nn.log_sigmoid(FORGET_BIAS_INIT + nrm(ks[4], (N_AB_LAYERS, DEC_BATCH, PAST_LEN, FOX_HEADS), 1.0)),
        'cache_sb_k': nrm(ks[5], (N_AB_LAYERS, DEC_BATCH, PAST_LEN, SB_HEADS, HEAD_DIM), 1.0),
        'cache_sb_v': nrm(ks[6], (N_AB_LAYERS, DEC_BATCH, PAST_LEN, SB_HEADS, HEAD_DIM), 1.0),
        'cache_mla_ckv': nrm(ks[7], (N_MLA_LAYERS, DEC_BATCH, PAST_LEN, KV_LORA), 1.0),
        'cache_mla_krope': nrm(ks[8], (N_MLA_LAYERS, DEC_BATCH, PAST_LEN, QK_ROPE), 1.0),
        'ab_w_in': nrm(ks[9], (N_AB_LAYERS, D, AB_IN), D ** -0.5),
        'ab_b_forget': FORGET_BIAS_INIT + nrm(ks[10], (N_AB_LAYERS, FOX_HEADS), 0.1),
        'ab_w_out': nrm(ks[11], (N_AB_LAYERS, AB_MIX, D), AB_MIX ** -0.5 * DEEPNORM_BETA),
        'mla_w_down': nrm(ks[12], (N_MLA_LAYERS, D, MLA_DOWN), D ** -0.5),
        'mla_g_q': 1.0 + nrm(ks[13], (N_MLA_LAYERS, Q_LORA), 0.01),
        'mla_g_kv': 1.0 + nrm(ks[14], (N_MLA_LAYERS, KV_LORA), 0.01),
        'mla_w_uq': nrm(ks[15], (N_MLA_LAYERS, Q_LORA, MLA_HEADS * (QK_NOPE + QK_ROPE)), Q_LORA ** -0.5),
        'mla_w_ukv': nrm(ks[16], (N_MLA_LAYERS, KV_LORA, MLA_HEADS * (QK_NOPE + V_DIM)), KV_LORA ** -0.5),
        'mla_w_out': nrm(ks[17], (N_MLA_LAYERS, MLA_HEADS * V_DIM, D), (MLA_HEADS * V_DIM) ** -0.5 * DEEPNORM_BETA),
        'moe_w_group': nrm(ks[18], (DEPTH, D, N_GROUPS), D ** -0.5),
        'moe_b_group': nrm(ks[19], (DEPTH, N_GROUPS), 0.01),
        'moe_w_router': nrm(ks[20], (DEPTH, N_GROUPS, D, EXPERTS_PER_GROUP), D ** -0.5),
        'moe_b_router': nrm(ks[21], (DEPTH, N_GROUPS, EXPERTS_PER_GROUP), 0.01),
        'moe_w_gate': nrm(ks[22], (DEPTH, N_GROUPS, EXPERTS_PER_GROUP, D, EXPERT_HIDDEN), D ** -0.5),
        'moe_w_up': nrm(ks[23], (DEPTH, N_GROUPS, EXPERTS_PER_GROUP, D, EXPERT_HIDDEN), D ** -0.5),
        'moe_w_down': nrm(ks[24], (DEPTH, N_GROUPS, EXPERTS_PER_GROUP, EXPERT_HIDDEN, D), EXPERT_HIDDEN ** -0.5 * DEEPNORM_BETA),
        'ln_g': 1.0 + nrm(ks[25], (DEPTH, 2, D), 0.01),
        'ln_b': nrm(ks[26], (DEPTH, 2, D), 0.01),
    }


def reference(x_prompt, x_sample, cache_fox_k, cache_fox_v, cache_fox_logf, cache_sb_k, cache_sb_v,
              cache_mla_ckv, cache_mla_krope, ab_w_in, ab_b_forget, ab_w_out, mla_w_down, mla_g_q,
              mla_g_kv, mla_w_uq, mla_w_ukv, mla_w_out, moe_w_group, moe_b_group, moe_w_router,
              moe_b_router, moe_w_gate, moe_w_up, moe_w_down, ln_g, ln_b):
    xp, xs = x_prompt, x_sample
    fk_p, fv_p, fl_p, sk_p, sv_p, mc_p, mr_p = [], [], [], [], [], [], []
    fk_s, fv_s, fl_s, sk_s, sv_s, mc_s, mr_s = [], [], [], [], [], [], []
    for layer in range(DEPTH):
        i = layer // 2
        if layer % 2 == 0:
            mp, ka, va, lf, kb, vb = ab_prompt(xp, ab_w_in[i], ab_b_forget[i], ab_w_out[i])
            ms, ka2, va2, lf2, kb2, vb2 = ab_sample(xs, cache_fox_k[i], cache_fox_v[i], cache_fox_logf[i],
                                                    cache_sb_k[i], cache_sb_v[i],
                                                    ab_w_in[i], ab_b_forget[i], ab_w_out[i])
            fk_p.append(ka); fv_p.append(va); fl_p.append(lf); sk_p.append(kb); sv_p.append(vb)
            fk_s.append(ka2); fv_s.append(va2); fl_s.append(lf2); sk_s.append(kb2); sv_s.append(vb2)
        else:
            mp, ckv, kr = mla_prompt(xp, mla_w_down[i], mla_g_q[i], mla_g_kv[i], mla_w_uq[i],
                                     mla_w_ukv[i], mla_w_out[i])
            ms, ckv2, kr2 = mla_sample(xs, cache_mla_ckv[i], cache_mla_krope[i], mla_w_down[i], mla_g_q[i],
                                       mla_g_kv[i], mla_w_uq[i], mla_w_ukv[i], mla_w_out[i])
            mc_p.append(ckv); mr_p.append(kr); mc_s.append(ckv2); mr_s.append(kr2)
        xp = layer_norm(DEEPNORM_ALPHA * xp + mp, ln_g[layer, 0], ln_b[layer, 0])
        xs = layer_norm(DEEPNORM_ALPHA * xs + ms, ln_g[layer, 0], ln_b[layer, 0])
        fp = hier_moe(xp, moe_w_group[layer], moe_b_group[layer], moe_w_router[layer], moe_b_router[layer],
                      moe_w_gate[layer], moe_w_up[layer], moe_w_down[layer])
        fs = hier_moe(xs, moe_w_group[layer], moe_b_group[layer], moe_w_router[layer], moe_b_router[layer],
                      moe_w_gate[layer], moe_w_up[layer], moe_w_down[layer])
        xp = layer_norm(DEEPNORM_ALPHA * xp + fp, ln_g[layer, 1], ln_b[layer, 1])
        xs = layer_norm(DEEPNORM_ALPHA * xs + fs, ln_g[layer, 1], ln_b[layer, 1])
    return (xp, xs,
            jnp.stack(fk_p), jnp.stack(fv_p), jnp.stack(fl_p), jnp.stack(sk_p), jnp.stack(sv_p),
            jnp.stack(mc_p), jnp.stack(mr_p),
            jnp.stack(fk_s), jnp.stack(fv_s), jnp.stack(fl_s), jnp.stack(sk_s), jnp.stack(sv_s),
            jnp.stack(mc_s), jnp.stack(mr_s))
```

```python
import functools

import jax
import jax.numpy as jnp
from jax import lax
from jax.experimental import pallas as pl
from jax.experimental.pallas import tpu as pltpu

F32 = jnp.float32
BF16 = jnp.bfloat16
NEG_INF = -1e30

LANES = 128
HEAD_DIM = 64
PAIR_W = 2 * HEAD_DIM
CHUNK = 64
LN_EPS = 1e-5
RMS_EPS = 1e-6
ROPE_BASE = 10000.0
VMEM_LIMIT = 56 * 1024 * 1024


def _cparams(*sem):
    return pltpu.CompilerParams(dimension_semantics=sem, vmem_limit_bytes=VMEM_LIMIT)


def _dot(a, b):
    return jnp.dot(a, b, preferred_element_type=F32)


def _dot_nt(a, b):
    return lax.dot_general(a, b, (((1,), (1,)), ((), ())), preferred_element_type=F32)


def _split_bf16(x, parts):
    out = []
    r = x
    for _ in range(parts):
        h = r.astype(BF16)
        out.append(h)
        r = r - h.astype(F32)
    return out


def _log_sigmoid(x):
    return jnp.minimum(x, 0.0) - jnp.log(1.0 + jnp.exp(-jnp.abs(x)))


def _pick_tile(n, pref, mult=8):
    t = min(pref, n)
    while n % t or t % mult:
        t -= 1
    return t


def _ab_proj_kernel(x_ref, w_ref, bf_ref, qa_ref, ka_ref, va_ref, qb_ref, kb_ref, vb_ref,
                    ka16_ref, va16_ref, kb16_ref, vb16_ref, lf_ref, *, hw, qscale):
    xb = x_ref[...].astype(BF16)

    def seg(j):
        return _dot(xb, w_ref[:, j * hw:(j + 1) * hw])

    qa_ref[...] = (seg(0) * qscale).astype(BF16)
    z = seg(1)
    ka_ref[...] = z
    ka16_ref[...] = z.astype(BF16)
    z = seg(2)
    va_ref[...] = z
    va16_ref[...] = z.astype(BF16)
    qb_ref[...] = (seg(3) * qscale).astype(BF16)
    z = seg(4)
    kb_ref[...] = z
    kb16_ref[...] = z.astype(BF16)
    z = seg(5)
    vb_ref[...] = z
    vb16_ref[...] = z.astype(BF16)
    f = _dot(xb, w_ref[:, 6 * hw:6 * hw + LANES]) + bf_ref[...]
    lf_ref[...] = _log_sigmoid(f)


def _ab_proj(x, w, bf, hw):
    T, D = x.shape
    tm = _pick_tile(T, 256)
    row = lambda w_: pl.BlockSpec((tm, w_), lambda i: (i, 0))
    f32o = jax.ShapeDtypeStruct((T, hw), F32)
    b16o = jax.ShapeDtypeStruct((T, hw), BF16)
    return pl.pallas_call(
        functools.partial(_ab_proj_kernel, hw=hw, qscale=HEAD_DIM ** -0.5),
        grid=(T // tm,),
        in_specs=[row(D), pl.BlockSpec(w.shape, lambda i: (0, 0)), pl.BlockSpec(bf.shape, lambda i: (0, 0))],
        out_specs=[row(hw)] * 10 + [row(LANES)],
        out_shape=[b16o, f32o, f32o, b16o, f32o, f32o, b16o, b16o, b16o, b16o,
                   jax.ShapeDtypeStruct((T, LANES), F32)],
        compiler_params=_cparams("parallel"),
        name="ab_proj",
    )(x, w, bf)


def _cumsum_kernel(x_ref, o_ref, carry_ref, *, tl):
    @pl.when(pl.program_id(1) == 0)
    def _():
        carry_ref[...] = jnp.zeros_like(carry_ref)

    x = x_ref[0]
    r = lax.broadcasted_iota(jnp.int32, (tl, tl), 0)
    c = lax.broadcasted_iota(jnp.int32, (tl, tl), 1)
    upper = (r <= c).astype(BF16)
    parts = jnp.concatenate(_split_bf16(x, 4), axis=0)
    y = _dot(parts, upper)
    cum = (y[0:8] + y[8:16]) + (y[16:24] + y[24:32]) + carry_ref[:, 0:1]
    o_ref[0] = cum
    carry_ref[...] = jnp.broadcast_to(cum[:, tl - 1:tl], carry_ref.shape)


def _cumsum_rows(x):
    B, H, L = x.shape
    tl = _pick_tile(L, 512, LANES)
    return pl.pallas_call(
        functools.partial(_cumsum_kernel, tl=tl),
        grid=(B, L // tl),
        in_specs=[pl.BlockSpec((1, H, tl), lambda b, j: (b, 0, j))],
        out_specs=pl.BlockSpec((1, H, tl), lambda b, j: (b, 0, j)),
        out_shape=jax.ShapeDtypeStruct((B, H, L), F32),
        scratch_shapes=[pltpu.VMEM((H, LANES), F32)],
        compiler_params=_cparams("parallel", "arbitrary"),
        name="cumsum_rows",
    )(x)


def _stack_pair(qp):
    lo = lax.broadcasted_iota(jnp.int32, qp.shape, 1) < HEAD_DIM
    zero = jnp.zeros_like(qp)
    return jnp.concatenate([jnp.where(lo, qp, zero), jnp.where(lo, zero, qp)], axis=0)


def _unstack_pair(o):
    lo = lax.broadcasted_iota(jnp.int32, o.shape[1:], 1) < HEAD_DIM
    return jnp.where(lo, o[0], o[1])


def _softmax_step(s, vb, carry):
    m, l, acc = carry
    two, tq, tk = s.shape
    m_new = jnp.maximum(m, jnp.max(s, axis=-1, keepdims=True))
    alpha = jnp.exp(m - m_new)
    p = jnp.exp(s - m_new)
    l = alpha * l + jnp.sum(p, axis=-1, keepdims=True)
    pv = _dot(p.reshape(two * tq, tk).astype(BF16), vb).reshape(two, tq, vb.shape[-1])
    return m_new, l, alpha * acc + pv


def _softmax_init(tq, width):
    return (jnp.full((2, tq, 1), NEG_INF, F32), jnp.zeros((2, tq, 1), F32), jnp.zeros((2, tq, width), F32))


def _sb_step(z, vb, strict_upper, carry, mask=None):
    run, acc = carry
    l1 = jnp.log(1.0 + jnp.exp(-jnp.abs(z)))
    log_beta = jnp.minimum(z, 0.0) - l1
    log_rest = jnp.minimum(-z, 0.0) - l1
    if mask is not None:
        log_beta = jnp.where(mask, log_beta, NEG_INF)
        log_rest = jnp.where(mask, log_rest, 0.0)
    hi, lo = _split_bf16(log_rest, 2)
    later = _dot(hi, strict_upper) + _dot(lo, strict_upper)
    a = jnp.exp(log_beta + later + run)
    acc = acc + _dot(a.astype(BF16), vb)
    run = run + jnp.sum(log_rest, axis=-1, keepdims=True)
    return run, acc


def _strict_upper(tk):
    r = lax.broadcasted_iota(jnp.int32, (tk, tk), 0)
    c = lax.broadcasted_iota(jnp.int32, (tk, tk), 1)
    return (r > c).astype(BF16)


def _fox_prompt_kernel(q_ref, k_ref, v_ref, cq_ref, ck_ref, o_ref, *, tq, npairs):
    tk = tq
    i = pl.program_id(1)
    row = lax.broadcasted_iota(jnp.int32, (tq, tk), 0)
    col = lax.broadcasted_iota(jnp.int32, (tq, tk), 1)
    causal = (col <= row)[None]
    for p in range(npairs):
        lanes = slice(p * PAIR_W, (p + 1) * PAIR_W)
        q2 = _stack_pair(q_ref[0, :, lanes])
        cqp = cq_ref[0, :, 2 * p:2 * p + 2]
        cq3 = jnp.stack([cqp[:, 0:1], cqp[:, 1:2]], axis=0)

        def step(kt, carry, masked):
            rows = pl.ds(pl.multiple_of(kt * tk, tk), tk)
            s = _dot_nt(q2, k_ref[0, rows, lanes]).reshape(2, tq, tk)
            ck = ck_ref[0, kt, 2 * p:2 * p + 2, :]
            s = s + (cq3 - ck[:, None, :])
            if masked:
                s = jnp.where(causal, s, NEG_INF)
            return _softmax_step(s, v_ref[0, rows, lanes], carry)

        carry = lax.fori_loop(0, i, lambda kt, c: step(kt, c, False), _softmax_init(tq, PAIR_W))
        m, l, acc = step(i, carry, True)
        o_ref[0, :, lanes] = _unstack_pair(acc / l).astype(o_ref.dtype)


def _fox_prompt(q, k, v, cq, ck4, tq):
    B, S, W = q.shape
    nq = S // tq
    H = cq.shape[-1]
    return pl.pallas_call(
        functools.partial(_fox_prompt_kernel, tq=tq, npairs=W // PAIR_W),
        grid=(B, nq),
        in_specs=[pl.BlockSpec((1, tq, W), lambda b, i: (b, i, 0)),
                  pl.BlockSpec((1, S, W), lambda b, i: (b, 0, 0)),
                  pl.BlockSpec((1, S, W), lambda b, i: (b, 0, 0)),
                  pl.BlockSpec((1, tq, H), lambda b, i: (b, i, 0)),
                  pl.BlockSpec((1, nq, H, tq), lambda b, i: (b, 0, 0, 0))],
        out_specs=pl.BlockSpec((1, tq, W), lambda b, i: (b, i, 0)),
        out_shape=jax.ShapeDtypeStruct((B, S, W), BF16),
        compiler_params=_cparams("parallel", "arbitrary"),
        name="fox_prompt",
    )(q, k, v, cq, ck4)


def _sb_prompt_kernel(q_ref, k_ref, v_ref, o_ref, *, tq, npairs):
    tk = tq
    i = pl.program_id(1)
    row = lax.broadcasted_iota(jnp.int32, (2 * tq, tk), 0)
    col = lax.broadcasted_iota(jnp.int32, (2 * tq, tk), 1)
    before = col < jnp.where(row >= tq, row - tq, row)
    upper = _strict_upper(tk)
    for p in range(npairs):
        lanes = slice(p * PAIR_W, (p + 1) * PAIR_W)
        q2 = _stack_pair(q_ref[0, :, lanes])

        def step(kt, carry, mask):
            rows = pl.ds(pl.multiple_of(kt * tk, tk), tk)
            z = _dot_nt(q2, k_ref[0, rows, lanes])
            return _sb_step(z, v_ref[0, rows, lanes], upper, carry, mask)

        carry = (jnp.zeros((2 * tq, 1), F32), jnp.zeros((2 * tq, PAIR_W), F32))
        carry = step(i, carry, before)
        run, acc = lax.fori_loop(0, i, lambda j, c: step(i - 1 - j, c, None), carry)
        o_ref[0, :, lanes] = _unstack_pair(acc.reshape(2, tq, PAIR_W)).astype(o_ref.dtype)


def _sb_prompt(q, k, v, tq):
    B, S, W = q.shape
    return pl.pallas_call(
        functools.partial(_sb_prompt_kernel, tq=tq, npairs=W // PAIR_W),
        grid=(B, S // tq),
        in_specs=[pl.BlockSpec((1, tq, W), lambda b, i: (b, i, 0)),
                  pl.BlockSpec((1, S, W), lambda b, i: (b, 0, 0)),
                  pl.BlockSpec((1, S, W), lambda b, i: (b, 0, 0))],
        out_specs=pl.BlockSpec((1, tq, W), lambda b, i: (b, i, 0)),
        out_shape=jax.ShapeDtypeStruct((B, S, W), BF16),
        compiler_params=_cparams("parallel", "arbitrary"),
        name="sb_prompt",
    )(q, k, v)


def _prep_ab_weights(w_in, b_f):
    D = w_in.shape[0]
    H = b_f.shape[0]
    hw = (w_in.shape[1] - H) // 6
    main = jnp.concatenate([w_in[:, :3 * hw], w_in[:, 3 * hw + H:]], axis=1)
    wf = jnp.zeros((D, LANES), w_in.dtype).at[:, :H].set(w_in[:, 3 * hw:3 * hw + H])
    bf = jnp.zeros((1, LANES), F32).at[0, :H].set(b_f.astype(F32))
    return jnp.concatenate([main, wf], axis=1).astype(BF16), bf


def _fox_cum(lf, B, S, tq):
    H = 8
    lt = jnp.transpose(lf[:, :H].reshape(B, S, H), (0, 2, 1))
    cum_t = _cumsum_rows(lt)
    cq = jnp.transpose(cum_t, (0, 2, 1))
    ck4 = jnp.transpose(cum_t.reshape(B, H, S // tq, tq), (0, 2, 1, 3))
    return cq, ck4


def _fox_sample_kernel(q_ref, kn_ref, vn_ref, kc_ref, vc_ref, cq_ref, ckp_ref, ckn_ref, o_ref,
                       m_ref, l_ref, acc_ref, *, n, npairs):
    kt = pl.program_id(1)

    @pl.when(kt == 0)
    def _():
        m_ref[...] = jnp.full(m_ref.shape, NEG_INF, F32)
        l_ref[...] = jnp.zeros_like(l_ref)
        acc_ref[...] = jnp.zeros_like(acc_ref)

    kc = kc_ref[0].astype(BF16)
    vc = vc_ref[0].astype(BF16)
    tk = kc.shape[0]

    def pair_inputs(p):
        lanes = slice(p * PAIR_W, (p + 1) * PAIR_W)
        cqp = cq_ref[0, :, 2 * p:2 * p + 2]
        return lanes, _stack_pair(q_ref[0, :, lanes]), jnp.stack([cqp[:, 0:1], cqp[:, 1:2]], axis=0)

    for p in range(npairs):
        lanes, q2, cq3 = pair_inputs(p)
        s = _dot_nt(q2, kc[:, lanes]).reshape(2, n, tk)
        s = s + (cq3 - ckp_ref[0, 0, 2 * p:2 * p + 2, :][:, None, :])
        m, l, acc = _softmax_step(s, vc[:, lanes], (m_ref[p], l_ref[p], acc_ref[p]))
        m_ref[p] = m
        l_ref[p] = l
        acc_ref[p] = acc

    @pl.when(kt == pl.num_programs(1) - 1)
    def _():
        row = lax.broadcasted_iota(jnp.int32, (n, n), 0)
        col = lax.broadcasted_iota(jnp.int32, (n, n), 1)
        causal = (col <= row)[None]
        for p in range(npairs):
            lanes, q2, cq3 = pair_inputs(p)
            s = _dot_nt(q2, kn_ref[0, :, lanes]).reshape(2, n, n)
            s = s + (cq3 - ckn_ref[0, 2 * p:2 * p + 2, :][:, None, :])
            s = jnp.where(causal, s, NEG_INF)
            m, l, acc = _softmax_step(s, vn_ref[0, :, lanes], (m_ref[p], l_ref[p], acc_ref[p]))
            o_ref[0, :, lanes] = _unstack_pair(acc / l).astype(o_ref.dtype)


def _fox_sample(q, kn, vn, kc, vc, cq, ckp, ckn, tk):
    DB, n, W = q.shape
    P = kc.shape[1]
    H = cq.shape[-1]
    npairs = W // PAIR_W
    new = pl.BlockSpec((1, n, W), lambda b, j: (b, 0, 0))
    cache = pl.BlockSpec((1, tk, W), lambda b, j: (b, j, 0))
    return pl.pallas_call(
        functools.partial(_fox_sample_kernel, n=n, npairs=npairs),
        grid=(DB, P // tk),
        in_specs=[new, new, new, cache, cache,
                  pl.BlockSpec((1, n, H), lambda b, j: (b, 0, 0)),
                  pl.BlockSpec((1, 1, H, tk), lambda b, j: (b, j, 0, 0)),
                  pl.BlockSpec((1, H, n), lambda b, j: (b, 0, 0))],
        out_specs=new,
        out_shape=jax.ShapeDtypeStruct((DB, n, W), BF16),
        scratch_shapes=[pltpu.VMEM((npairs, 2, n, 1), F32), pltpu.VMEM((npairs, 2, n, 1), F32),
                        pltpu.VMEM((npairs, 2, n, PAIR_W), F32)],
        compiler_params=_cparams("parallel", "arbitrary"),
        name="fox_sample",
    )(q, kn, vn, kc, vc, cq, ckp, ckn)


def _sb_sample_kernel(q_ref, kn_ref, vn_ref, kc_ref, vc_ref, o_ref, run_ref, acc_ref, *, n, npairs, sub):
    kt = pl.program_id(1)
    upper = _strict_upper(sub)

    @pl.when(kt == 0)
    def _():
        row = lax.broadcasted_iota(jnp.int32, (2 * n, n), 0)
        col = lax.broadcasted_iota(jnp.int32, (2 * n, n), 1)
        before = col < jnp.where(row >= n, row - n, row)
        upper_n = _strict_upper(n)
        for p in range(npairs):
            lanes = slice(p * PAIR_W, (p + 1) * PAIR_W)
            z = _dot_nt(_stack_pair(q_ref[0, :, lanes]), kn_ref[0, :, lanes])
            carry = (jnp.zeros((2 * n, 1), F32), jnp.zeros((2 * n, PAIR_W), F32))
            run, acc = _sb_step(z, vn_ref[0, :, lanes], upper_n, carry, before)
            run_ref[p] = run
            acc_ref[p] = acc

    tk = kc_ref.shape[1]
    for p in range(npairs):
        lanes = slice(p * PAIR_W, (p + 1) * PAIR_W)
        q2 = _stack_pair(q_ref[0, :, lanes])
        carry = (run_ref[p], acc_ref[p])
        for c in reversed(range(tk // sub)):
            rows = slice(c * sub, (c + 1) * sub)
            z = _dot_nt(q2, kc_ref[0, rows, lanes].astype(BF16))
            carry = _sb_step(z, vc_ref[0, rows, lanes].astype(BF16), upper, carry)
        run_ref[p] = carry[0]
        acc_ref[p] = carry[1]

    @pl.when(kt == pl.num_programs(1) - 1)
    def _():
        for p in range(npairs):
            lanes = slice(p * PAIR_W, (p + 1) * PAIR_W)
            o_ref[0, :, lanes] = _unstack_pair(acc_ref[p].reshape(2, n, PAIR_W)).astype(o_ref.dtype)


def _sb_sample(q, kn, vn, kc, vc, tk, sub):
    DB, n, W = q.shape
    P = kc.shape[1]
    nk = P // tk
    npairs = W // PAIR_W
    new = pl.BlockSpec((1, n, W), lambda b, j: (b, 0, 0))
    cache = pl.BlockSpec((1, tk, W), lambda b, j: (b, nk - 1 - j, 0))
    return pl.pallas_call(
        functools.partial(_sb_sample_kernel, n=n, npairs=npairs, sub=sub),
        grid=(DB, nk),
        in_specs=[new, new, new, cache, cache],
        out_specs=new,
        out_shape=jax.ShapeDtypeStruct((DB, n, W), BF16),
        scratch_shapes=[pltpu.VMEM((npairs, 2 * n, 1), F32), pltpu.VMEM((npairs, 2 * n, PAIR_W), F32)],
        compiler_params=_cparams("parallel", "arbitrary"),
        name="sb_sample",
    )(q, kn, vn, kc, vc)


def _fox_sample_cum(lf_new, lf_past, tk):
    DB, P, H = lf_past.shape
    n = lf_new.shape[1]
    L = -(-(P + n) // LANES) * LANES
    both = jnp.concatenate([lf_past.astype(F32), lf_new, jnp.zeros((DB, L - P - n, H), F32)], axis=1)
    cum_t = _cumsum_rows(jnp.transpose(both, (0, 2, 1)))
    ckp = jnp.transpose(cum_t[:, :, :P].reshape(DB, H, P // tk, tk), (0, 2, 1, 3))
    ckn = cum_t[:, :, P:P + n]
    return jnp.transpose(ckn, (0, 2, 1)), ckp, ckn


ROUTER_ROWS = 48


def _layer_norm(y, g, b):
    mu = jnp.mean(y, axis=-1, keepdims=True)
    yc = y - mu
    var = jnp.mean(yc * yc, axis=-1, keepdims=True)
    return yc * lax.rsqrt(var + LN_EPS) * g + b


def _first_argmax(v, ridx):
    vmax = jnp.max(v, axis=0, keepdims=True)
    idx = jnp.min(jnp.where(v == vmax, ridx, v.shape[0]), axis=0, keepdims=True)
    return vmax, idx


def _mix_out_kernel(o_ref, w_ref, x_ref, g_ref, b_ref, wrh_ref, wrl_ref, rb_ref, h_ref, ids_ref, wts_ref,
                    *, alpha, n_groups):
    h = _layer_norm(alpha * x_ref[...] + _dot(o_ref[...], w_ref[...]), g_ref[...], b_ref[...])
    h_ref[...] = h
    hh, hl = _split_bf16(h, 2)
    wrh = wrh_ref[...]
    lg = _dot_nt(wrh, hh) + (_dot_nt(wrh, hl) + _dot_nt(wrl_ref[...], hh)) + rb_ref[...]
    tm = lg.shape[1]
    ridx = lax.broadcasted_iota(jnp.int32, (8, tm), 0)
    g = jnp.where(ridx < n_groups, lg[0:8], NEG_INF)
    gmax, gidx = _first_argmax(g, ridx)
    gate = 1.0 / jnp.sum(jnp.exp(g - gmax), axis=0, keepdims=True)
    esel = lg[8:16]
    for gg in range(1, n_groups):
        esel = jnp.where(gidx == gg, lg[8 + 8 * gg:16 + 8 * gg], esel)
    v1, i1 = _first_argmax(esel, ridx)
    v2, i2 = _first_argmax(jnp.where(ridx == i1, NEG_INF, esel), ridx)
    t = jnp.exp(v2 - v1)
    w1 = 1.0 / (1.0 + t)
    ids_ref[...] = jnp.where(ridx == 0, gidx * 8 + i1, jnp.where(ridx == 1, gidx * 8 + i2, 0))
    wts_ref[...] = jnp.where(ridx == 0, gate * w1, jnp.where(ridx == 1, gate * (t * w1), 0.0))


def _mix_out(o, w, x, g, b, wrh, wrl, rb, alpha, n_groups):
    T, D = x.shape
    tm = _pick_tile(T, 256, LANES)
    const = lambda a: pl.BlockSpec(a.shape, lambda i: (0,) * a.ndim)
    rb_t = jnp.broadcast_to(rb, (ROUTER_ROWS, tm))
    return pl.pallas_call(
        functools.partial(_mix_out_kernel, alpha=alpha, n_groups=n_groups),
        grid=(T // tm,),
        in_specs=[pl.BlockSpec((tm, o.shape[1]), lambda i: (i, 0)), const(w),
                  pl.BlockSpec((tm, D), lambda i: (i, 0)), const(g), const(b), const(wrh), const(wrl), const(rb_t)],
        out_specs=[pl.BlockSpec((tm, D), lambda i: (i, 0)), pl.BlockSpec((8, tm), lambda i: (0, i)),
                   pl.BlockSpec((8, tm), lambda i: (0, i))],
        out_shape=[jax.ShapeDtypeStruct((T, D), F32), jax.ShapeDtypeStruct((8, T), jnp.int32),
                   jax.ShapeDtypeStruct((8, T), F32)],
        compiler_params=_cparams("parallel"),
        name="mix_out_ln_router",
    )(o, w, x, g, b, wrh, wrl, rb_t)


def _prep_router(w_group, b_group, w_router, b_router):
    D, G = w_group.shape
    E = w_router.shape[-1]
    wr = jnp.zeros((ROUTER_ROWS, D), F32)
    wr = wr.at[:G].set(w_group.T.astype(F32))
    wr = wr.at[8:8 + G * E].set(jnp.transpose(w_router, (0, 2, 1)).reshape(G * E, D).astype(F32))
    rb = jnp.zeros((ROUTER_ROWS, 1), F32)
    rb = rb.at[:G, 0].set(b_group.astype(F32)).at[8:8 + G * E, 0].set(b_router.reshape(-1).astype(F32))
    hi = wr.astype(BF16)
    return hi, (wr - hi.astype(F32)).astype(BF16), rb


def _gather_rows(idx_ref, n, src_hbm, dst, sem):
    def body(r, carry):
        pltpu.make_async_copy(src_hbm.at[pl.ds(idx_ref[0, 0, r], 1)], dst.at[pl.ds(r, 1)], sem).start()
        return carry
    lax.fori_loop(0, n, body, 0, unroll=8)


def _wait_rows(n, src_hbm, dst, sem):
    pltpu.make_async_copy(src_hbm.at[pl.ds(0, n)], dst, sem).wait()


def _moe_experts_kernel(te_ref, tv_ref, src_ref, nxt_ref, x_hbm, wg_ref, wu_ref, wd_ref, y_ref, xbuf, sem, *, tm):
    i = pl.program_id(0)
    nt = pl.num_programs(0)
    slot = i % 2

    @pl.when(jnp.logical_and(i == 0, tv_ref[0] > 0))
    def _():
        _gather_rows(src_ref, tm, x_hbm, xbuf.at[0], sem.at[0])

    @pl.when(jnp.logical_and(i + 1 < nt, tv_ref[jnp.minimum(i + 1, nt - 1)] > 0))
    def _():
        _gather_rows(nxt_ref, tm, x_hbm, xbuf.at[1 - slot], sem.at[1 - slot])

    @pl.when(tv_ref[i] > 0)
    def _():
        _wait_rows(tm, x_hbm, xbuf.at[slot], sem.at[slot])
        xb = xbuf[slot].astype(BF16)
        a = _dot(xb, wg_ref[0].astype(BF16))
        u = _dot(xb, wu_ref[0].astype(BF16))
        hid = (a / (1.0 + jnp.exp(-a))) * u
        y_ref[...] = _dot(hid.astype(BF16), wd_ref[0].astype(BF16))

    @pl.when(tv_ref[i] == 0)
    def _():
        y_ref[...] = jnp.zeros_like(y_ref)


def _moe_experts(x, w_gate, w_up, w_down, tile_expert, tile_valid, src, tm):
    T, D = x.shape
    F = w_gate.shape[-1]
    NT = tile_expert.shape[0]
    grid_spec = pltpu.PrefetchScalarGridSpec(
        num_scalar_prefetch=2,
        grid=(NT,),
        in_specs=[pl.BlockSpec((1, 1, tm), lambda i, te, tv: (i, 0, 0), memory_space=pltpu.SMEM),
                  pl.BlockSpec((1, 1, tm), lambda i, te, tv: (jnp.minimum(i + 1, NT - 1), 0, 0),
                               memory_space=pltpu.SMEM),
                  pl.BlockSpec(memory_space=pl.ANY),
                  pl.BlockSpec((1, D, F), lambda i, te, tv: (te[i], 0, 0)),
                  pl.BlockSpec((1, D, F), lambda i, te, tv: (te[i], 0, 0)),
                  pl.BlockSpec((1, F, D), lambda i, te, tv: (te[i], 0, 0))],
        out_specs=pl.BlockSpec((tm, D), lambda i, te, tv: (i, 0)),
        scratch_shapes=[pltpu.VMEM((2, tm, D), F32), pltpu.SemaphoreType.DMA((2,))],
    )
    return pl.pallas_call(
        functools.partial(_moe_experts_kernel, tm=tm),
        grid_spec=grid_spec,
        out_shape=jax.ShapeDtypeStruct((NT * tm, D), F32),
        compiler_params=_cparams("arbitrary"),
        name="moe_experts",
    )(tile_expert, tile_valid, src, src, x, w_gate, w_up, w_down)


def _moe_combine_kernel(pos_ref, nxt_ref, ys_hbm, h_ref, w_ref, g_ref, b_ref, o_ref, buf, sem, *, tm, alpha):
    i = pl.program_id(0)
    nt = pl.num_programs(0)
    slot = i % 2

    @pl.when(i == 0)
    def _():
        _gather_rows(pos_ref, 2 * tm, ys_hbm, buf.at[0], sem.at[0])

    @pl.when(i + 1 < nt)
    def _():
        _gather_rows(nxt_ref, 2 * tm, ys_hbm, buf.at[1 - slot], sem.at[1 - slot])

    _wait_rows(2 * tm, ys_hbm, buf.at[slot], sem.at[slot])
    w = w_ref[...]
    y = alpha * h_ref[...] + (w[:, 0:1] * buf[slot, 0:tm] + w[:, 1:2] * buf[slot, tm:2 * tm])
    o_ref[...] = _layer_norm(y, g_ref[...], b_ref[...])


def _moe_combine(ys, h, pos, wts, g, b, alpha, tm):
    T, D = h.shape
    nt = T // tm
    const = lambda a: pl.BlockSpec(a.shape, lambda i: (0,) * a.ndim)
    return pl.pallas_call(
        functools.partial(_moe_combine_kernel, tm=tm, alpha=alpha),
        grid=(nt,),
        in_specs=[pl.BlockSpec((1, 1, 2 * tm), lambda i: (i, 0, 0), memory_space=pltpu.SMEM),
                  pl.BlockSpec((1, 1, 2 * tm), lambda i: (jnp.minimum(i + 1, nt - 1), 0, 0),
                               memory_space=pltpu.SMEM),
                  pl.BlockSpec(memory_space=pl.ANY),
                  pl.BlockSpec((tm, D), lambda i: (i, 0)),
                  pl.BlockSpec((tm, 2), lambda i: (i, 0)), const(g), const(b)],
        out_specs=pl.BlockSpec((tm, D), lambda i: (i, 0)),
        out_shape=jax.ShapeDtypeStruct((T, D), F32),
        scratch_shapes=[pltpu.VMEM((2, 2 * tm, D), F32), pltpu.SemaphoreType.DMA((2,))],
        compiler_params=_cparams("arbitrary"),
        name="moe_combine_ln",
    )(pos, pos, ys, h, wts, g, b)


def _route(ids, n_experts, tm):
    T = ids.shape[1]
    flat = ids.reshape(-1)
    order = jnp.argsort(flat, stable=True)
    counts = jnp.zeros((n_experts,), jnp.int32).at[flat].add(1)
    padded = (counts + tm - 1) // tm * tm
    starts = jnp.cumsum(padded) - padded
    raw_starts = jnp.cumsum(counts) - counts
    sorted_e = flat[order]
    dest = starts[sorted_e] + (jnp.arange(2 * T, dtype=jnp.int32) - raw_starts[sorted_e])
    NT = (2 * T + n_experts * (tm - 1)) // tm
    src = jnp.zeros((NT * tm,), jnp.int32).at[dest].set((order % T).astype(jnp.int32))
    pos = jnp.zeros((2 * T,), jnp.int32).at[order].set(dest.astype(jnp.int32)).reshape(2, T)
    ends = jnp.cumsum(padded)
    tile_start = jnp.arange(NT, dtype=jnp.int32) * tm
    tile_expert = jnp.minimum(jnp.searchsorted(ends, tile_start, side="right"), n_experts - 1).astype(jnp.int32)
    tile_valid = (tile_start < ends[-1]).astype(jnp.int32)
    return tile_expert, tile_valid, src.reshape(NT, 1, tm), pos


MLA_PAIR_W = 256
QK_NOPE = 64
QK_ROPE = 32


def _mla_proj_kernel(h_ref, wdn_ref, gq_ref, gkv_ref, wq_ref, wqr_ref, wk_ref, wv_ref, cos_ref, sin_ref,
                     ckv_ref, kr_ref, qcat_ref, kcat_ref, v_ref, *, q_lora, kv_lora, npairs, scale):
    z = _dot(h_ref[...].astype(BF16), wdn_ref[...])
    cq = z[:, :q_lora]
    ckv = z[:, q_lora:q_lora + kv_lora]
    o = q_lora + kv_lora
    kr_raw = z[:, o:o + MLA_PAIR_W]
    kr_rot = z[:, o + MLA_PAIR_W:o + 2 * MLA_PAIR_W]
    cq = cq * lax.rsqrt(jnp.mean(cq * cq, axis=-1, keepdims=True) + RMS_EPS) * gq_ref[...]
    ckv = ckv * lax.rsqrt(jnp.mean(ckv * ckv, axis=-1, keepdims=True) + RMS_EPS) * gkv_ref[...]
    ckv_ref[...] = ckv
    cos = cos_ref[...]
    sin = sin_ref[...]
    kr_tile = kr_raw * cos + kr_rot * sin
    kr_ref[...] = kr_tile[:, 2 * QK_NOPE:2 * QK_NOPE + QK_ROPE]
    cqb = cq.astype(BF16)
    ckb = ckv.astype(BF16)
    for p in range(npairs):
        lanes = slice(p * MLA_PAIR_W, (p + 1) * MLA_PAIR_W)
        q = _dot(cqb, wq_ref[:, lanes]) * cos + _dot(cqb, wqr_ref[:, lanes]) * sin
        qcat_ref[:, lanes] = (q * scale).astype(BF16)
        kcat_ref[:, lanes] = (_dot(ckb, wk_ref[:, lanes]) + kr_tile).astype(BF16)
    v_ref[...] = _dot(ckb, wv_ref[...]).astype(BF16)


def _mla_proj(h, wdn, gq, gkv, wq, wqr, wk, wv, cos_t, sin_t, table_block, tm):
    T, D = h.shape
    q_lora, kv_lora = gq.shape[1], gkv.shape[1]
    npairs = wq.shape[1] // MLA_PAIR_W
    const = lambda a: pl.BlockSpec(a.shape, lambda i: (0,) * a.ndim)
    row = lambda w_: pl.BlockSpec((tm, w_), lambda i: (i, 0))
    table = pl.BlockSpec((tm, MLA_PAIR_W), lambda i: (table_block(i), 0))
    return pl.pallas_call(
        functools.partial(_mla_proj_kernel, q_lora=q_lora, kv_lora=kv_lora, npairs=npairs,
                          scale=(QK_NOPE + QK_ROPE) ** -0.5),
        grid=(T // tm,),
        in_specs=[row(D), const(wdn), const(gq), const(gkv), const(wq), const(wqr), const(wk), const(wv),
                  table, table],
        out_specs=[row(kv_lora), row(QK_ROPE), row(wq.shape[1]), row(wk.shape[1]), row(wv.shape[1])],
        out_shape=[jax.ShapeDtypeStruct((T, kv_lora), F32), jax.ShapeDtypeStruct((T, QK_ROPE), F32),
                   jax.ShapeDtypeStruct((T, wq.shape[1]), BF16), jax.ShapeDtypeStruct((T, wk.shape[1]), BF16),
                   jax.ShapeDtypeStruct((T, wv.shape[1]), BF16)],
        compiler_params=_cparams("parallel"),
        name="mla_proj",
    )(h, wdn, gq, gkv, wq, wqr, wk, wv, cos_t, sin_t)


def _rot_half(w):
    half = w.shape[-1] // 2
    return jnp.concatenate([-w[..., half:], w[..., :half]], axis=-1)


def _prep_mla_weights(w_down, w_uq, w_ukv, heads, q_lora, kv_lora):
    D = w_down.shape[0]
    hp = heads // 2
    w_kr = w_down[:, q_lora + kv_lora:]
    slot = lambda w: jnp.concatenate([jnp.zeros((D, 2 * QK_NOPE), w.dtype), w, w,
                                      jnp.zeros((D, MLA_PAIR_W - 2 * QK_NOPE - 2 * QK_ROPE), w.dtype)], axis=1)
    wdn = jnp.concatenate([w_down[:, :q_lora + kv_lora], slot(w_kr), slot(_rot_half(w_kr))], axis=1)
    wq4 = w_uq.reshape(q_lora, hp, 2, QK_NOPE + QK_ROPE)
    nope, ropew = wq4[..., :QK_NOPE], wq4[..., QK_NOPE:]
    zpad = jnp.zeros((q_lora, hp, MLA_PAIR_W - 2 * QK_NOPE - 2 * QK_ROPE), w_uq.dtype)
    wq = jnp.concatenate([nope[:, :, 0], nope[:, :, 1], ropew[:, :, 0], ropew[:, :, 1], zpad], axis=-1)
    rot = _rot_half(ropew)
    wqr = jnp.concatenate([jnp.zeros((q_lora, hp, 2 * QK_NOPE), w_uq.dtype), rot[:, :, 0], rot[:, :, 1], zpad],
                          axis=-1)
    wkv4 = w_ukv.reshape(kv_lora, hp, 2, QK_NOPE + HEAD_DIM)
    kn = wkv4[..., :QK_NOPE]
    wk = jnp.concatenate([kn[:, :, 0], kn[:, :, 1], jnp.zeros((kv_lora, hp, MLA_PAIR_W - 2 * QK_NOPE), w_ukv.dtype)],
                         axis=-1)
    wv = wkv4[..., QK_NOPE:].reshape(kv_lora, heads * HEAD_DIM)
    w_uk = w_ukv.reshape(kv_lora, heads, QK_NOPE + HEAD_DIM)[..., :QK_NOPE]
    w_uv = w_ukv.reshape(kv_lora, heads, QK_NOPE + HEAD_DIM)[..., QK_NOPE:]
    b16 = lambda a: a.astype(BF16)
    return (b16(wdn), b16(wq.reshape(q_lora, -1)), b16(wqr.reshape(q_lora, -1)), b16(wk.reshape(kv_lora, -1)),
            b16(wv), b16(jnp.transpose(w_uk, (1, 2, 0))), b16(jnp.transpose(w_uv, (1, 0, 2))))


def _rope_tables(pos):
    half = QK_ROPE // 2
    inv_freq = ROPE_BASE ** (-jnp.arange(half, dtype=F32) / half)
    ang = pos.astype(F32)[:, None] * inv_freq[None, :]
    n = pos.shape[0]
    pad = jnp.zeros((n, MLA_PAIR_W - 2 * QK_NOPE - 2 * QK_ROPE), F32)
    cos = jnp.concatenate([jnp.ones((n, 2 * QK_NOPE), F32)] + [jnp.cos(ang)] * 4 + [pad], axis=1)
    sin = jnp.concatenate([jnp.zeros((n, 2 * QK_NOPE), F32)] + [jnp.sin(ang)] * 4 + [pad], axis=1)
    return cos, sin


def _mla_prompt_kernel(q_ref, k_ref, v_ref, o_ref, *, tq, npairs):
    tk = tq
    i = pl.program_id(2)
    row = lax.broadcasted_iota(jnp.int32, (tq, tk), 0)
    col = lax.broadcasted_iota(jnp.int32, (tq, tk), 1)
    visible = ((col // CHUNK) <= (row // CHUNK))[None]
    lane = lax.broadcasted_iota(jnp.int32, (tq, MLA_PAIR_W), 1)
    rope_a = jnp.logical_and(lane >= 2 * QK_NOPE, lane < 2 * QK_NOPE + QK_ROPE)
    rope_b = jnp.logical_and(lane >= 2 * QK_NOPE + QK_ROPE, lane < 2 * QK_NOPE + 2 * QK_ROPE)
    keep_a = jnp.logical_or(lane < QK_NOPE, rope_a)
    keep_b = jnp.logical_or(jnp.logical_and(lane >= QK_NOPE, lane < 2 * QK_NOPE), rope_b)
    for p in range(npairs):
        klanes = slice(p * MLA_PAIR_W, (p + 1) * MLA_PAIR_W)
        vlanes = slice(p * PAIR_W, (p + 1) * PAIR_W)
        qp = q_ref[0, :, klanes]
        zero = jnp.zeros_like(qp)
        q2 = jnp.concatenate([jnp.where(keep_a, qp, zero), jnp.where(keep_b, qp, zero)], axis=0)

        def step(kt, carry, masked):
            rows = pl.ds(pl.multiple_of(kt * tk, tk), tk)
            s = _dot_nt(q2, k_ref[0, rows, klanes]).reshape(2, tq, tk)
            if masked:
                s = jnp.where(visible, s, NEG_INF)
            return _softmax_step(s, v_ref[0, rows, vlanes], carry)

        carry = lax.fori_loop(0, i, lambda kt, c: step(kt, c, False), _softmax_init(tq, PAIR_W))
        m, l, acc = step(i, carry, True)
        o_ref[0, :, vlanes] = _unstack_pair(acc / l).astype(o_ref.dtype)


def _mla_prompt(qcat, kcat, v, tq, pairs_per_step):
    B, S, WQ = qcat.shape
    ngroups = WQ // (MLA_PAIR_W * pairs_per_step)
    kw = MLA_PAIR_W * pairs_per_step
    vw = PAIR_W * pairs_per_step
    return pl.pallas_call(
        functools.partial(_mla_prompt_kernel, tq=tq, npairs=pairs_per_step),
        grid=(B, ngroups, S // tq),
        in_specs=[pl.BlockSpec((1, tq, kw), lambda b, g, i: (b, i, g)),
                  pl.BlockSpec((1, S, kw), lambda b, g, i: (b, 0, g)),
                  pl.BlockSpec((1, S, vw), lambda b, g, i: (b, 0, g))],
        out_specs=pl.BlockSpec((1, tq, vw), lambda b, g, i: (b, i, g)),
        out_shape=jax.ShapeDtypeStruct((B, S, v.shape[-1]), BF16),
        compiler_params=_cparams("parallel", "parallel", "arbitrary"),
        name="mla_prompt",
    )(qcat, kcat, v)


def _mla_sample_queries(qcat, heads):
    DB, n, _ = qcat.shape
    q4 = qcat.reshape(DB, n, heads // 2, MLA_PAIR_W)
    nope = q4[..., :2 * QK_NOPE].reshape(DB, n, heads, QK_NOPE)
    ropeq = q4[..., 2 * QK_NOPE:2 * QK_NOPE + 2 * QK_ROPE].reshape(DB, n, heads, QK_ROPE)
    rows = lambda a: jnp.transpose(a, (0, 2, 1, 3)).reshape(DB, heads * n, a.shape[-1])
    return rows(nope), rows(ropeq)


def _mla_sample_kernel(qn_ref, qr_ref, wuk_ref, wuv_ref, cc_ref, rc_ref, cn_ref, rn_ref, o_ref,
                       qlat_ref, m_ref, l_ref, acc_ref, *, n, heads):
    kt = pl.program_id(1)

    @pl.when(kt == 0)
    def _():
        for h in range(heads):
            rows = slice(h * n, (h + 1) * n)
            qlat_ref[rows, :] = _dot(qn_ref[0, rows, :], wuk_ref[h]).astype(BF16)
        m_ref[...] = jnp.full(m_ref.shape, NEG_INF, F32)
        l_ref[...] = jnp.zeros_like(l_ref)
        acc_ref[...] = jnp.zeros_like(acc_ref)

    def update(ckv, kr):
        s = _dot_nt(qlat_ref[...], ckv) + _dot_nt(qr_ref[0], kr)
        m, l, acc = _softmax_step(s[None], ckv, (m_ref[...], l_ref[...], acc_ref[...]))
        m_ref[...] = m
        l_ref[...] = l
        acc_ref[...] = acc

    update(cc_ref[0].astype(BF16), rc_ref[0].astype(BF16))

    @pl.when(kt == pl.num_programs(1) - 1)
    def _():
        update(cn_ref[0], rn_ref[0])
        o_lat = (acc_ref[0] / l_ref[0]).astype(BF16)
        for h in range(heads):
            o_ref[0, :, h * HEAD_DIM:(h + 1) * HEAD_DIM] = _dot(o_lat[h * n:(h + 1) * n], wuv_ref[h]).astype(o_ref.dtype)


def _mla_sample(qn, qr, wuk, wuv, ckv_c, kr_c, ckv_n, kr_n, n, tk):
    DB, R, _ = qn.shape
    heads = R // n
    P, C = ckv_c.shape[1], ckv_c.shape[2]
    const = lambda a: pl.BlockSpec(a.shape, lambda b, j: (0,) * a.ndim)
    per_b = lambda a: pl.BlockSpec((1,) + a.shape[1:], lambda b, j: (b, 0, 0))
    return pl.pallas_call(
        functools.partial(_mla_sample_kernel, n=n, heads=heads),
        grid=(DB, P // tk),
        in_specs=[per_b(qn), per_b(qr), const(wuk), const(wuv),
                  pl.BlockSpec((1, tk, C), lambda b, j: (b, j, 0)),
                  pl.BlockSpec((1, tk, QK_ROPE), lambda b, j: (b, j, 0)),
                  per_b(ckv_n), per_b(kr_n)],
        out_specs=pl.BlockSpec((1, n, heads * HEAD_DIM), lambda b, j: (b, 0, 0)),
        out_shape=jax.ShapeDtypeStruct((DB, n, heads * HEAD_DIM), BF16),
        scratch_shapes=[pltpu.VMEM((R, C), BF16), pltpu.VMEM((1, R, 1), F32), pltpu.VMEM((1, R, 1), F32),
                        pltpu.VMEM((1, R, C), F32)],
        compiler_params=_cparams("parallel", "arbitrary"),
        name="mla_sample",
    )(qn, qr, wuk, wuv, ckv_c, kr_c, ckv_n, kr_n)


def _moe_layer(h, ids, wts, w_gate, w_up, w_down, g, b, alpha, tm):
    T, D = h.shape
    n_experts = w_gate.shape[0] * w_gate.shape[1]
    tile_expert, tile_valid, src, pos = _route(ids[:2], n_experts, tm)
    flat3 = lambda w: w.reshape((n_experts,) + w.shape[2:])
    ys = _moe_experts(h, flat3(w_gate), flat3(w_up), flat3(w_down), tile_expert, tile_valid, src, tm)
    pos_t = jnp.transpose(pos.reshape(2, T // tm, tm), (1, 0, 2)).reshape(T // tm, 1, 2 * tm)
    return _moe_combine(ys, h, pos_t, jnp.transpose(wts[:2]), g, b, alpha, tm)


TOKEN_TILE = 256
ATTN_TILE = 256
CACHE_TILE = 1024
SB_SUB_TILE = 256


def kernel(x_prompt, x_sample, cache_fox_k, cache_fox_v, cache_fox_logf, cache_sb_k, cache_sb_v, cache_mla_ckv, cache_mla_krope, ab_w_in, ab_b_forget, ab_w_out, mla_w_down, mla_g_q, mla_g_kv, mla_w_uq, mla_w_ukv, mla_w_out, moe_w_group, moe_b_group, moe_w_router, moe_b_router, moe_w_gate, moe_w_up, moe_w_down, ln_g, ln_b):
    B, S, D = x_prompt.shape
    DB, n, _ = x_sample.shape
    P = cache_fox_k.shape[2]
    TP, TS = B * S, DB * n
    depth = ln_g.shape[0]
    n_groups = moe_w_group.shape[-1]
    assert depth == 2 and ab_w_in.shape[0] == 1 and mla_w_down.shape[0] == 1
    assert S % ATTN_TILE == 0 and TP % TOKEN_TILE == 0 and TS % TOKEN_TILE == 0 and TOKEN_TILE % n == 0
    assert P % CACHE_TILE == 0 and P % CHUNK == 0 and n == CHUNK
    alpha = (2 * depth) ** 0.25
    tk = CACHE_TILE

    x = jnp.concatenate([x_prompt.reshape(TP, D), x_sample.reshape(TS, D)], axis=0)
    prompt3 = lambda a: a[:TP].reshape(B, S, -1)
    sample3 = lambda a: a[TP:].reshape(DB, n, -1)

    def ffn(o, w_out, resid, layer):
        wrh, wrl, rb = _prep_router(moe_w_group[layer], moe_b_group[layer], moe_w_router[layer], moe_b_router[layer])
        h, ids, wts = _mix_out(o, w_out.astype(BF16), resid, ln_g[layer, 0][None], ln_b[layer, 0][None],
                               wrh, wrl, rb, alpha, n_groups)
        return _moe_layer(h, ids, wts, moe_w_gate[layer], moe_w_up[layer], moe_w_down[layer],
                          ln_g[layer, 1][None], ln_b[layer, 1][None], alpha, TOKEN_TILE)

    fox_heads = ab_b_forget.shape[1]
    hw = (ab_w_in.shape[2] - fox_heads) // 6
    w_ab, b_forget = _prep_ab_weights(ab_w_in[0], ab_b_forget[0])
    qa, ka, va, qb, kb, vb, ka16, va16, kb16, vb16, lf = _ab_proj(x, w_ab, b_forget, hw)
    cq, ck4 = _fox_cum(lf[:TP], B, S, ATTN_TILE)
    o_fox_p = _fox_prompt(prompt3(qa), prompt3(ka16), prompt3(va16), cq, ck4, ATTN_TILE)
    o_sb_p = _sb_prompt(prompt3(qb), prompt3(kb16), prompt3(vb16), ATTN_TILE)
    lf_s = sample3(lf)[:, :, :fox_heads]
    cq_s, ck_past, ck_new = _fox_sample_cum(lf_s, cache_fox_logf[0], tk)
    cache2 = lambda c: c[0].reshape(DB, P, hw)
    o_fox_s = _fox_sample(sample3(qa), sample3(ka16), sample3(va16), cache2(cache_fox_k), cache2(cache_fox_v),
                          cq_s, ck_past, ck_new, tk)
    o_sb_s = _sb_sample(sample3(qb), sample3(kb16), sample3(vb16), cache2(cache_sb_k), cache2(cache_sb_v),
                        tk, SB_SUB_TILE)
    o = jnp.concatenate([jnp.concatenate([o_fox_p, o_sb_p], axis=-1).reshape(TP, 2 * hw),
                         jnp.concatenate([o_fox_s, o_sb_s], axis=-1).reshape(TS, 2 * hw)], axis=0)
    x = ffn(o, ab_w_out[0], x, 0)

    q_lora, kv_lora = mla_g_q.shape[1], mla_g_kv.shape[1]
    heads = mla_w_uq.shape[2] // (QK_NOPE + QK_ROPE)
    wdn, wq, wqr, wk, wv, wuk_t, wuv = _prep_mla_weights(mla_w_down[0], mla_w_uq[0], mla_w_ukv[0], heads, q_lora, kv_lora)
    tm = TOKEN_TILE
    pos = jnp.concatenate([jnp.arange(S, dtype=jnp.int32), P + jnp.arange(tm, dtype=jnp.int32) % n])
    cos_t, sin_t = _rope_tables(pos)
    blocks_per_seq, prompt_blocks = S // tm, TP // tm
    table_block = lambda i: jnp.where(i < prompt_blocks, i % blocks_per_seq, blocks_per_seq)
    ckv, kr, qcat, kcat, v = _mla_proj(x, wdn, mla_g_q[0][None], mla_g_kv[0][None], wq, wqr, wk, wv,
                                       cos_t, sin_t, table_block, tm)
    o_p = _mla_prompt(prompt3(qcat), prompt3(kcat), prompt3(v), ATTN_TILE, 2)
    qn, qr = _mla_sample_queries(sample3(qcat), heads)
    o_s = _mla_sample(qn, qr, wuk_t, wuv, cache_mla_ckv[0], cache_mla_krope[0],
                      sample3(ckv).astype(BF16), sample3(kr).astype(BF16), n, tk)
    o = jnp.concatenate([o_p.reshape(TP, -1), o_s.reshape(TS, -1)], axis=0)
    x = ffn(o, mla_w_out[0], x, 1)

    hd = lambda a, rows, lead: a[rows].reshape((1,) + lead + (fox_heads, hw // fox_heads))
    pr, sr = slice(0, TP), slice(TP, TP + TS)
    return (x[pr].reshape(B, S, D), x[sr].reshape(DB, n, D),
            hd(ka, pr, (B, S)), hd(va, pr, (B, S)), lf[pr, :fox_heads].reshape(1, B, S, fox_heads),
            hd(kb, pr, (B, S)), hd(vb, pr, (B, S)),
            ckv[pr].reshape(1, B, S, kv_lora), kr[pr].reshape(1, B, S, QK_ROPE),
            hd(ka, sr, (DB, n)), hd(va, sr, (DB, n)), lf[sr, :fox_heads].reshape(1, DB, n, fox_heads),
            hd(kb, sr, (DB, n)), hd(vb, sr, (DB, n)),
            ckv[sr].reshape(1, DB, n, kv_lora), kr[sr].reshape(1, DB, n, QK_ROPE))
```

```python
import functools

import jax
import jax.numpy as jnp
from jax import lax
from jax.experimental import pallas as pl
from jax.experimental.pallas import tpu as pltpu

F32 = jnp.float32
BF16 = jnp.bfloat16
NEG_INF = -1e30
LOG2E = 1.4426950408889634

LANES = 128
HEAD_DIM = 64
PAIR_W = 2 * HEAD_DIM
CHUNK = 64
LN_EPS = 1e-5
RMS_EPS = 1e-6
ROPE_BASE = 10000.0
VMEM_LIMIT = 56 * 1024 * 1024


def _cparams(*sem):
    return pltpu.CompilerParams(dimension_semantics=sem, vmem_limit_bytes=VMEM_LIMIT)


def _dot(a, b):
    return jnp.dot(a, b, preferred_element_type=F32)


def _dot_nt(a, b):
    return lax.dot_general(a, b, (((1,), (1,)), ((), ())), preferred_element_type=F32)


def _split_bf16(x, parts):
    out = []
    r = x
    for _ in range(parts):
        h = r.astype(BF16)
        out.append(h)
        r = r - h.astype(F32)
    return out


def _split_bf16_trunc(x, parts):
    out = []
    r = x
    for _ in range(parts):
        bits = lax.bitcast_convert_type(r, jnp.uint32) & jnp.uint32(0xFFFF0000)
        h = lax.bitcast_convert_type(bits, F32)
        out.append(h.astype(BF16))
        r = r - h
    return out


def _log_sigmoid(x):
    return jnp.minimum(x, 0.0) - jnp.log(1.0 + jnp.exp(-jnp.abs(x)))


def _pick_tile(n, pref, mult=8):
    t = min(pref, n)
    while n % t or t % mult:
        t -= 1
    return t


def _ab_proj_kernel(x_ref, w_ref, bf_ref, qa_ref, ka_ref, va_ref, qb_ref, kb_ref, vb_ref,
                    ka16_ref, va16_ref, kb16_ref, vb16_ref, lf_ref, *, hw, qscale):
    xb = x_ref[...].astype(BF16)

    def seg(j):
        return _dot(xb, w_ref[:, j * hw:(j + 1) * hw])

    qa_ref[...] = (seg(0) * qscale).astype(BF16)
    z = seg(1)
    ka_ref[...] = z
    ka16_ref[...] = z.astype(BF16)
    z = seg(2)
    va_ref[...] = z
    va16_ref[...] = z.astype(BF16)
    qb_ref[...] = (seg(3) * qscale).astype(BF16)
    z = seg(4)
    kb_ref[...] = z
    kb16_ref[...] = z.astype(BF16)
    z = seg(5)
    vb_ref[...] = z
    vb16_ref[...] = z.astype(BF16)
    f = _dot(xb, w_ref[:, 6 * hw:6 * hw + LANES]) + bf_ref[...]
    lf_ref[...] = _log_sigmoid(f)


def _ab_proj(x, w, bf, hw):
    T, D = x.shape
    tm = _pick_tile(T, 256)
    row = lambda w_: pl.BlockSpec((tm, w_), lambda i: (i, 0))
    f32o = jax.ShapeDtypeStruct((T, hw), F32)
    b16o = jax.ShapeDtypeStruct((T, hw), BF16)
    return pl.pallas_call(
        functools.partial(_ab_proj_kernel, hw=hw, qscale=HEAD_DIM ** -0.5 * LOG2E),
        grid=(T // tm,),
        in_specs=[row(D), pl.BlockSpec(w.shape, lambda i: (0, 0)), pl.BlockSpec(bf.shape, lambda i: (0, 0))],
        out_specs=[row(hw)] * 10 + [row(LANES)],
        out_shape=[b16o, f32o, f32o, b16o, f32o, f32o, b16o, b16o, b16o, b16o,
                   jax.ShapeDtypeStruct((T, LANES), F32)],
        compiler_params=_cparams("parallel"),
        name="ab_proj",
    )(x, w, bf)


def _cumsum_kernel(x_ref, o_ref, carry_ref, *, tl):
    @pl.when(pl.program_id(1) == 0)
    def _():
        carry_ref[...] = jnp.zeros_like(carry_ref)

    x = x_ref[0]
    r = lax.broadcasted_iota(jnp.int32, (tl, tl), 0)
    c = lax.broadcasted_iota(jnp.int32, (tl, tl), 1)
    upper = (r <= c).astype(BF16)
    parts = jnp.concatenate(_split_bf16(x, 4), axis=0)
    y = _dot(parts, upper)
    cum = (y[0:8] + y[8:16]) + (y[16:24] + y[24:32]) + carry_ref[:, 0:1]
    o_ref[0] = cum
    carry_ref[...] = jnp.broadcast_to(cum[:, tl - 1:tl], carry_ref.shape)


def _cumsum_rows(x):
    B, H, L = x.shape
    tl = _pick_tile(L, 512, LANES)
    return pl.pallas_call(
        functools.partial(_cumsum_kernel, tl=tl),
        grid=(B, L // tl),
        in_specs=[pl.BlockSpec((1, H, tl), lambda b, j: (b, 0, j))],
        out_specs=pl.BlockSpec((1, H, tl), lambda b, j: (b, 0, j)),
        out_shape=jax.ShapeDtypeStruct((B, H, L), F32),
        scratch_shapes=[pltpu.VMEM((H, LANES), F32)],
        compiler_params=_cparams("parallel", "arbitrary"),
        name="cumsum_rows",
    )(x)


def _stack_pair(qp):
    lo = lax.broadcasted_iota(jnp.int32, qp.shape, 1) < HEAD_DIM
    zero = jnp.zeros_like(qp)
    return jnp.concatenate([jnp.where(lo, qp, zero), jnp.where(lo, zero, qp)], axis=0)


def _unstack_pair(o):
    lo = lax.broadcasted_iota(jnp.int32, o.shape[1:], 1) < HEAD_DIM
    return jnp.where(lo, o[0], o[1])


def _softmax_step(s, vb, carry):
    m, l, acc = carry
    two, tq, tk = s.shape
    m_new = jnp.maximum(m, jnp.max(s, axis=-1, keepdims=True))
    alpha = jnp.exp2(m - m_new)
    p = jnp.exp2(s - m_new)
    l = alpha * l + jnp.sum(p, axis=-1, keepdims=True)
    pv = _dot(p.reshape(two * tq, tk).astype(BF16), vb).reshape(two, tq, vb.shape[-1])
    return m_new, l, alpha * acc + pv


def _log2_sigmoid_pair(z2):
    l1 = jnp.log2(1.0 + jnp.exp2(-jnp.abs(z2)))
    return jnp.minimum(z2, 0.0) - l1, jnp.minimum(-z2, 0.0) - l1


def _softmax_init(tq, width):
    return (jnp.full((2, tq, 1), NEG_INF, F32), jnp.zeros((2, tq, 1), F32), jnp.zeros((2, tq, width), F32))


def _sb_step(z, vb, strict_upper, carry, mask=None):
    run, acc = carry
    log_beta, log_rest = _log2_sigmoid_pair(z)
    if mask is not None:
        log_beta = jnp.where(mask, log_beta, NEG_INF)
        log_rest = jnp.where(mask, log_rest, 0.0)
    hi, lo = _split_bf16(log_rest, 2)
    later = _dot(hi, strict_upper) + _dot(lo, strict_upper)
    a = jnp.exp2(log_beta + later + run)
    acc = acc + _dot(a.astype(BF16), vb)
    run = run + jnp.sum(log_rest, axis=-1, keepdims=True)
    return run, acc


def _strict_upper(tk):
    r = lax.broadcasted_iota(jnp.int32, (tk, tk), 0)
    c = lax.broadcasted_iota(jnp.int32, (tk, tk), 1)
    return (r > c).astype(BF16)


XPAIR_W = 256
VT_ROWS = PAIR_W + 16


def _stack_xpair(qp, extra):
    lane = lax.broadcasted_iota(jnp.int32, qp.shape, 1)
    own = lambda h: jnp.logical_or(
        jnp.logical_and(lane >= h * HEAD_DIM, lane < (h + 1) * HEAD_DIM),
        jnp.logical_and(lane >= PAIR_W + h * extra, lane < PAIR_W + (h + 1) * extra))
    zero = jnp.zeros_like(qp)
    return jnp.concatenate([jnp.where(own(0), qp, zero), jnp.where(own(1), qp, zero)], axis=0)


def _unstack_t(o_t, tq):
    top = lax.broadcasted_iota(jnp.int32, (PAIR_W, tq), 0) < HEAD_DIM
    return jnp.transpose(jnp.where(top, o_t[:, :tq], o_t[:, tq:]))


def _key_query_pos(tk, tq):
    key = lax.broadcasted_iota(jnp.int32, (tk, 2 * tq), 0)
    col = lax.broadcasted_iota(jnp.int32, (tk, 2 * tq), 1)
    return key, jnp.where(col >= tq, col - tq, col)


def _flash_prompt_kernel(q_ref, k_ref, vt_ref, o_ref, *, tq, npairs, extra, chunk):
    tk = tq
    i = pl.program_id(2)
    key, query = _key_query_pos(tk, tq)
    visible = (key // chunk) <= (query // chunk)
    for p in range(npairs):
        lanes = slice(p * XPAIR_W, (p + 1) * XPAIR_W)
        q2 = _stack_xpair(q_ref[0, :, lanes], extra)

        def step(kt, carry, masked):
            m, acc = carry
            rows = pl.ds(pl.multiple_of(kt * tk, tk), tk)
            s_t = _dot_nt(k_ref[0, rows, lanes], q2)
            if masked:
                s_t = jnp.where(visible, s_t, NEG_INF)
            m_new = jnp.maximum(m, jnp.max(s_t, axis=0, keepdims=True))
            p_t = jnp.exp2(s_t - m_new).astype(BF16)
            pv = _dot(vt_ref[0, kt, p * VT_ROWS:(p + 1) * VT_ROWS, :], p_t)
            return m_new, jnp.exp2(m - m_new) * acc + pv

        init = (jnp.full((1, 2 * tq), NEG_INF, F32), jnp.zeros((VT_ROWS, 2 * tq), F32))
        carry = lax.fori_loop(0, i, lambda kt, c: step(kt, c, False), init)
        m, acc = step(i, carry, True)
        o_t = acc[:PAIR_W] / acc[PAIR_W:PAIR_W + 1]
        o_ref[0, :, p * PAIR_W:(p + 1) * PAIR_W] = _unstack_t(o_t, tq).astype(o_ref.dtype)


def _flash_prompt(qx, kx, vt4, tq, pairs_per_step, extra, chunk, name):
    B, S, W = qx.shape
    ngroups = W // (XPAIR_W * pairs_per_step)
    kw = XPAIR_W * pairs_per_step
    nk = vt4.shape[1]
    return pl.pallas_call(
        functools.partial(_flash_prompt_kernel, tq=tq, npairs=pairs_per_step, extra=extra, chunk=chunk),
        grid=(B, ngroups, S // tq),
        in_specs=[pl.BlockSpec((1, tq, kw), lambda b, g, i: (b, i, g)),
                  pl.BlockSpec((1, S, kw), lambda b, g, i: (b, 0, g)),
                  pl.BlockSpec((1, nk, VT_ROWS * pairs_per_step, tq), lambda b, g, i: (b, 0, g, 0))],
        out_specs=pl.BlockSpec((1, tq, PAIR_W * pairs_per_step), lambda b, g, i: (b, i, g)),
        out_shape=jax.ShapeDtypeStruct((B, S, W // XPAIR_W * PAIR_W), BF16),
        compiler_params=_cparams("parallel", "parallel", "arbitrary"),
        name=name,
    )(qx, kx, vt4)


def _transpose_values(v, tk):
    B, S, W = v.shape
    pairs = W // PAIR_W
    vt = jnp.transpose(v.reshape(B, S // tk, tk, pairs, PAIR_W), (0, 1, 3, 4, 2))
    ones = jnp.ones((B, S // tk, pairs, VT_ROWS - PAIR_W, tk), v.dtype)
    return jnp.concatenate([vt, ones], axis=3).reshape(B, S // tk, pairs * VT_ROWS, tk)


def _sb_prompt_kernel(q_ref, k_ref, vt_ref, o_ref, *, tq, npairs):
    tk = tq
    i = pl.program_id(1)
    key, query = _key_query_pos(tk, tq)
    before = key < query
    r = lax.broadcasted_iota(jnp.int32, (tk, tk), 0)
    c = lax.broadcasted_iota(jnp.int32, (tk, tk), 1)
    after = (c > r).astype(BF16)
    for p in range(npairs):
        lanes = slice(p * PAIR_W, (p + 1) * PAIR_W)
        q2 = _stack_pair(q_ref[0, :, lanes])

        def step(kt, carry, masked):
            run, acc = carry
            rows = pl.ds(pl.multiple_of(kt * tk, tk), tk)
            z_t = _dot_nt(k_ref[0, rows, lanes], q2)
            log_beta, log_rest = _log2_sigmoid_pair(z_t)
            if masked:
                log_beta = jnp.where(before, log_beta, NEG_INF)
                log_rest = jnp.where(before, log_rest, 0.0)
            hi, lo = _split_bf16(log_rest, 2)
            later = _dot(after, hi) + _dot(after, lo)
            a_t = jnp.exp2(log_beta + later + run).astype(BF16)
            acc = acc + _dot(vt_ref[0, kt, p * VT_ROWS:p * VT_ROWS + PAIR_W, :], a_t)
            return run + later[0:1] + log_rest[0:1], acc

        carry = (jnp.zeros((1, 2 * tq), F32), jnp.zeros((PAIR_W, 2 * tq), F32))
        carry = step(i, carry, True)
        run, acc = lax.fori_loop(0, i, lambda j, cr: step(i - 1 - j, cr, False), carry)
        o_ref[0, :, lanes] = _unstack_t(acc, tq).astype(o_ref.dtype)


def _sb_prompt(q, k, vt4, tq):
    B, S, W = q.shape
    nk = vt4.shape[1]
    return pl.pallas_call(
        functools.partial(_sb_prompt_kernel, tq=tq, npairs=W // PAIR_W),
        grid=(B, S // tq),
        in_specs=[pl.BlockSpec((1, tq, W), lambda b, i: (b, i, 0)),
                  pl.BlockSpec((1, S, W), lambda b, i: (b, 0, 0)),
                  pl.BlockSpec((1, nk, vt4.shape[2], tq), lambda b, i: (b, 0, 0, 0))],
        out_specs=pl.BlockSpec((1, tq, W), lambda b, i: (b, i, 0)),
        out_shape=jax.ShapeDtypeStruct((B, S, W), BF16),
        compiler_params=_cparams("parallel", "arbitrary"),
        name="sb_prompt",
    )(q, k, vt4)


def _prep_ab_weights(w_in, b_f):
    D = w_in.shape[0]
    H = b_f.shape[0]
    hw = (w_in.shape[1] - H) // 6
    main = jnp.concatenate([w_in[:, :3 * hw], w_in[:, 3 * hw + H:]], axis=1)
    wf = jnp.zeros((D, LANES), w_in.dtype).at[:, :H].set(w_in[:, 3 * hw:3 * hw + H])
    bf = jnp.zeros((1, LANES), F32).at[0, :H].set(b_f.astype(F32))
    return jnp.concatenate([main, wf], axis=1).astype(BF16), bf


def _fox_prompt_operands(q, k, lf, H):
    B, S, W = q.shape
    lt = jnp.transpose(lf[:, :H].reshape(B, S, H), (0, 2, 1))
    cum = jnp.transpose(_cumsum_rows(lt), (0, 2, 1)) * LOG2E
    c = jnp.stack(_split_bf16_trunc(cum, 3), axis=-1)
    one = jnp.ones_like(c)
    pad = jnp.zeros((B, S, H, 2), BF16)
    pairs = H // 2

    def build(a, e):
        gap = jnp.zeros((B, S, pairs, XPAIR_W - PAIR_W - 16), BF16)
        return jnp.concatenate([a.reshape(B, S, pairs, PAIR_W), e.reshape(B, S, pairs, 16), gap],
                               axis=-1).reshape(B, S, pairs * XPAIR_W)

    return (build(q, jnp.concatenate([c, one, pad], axis=-1)),
            build(k, jnp.concatenate([one, -c, pad], axis=-1)))


def _fox_sample_kernel(q_ref, kn_ref, vn_ref, kc_ref, vc_ref, cq_ref, ckp_ref, ckn_ref, o_ref,
                       m_ref, l_ref, acc_ref, *, n, npairs):
    kt = pl.program_id(1)

    @pl.when(kt == 0)
    def _():
        m_ref[...] = jnp.full(m_ref.shape, NEG_INF, F32)
        l_ref[...] = jnp.zeros_like(l_ref)
        acc_ref[...] = jnp.zeros_like(acc_ref)

    kc = kc_ref[0].astype(BF16)
    vc = vc_ref[0].astype(BF16)
    tk = kc.shape[0]

    def pair_inputs(p):
        lanes = slice(p * PAIR_W, (p + 1) * PAIR_W)
        cqp = cq_ref[0, :, 2 * p:2 * p + 2]
        return lanes, _stack_pair(q_ref[0, :, lanes]), jnp.stack([cqp[:, 0:1], cqp[:, 1:2]], axis=0)

    for p in range(npairs):
        lanes, q2, cq3 = pair_inputs(p)
        s = _dot_nt(q2, kc[:, lanes]).reshape(2, n, tk)
        s = s + (cq3 - ckp_ref[0, 0, 2 * p:2 * p + 2, :][:, None, :])
        m, l, acc = _softmax_step(s, vc[:, lanes], (m_ref[p], l_ref[p], acc_ref[p]))
        m_ref[p] = m
        l_ref[p] = l
        acc_ref[p] = acc

    @pl.when(kt == pl.num_programs(1) - 1)
    def _():
        row = lax.broadcasted_iota(jnp.int32, (n, n), 0)
        col = lax.broadcasted_iota(jnp.int32, (n, n), 1)
        causal = (col <= row)[None]
        for p in range(npairs):
            lanes, q2, cq3 = pair_inputs(p)
            s = _dot_nt(q2, kn_ref[0, :, lanes]).reshape(2, n, n)
            s = s + (cq3 - ckn_ref[0, 2 * p:2 * p + 2, :][:, None, :])
            s = jnp.where(causal, s, NEG_INF)
            m, l, acc = _softmax_step(s, vn_ref[0, :, lanes], (m_ref[p], l_ref[p], acc_ref[p]))
            o_ref[0, :, lanes] = _unstack_pair(acc / l).astype(o_ref.dtype)


def _fox_sample(q, kn, vn, kc, vc, cq, ckp, ckn, tk):
    DB, n, W = q.shape
    P = kc.shape[1]
    H = cq.shape[-1]
    npairs = W // PAIR_W
    new = pl.BlockSpec((1, n, W), lambda b, j: (b, 0, 0))
    cache = pl.BlockSpec((1, tk, W), lambda b, j: (b, j, 0))
    return pl.pallas_call(
        functools.partial(_fox_sample_kernel, n=n, npairs=npairs),
        grid=(DB, P // tk),
        in_specs=[new, new, new, cache, cache,
                  pl.BlockSpec((1, n, H), lambda b, j: (b, 0, 0)),
                  pl.BlockSpec((1, 1, H, tk), lambda b, j: (b, j, 0, 0)),
                  pl.BlockSpec((1, H, n), lambda b, j: (b, 0, 0))],
        out_specs=new,
        out_shape=jax.ShapeDtypeStruct((DB, n, W), BF16),
        scratch_shapes=[pltpu.VMEM((npairs, 2, n, 1), F32), pltpu.VMEM((npairs, 2, n, 1), F32),
                        pltpu.VMEM((npairs, 2, n, PAIR_W), F32)],
        compiler_params=_cparams("parallel", "arbitrary"),
        name="fox_sample",
    )(q, kn, vn, kc, vc, cq, ckp, ckn)


def _sb_sample_kernel(q_ref, kn_ref, vn_ref, kc_ref, vc_ref, o_ref, run_ref, acc_ref, *, n, npairs, sub):
    kt = pl.program_id(1)
    upper = _strict_upper(sub)

    @pl.when(kt == 0)
    def _():
        row = lax.broadcasted_iota(jnp.int32, (2 * n, n), 0)
        col = lax.broadcasted_iota(jnp.int32, (2 * n, n), 1)
        before = col < jnp.where(row >= n, row - n, row)
        upper_n = _strict_upper(n)
        for p in range(npairs):
            lanes = slice(p * PAIR_W, (p + 1) * PAIR_W)
            z = _dot_nt(_stack_pair(q_ref[0, :, lanes]), kn_ref[0, :, lanes])
            carry = (jnp.zeros((2 * n, 1), F32), jnp.zeros((2 * n, PAIR_W), F32))
            run, acc = _sb_step(z, vn_ref[0, :, lanes], upper_n, carry, before)
            run_ref[p] = run
            acc_ref[p] = acc

    tk = kc_ref.shape[1]
    for p in range(npairs):
        lanes = slice(p * PAIR_W, (p + 1) * PAIR_W)
        q2 = _stack_pair(q_ref[0, :, lanes])
        carry = (run_ref[p], acc_ref[p])
        for c in reversed(range(tk // sub)):
            rows = slice(c * sub, (c + 1) * sub)
            z = _dot_nt(q2, kc_ref[0, rows, lanes].astype(BF16))
            carry = _sb_step(z, vc_ref[0, rows, lanes].astype(BF16), upper, carry)
        run_ref[p] = carry[0]
        acc_ref[p] = carry[1]

    @pl.when(kt == pl.num_programs(1) - 1)
    def _():
        for p in range(npairs):
            lanes = slice(p * PAIR_W, (p + 1) * PAIR_W)
            o_ref[0, :, lanes] = _unstack_pair(acc_ref[p].reshape(2, n, PAIR_W)).astype(o_ref.dtype)


def _sb_sample(q, kn, vn, kc, vc, tk, sub):
    DB, n, W = q.shape
    P = kc.shape[1]
    nk = P // tk
    npairs = W // PAIR_W
    new = pl.BlockSpec((1, n, W), lambda b, j: (b, 0, 0))
    cache = pl.BlockSpec((1, tk, W), lambda b, j: (b, nk - 1 - j, 0))
    return pl.pallas_call(
        functools.partial(_sb_sample_kernel, n=n, npairs=npairs, sub=sub),
        grid=(DB, nk),
        in_specs=[new, new, new, cache, cache],
        out_specs=new,
        out_shape=jax.ShapeDtypeStruct((DB, n, W), BF16),
        scratch_shapes=[pltpu.VMEM((npairs, 2 * n, 1), F32), pltpu.VMEM((npairs, 2 * n, PAIR_W), F32)],
        compiler_params=_cparams("parallel", "arbitrary"),
        name="sb_sample",
    )(q, kn, vn, kc, vc)


def _fox_sample_cum(lf_new, lf_past, tk):
    DB, P, H = lf_past.shape
    n = lf_new.shape[1]
    L = -(-(P + n) // LANES) * LANES
    both = jnp.concatenate([lf_past.astype(F32), lf_new, jnp.zeros((DB, L - P - n, H), F32)], axis=1)
    cum_t = _cumsum_rows(jnp.transpose(both, (0, 2, 1))) * LOG2E
    ckp = jnp.transpose(cum_t[:, :, :P].reshape(DB, H, P // tk, tk), (0, 2, 1, 3))
    ckn = cum_t[:, :, P:P + n]
    return jnp.transpose(ckn, (0, 2, 1)), ckp, ckn


ROUTER_ROWS = 48


def _layer_norm(y, g, b):
    mu = jnp.mean(y, axis=-1, keepdims=True)
    yc = y - mu
    var = jnp.mean(yc * yc, axis=-1, keepdims=True)
    return yc * lax.rsqrt(var + LN_EPS) * g + b


def _first_argmax(v, ridx):
    vmax = jnp.max(v, axis=0, keepdims=True)
    idx = jnp.min(jnp.where(v == vmax, ridx, v.shape[0]), axis=0, keepdims=True)
    return vmax, idx


def _mix_out_kernel(o_ref, w_ref, x_ref, g_ref, b_ref, wrh_ref, wrl_ref, rb_ref, h_ref, ids_ref, wts_ref,
                    *, alpha, n_groups):
    h = _layer_norm(alpha * x_ref[...] + _dot(o_ref[...], w_ref[...]), g_ref[...], b_ref[...])
    h_ref[...] = h
    hh, hl = _split_bf16(h, 2)
    wrh = wrh_ref[...]
    lg = _dot_nt(wrh, hh) + (_dot_nt(wrh, hl) + _dot_nt(wrl_ref[...], hh)) + rb_ref[...]
    tm = lg.shape[1]
    ridx = lax.broadcasted_iota(jnp.int32, (8, tm), 0)
    g = jnp.where(ridx < n_groups, lg[0:8], NEG_INF)
    gmax, gidx = _first_argmax(g, ridx)
    gate = 1.0 / jnp.sum(jnp.exp(g - gmax), axis=0, keepdims=True)
    esel = lg[8:16]
    for gg in range(1, n_groups):
        esel = jnp.where(gidx == gg, lg[8 + 8 * gg:16 + 8 * gg], esel)
    v1, i1 = _first_argmax(esel, ridx)
    v2, i2 = _first_argmax(jnp.where(ridx == i1, NEG_INF, esel), ridx)
    t = jnp.exp(v2 - v1)
    w1 = 1.0 / (1.0 + t)
    ids_ref[...] = jnp.where(ridx == 0, gidx * 8 + i1, jnp.where(ridx == 1, gidx * 8 + i2, 0))
    wts_ref[...] = jnp.where(ridx == 0, gate * w1, jnp.where(ridx == 1, gate * (t * w1), 0.0))


def _mix_out(o, w, x, g, b, wrh, wrl, rb, alpha, n_groups):
    T, D = x.shape
    tm = _pick_tile(T, 256, LANES)
    const = lambda a: pl.BlockSpec(a.shape, lambda i: (0,) * a.ndim)
    rb_t = jnp.broadcast_to(rb, (ROUTER_ROWS, tm))
    return pl.pallas_call(
        functools.partial(_mix_out_kernel, alpha=alpha, n_groups=n_groups),
        grid=(T // tm,),
        in_specs=[pl.BlockSpec((tm, o.shape[1]), lambda i: (i, 0)), const(w),
                  pl.BlockSpec((tm, D), lambda i: (i, 0)), const(g), const(b), const(wrh), const(wrl), const(rb_t)],
        out_specs=[pl.BlockSpec((tm, D), lambda i: (i, 0)), pl.BlockSpec((8, tm), lambda i: (0, i)),
                   pl.BlockSpec((8, tm), lambda i: (0, i))],
        out_shape=[jax.ShapeDtypeStruct((T, D), F32), jax.ShapeDtypeStruct((8, T), jnp.int32),
                   jax.ShapeDtypeStruct((8, T), F32)],
        compiler_params=_cparams("parallel"),
        name="mix_out_ln_router",
    )(o, w, x, g, b, wrh, wrl, rb_t)


def _prep_router(w_group, b_group, w_router, b_router):
    D, G = w_group.shape
    E = w_router.shape[-1]
    wr = jnp.zeros((ROUTER_ROWS, D), F32)
    wr = wr.at[:G].set(w_group.T.astype(F32))
    wr = wr.at[8:8 + G * E].set(jnp.transpose(w_router, (0, 2, 1)).reshape(G * E, D).astype(F32))
    rb = jnp.zeros((ROUTER_ROWS, 1), F32)
    rb = rb.at[:G, 0].set(b_group.astype(F32)).at[8:8 + G * E, 0].set(b_router.reshape(-1).astype(F32))
    hi, lo = _split_bf16_trunc(wr, 2)
    return hi, lo, rb


def _gather_rows(idx_ref, n, src_hbm, dst, sem):
    def body(r, carry):
        pltpu.make_async_copy(src_hbm.at[pl.ds(idx_ref[0, 0, r], 1)], dst.at[pl.ds(r, 1)], sem).start()
        return carry
    lax.fori_loop(0, n, body, 0, unroll=8)


def _wait_rows(n, src_hbm, dst, sem):
    pltpu.make_async_copy(src_hbm.at[pl.ds(0, n)], dst, sem).wait()


def _moe_experts_kernel(te_ref, tv_ref, src_ref, nxt_ref, x_hbm, wg_ref, wu_ref, wd_ref, y_ref, xbuf, sem, *, tm):
    i = pl.program_id(0)
    nt = pl.num_programs(0)
    slot = i % 2

    @pl.when(jnp.logical_and(i == 0, tv_ref[0] > 0))
    def _():
        _gather_rows(src_ref, tm, x_hbm, xbuf.at[0], sem.at[0])

    @pl.when(jnp.logical_and(i + 1 < nt, tv_ref[jnp.minimum(i + 1, nt - 1)] > 0))
    def _():
        _gather_rows(nxt_ref, tm, x_hbm, xbuf.at[1 - slot], sem.at[1 - slot])

    @pl.when(tv_ref[i] > 0)
    def _():
        _wait_rows(tm, x_hbm, xbuf.at[slot], sem.at[slot])
        xb = xbuf[slot].astype(BF16)
        a = _dot(xb, wg_ref[0].astype(BF16))
        u = _dot(xb, wu_ref[0].astype(BF16))
        hid = (a / (1.0 + jnp.exp(-a))) * u
        y_ref[...] = _dot(hid.astype(BF16), wd_ref[0].astype(BF16))

    @pl.when(tv_ref[i] == 0)
    def _():
        y_ref[...] = jnp.zeros_like(y_ref)


def _moe_experts(x, w_gate, w_up, w_down, tile_expert, tile_valid, src, tm):
    T, D = x.shape
    F = w_gate.shape[-1]
    NT = tile_expert.shape[0]
    grid_spec = pltpu.PrefetchScalarGridSpec(
        num_scalar_prefetch=2,
        grid=(NT,),
        in_specs=[pl.BlockSpec((1, 1, tm), lambda i, te, tv: (i, 0, 0), memory_space=pltpu.SMEM),
                  pl.BlockSpec((1, 1, tm), lambda i, te, tv: (jnp.minimum(i + 1, NT - 1), 0, 0),
                               memory_space=pltpu.SMEM),
                  pl.BlockSpec(memory_space=pl.ANY),
                  pl.BlockSpec((1, D, F), lambda i, te, tv: (te[i], 0, 0)),
                  pl.BlockSpec((1, D, F), lambda i, te, tv: (te[i], 0, 0)),
                  pl.BlockSpec((1, F, D), lambda i, te, tv: (te[i], 0, 0))],
        out_specs=pl.BlockSpec((tm, D), lambda i, te, tv: (i, 0)),
        scratch_shapes=[pltpu.VMEM((2, tm, D), F32), pltpu.SemaphoreType.DMA((2,))],
    )
    return pl.pallas_call(
        functools.partial(_moe_experts_kernel, tm=tm),
        grid_spec=grid_spec,
        out_shape=jax.ShapeDtypeStruct((NT * tm, D), F32),
        compiler_params=_cparams("arbitrary"),
        name="moe_experts",
    )(tile_expert, tile_valid, src, src, x, w_gate, w_up, w_down)


def _moe_combine_kernel(pos_ref, nxt_ref, ys_hbm, h_ref, w_ref, g_ref, b_ref, o_ref, buf, sem, *, tm, alpha):
    i = pl.program_id(0)
    nt = pl.num_programs(0)
    slot = i % 2

    @pl.when(i == 0)
    def _():
        _gather_rows(pos_ref, 2 * tm, ys_hbm, buf.at[0], sem.at[0])

    @pl.when(i + 1 < nt)
    def _():
        _gather_rows(nxt_ref, 2 * tm, ys_hbm, buf.at[1 - slot], sem.at[1 - slot])

    _wait_rows(2 * tm, ys_hbm, buf.at[slot], sem.at[slot])
    w = w_ref[...]
    y = alpha * h_ref[...] + (w[:, 0:1] * buf[slot, 0:tm] + w[:, 1:2] * buf[slot, tm:2 * tm])
    o_ref[...] = _layer_norm(y, g_ref[...], b_ref[...])


def _moe_combine(ys, h, pos, wts, g, b, alpha, tm):
    T, D = h.shape
    nt = T // tm
    const = lambda a: pl.BlockSpec(a.shape, lambda i: (0,) * a.ndim)
    return pl.pallas_call(
        functools.partial(_moe_combine_kernel, tm=tm, alpha=alpha),
        grid=(nt,),
        in_specs=[pl.BlockSpec((1, 1, 2 * tm), lambda i: (i, 0, 0), memory_space=pltpu.SMEM),
                  pl.BlockSpec((1, 1, 2 * tm), lambda i: (jnp.minimum(i + 1, nt - 1), 0, 0),
                               memory_space=pltpu.SMEM),
                  pl.BlockSpec(memory_space=pl.ANY),
                  pl.BlockSpec((tm, D), lambda i: (i, 0)),
                  pl.BlockSpec((tm, 2), lambda i: (i, 0)), const(g), const(b)],
        out_specs=pl.BlockSpec((tm, D), lambda i: (i, 0)),
        out_shape=jax.ShapeDtypeStruct((T, D), F32),
        scratch_shapes=[pltpu.VMEM((2, 2 * tm, D), F32), pltpu.SemaphoreType.DMA((2,))],
        compiler_params=_cparams("arbitrary"),
        name="moe_combine_ln",
    )(pos, pos, ys, h, wts, g, b)


def _route(ids, n_experts, tm):
    T = ids.shape[1]
    flat = ids.reshape(-1)
    order = jnp.argsort(flat, stable=True)
    counts = jnp.zeros((n_experts,), jnp.int32).at[flat].add(1)
    padded = (counts + tm - 1) // tm * tm
    starts = jnp.cumsum(padded) - padded
    raw_starts = jnp.cumsum(counts) - counts
    sorted_e = flat[order]
    dest = starts[sorted_e] + (jnp.arange(2 * T, dtype=jnp.int32) - raw_starts[sorted_e])
    NT = (2 * T + n_experts * (tm - 1)) // tm
    src = jnp.zeros((NT * tm,), jnp.int32).at[dest].set((order % T).astype(jnp.int32))
    pos = jnp.zeros((2 * T,), jnp.int32).at[order].set(dest.astype(jnp.int32)).reshape(2, T)
    ends = jnp.cumsum(padded)
    tile_start = jnp.arange(NT, dtype=jnp.int32) * tm
    tile_expert = jnp.minimum(jnp.searchsorted(ends, tile_start, side="right"), n_experts - 1).astype(jnp.int32)
    tile_valid = (tile_start < ends[-1]).astype(jnp.int32)
    return tile_expert, tile_valid, src.reshape(NT, 1, tm), pos


MLA_PAIR_W = 256
QK_NOPE = 64
QK_ROPE = 32


def _mla_proj_kernel(h_ref, wdn_ref, gq_ref, gkv_ref, wq_ref, wqr_ref, wk_ref, wv_ref, cos_ref, sin_ref,
                     ckv_ref, kr_ref, qcat_ref, kcat_ref, v_ref, *, q_lora, kv_lora, npairs, scale):
    z = _dot(h_ref[...].astype(BF16), wdn_ref[...])
    cq = z[:, :q_lora]
    ckv = z[:, q_lora:q_lora + kv_lora]
    o = q_lora + kv_lora
    kr_raw = z[:, o:o + MLA_PAIR_W]
    kr_rot = z[:, o + MLA_PAIR_W:o + 2 * MLA_PAIR_W]
    cq = cq * lax.rsqrt(jnp.mean(cq * cq, axis=-1, keepdims=True) + RMS_EPS) * gq_ref[...]
    ckv = ckv * lax.rsqrt(jnp.mean(ckv * ckv, axis=-1, keepdims=True) + RMS_EPS) * gkv_ref[...]
    ckv_ref[...] = ckv
    cos = cos_ref[...]
    sin = sin_ref[...]
    kr_tile = kr_raw * cos + kr_rot * sin
    kr_ref[...] = kr_tile[:, 2 * QK_NOPE:2 * QK_NOPE + QK_ROPE]
    cqb = cq.astype(BF16)
    ckb = ckv.astype(BF16)
    for p in range(npairs):
        lanes = slice(p * MLA_PAIR_W, (p + 1) * MLA_PAIR_W)
        q = _dot(cqb, wq_ref[:, lanes]) * cos + _dot(cqb, wqr_ref[:, lanes]) * sin
        qcat_ref[:, lanes] = (q * scale).astype(BF16)
        kcat_ref[:, lanes] = (_dot(ckb, wk_ref[:, lanes]) + kr_tile).astype(BF16)
    v_ref[...] = _dot(ckb, wv_ref[...]).astype(BF16)


def _mla_proj(h, wdn, gq, gkv, wq, wqr, wk, wv, cos_t, sin_t, table_block, tm):
    T, D = h.shape
    q_lora, kv_lora = gq.shape[1], gkv.shape[1]
    npairs = wq.shape[1] // MLA_PAIR_W
    const = lambda a: pl.BlockSpec(a.shape, lambda i: (0,) * a.ndim)
    row = lambda w_: pl.BlockSpec((tm, w_), lambda i: (i, 0))
    table = pl.BlockSpec((tm, MLA_PAIR_W), lambda i: (table_block(i), 0))
    return pl.pallas_call(
        functools.partial(_mla_proj_kernel, q_lora=q_lora, kv_lora=kv_lora, npairs=npairs,
                          scale=(QK_NOPE + QK_ROPE) ** -0.5 * LOG2E),
        grid=(T // tm,),
        in_specs=[row(D), const(wdn), const(gq), const(gkv), const(wq), const(wqr), const(wk), const(wv),
                  table, table],
        out_specs=[row(kv_lora), row(QK_ROPE), row(wq.shape[1]), row(wk.shape[1]), row(wv.shape[1])],
        out_shape=[jax.ShapeDtypeStruct((T, kv_lora), F32), jax.ShapeDtypeStruct((T, QK_ROPE), F32),
                   jax.ShapeDtypeStruct((T, wq.shape[1]), BF16), jax.ShapeDtypeStruct((T, wk.shape[1]), BF16),
                   jax.ShapeDtypeStruct((T, wv.shape[1]), BF16)],
        compiler_params=_cparams("parallel"),
        name="mla_proj",
    )(h, wdn, gq, gkv, wq, wqr, wk, wv, cos_t, sin_t)


def _rot_half(w):
    half = w.shape[-1] // 2
    return jnp.concatenate([-w[..., half:], w[..., :half]], axis=-1)


def _prep_mla_weights(w_down, w_uq, w_ukv, heads, q_lora, kv_lora):
    D = w_down.shape[0]
    hp = heads // 2
    w_kr = w_down[:, q_lora + kv_lora:]
    slot = lambda w: jnp.concatenate([jnp.zeros((D, 2 * QK_NOPE), w.dtype), w, w,
                                      jnp.zeros((D, MLA_PAIR_W - 2 * QK_NOPE - 2 * QK_ROPE), w.dtype)], axis=1)
    wdn = jnp.concatenate([w_down[:, :q_lora + kv_lora], slot(w_kr), slot(_rot_half(w_kr))], axis=1)
    wq4 = w_uq.reshape(q_lora, hp, 2, QK_NOPE + QK_ROPE)
    nope, ropew = wq4[..., :QK_NOPE], wq4[..., QK_NOPE:]
    zpad = jnp.zeros((q_lora, hp, MLA_PAIR_W - 2 * QK_NOPE - 2 * QK_ROPE), w_uq.dtype)
    wq = jnp.concatenate([nope[:, :, 0], nope[:, :, 1], ropew[:, :, 0], ropew[:, :, 1], zpad], axis=-1)
    rot = _rot_half(ropew)
    wqr = jnp.concatenate([jnp.zeros((q_lora, hp, 2 * QK_NOPE), w_uq.dtype), rot[:, :, 0], rot[:, :, 1], zpad],
                          axis=-1)
    wkv4 = w_ukv.reshape(kv_lora, hp, 2, QK_NOPE + HEAD_DIM)
    kn = wkv4[..., :QK_NOPE]
    wk = jnp.concatenate([kn[:, :, 0], kn[:, :, 1], jnp.zeros((kv_lora, hp, MLA_PAIR_W - 2 * QK_NOPE), w_ukv.dtype)],
                         axis=-1)
    wv = wkv4[..., QK_NOPE:].reshape(kv_lora, heads * HEAD_DIM)
    w_uk = w_ukv.reshape(kv_lora, heads, QK_NOPE + HEAD_DIM)[..., :QK_NOPE]
    w_uv = w_ukv.reshape(kv_lora, heads, QK_NOPE + HEAD_DIM)[..., QK_NOPE:]
    b16 = lambda a: a.astype(BF16)
    return (b16(wdn), b16(wq.reshape(q_lora, -1)), b16(wqr.reshape(q_lora, -1)), b16(wk.reshape(kv_lora, -1)),
            b16(wv), b16(jnp.transpose(w_uk, (1, 2, 0))), b16(jnp.transpose(w_uv, (1, 0, 2))))


def _rope_tables(pos):
    half = QK_ROPE // 2
    inv_freq = ROPE_BASE ** (-jnp.arange(half, dtype=F32) / half)
    ang = pos.astype(F32)[:, None] * inv_freq[None, :]
    n = pos.shape[0]
    pad = jnp.zeros((n, MLA_PAIR_W - 2 * QK_NOPE - 2 * QK_ROPE), F32)
    cos = jnp.concatenate([jnp.ones((n, 2 * QK_NOPE), F32)] + [jnp.cos(ang)] * 4 + [pad], axis=1)
    sin = jnp.concatenate([jnp.zeros((n, 2 * QK_NOPE), F32)] + [jnp.sin(ang)] * 4 + [pad], axis=1)
    return cos, sin


def _mla_sample_queries(qcat, heads):
    DB, n, _ = qcat.shape
    q4 = qcat.reshape(DB, n, heads // 2, MLA_PAIR_W)
    nope = q4[..., :2 * QK_NOPE].reshape(DB, n, heads, QK_NOPE)
    ropeq = q4[..., 2 * QK_NOPE:2 * QK_NOPE + 2 * QK_ROPE].reshape(DB, n, heads, QK_ROPE)
    rows = lambda a: jnp.transpose(a, (0, 2, 1, 3)).reshape(DB, heads * n, a.shape[-1])
    return rows(nope), rows(ropeq)


def _mla_sample_kernel(qn_ref, qr_ref, wuk_ref, wuv_ref, cc_ref, rc_ref, cn_ref, rn_ref, o_ref,
                       qlat_ref, m_ref, l_ref, acc_ref, *, n, heads):
    kt = pl.program_id(1)

    @pl.when(kt == 0)
    def _():
        for h in range(heads):
            rows = slice(h * n, (h + 1) * n)
            qlat_ref[rows, :] = _dot(qn_ref[0, rows, :], wuk_ref[h]).astype(BF16)
        m_ref[...] = jnp.full(m_ref.shape, NEG_INF, F32)
        l_ref[...] = jnp.zeros_like(l_ref)
        acc_ref[...] = jnp.zeros_like(acc_ref)

    def update(ckv, kr):
        s = _dot_nt(qlat_ref[...], ckv) + _dot_nt(qr_ref[0], kr)
        m, l, acc = _softmax_step(s[None], ckv, (m_ref[...], l_ref[...], acc_ref[...]))
        m_ref[...] = m
        l_ref[...] = l
        acc_ref[...] = acc

    update(cc_ref[0].astype(BF16), rc_ref[0].astype(BF16))

    @pl.when(kt == pl.num_programs(1) - 1)
    def _():
        update(cn_ref[0], rn_ref[0])
        o_lat = (acc_ref[0] / l_ref[0]).astype(BF16)
        for h in range(heads):
            o_ref[0, :, h * HEAD_DIM:(h + 1) * HEAD_DIM] = _dot(o_lat[h * n:(h + 1) * n], wuv_ref[h]).astype(o_ref.dtype)


def _mla_sample(qn, qr, wuk, wuv, ckv_c, kr_c, ckv_n, kr_n, n, tk):
    DB, R, _ = qn.shape
    heads = R // n
    P, C = ckv_c.shape[1], ckv_c.shape[2]
    const = lambda a: pl.BlockSpec(a.shape, lambda b, j: (0,) * a.ndim)
    per_b = lambda a: pl.BlockSpec((1,) + a.shape[1:], lambda b, j: (b, 0, 0))
    return pl.pallas_call(
        functools.partial(_mla_sample_kernel, n=n, heads=heads),
        grid=(DB, P // tk),
        in_specs=[per_b(qn), per_b(qr), const(wuk), const(wuv),
                  pl.BlockSpec((1, tk, C), lambda b, j: (b, j, 0)),
                  pl.BlockSpec((1, tk, QK_ROPE), lambda b, j: (b, j, 0)),
                  per_b(ckv_n), per_b(kr_n)],
        out_specs=pl.BlockSpec((1, n, heads * HEAD_DIM), lambda b, j: (b, 0, 0)),
        out_shape=jax.ShapeDtypeStruct((DB, n, heads * HEAD_DIM), BF16),
        scratch_shapes=[pltpu.VMEM((R, C), BF16), pltpu.VMEM((1, R, 1), F32), pltpu.VMEM((1, R, 1), F32),
                        pltpu.VMEM((1, R, C), F32)],
        compiler_params=_cparams("parallel", "arbitrary"),
        name="mla_sample",
    )(qn, qr, wuk, wuv, ckv_c, kr_c, ckv_n, kr_n)


def _moe_layer(h, ids, wts, w_gate, w_up, w_down, g, b, alpha, tm):
    T, D = h.shape
    n_experts = w_gate.shape[0] * w_gate.shape[1]
    tile_expert, tile_valid, src, pos = _route(ids[:2], n_experts, tm)
    flat3 = lambda w: w.reshape((n_experts,) + w.shape[2:])
    ys = _moe_experts(h, flat3(w_gate), flat3(w_up), flat3(w_down), tile_expert, tile_valid, src, tm)
    pos_t = jnp.transpose(pos.reshape(2, T // tm, tm), (1, 0, 2)).reshape(T // tm, 1, 2 * tm)
    return _moe_combine(ys, h, pos_t, jnp.transpose(wts[:2]), g, b, alpha, tm)


TOKEN_TILE = 256
ATTN_TILE = 256
CACHE_TILE = 1024
SB_SUB_TILE = 256


def kernel(x_prompt, x_sample, cache_fox_k, cache_fox_v, cache_fox_logf, cache_sb_k, cache_sb_v, cache_mla_ckv, cache_mla_krope, ab_w_in, ab_b_forget, ab_w_out, mla_w_down, mla_g_q, mla_g_kv, mla_w_uq, mla_w_ukv, mla_w_out, moe_w_group, moe_b_group, moe_w_router, moe_b_router, moe_w_gate, moe_w_up, moe_w_down, ln_g, ln_b):
    B, S, D = x_prompt.shape
    DB, n, _ = x_sample.shape
    P = cache_fox_k.shape[2]
    TP, TS = B * S, DB * n
    depth = ln_g.shape[0]
    n_groups = moe_w_group.shape[-1]
    assert depth == 2 and ab_w_in.shape[0] == 1 and mla_w_down.shape[0] == 1
    assert S % ATTN_TILE == 0 and TP % TOKEN_TILE == 0 and TS % TOKEN_TILE == 0 and TOKEN_TILE % n == 0
    assert P % CACHE_TILE == 0 and P % CHUNK == 0 and n == CHUNK
    alpha = (2 * depth) ** 0.25
    tk = CACHE_TILE

    x = jnp.concatenate([x_prompt.reshape(TP, D), x_sample.reshape(TS, D)], axis=0)
    prompt3 = lambda a: a[:TP].reshape(B, S, -1)
    sample3 = lambda a: a[TP:].reshape(DB, n, -1)

    def ffn(o, w_out, resid, layer):
        wrh, wrl, rb = _prep_router(moe_w_group[layer], moe_b_group[layer], moe_w_router[layer], moe_b_router[layer])
        h, ids, wts = _mix_out(o, w_out.astype(BF16), resid, ln_g[layer, 0][None], ln_b[layer, 0][None],
                               wrh, wrl, rb, alpha, n_groups)
        return _moe_layer(h, ids, wts, moe_w_gate[layer], moe_w_up[layer], moe_w_down[layer],
                          ln_g[layer, 1][None], ln_b[layer, 1][None], alpha, TOKEN_TILE)

    fox_heads = ab_b_forget.shape[1]
    hw = (ab_w_in.shape[2] - fox_heads) // 6
    w_ab, b_forget = _prep_ab_weights(ab_w_in[0], ab_b_forget[0])
    qa, ka, va, qb, kb, vb, ka16, va16, kb16, vb16, lf = _ab_proj(x, w_ab, b_forget, hw)
    qx, kx = _fox_prompt_operands(prompt3(qa), prompt3(ka16), lf[:TP], fox_heads)
    o_fox_p = _flash_prompt(qx, kx, _transpose_values(prompt3(va16), ATTN_TILE), ATTN_TILE,
                            fox_heads // 2, 8, 1, "fox_prompt")
    o_sb_p = _sb_prompt(prompt3(qb), prompt3(kb16), _transpose_values(prompt3(vb16), ATTN_TILE), ATTN_TILE)
    lf_s = sample3(lf)[:, :, :fox_heads]
    cq_s, ck_past, ck_new = _fox_sample_cum(lf_s, cache_fox_logf[0], tk)
    cache2 = lambda c: c[0].reshape(DB, P, hw)
    o_fox_s = _fox_sample(sample3(qa), sample3(ka16), sample3(va16), cache2(cache_fox_k), cache2(cache_fox_v),
                          cq_s, ck_past, ck_new, tk)
    o_sb_s = _sb_sample(sample3(qb), sample3(kb16), sample3(vb16), cache2(cache_sb_k), cache2(cache_sb_v),
                        tk, SB_SUB_TILE)
    o = jnp.concatenate([jnp.concatenate([o_fox_p, o_sb_p], axis=-1).reshape(TP, 2 * hw),
                         jnp.concatenate([o_fox_s, o_sb_s], axis=-1).reshape(TS, 2 * hw)], axis=0)
    x = ffn(o, ab_w_out[0], x, 0)

    q_lora, kv_lora = mla_g_q.shape[1], mla_g_kv.shape[1]
    heads = mla_w_uq.shape[2] // (QK_NOPE + QK_ROPE)
    wdn, wq, wqr, wk, wv, wuk_t, wuv = _prep_mla_weights(mla_w_down[0], mla_w_uq[0], mla_w_ukv[0], heads, q_lora, kv_lora)
    tm = TOKEN_TILE
    pos = jnp.concatenate([jnp.arange(S, dtype=jnp.int32), P + jnp.arange(tm, dtype=jnp.int32) % n])
    cos_t, sin_t = _rope_tables(pos)
    blocks_per_seq, prompt_blocks = S // tm, TP // tm
    table_block = lambda i: jnp.where(i < prompt_blocks, i % blocks_per_seq, blocks_per_seq)
    ckv, kr, qcat, kcat, v = _mla_proj(x, wdn, mla_g_q[0][None], mla_g_kv[0][None], wq, wqr, wk, wv,
                                       cos_t, sin_t, table_block, tm)
    o_p = _flash_prompt(prompt3(qcat), prompt3(kcat), _transpose_values(prompt3(v), ATTN_TILE), ATTN_TILE,
                        4, QK_ROPE, CHUNK, "mla_prompt")
    qn, qr = _mla_sample_queries(sample3(qcat), heads)
    o_s = _mla_sample(qn, qr, wuk_t, wuv, cache_mla_ckv[0], cache_mla_krope[0],
                      sample3(ckv).astype(BF16), sample3(kr).astype(BF16), n, tk)
    o = jnp.concatenate([o_p.reshape(TP, -1), o_s.reshape(TS, -1)], axis=0)
    x = ffn(o, mla_w_out[0], x, 1)

    hd = lambda a, rows, lead: a[rows].reshape((1,) + lead + (fox_heads, hw // fox_heads))
    pr, sr = slice(0, TP), slice(TP, TP + TS)
    return (x[pr].reshape(B, S, D), x[sr].reshape(DB, n, D),
            hd(ka, pr, (B, S)), hd(va, pr, (B, S)), lf[pr, :fox_heads].reshape(1, B, S, fox_heads),
            hd(kb, pr, (B, S)), hd(vb, pr, (B, S)),
            ckv[pr].reshape(1, B, S, kv_lora), kr[pr].reshape(1, B, S, QK_ROPE),
            hd(ka, sr, (DB, n)), hd(va, sr, (DB, n)), lf[sr, :fox_heads].reshape(1, DB, n, fox_heads),
            hd(kb, sr, (DB, n)), hd(vb, sr, (DB, n)),
            ckv[sr].reshape(1, DB, n, kv_lora), kr[sr].reshape(1, DB, n, QK_ROPE))
```

```python
import functools

import jax
import jax.numpy as jnp
from jax import lax
from jax.experimental import pallas as pl
from jax.experimental.pallas import tpu as pltpu

F32 = jnp.float32
BF16 = jnp.bfloat16
NEG_INF = -1e30
LOG2E = 1.4426950408889634

LANES = 128
HEAD_DIM = 64
PAIR_W = 2 * HEAD_DIM
CHUNK = 64
LN_EPS = 1e-5
RMS_EPS = 1e-6
ROPE_BASE = 10000.0
VMEM_LIMIT = 56 * 1024 * 1024


def _cparams(*sem):
    return pltpu.CompilerParams(dimension_semantics=sem, vmem_limit_bytes=VMEM_LIMIT)


def _dot(a, b):
    return jnp.dot(a, b, preferred_element_type=F32)


def _dot_nt(a, b):
    return lax.dot_general(a, b, (((1,), (1,)), ((), ())), preferred_element_type=F32)


def _split_bf16(x, parts):
    out = []
    r = x
    for _ in range(parts):
        h = r.astype(BF16)
        out.append(h)
        r = r - h.astype(F32)
    return out


def _split_bf16_trunc(x, parts):
    out = []
    r = x
    for _ in range(parts):
        bits = lax.bitcast_convert_type(r, jnp.uint32) & jnp.uint32(0xFFFF0000)
        h = lax.bitcast_convert_type(bits, F32)
        out.append(h.astype(BF16))
        r = r - h
    return out


def _log_sigmoid(x):
    return jnp.minimum(x, 0.0) - jnp.log(1.0 + jnp.exp(-jnp.abs(x)))


def _pick_tile(n, pref, mult=8):
    t = min(pref, n)
    while n % t or t % mult:
        t -= 1
    return t


def _ab_proj_kernel(x_ref, w_ref, bf_ref, qa_ref, ka_ref, va_ref, qb_ref, kb_ref, vb_ref,
                    ka16_ref, va16_ref, kb16_ref, vb16_ref, lf_ref, *, hw, qscale):
    xb = x_ref[...].astype(BF16)

    def seg(j):
        return _dot(xb, w_ref[:, j * hw:(j + 1) * hw])

    qa_ref[...] = (seg(0) * qscale).astype(BF16)
    z = seg(1)
    ka_ref[...] = z
    ka16_ref[...] = z.astype(BF16)
    z = seg(2)
    va_ref[...] = z
    va16_ref[...] = z.astype(BF16)
    qb_ref[...] = (seg(3) * qscale).astype(BF16)
    z = seg(4)
    kb_ref[...] = z
    kb16_ref[...] = z.astype(BF16)
    z = seg(5)
    vb_ref[...] = z
    vb16_ref[...] = z.astype(BF16)
    f = _dot(xb, w_ref[:, 6 * hw:6 * hw + LANES]) + bf_ref[...]
    lf_ref[...] = _log_sigmoid(f)


def _ab_proj(x, w, bf, hw):
    T, D = x.shape
    tm = _pick_tile(T, 256)
    row = lambda w_: pl.BlockSpec((tm, w_), lambda i: (i, 0))
    f32o = jax.ShapeDtypeStruct((T, hw), F32)
    b16o = jax.ShapeDtypeStruct((T, hw), BF16)
    return pl.pallas_call(
        functools.partial(_ab_proj_kernel, hw=hw, qscale=HEAD_DIM ** -0.5 * LOG2E),
        grid=(T // tm,),
        in_specs=[row(D), pl.BlockSpec(w.shape, lambda i: (0, 0)), pl.BlockSpec(bf.shape, lambda i: (0, 0))],
        out_specs=[row(hw)] * 10 + [row(LANES)],
        out_shape=[b16o, f32o, f32o, b16o, f32o, f32o, b16o, b16o, b16o, b16o,
                   jax.ShapeDtypeStruct((T, LANES), F32)],
        compiler_params=_cparams("parallel"),
        name="ab_proj",
    )(x, w, bf)


def _cumsum_kernel(x_ref, o_ref, carry_ref, *, tl):
    @pl.when(pl.program_id(1) == 0)
    def _():
        carry_ref[...] = jnp.zeros_like(carry_ref)

    x = x_ref[0]
    r = lax.broadcasted_iota(jnp.int32, (tl, tl), 0)
    c = lax.broadcasted_iota(jnp.int32, (tl, tl), 1)
    upper = (r <= c).astype(BF16)
    parts = jnp.concatenate(_split_bf16(x, 4), axis=0)
    y = _dot(parts, upper)
    cum = (y[0:8] + y[8:16]) + (y[16:24] + y[24:32]) + carry_ref[:, 0:1]
    o_ref[0] = cum
    carry_ref[...] = jnp.broadcast_to(cum[:, tl - 1:tl], carry_ref.shape)


def _cumsum_rows(x):
    B, H, L = x.shape
    tl = _pick_tile(L, 512, LANES)
    return pl.pallas_call(
        functools.partial(_cumsum_kernel, tl=tl),
        grid=(B, L // tl),
        in_specs=[pl.BlockSpec((1, H, tl), lambda b, j: (b, 0, j))],
        out_specs=pl.BlockSpec((1, H, tl), lambda b, j: (b, 0, j)),
        out_shape=jax.ShapeDtypeStruct((B, H, L), F32),
        scratch_shapes=[pltpu.VMEM((H, LANES), F32)],
        compiler_params=_cparams("parallel", "arbitrary"),
        name="cumsum_rows",
    )(x)


def _stack_pair(qp):
    lo = lax.broadcasted_iota(jnp.int32, qp.shape, 1) < HEAD_DIM
    zero = jnp.zeros_like(qp)
    return jnp.concatenate([jnp.where(lo, qp, zero), jnp.where(lo, zero, qp)], axis=0)


def _unstack_pair(o):
    lo = lax.broadcasted_iota(jnp.int32, o.shape[1:], 1) < HEAD_DIM
    return jnp.where(lo, o[0], o[1])


def _softmax_step(s, vb, carry):
    m, l, acc = carry
    two, tq, tk = s.shape
    m_new = jnp.maximum(m, jnp.max(s, axis=-1, keepdims=True))
    alpha = jnp.exp2(m - m_new)
    p = jnp.exp2(s - m_new)
    l = alpha * l + jnp.sum(p, axis=-1, keepdims=True)
    pv = _dot(p.reshape(two * tq, tk).astype(BF16), vb).reshape(two, tq, vb.shape[-1])
    return m_new, l, alpha * acc + pv


def _log2_sigmoid_pair(z2):
    l1 = jnp.log2(1.0 + jnp.exp2(-jnp.abs(z2)))
    return jnp.minimum(z2, 0.0) - l1, jnp.minimum(-z2, 0.0) - l1


def _softmax_init(tq, width):
    return (jnp.full((2, tq, 1), NEG_INF, F32), jnp.zeros((2, tq, 1), F32), jnp.zeros((2, tq, width), F32))


def _sb_step(z, vb, strict_upper, carry, mask=None):
    run, acc = carry
    log_beta, log_rest = _log2_sigmoid_pair(z)
    if mask is not None:
        log_beta = jnp.where(mask, log_beta, NEG_INF)
        log_rest = jnp.where(mask, log_rest, 0.0)
    hi, lo = _split_bf16(log_rest, 2)
    later = _dot(hi, strict_upper) + _dot(lo, strict_upper)
    a = jnp.exp2(log_beta + later + run)
    acc = acc + _dot(a.astype(BF16), vb)
    run = run + jnp.sum(log_rest, axis=-1, keepdims=True)
    return run, acc


def _strict_upper(tk):
    r = lax.broadcasted_iota(jnp.int32, (tk, tk), 0)
    c = lax.broadcasted_iota(jnp.int32, (tk, tk), 1)
    return (r > c).astype(BF16)


HEAD_W = LANES
VT_ROWS = HEAD_DIM + 16
SB_DEAD_LOG2 = -160.0


def _pipeline_ahead(stage, first, count, cur, nxt):
    if first < count:
        stage(first, cur)
    elif nxt is not None:
        stage(first - count, nxt)


def _flash_prompt_kernel(q_ref, k_ref, vt_ref, o_ref, s_ref, m_ref, acc_ref, *, tq, tk, heads, chunk):
    i = pl.program_id(2)
    sub = tq // tk
    key = lax.broadcasted_iota(jnp.int32, (tk, tq), 0)
    query = i * tq + lax.broadcasted_iota(jnp.int32, (tk, tq), 1)
    m_ref[...] = jnp.full(m_ref.shape, NEG_INF, F32)
    acc_ref[...] = jnp.zeros_like(acc_ref)

    def scores(h, kt):
        rows = pl.ds(pl.multiple_of(kt * tk, tk), tk)
        lanes = slice(h * HEAD_W, (h + 1) * HEAD_W)
        s_ref[h] = _dot_nt(k_ref[0, rows, lanes], q_ref[0, :, lanes])

    def absorb(h, kt, masked):
        s_t = s_ref[h]
        if masked:
            s_t = jnp.where(((kt * tk + key) // chunk) <= (query // chunk), s_t, NEG_INF)
        m = m_ref[h]
        m_new = jnp.maximum(m, jnp.max(s_t, axis=0, keepdims=True))
        p_t = jnp.exp2(s_t - m_new).astype(BF16)
        pv = _dot(vt_ref[0, kt, h * VT_ROWS:(h + 1) * VT_ROWS, :], p_t)
        acc_ref[h] = jnp.exp2(m - m_new) * acc_ref[h] + pv
        m_ref[h] = m_new

    scores(0, 0)
    scores(1, 0)

    def body(kt, carry):
        for h in range(heads):
            _pipeline_ahead(scores, h + 2, heads, kt, kt + 1)
            absorb(h, kt, False)
        return carry

    lax.fori_loop(0, sub * i, body, 0)
    for s in range(sub):
        kt = sub * i + s
        for h in range(heads):
            _pipeline_ahead(scores, h + 2, heads, kt, kt + 1 if s + 1 < sub else None)
            absorb(h, kt, True)
    for j in range(heads // 2):
        a0, a1 = acc_ref[2 * j], acc_ref[2 * j + 1]
        o_t = jnp.concatenate([a0[:HEAD_DIM] / a0[HEAD_DIM:HEAD_DIM + 1],
                               a1[:HEAD_DIM] / a1[HEAD_DIM:HEAD_DIM + 1]], axis=0)
        o_ref[0, :, j * PAIR_W:(j + 1) * PAIR_W] = jnp.transpose(o_t).astype(o_ref.dtype)


def _flash_prompt(qx, kx, vt4, tq, heads_per_step, chunk, name):
    B, S, W = qx.shape
    hs = heads_per_step
    nk, tk = vt4.shape[1], vt4.shape[3]
    return pl.pallas_call(
        functools.partial(_flash_prompt_kernel, tq=tq, tk=tk, heads=hs, chunk=chunk),
        grid=(B, W // (HEAD_W * hs), S // tq),
        in_specs=[pl.BlockSpec((1, tq, HEAD_W * hs), lambda b, g, i: (b, i, g)),
                  pl.BlockSpec((1, S, HEAD_W * hs), lambda b, g, i: (b, 0, g)),
                  pl.BlockSpec((1, nk, VT_ROWS * hs, tk), lambda b, g, i: (b, 0, g, 0))],
        out_specs=pl.BlockSpec((1, tq, HEAD_DIM * hs), lambda b, g, i: (b, i, g)),
        out_shape=jax.ShapeDtypeStruct((B, S, W // HEAD_W * HEAD_DIM), BF16),
        scratch_shapes=[pltpu.VMEM((hs, tk, tq), F32), pltpu.VMEM((hs, 1, tq), F32),
                        pltpu.VMEM((hs, VT_ROWS, tq), F32)],
        compiler_params=_cparams("parallel", "parallel", "arbitrary"),
        name=name,
    )(qx, kx, vt4)


def _transpose_values(v, tk):
    B, S, W = v.shape
    heads = W // HEAD_DIM
    vt = jnp.transpose(v.reshape(B, S // tk, tk, heads, HEAD_DIM), (0, 1, 3, 4, 2))
    ones = jnp.ones((B, S // tk, heads, VT_ROWS - HEAD_DIM, tk), v.dtype)
    return jnp.concatenate([vt, ones], axis=3).reshape(B, S // tk, heads * VT_ROWS, tk)


def _pad_heads(a, extras=()):
    B, S, W = a.shape
    heads = W // HEAD_DIM
    parts = [a.reshape(B, S, heads, HEAD_DIM)] + list(extras)
    used = sum(p.shape[-1] for p in parts)
    parts.append(jnp.zeros((B, S, heads, HEAD_W - used), a.dtype))
    return jnp.concatenate(parts, axis=-1).reshape(B, S, heads * HEAD_W)


def _sb_prompt_kernel(q_ref, k_ref, vt_ref, o_ref, z_ref, lw_ref, tot_ref, run_ref, acc_ref, *, tq, tk, heads):
    i = pl.program_id(1)
    sub = tq // tk
    key = lax.broadcasted_iota(jnp.int32, (tk, tq), 0)
    query = lax.broadcasted_iota(jnp.int32, (tk, tq), 1)
    r = lax.broadcasted_iota(jnp.int32, (tk, tk), 0)
    c = lax.broadcasted_iota(jnp.int32, (tk, tk), 1)
    after = (c > r).astype(BF16)
    run_ref[...] = jnp.zeros_like(run_ref)
    acc_ref[...] = jnp.zeros_like(acc_ref)

    def logits(h, kt):
        rows = pl.ds(pl.multiple_of(kt * tk, tk), tk)
        lanes = slice(h * HEAD_W, (h + 1) * HEAD_W)
        z_ref[h] = _dot_nt(k_ref[0, rows, lanes], q_ref[0, :, lanes])

    def log_weights(masked, h, kt):
        log_beta, log_rest = _log2_sigmoid_pair(z_ref[h])
        if masked:
            before = (kt * tk + key) < (i * tq + query)
            log_beta = jnp.where(before, log_beta, NEG_INF)
            log_rest = jnp.where(before, log_rest, 0.0)
        hi, lo = _split_bf16(log_rest, 2)
        later = _dot(after, hi) + _dot(after, lo)
        lw_ref[h] = log_beta + later
        tot_ref[h] = later[0:1] + log_rest[0:1]

    def accumulate(h, kt):
        run = run_ref[h]
        a_t = jnp.exp2(lw_ref[h] + run).astype(BF16)
        acc_ref[h] += _dot(vt_ref[0, kt, h * VT_ROWS:h * VT_ROWS + HEAD_DIM, :], a_t)
        run_ref[h] = run + tot_ref[h]

    def step(kt, masked, nxt, nxt_masked):
        for h in range(heads):
            _pipeline_ahead(logits, h + 2, heads, kt, nxt)
            if h + 1 < heads:
                log_weights(masked, h + 1, kt)
            elif nxt is not None:
                log_weights(nxt_masked, 0, nxt)
            accumulate(h, kt)

    unmasked = sub * i
    first = unmasked + sub - 1
    logits(0, first)
    logits(1, first)
    log_weights(True, 0, first)
    for s in range(sub):
        kt = first - s
        if s + 1 < sub:
            step(kt, True, kt - 1, True)
        else:
            step(kt, True, jnp.maximum(kt - 1, 0), False)

    def alive():
        return (jnp.max(run_ref[...]) > SB_DEAD_LOG2).astype(jnp.int32)

    def body(carry):
        kt, _ = carry
        step(kt, False, jnp.maximum(kt - 1, 0), False)
        return kt - 1, alive()

    lax.while_loop(lambda c: jnp.logical_and(c[0] >= 0, c[1] > 0), body, (unmasked - 1, alive()))

    for j in range(heads // 2):
        o_t = jnp.concatenate([acc_ref[2 * j], acc_ref[2 * j + 1]], axis=0)
        o_ref[0, :, j * PAIR_W:(j + 1) * PAIR_W] = jnp.transpose(o_t).astype(o_ref.dtype)


def _sb_prompt(qx, kx, vt4, tq):
    B, S, W = qx.shape
    heads = W // HEAD_W
    nk, tk = vt4.shape[1], vt4.shape[3]
    return pl.pallas_call(
        functools.partial(_sb_prompt_kernel, tq=tq, tk=tk, heads=heads),
        grid=(B, S // tq),
        in_specs=[pl.BlockSpec((1, tq, W), lambda b, i: (b, i, 0)),
                  pl.BlockSpec((1, S, W), lambda b, i: (b, 0, 0)),
                  pl.BlockSpec((1, nk, vt4.shape[2], tk), lambda b, i: (b, 0, 0, 0))],
        out_specs=pl.BlockSpec((1, tq, heads * HEAD_DIM), lambda b, i: (b, i, 0)),
        out_shape=jax.ShapeDtypeStruct((B, S, heads * HEAD_DIM), BF16),
        scratch_shapes=[pltpu.VMEM((heads, tk, tq), F32), pltpu.VMEM((heads, tk, tq), F32),
                        pltpu.VMEM((heads, 1, tq), F32), pltpu.VMEM((heads, 1, tq), F32),
                        pltpu.VMEM((heads, HEAD_DIM, tq), F32)],
        compiler_params=_cparams("parallel", "arbitrary"),
        name="sb_prompt",
    )(qx, kx, vt4)


def _prep_ab_weights(w_in, b_f):
    D = w_in.shape[0]
    H = b_f.shape[0]
    hw = (w_in.shape[1] - H) // 6
    main = jnp.concatenate([w_in[:, :3 * hw], w_in[:, 3 * hw + H:]], axis=1)
    wf = jnp.zeros((D, LANES), w_in.dtype).at[:, :H].set(w_in[:, 3 * hw:3 * hw + H])
    bf = jnp.zeros((1, LANES), F32).at[0, :H].set(b_f.astype(F32))
    return jnp.concatenate([main, wf], axis=1).astype(BF16), bf


def _fox_prompt_operands(q, k, lf, H):
    B, S, W = q.shape
    lt = jnp.transpose(lf[:, :H].reshape(B, S, H), (0, 2, 1))
    cum = jnp.transpose(_cumsum_rows(lt), (0, 2, 1)) * LOG2E
    c = jnp.stack(_split_bf16_trunc(cum, 3), axis=-1)
    one = jnp.ones_like(c)
    return _pad_heads(q, [c, one]), _pad_heads(k, [one, -c])


def _fox_sample_kernel(q_ref, kn_ref, vn_ref, kc_ref, vc_ref, cq_ref, ckp_ref, ckn_ref, o_ref,
                       m_ref, l_ref, acc_ref, *, n, npairs):
    kt = pl.program_id(1)

    @pl.when(kt == 0)
    def _():
        m_ref[...] = jnp.full(m_ref.shape, NEG_INF, F32)
        l_ref[...] = jnp.zeros_like(l_ref)
        acc_ref[...] = jnp.zeros_like(acc_ref)

    kc = kc_ref[0].astype(BF16)
    vc = vc_ref[0].astype(BF16)
    tk = kc.shape[0]

    def pair_inputs(p):
        lanes = slice(p * PAIR_W, (p + 1) * PAIR_W)
        cqp = cq_ref[0, :, 2 * p:2 * p + 2]
        return lanes, _stack_pair(q_ref[0, :, lanes]), jnp.stack([cqp[:, 0:1], cqp[:, 1:2]], axis=0)

    for p in range(npairs):
        lanes, q2, cq3 = pair_inputs(p)
        s = _dot_nt(q2, kc[:, lanes]).reshape(2, n, tk)
        s = s + (cq3 - ckp_ref[0, 0, 2 * p:2 * p + 2, :][:, None, :])
        m, l, acc = _softmax_step(s, vc[:, lanes], (m_ref[p], l_ref[p], acc_ref[p]))
        m_ref[p] = m
        l_ref[p] = l
        acc_ref[p] = acc

    @pl.when(kt == pl.num_programs(1) - 1)
    def _():
        row = lax.broadcasted_iota(jnp.int32, (n, n), 0)
        col = lax.broadcasted_iota(jnp.int32, (n, n), 1)
        causal = (col <= row)[None]
        for p in range(npairs):
            lanes, q2, cq3 = pair_inputs(p)
            s = _dot_nt(q2, kn_ref[0, :, lanes]).reshape(2, n, n)
            s = s + (cq3 - ckn_ref[0, 2 * p:2 * p + 2, :][:, None, :])
            s = jnp.where(causal, s, NEG_INF)
            m, l, acc = _softmax_step(s, vn_ref[0, :, lanes], (m_ref[p], l_ref[p], acc_ref[p]))
            o_ref[0, :, lanes] = _unstack_pair(acc / l).astype(o_ref.dtype)


def _fox_sample(q, kn, vn, kc, vc, cq, ckp, ckn, tk):
    DB, n, W = q.shape
    P = kc.shape[1]
    H = cq.shape[-1]
    npairs = W // PAIR_W
    new = pl.BlockSpec((1, n, W), lambda b, j: (b, 0, 0))
    cache = pl.BlockSpec((1, tk, W), lambda b, j: (b, j, 0))
    return pl.pallas_call(
        functools.partial(_fox_sample_kernel, n=n, npairs=npairs),
        grid=(DB, P // tk),
        in_specs=[new, new, new, cache, cache,
                  pl.BlockSpec((1, n, H), lambda b, j: (b, 0, 0)),
                  pl.BlockSpec((1, 1, H, tk), lambda b, j: (b, j, 0, 0)),
                  pl.BlockSpec((1, H, n), lambda b, j: (b, 0, 0))],
        out_specs=new,
        out_shape=jax.ShapeDtypeStruct((DB, n, W), BF16),
        scratch_shapes=[pltpu.VMEM((npairs, 2, n, 1), F32), pltpu.VMEM((npairs, 2, n, 1), F32),
                        pltpu.VMEM((npairs, 2, n, PAIR_W), F32)],
        compiler_params=_cparams("parallel", "arbitrary"),
        name="fox_sample",
    )(q, kn, vn, kc, vc, cq, ckp, ckn)


def _sb_sample_kernel(q_ref, kn_ref, vn_ref, kc_ref, vc_ref, o_ref, run_ref, acc_ref, *, n, npairs, sub):
    kt = pl.program_id(1)
    upper = _strict_upper(sub)

    @pl.when(kt == 0)
    def _():
        row = lax.broadcasted_iota(jnp.int32, (2 * n, n), 0)
        col = lax.broadcasted_iota(jnp.int32, (2 * n, n), 1)
        before = col < jnp.where(row >= n, row - n, row)
        upper_n = _strict_upper(n)
        for p in range(npairs):
            lanes = slice(p * PAIR_W, (p + 1) * PAIR_W)
            z = _dot_nt(_stack_pair(q_ref[0, :, lanes]), kn_ref[0, :, lanes])
            carry = (jnp.zeros((2 * n, 1), F32), jnp.zeros((2 * n, PAIR_W), F32))
            run, acc = _sb_step(z, vn_ref[0, :, lanes], upper_n, carry, before)
            run_ref[p] = run
            acc_ref[p] = acc

    tk = kc_ref.shape[1]
    for p in range(npairs):
        lanes = slice(p * PAIR_W, (p + 1) * PAIR_W)
        q2 = _stack_pair(q_ref[0, :, lanes])
        carry = (run_ref[p], acc_ref[p])
        for c in reversed(range(tk // sub)):
            rows = slice(c * sub, (c + 1) * sub)
            z = _dot_nt(q2, kc_ref[0, rows, lanes].astype(BF16))
            carry = _sb_step(z, vc_ref[0, rows, lanes].astype(BF16), upper, carry)
        run_ref[p] = carry[0]
        acc_ref[p] = carry[1]

    @pl.when(kt == pl.num_programs(1) - 1)
    def _():
        for p in range(npairs):
            lanes = slice(p * PAIR_W, (p + 1) * PAIR_W)
            o_ref[0, :, lanes] = _unstack_pair(acc_ref[p].reshape(2, n, PAIR_W)).astype(o_ref.dtype)


def _sb_sample(q, kn, vn, kc, vc, tk, sub):
    DB, n, W = q.shape
    P = kc.shape[1]
    nk = P // tk
    npairs = W // PAIR_W
    new = pl.BlockSpec((1, n, W), lambda b, j: (b, 0, 0))
    cache = pl.BlockSpec((1, tk, W), lambda b, j: (b, nk - 1 - j, 0))
    return pl.pallas_call(
        functools.partial(_sb_sample_kernel, n=n, npairs=npairs, sub=sub),
        grid=(DB, nk),
        in_specs=[new, new, new, cache, cache],
        out_specs=new,
        out_shape=jax.ShapeDtypeStruct((DB, n, W), BF16),
        scratch_shapes=[pltpu.VMEM((npairs, 2 * n, 1), F32), pltpu.VMEM((npairs, 2 * n, PAIR_W), F32)],
        compiler_params=_cparams("parallel", "arbitrary"),
        name="sb_sample",
    )(q, kn, vn, kc, vc)


def _fox_sample_cum(lf_new, lf_past, tk):
    DB, P, H = lf_past.shape
    n = lf_new.shape[1]
    L = -(-(P + n) // LANES) * LANES
    both = jnp.concatenate([lf_past.astype(F32), lf_new, jnp.zeros((DB, L - P - n, H), F32)], axis=1)
    cum_t = _cumsum_rows(jnp.transpose(both, (0, 2, 1))) * LOG2E
    ckp = jnp.transpose(cum_t[:, :, :P].reshape(DB, H, P // tk, tk), (0, 2, 1, 3))
    ckn = cum_t[:, :, P:P + n]
    return jnp.transpose(ckn, (0, 2, 1)), ckp, ckn


ROUTER_ROWS = 48


def _layer_norm(y, g, b):
    mu = jnp.mean(y, axis=-1, keepdims=True)
    yc = y - mu
    var = jnp.mean(yc * yc, axis=-1, keepdims=True)
    return yc * lax.rsqrt(var + LN_EPS) * g + b


def _first_argmax(v, ridx):
    vmax = jnp.max(v, axis=0, keepdims=True)
    idx = jnp.min(jnp.where(v == vmax, ridx, v.shape[0]), axis=0, keepdims=True)
    return vmax, idx


def _mix_out_kernel(o_ref, w_ref, x_ref, g_ref, b_ref, wrh_ref, wrl_ref, rb_ref, h_ref, ids_ref, wts_ref,
                    *, alpha, n_groups):
    h = _layer_norm(alpha * x_ref[...] + _dot(o_ref[...], w_ref[...]), g_ref[...], b_ref[...])
    h_ref[...] = h
    hh, hl = _split_bf16(h, 2)
    wrh = wrh_ref[...]
    lg = _dot_nt(wrh, hh) + (_dot_nt(wrh, hl) + _dot_nt(wrl_ref[...], hh)) + rb_ref[...]
    tm = lg.shape[1]
    ridx = lax.broadcasted_iota(jnp.int32, (8, tm), 0)
    g = jnp.where(ridx < n_groups, lg[0:8], NEG_INF)
    gmax, gidx = _first_argmax(g, ridx)
    gate = 1.0 / jnp.sum(jnp.exp(g - gmax), axis=0, keepdims=True)
    esel = lg[8:16]
    for gg in range(1, n_groups):
        esel = jnp.where(gidx == gg, lg[8 + 8 * gg:16 + 8 * gg], esel)
    v1, i1 = _first_argmax(esel, ridx)
    v2, i2 = _first_argmax(jnp.where(ridx == i1, NEG_INF, esel), ridx)
    t = jnp.exp(v2 - v1)
    w1 = 1.0 / (1.0 + t)
    ids_ref[...] = jnp.where(ridx == 0, gidx * 8 + i1, jnp.where(ridx == 1, gidx * 8 + i2, 0))
    wts_ref[...] = jnp.where(ridx == 0, gate * w1, jnp.where(ridx == 1, gate * (t * w1), 0.0))


def _mix_out(o, w, x, g, b, wrh, wrl, rb, alpha, n_groups):
    T, D = x.shape
    tm = _pick_tile(T, 256, LANES)
    const = lambda a: pl.BlockSpec(a.shape, lambda i: (0,) * a.ndim)
    rb_t = jnp.broadcast_to(rb, (ROUTER_ROWS, tm))
    return pl.pallas_call(
        functools.partial(_mix_out_kernel, alpha=alpha, n_groups=n_groups),
        grid=(T // tm,),
        in_specs=[pl.BlockSpec((tm, o.shape[1]), lambda i: (i, 0)), const(w),
                  pl.BlockSpec((tm, D), lambda i: (i, 0)), const(g), const(b), const(wrh), const(wrl), const(rb_t)],
        out_specs=[pl.BlockSpec((tm, D), lambda i: (i, 0)), pl.BlockSpec((8, tm), lambda i: (0, i)),
                   pl.BlockSpec((8, tm), lambda i: (0, i))],
        out_shape=[jax.ShapeDtypeStruct((T, D), F32), jax.ShapeDtypeStruct((8, T), jnp.int32),
                   jax.ShapeDtypeStruct((8, T), F32)],
        compiler_params=_cparams("parallel"),
        name="mix_out_ln_router",
    )(o, w, x, g, b, wrh, wrl, rb_t)


def _prep_router(w_group, b_group, w_router, b_router):
    D, G = w_group.shape
    E = w_router.shape[-1]
    wr = jnp.zeros((ROUTER_ROWS, D), F32)
    wr = wr.at[:G].set(w_group.T.astype(F32))
    wr = wr.at[8:8 + G * E].set(jnp.transpose(w_router, (0, 2, 1)).reshape(G * E, D).astype(F32))
    rb = jnp.zeros((ROUTER_ROWS, 1), F32)
    rb = rb.at[:G, 0].set(b_group.astype(F32)).at[8:8 + G * E, 0].set(b_router.reshape(-1).astype(F32))
    hi, lo = _split_bf16_trunc(wr, 2)
    return hi, lo, rb


def _gather_rows(idx_ref, n, src_hbm, dst, sem):
    def body(r, carry):
        pltpu.make_async_copy(src_hbm.at[pl.ds(idx_ref[0, 0, r], 1)], dst.at[pl.ds(r, 1)], sem).start()
        return carry
    lax.fori_loop(0, n, body, 0, unroll=8)


def _wait_rows(n, src_hbm, dst, sem):
    pltpu.make_async_copy(src_hbm.at[pl.ds(0, n)], dst, sem).wait()


def _moe_experts_kernel(te_ref, tv_ref, src_ref, nxt_ref, x_hbm, wg_ref, wu_ref, wd_ref, y_ref, xbuf, sem, *, tm):
    i = pl.program_id(0)
    nt = pl.num_programs(0)
    slot = i % 2

    @pl.when(jnp.logical_and(i == 0, tv_ref[0] > 0))
    def _():
        _gather_rows(src_ref, tm, x_hbm, xbuf.at[0], sem.at[0])

    @pl.when(jnp.logical_and(i + 1 < nt, tv_ref[jnp.minimum(i + 1, nt - 1)] > 0))
    def _():
        _gather_rows(nxt_ref, tm, x_hbm, xbuf.at[1 - slot], sem.at[1 - slot])

    @pl.when(tv_ref[i] > 0)
    def _():
        _wait_rows(tm, x_hbm, xbuf.at[slot], sem.at[slot])
        xb = xbuf[slot].astype(BF16)
        a = _dot(xb, wg_ref[0].astype(BF16))
        u = _dot(xb, wu_ref[0].astype(BF16))
        hid = (a / (1.0 + jnp.exp(-a))) * u
        y_ref[...] = _dot(hid.astype(BF16), wd_ref[0].astype(BF16))

    @pl.when(tv_ref[i] == 0)
    def _():
        y_ref[...] = jnp.zeros_like(y_ref)


def _moe_experts(x, w_gate, w_up, w_down, tile_expert, tile_valid, src, tm):
    T, D = x.shape
    F = w_gate.shape[-1]
    NT = tile_expert.shape[0]
    grid_spec = pltpu.PrefetchScalarGridSpec(
        num_scalar_prefetch=2,
        grid=(NT,),
        in_specs=[pl.BlockSpec((1, 1, tm), lambda i, te, tv: (i, 0, 0), memory_space=pltpu.SMEM),
                  pl.BlockSpec((1, 1, tm), lambda i, te, tv: (jnp.minimum(i + 1, NT - 1), 0, 0),
                               memory_space=pltpu.SMEM),
                  pl.BlockSpec(memory_space=pl.ANY),
                  pl.BlockSpec((1, D, F), lambda i, te, tv: (te[i], 0, 0)),
                  pl.BlockSpec((1, D, F), lambda i, te, tv: (te[i], 0, 0)),
                  pl.BlockSpec((1, F, D), lambda i, te, tv: (te[i], 0, 0))],
        out_specs=pl.BlockSpec((tm, D), lambda i, te, tv: (i, 0)),
        scratch_shapes=[pltpu.VMEM((2, tm, D), F32), pltpu.SemaphoreType.DMA((2,))],
    )
    return pl.pallas_call(
        functools.partial(_moe_experts_kernel, tm=tm),
        grid_spec=grid_spec,
        out_shape=jax.ShapeDtypeStruct((NT * tm, D), F32),
        compiler_params=_cparams("arbitrary"),
        name="moe_experts",
    )(tile_expert, tile_valid, src, src, x, w_gate, w_up, w_down)


def _moe_combine_kernel(pos_ref, nxt_ref, ys_hbm, h_ref, w_ref, g_ref, b_ref, o_ref, buf, sem, *, tm, alpha):
    i = pl.program_id(0)
    nt = pl.num_programs(0)
    slot = i % 2

    @pl.when(i == 0)
    def _():
        _gather_rows(pos_ref, 2 * tm, ys_hbm, buf.at[0], sem.at[0])

    @pl.when(i + 1 < nt)
    def _():
        _gather_rows(nxt_ref, 2 * tm, ys_hbm, buf.at[1 - slot], sem.at[1 - slot])

    _wait_rows(2 * tm, ys_hbm, buf.at[slot], sem.at[slot])
    w = w_ref[...]
    y = alpha * h_ref[...] + (w[:, 0:1] * buf[slot, 0:tm] + w[:, 1:2] * buf[slot, tm:2 * tm])
    o_ref[...] = _layer_norm(y, g_ref[...], b_ref[...])


def _moe_combine(ys, h, pos, wts, g, b, alpha, tm):
    T, D = h.shape
    nt = T // tm
    const = lambda a: pl.BlockSpec(a.shape, lambda i: (0,) * a.ndim)
    return pl.pallas_call(
        functools.partial(_moe_combine_kernel, tm=tm, alpha=alpha),
        grid=(nt,),
        in_specs=[pl.BlockSpec((1, 1, 2 * tm), lambda i: (i, 0, 0), memory_space=pltpu.SMEM),
                  pl.BlockSpec((1, 1, 2 * tm), lambda i: (jnp.minimum(i + 1, nt - 1), 0, 0),
                               memory_space=pltpu.SMEM),
                  pl.BlockSpec(memory_space=pl.ANY),
                  pl.BlockSpec((tm, D), lambda i: (i, 0)),
                  pl.BlockSpec((tm, 2), lambda i: (i, 0)), const(g), const(b)],
        out_specs=pl.BlockSpec((tm, D), lambda i: (i, 0)),
        out_shape=jax.ShapeDtypeStruct((T, D), F32),
        scratch_shapes=[pltpu.VMEM((2, 2 * tm, D), F32), pltpu.SemaphoreType.DMA((2,))],
        compiler_params=_cparams("arbitrary"),
        name="moe_combine_ln",
    )(pos, pos, ys, h, wts, g, b)


def _route(ids, n_experts, tm):
    T = ids.shape[1]
    flat = ids.reshape(-1)
    iota = jnp.arange(2 * T, dtype=jnp.int32)
    sorted_e, order = lax.sort((flat, iota), num_keys=1, is_stable=True)
    _, inverse = lax.sort((order, iota), num_keys=1)
    experts = jnp.arange(n_experts, dtype=jnp.int32)
    counts = jnp.sum((flat[:, None] == experts[None, :]).astype(jnp.int32), axis=0)
    padded = (counts + tm - 1) // tm * tm
    ends = jnp.cumsum(padded)
    shift = (ends - padded) - (jnp.cumsum(counts) - counts)
    NT = (2 * T + n_experts * (tm - 1)) // tm
    tile_start = jnp.arange(NT, dtype=jnp.int32) * tm
    tile_expert = jnp.minimum(jnp.sum((tile_start[:, None] >= ends[None, :]).astype(jnp.int32), axis=1),
                              n_experts - 1)
    tile_valid = (tile_start < ends[-1]).astype(jnp.int32)
    pos = (inverse + shift[flat]).reshape(2, T)
    row = jnp.arange(NT * tm, dtype=jnp.int32)
    src = (order % T)[jnp.clip(row - jnp.repeat(shift[tile_expert], tm), 0, 2 * T - 1)]
    return tile_expert, tile_valid, src.reshape(NT, 1, tm), pos


MLA_PAIR_W = 2 * LANES
QK_NOPE = 64
QK_ROPE = 32


def _mla_proj_kernel(h_ref, wdn_ref, gq_ref, gkv_ref, wq_ref, wqr_ref, wk_ref, wv_ref, cos_ref, sin_ref,
                     ckv_ref, kr_ref, qcat_ref, kcat_ref, v_ref, *, q_lora, kv_lora, npairs, scale):
    z = _dot(h_ref[...].astype(BF16), wdn_ref[...])
    cq = z[:, :q_lora]
    ckv = z[:, q_lora:q_lora + kv_lora]
    o = q_lora + kv_lora
    kr_raw = z[:, o:o + HEAD_W]
    kr_rot = z[:, o + HEAD_W:o + 2 * HEAD_W]
    cq = cq * lax.rsqrt(jnp.mean(cq * cq, axis=-1, keepdims=True) + RMS_EPS) * gq_ref[...]
    ckv = ckv * lax.rsqrt(jnp.mean(ckv * ckv, axis=-1, keepdims=True) + RMS_EPS) * gkv_ref[...]
    ckv_ref[...] = ckv
    cos = cos_ref[...]
    sin = sin_ref[...]
    kr_tile = kr_raw * cos + kr_rot * sin
    kr_ref[...] = kr_tile[:, QK_NOPE:QK_NOPE + QK_ROPE]
    cqb = cq.astype(BF16)
    ckb = ckv.astype(BF16)
    cos2 = jnp.concatenate([cos, cos], axis=1)
    sin2 = jnp.concatenate([sin, sin], axis=1)
    kr2 = jnp.concatenate([kr_tile, kr_tile], axis=1)
    for p in range(npairs):
        lanes = slice(p * MLA_PAIR_W, (p + 1) * MLA_PAIR_W)
        q = _dot(cqb, wq_ref[:, lanes]) * cos2 + _dot(cqb, wqr_ref[:, lanes]) * sin2
        qcat_ref[:, lanes] = (q * scale).astype(BF16)
        kcat_ref[:, lanes] = (_dot(ckb, wk_ref[:, lanes]) + kr2).astype(BF16)
    v_ref[...] = _dot(ckb, wv_ref[...]).astype(BF16)


def _mla_proj(h, wdn, gq, gkv, wq, wqr, wk, wv, cos_t, sin_t, table_block, tm):
    T, D = h.shape
    q_lora, kv_lora = gq.shape[1], gkv.shape[1]
    npairs = wq.shape[1] // MLA_PAIR_W
    const = lambda a: pl.BlockSpec(a.shape, lambda i: (0,) * a.ndim)
    row = lambda w_: pl.BlockSpec((tm, w_), lambda i: (i, 0))
    table = pl.BlockSpec((tm, HEAD_W), lambda i: (table_block(i), 0))
    return pl.pallas_call(
        functools.partial(_mla_proj_kernel, q_lora=q_lora, kv_lora=kv_lora, npairs=npairs,
                          scale=(QK_NOPE + QK_ROPE) ** -0.5 * LOG2E),
        grid=(T // tm,),
        in_specs=[row(D), const(wdn), const(gq), const(gkv), const(wq), const(wqr), const(wk), const(wv),
                  table, table],
        out_specs=[row(kv_lora), row(QK_ROPE), row(wq.shape[1]), row(wk.shape[1]), row(wv.shape[1])],
        out_shape=[jax.ShapeDtypeStruct((T, kv_lora), F32), jax.ShapeDtypeStruct((T, QK_ROPE), F32),
                   jax.ShapeDtypeStruct((T, wq.shape[1]), BF16), jax.ShapeDtypeStruct((T, wk.shape[1]), BF16),
                   jax.ShapeDtypeStruct((T, wv.shape[1]), BF16)],
        compiler_params=_cparams("parallel"),
        name="mla_proj",
    )(h, wdn, gq, gkv, wq, wqr, wk, wv, cos_t, sin_t)


def _rot_half(w):
    half = w.shape[-1] // 2
    return jnp.concatenate([-w[..., half:], w[..., :half]], axis=-1)


def _prep_mla_weights(w_down, w_uq, w_ukv, heads, q_lora, kv_lora):
    D = w_down.shape[0]
    tail = HEAD_W - QK_NOPE - QK_ROPE
    w_kr = w_down[:, q_lora + kv_lora:]
    slot = lambda w: jnp.concatenate([jnp.zeros((D, QK_NOPE), w.dtype), w, jnp.zeros((D, tail), w.dtype)], axis=1)
    wdn = jnp.concatenate([w_down[:, :q_lora + kv_lora], slot(w_kr), slot(_rot_half(w_kr))], axis=1)
    wq3 = w_uq.reshape(q_lora, heads, QK_NOPE + QK_ROPE)
    nope, ropew = wq3[..., :QK_NOPE], wq3[..., QK_NOPE:]
    zpad = jnp.zeros((q_lora, heads, tail), w_uq.dtype)
    wq = jnp.concatenate([nope, ropew, zpad], axis=-1)
    wqr = jnp.concatenate([jnp.zeros_like(nope), _rot_half(ropew), zpad], axis=-1)
    wkv3 = w_ukv.reshape(kv_lora, heads, QK_NOPE + HEAD_DIM)
    w_uk, w_uv = wkv3[..., :QK_NOPE], wkv3[..., QK_NOPE:]
    wk = jnp.concatenate([w_uk, jnp.zeros((kv_lora, heads, HEAD_W - QK_NOPE), w_ukv.dtype)], axis=-1)
    wv = w_uv.reshape(kv_lora, heads * HEAD_DIM)
    b16 = lambda a: a.astype(BF16)
    return (b16(wdn), b16(wq.reshape(q_lora, -1)), b16(wqr.reshape(q_lora, -1)), b16(wk.reshape(kv_lora, -1)),
            b16(wv), b16(jnp.transpose(w_uk, (1, 2, 0))), b16(jnp.transpose(w_uv, (1, 0, 2))))


def _rope_tables(pos):
    half = QK_ROPE // 2
    inv_freq = ROPE_BASE ** (-jnp.arange(half, dtype=F32) / half)
    ang = pos.astype(F32)[:, None] * inv_freq[None, :]
    n = pos.shape[0]
    pad = jnp.zeros((n, HEAD_W - QK_NOPE - QK_ROPE), F32)
    cos = jnp.concatenate([jnp.ones((n, QK_NOPE), F32)] + [jnp.cos(ang)] * 2 + [pad], axis=1)
    sin = jnp.concatenate([jnp.zeros((n, QK_NOPE), F32)] + [jnp.sin(ang)] * 2 + [pad], axis=1)
    return cos, sin


def _mla_sample_queries(qcat, heads):
    DB, n, _ = qcat.shape
    q4 = qcat.reshape(DB, n, heads, HEAD_W)
    rows = lambda a: jnp.transpose(a, (0, 2, 1, 3)).reshape(DB, heads * n, a.shape[-1])
    return rows(q4[..., :QK_NOPE]), rows(q4[..., QK_NOPE:QK_NOPE + QK_ROPE])


def _mla_sample_kernel(qn_ref, qr_ref, wuk_ref, wuv_ref, cc_ref, rc_ref, cn_ref, rn_ref, o_ref,
                       qlat_ref, m_ref, l_ref, acc_ref, *, n, heads):
    kt = pl.program_id(1)

    @pl.when(kt == 0)
    def _():
        for h in range(heads):
            rows = slice(h * n, (h + 1) * n)
            qlat_ref[rows, :] = _dot(qn_ref[0, rows, :], wuk_ref[h]).astype(BF16)
        m_ref[...] = jnp.full(m_ref.shape, NEG_INF, F32)
        l_ref[...] = jnp.zeros_like(l_ref)
        acc_ref[...] = jnp.zeros_like(acc_ref)

    def update(ckv, kr):
        s = _dot_nt(qlat_ref[...], ckv) + _dot_nt(qr_ref[0], kr)
        m, l, acc = _softmax_step(s[None], ckv, (m_ref[...], l_ref[...], acc_ref[...]))
        m_ref[...] = m
        l_ref[...] = l
        acc_ref[...] = acc

    update(cc_ref[0].astype(BF16), rc_ref[0].astype(BF16))

    @pl.when(kt == pl.num_programs(1) - 1)
    def _():
        update(cn_ref[0], rn_ref[0])
        o_lat = (acc_ref[0] / l_ref[0]).astype(BF16)
        for h in range(heads):
            o_ref[0, :, h * HEAD_DIM:(h + 1) * HEAD_DIM] = _dot(o_lat[h * n:(h + 1) * n], wuv_ref[h]).astype(o_ref.dtype)


def _mla_sample(qn, qr, wuk, wuv, ckv_c, kr_c, ckv_n, kr_n, n, tk):
    DB, R, _ = qn.shape
    heads = R // n
    P, C = ckv_c.shape[1], ckv_c.shape[2]
    const = lambda a: pl.BlockSpec(a.shape, lambda b, j: (0,) * a.ndim)
    per_b = lambda a: pl.BlockSpec((1,) + a.shape[1:], lambda b, j: (b, 0, 0))
    return pl.pallas_call(
        functools.partial(_mla_sample_kernel, n=n, heads=heads),
        grid=(DB, P // tk),
        in_specs=[per_b(qn), per_b(qr), const(wuk), const(wuv),
                  pl.BlockSpec((1, tk, C), lambda b, j: (b, j, 0)),
                  pl.BlockSpec((1, tk, QK_ROPE), lambda b, j: (b, j, 0)),
                  per_b(ckv_n), per_b(kr_n)],
        out_specs=pl.BlockSpec((1, n, heads * HEAD_DIM), lambda b, j: (b, 0, 0)),
        out_shape=jax.ShapeDtypeStruct((DB, n, heads * HEAD_DIM), BF16),
        scratch_shapes=[pltpu.VMEM((R, C), BF16), pltpu.VMEM((1, R, 1), F32), pltpu.VMEM((1, R, 1), F32),
                        pltpu.VMEM((1, R, C), F32)],
        compiler_params=_cparams("parallel", "arbitrary"),
        name="mla_sample",
    )(qn, qr, wuk, wuv, ckv_c, kr_c, ckv_n, kr_n)


def _moe_layer(h, ids, wts, w_gate, w_up, w_down, g, b, alpha, tm):
    T, D = h.shape
    n_experts = w_gate.shape[0] * w_gate.shape[1]
    tile_expert, tile_valid, src, pos = _route(ids[:2], n_experts, tm)
    flat3 = lambda w: w.reshape((n_experts,) + w.shape[2:])
    ys = _moe_experts(h, flat3(w_gate), flat3(w_up), flat3(w_down), tile_expert, tile_valid, src, tm)
    pos_t = jnp.transpose(pos.reshape(2, T // tm, tm), (1, 0, 2)).reshape(T // tm, 1, 2 * tm)
    return _moe_combine(ys, h, pos_t, jnp.transpose(wts[:2]), g, b, alpha, tm)


TOKEN_TILE = 256
FLASH_Q_TILE = 512
FLASH_KEY_TILE = 256
SB_Q_TILE = 256
SB_KEY_TILE = 128
CACHE_TILE = 1024
SB_SUB_TILE = 256


def kernel(x_prompt, x_sample, cache_fox_k, cache_fox_v, cache_fox_logf, cache_sb_k, cache_sb_v, cache_mla_ckv, cache_mla_krope, ab_w_in, ab_b_forget, ab_w_out, mla_w_down, mla_g_q, mla_g_kv, mla_w_uq, mla_w_ukv, mla_w_out, moe_w_group, moe_b_group, moe_w_router, moe_b_router, moe_w_gate, moe_w_up, moe_w_down, ln_g, ln_b):
    B, S, D = x_prompt.shape
    DB, n, _ = x_sample.shape
    P = cache_fox_k.shape[2]
    TP, TS = B * S, DB * n
    depth = ln_g.shape[0]
    n_groups = moe_w_group.shape[-1]
    assert depth == 2 and ab_w_in.shape[0] == 1 and mla_w_down.shape[0] == 1
    assert S % FLASH_Q_TILE == 0 and S % SB_Q_TILE == 0 and TP % TOKEN_TILE == 0 and TS % TOKEN_TILE == 0 and TOKEN_TILE % n == 0
    assert P % CACHE_TILE == 0 and P % CHUNK == 0 and n == CHUNK
    alpha = (2 * depth) ** 0.25
    tk = CACHE_TILE

    x = jnp.concatenate([x_prompt.reshape(TP, D), x_sample.reshape(TS, D)], axis=0)
    prompt3 = lambda a: a[:TP].reshape(B, S, -1)
    sample3 = lambda a: a[TP:].reshape(DB, n, -1)

    def ffn(o, w_out, resid, layer):
        wrh, wrl, rb = _prep_router(moe_w_group[layer], moe_b_group[layer], moe_w_router[layer], moe_b_router[layer])
        h, ids, wts = _mix_out(o, w_out.astype(BF16), resid, ln_g[layer, 0][None], ln_b[layer, 0][None],
                               wrh, wrl, rb, alpha, n_groups)
        return _moe_layer(h, ids, wts, moe_w_gate[layer], moe_w_up[layer], moe_w_down[layer],
                          ln_g[layer, 1][None], ln_b[layer, 1][None], alpha, TOKEN_TILE)

    fox_heads = ab_b_forget.shape[1]
    hw = (ab_w_in.shape[2] - fox_heads) // 6
    w_ab, b_forget = _prep_ab_weights(ab_w_in[0], ab_b_forget[0])
    qa, ka, va, qb, kb, vb, ka16, va16, kb16, vb16, lf = _ab_proj(x, w_ab, b_forget, hw)
    qx, kx = _fox_prompt_operands(prompt3(qa), prompt3(ka16), lf[:TP], fox_heads)
    o_fox_p = _flash_prompt(qx, kx, _transpose_values(prompt3(va16), FLASH_KEY_TILE), FLASH_Q_TILE,
                            fox_heads, 1, "fox_prompt")
    o_sb_p = _sb_prompt(_pad_heads(prompt3(qb)), _pad_heads(prompt3(kb16)),
                        _transpose_values(prompt3(vb16), SB_KEY_TILE), SB_Q_TILE)
    lf_s = sample3(lf)[:, :, :fox_heads]
    cq_s, ck_past, ck_new = _fox_sample_cum(lf_s, cache_fox_logf[0], tk)
    cache2 = lambda c: c[0].reshape(DB, P, hw)
    o_fox_s = _fox_sample(sample3(qa), sample3(ka16), sample3(va16), cache2(cache_fox_k), cache2(cache_fox_v),
                          cq_s, ck_past, ck_new, tk)
    o_sb_s = _sb_sample(sample3(qb), sample3(kb16), sample3(vb16), cache2(cache_sb_k), cache2(cache_sb_v),
                        tk, SB_SUB_TILE)
    o = jnp.concatenate([jnp.concatenate([o_fox_p, o_sb_p], axis=-1).reshape(TP, 2 * hw),
                         jnp.concatenate([o_fox_s, o_sb_s], axis=-1).reshape(TS, 2 * hw)], axis=0)
    x = ffn(o, ab_w_out[0], x, 0)

    q_lora, kv_lora = mla_g_q.shape[1], mla_g_kv.shape[1]
    heads = mla_w_uq.shape[2] // (QK_NOPE + QK_ROPE)
    wdn, wq, wqr, wk, wv, wuk_t, wuv = _prep_mla_weights(mla_w_down[0], mla_w_uq[0], mla_w_ukv[0], heads, q_lora, kv_lora)
    tm = TOKEN_TILE
    pos = jnp.concatenate([jnp.arange(S, dtype=jnp.int32), P + jnp.arange(tm, dtype=jnp.int32) % n])
    cos_t, sin_t = _rope_tables(pos)
    blocks_per_seq, prompt_blocks = S // tm, TP // tm
    table_block = lambda i: jnp.where(i < prompt_blocks, i % blocks_per_seq, blocks_per_seq)
    ckv, kr, qcat, kcat, v = _mla_proj(x, wdn, mla_g_q[0][None], mla_g_kv[0][None], wq, wqr, wk, wv,
                                       cos_t, sin_t, table_block, tm)
    o_p = _flash_prompt(prompt3(qcat), prompt3(kcat), _transpose_values(prompt3(v), FLASH_KEY_TILE), FLASH_Q_TILE,
                        8, CHUNK, "mla_prompt")
    qn, qr = _mla_sample_queries(sample3(qcat), heads)
    o_s = _mla_sample(qn, qr, wuk_t, wuv, cache_mla_ckv[0], cache_mla_krope[0],
                      sample3(ckv).astype(BF16), sample3(kr).astype(BF16), n, tk)
    o = jnp.concatenate([o_p.reshape(TP, -1), o_s.reshape(TS, -1)], axis=0)
    x = ffn(o, mla_w_out[0], x, 1)

    hd = lambda a, rows, lead: a[rows].reshape((1,) + lead + (fox_heads, hw // fox_heads))
    pr, sr = slice(0, TP), slice(TP, TP + TS)
    return (x[pr].reshape(B, S, D), x[sr].reshape(DB, n, D),
            hd(ka, pr, (B, S)), hd(va, pr, (B, S)), lf[pr, :fox_heads].reshape(1, B, S, fox_heads),
            hd(kb, pr, (B, S)), hd(vb, pr, (B, S)),
            ckv[pr].reshape(1, B, S, kv_lora), kr[pr].reshape(1, B, S, QK_ROPE),
            hd(ka, sr, (DB, n)), hd(va, sr, (DB, n)), lf[sr, :fox_heads].reshape(1, DB, n, fox_heads),
            hd(kb, sr, (DB, n)), hd(vb, sr, (DB, n)),
            ckv[sr].reshape(1, DB, n, kv_lora), kr[sr].reshape(1, DB, n, QK_ROPE))
```

```python
import functools

import jax
import jax.numpy as jnp
from jax import lax
from jax.experimental import pallas as pl
from jax.experimental.pallas import tpu as pltpu

F32 = jnp.float32
BF16 = jnp.bfloat16
NEG_INF = -1e30
LOG2E = 1.4426950408889634

LANES = 128
HEAD_DIM = 64
PAIR_W = 2 * HEAD_DIM
CHUNK = 64
LN_EPS = 1e-5
RMS_EPS = 1e-6
ROPE_BASE = 10000.0
VMEM_LIMIT = 56 * 1024 * 1024


def _cparams(*sem):
    return pltpu.CompilerParams(dimension_semantics=sem, vmem_limit_bytes=VMEM_LIMIT)


def _dot(a, b):
    return jnp.dot(a, b, preferred_element_type=F32)


def _dot_nt(a, b):
    return lax.dot_general(a, b, (((1,), (1,)), ((), ())), preferred_element_type=F32)


def _split_bf16(x, parts):
    out = []
    r = x
    for _ in range(parts):
        h = r.astype(BF16)
        out.append(h)
        r = r - h.astype(F32)
    return out


def _split_bf16_trunc(x, parts):
    out = []
    r = x
    for _ in range(parts):
        bits = lax.bitcast_convert_type(r, jnp.uint32) & jnp.uint32(0xFFFF0000)
        h = lax.bitcast_convert_type(bits, F32)
        out.append(h.astype(BF16))
        r = r - h
    return out


def _log_sigmoid(x):
    return jnp.minimum(x, 0.0) - jnp.log(1.0 + jnp.exp(-jnp.abs(x)))


def _pick_tile(n, pref, mult=8):
    t = min(pref, n)
    while n % t or t % mult:
        t -= 1
    return t


def _ab_proj_kernel(x_ref, w_ref, bf_ref, qa_ref, ka_ref, va_ref, qb_ref, kb_ref, vb_ref,
                    ka16_ref, va16_ref, kb16_ref, vb16_ref, lf_ref, *, hw, qscale):
    xb = x_ref[...].astype(BF16)

    def seg(j):
        return _dot(xb, w_ref[:, j * hw:(j + 1) * hw])

    qa_ref[...] = (seg(0) * qscale).astype(BF16)
    z = seg(1)
    ka_ref[...] = z
    ka16_ref[...] = z.astype(BF16)
    z = seg(2)
    va_ref[...] = z
    va16_ref[...] = z.astype(BF16)
    qb_ref[...] = (seg(3) * qscale).astype(BF16)
    z = seg(4)
    kb_ref[...] = z
    kb16_ref[...] = z.astype(BF16)
    z = seg(5)
    vb_ref[...] = z
    vb16_ref[...] = z.astype(BF16)
    f = _dot(xb, w_ref[:, 6 * hw:6 * hw + LANES]) + bf_ref[...]
    lf_ref[...] = _log_sigmoid(f)


def _ab_proj(x, w, bf, hw):
    T, D = x.shape
    tm = _pick_tile(T, 256)
    row = lambda w_: pl.BlockSpec((tm, w_), lambda i: (i, 0))
    f32o = jax.ShapeDtypeStruct((T, hw), F32)
    b16o = jax.ShapeDtypeStruct((T, hw), BF16)
    return pl.pallas_call(
        functools.partial(_ab_proj_kernel, hw=hw, qscale=HEAD_DIM ** -0.5 * LOG2E),
        grid=(T // tm,),
        in_specs=[row(D), pl.BlockSpec(w.shape, lambda i: (0, 0)), pl.BlockSpec(bf.shape, lambda i: (0, 0))],
        out_specs=[row(hw)] * 10 + [row(LANES)],
        out_shape=[b16o, f32o, f32o, b16o, f32o, f32o, b16o, b16o, b16o, b16o,
                   jax.ShapeDtypeStruct((T, LANES), F32)],
        compiler_params=_cparams("parallel"),
        name="ab_proj",
    )(x, w, bf)


def _ab_proj_prompt_kernel(x_ref, w_ref, bf_ref, wt_ref, qa_ref, ka_ref, qb_ref, kb_ref, lf_ref,
                           kat_ref, vat_ref, kbt_ref, vbt_ref, vat16_ref, vbt16_ref, *, hw, heads, qscale, sb_tk):
    xb = x_ref[...].astype(BF16)
    tm = xb.shape[0]

    def seg(j):
        return _dot(xb, w_ref[:, j * hw:(j + 1) * hw])

    def seg_t(j):
        return _dot_nt(wt_ref[j * hw:(j + 1) * hw, :], xb).reshape(heads, hw // heads, tm)

    qa_ref[...] = (seg(0) * qscale).astype(BF16)
    ka_ref[...] = seg(1).astype(BF16)
    qb_ref[...] = (seg(2) * qscale).astype(BF16)
    kb_ref[...] = seg(3).astype(BF16)
    lf_ref[...] = _log_sigmoid(_dot(xb, w_ref[:, 4 * hw:4 * hw + LANES]) + bf_ref[...])
    kat_ref[0] = seg_t(0)
    z = seg_t(1)
    vat_ref[0] = z
    vat16_ref[0] = z.astype(BF16)
    kbt_ref[0] = seg_t(2)
    z = seg_t(3)
    vbt_ref[0] = z
    for c in range(tm // sb_tk):
        vbt16_ref[c] = z[:, :, c * sb_tk:(c + 1) * sb_tk].astype(BF16)


def _ab_proj_prompt(x, w_tok, bf, w_t, B, S, heads, sb_tk):
    TP, D = x.shape
    hw = w_t.shape[0] // 4
    tm = FLASH_KEY_TILE
    nj = S // tm
    const = lambda a: pl.BlockSpec(a.shape, lambda b, j: (0,) * a.ndim)
    row = lambda w_: pl.BlockSpec((tm, w_), lambda b, j: (b * nj + j, 0))
    t_spec = pl.BlockSpec((1, heads, hw // heads, tm), lambda b, j: (b, 0, 0, j))
    b16 = jax.ShapeDtypeStruct((TP, hw), BF16)
    t32 = jax.ShapeDtypeStruct((B, heads, hw // heads, S), F32)
    return pl.pallas_call(
        functools.partial(_ab_proj_prompt_kernel, hw=hw, heads=heads, qscale=HEAD_DIM ** -0.5 * LOG2E, sb_tk=sb_tk),
        grid=(B, nj),
        in_specs=[row(D), const(w_tok), const(bf), const(w_t)],
        out_specs=[row(hw)] * 4 + [row(LANES)] + [t_spec] * 4 + [
            pl.BlockSpec((1, heads, hw // heads, tm), lambda b, j: (b * nj + j, 0, 0, 0)),
            pl.BlockSpec((tm // sb_tk, heads, hw // heads, sb_tk), lambda b, j: (b * nj + j, 0, 0, 0))],
        out_shape=[b16] * 4 + [jax.ShapeDtypeStruct((TP, LANES), F32)] + [t32] * 4 + [
            jax.ShapeDtypeStruct((TP // tm, heads, hw // heads, tm), BF16),
            jax.ShapeDtypeStruct((TP // sb_tk, heads, hw // heads, sb_tk), BF16)],
        compiler_params=_cparams("parallel", "parallel"),
        name="ab_proj_prompt",
    )(x, w_tok, bf, w_t)


def _cumsum_kernel(x_ref, o_ref, carry_ref, *, tl):
    @pl.when(pl.program_id(1) == 0)
    def _():
        carry_ref[...] = jnp.zeros_like(carry_ref)

    x = x_ref[0]
    r = lax.broadcasted_iota(jnp.int32, (tl, tl), 0)
    c = lax.broadcasted_iota(jnp.int32, (tl, tl), 1)
    upper = (r <= c).astype(BF16)
    parts = jnp.concatenate(_split_bf16(x, 4), axis=0)
    y = _dot(parts, upper)
    cum = (y[0:8] + y[8:16]) + (y[16:24] + y[24:32]) + carry_ref[:, 0:1]
    o_ref[0] = cum
    carry_ref[...] = jnp.broadcast_to(cum[:, tl - 1:tl], carry_ref.shape)


def _cumsum_rows(x):
    B, H, L = x.shape
    tl = _pick_tile(L, 512, LANES)
    return pl.pallas_call(
        functools.partial(_cumsum_kernel, tl=tl),
        grid=(B, L // tl),
        in_specs=[pl.BlockSpec((1, H, tl), lambda b, j: (b, 0, j))],
        out_specs=pl.BlockSpec((1, H, tl), lambda b, j: (b, 0, j)),
        out_shape=jax.ShapeDtypeStruct((B, H, L), F32),
        scratch_shapes=[pltpu.VMEM((H, LANES), F32)],
        compiler_params=_cparams("parallel", "arbitrary"),
        name="cumsum_rows",
    )(x)


def _stack_pair(qp):
    lo = lax.broadcasted_iota(jnp.int32, qp.shape, 1) < HEAD_DIM
    zero = jnp.zeros_like(qp)
    return jnp.concatenate([jnp.where(lo, qp, zero), jnp.where(lo, zero, qp)], axis=0)


def _unstack_pair(o):
    lo = lax.broadcasted_iota(jnp.int32, o.shape[1:], 1) < HEAD_DIM
    return jnp.where(lo, o[0], o[1])


def _softmax_step(s, vb, carry):
    m, l, acc = carry
    two, tq, tk = s.shape
    m_new = jnp.maximum(m, jnp.max(s, axis=-1, keepdims=True))
    alpha = jnp.exp2(m - m_new)
    p = jnp.exp2(s - m_new)
    l = alpha * l + jnp.sum(p, axis=-1, keepdims=True)
    pv = _dot(p.reshape(two * tq, tk).astype(BF16), vb).reshape(two, tq, vb.shape[-1])
    return m_new, l, alpha * acc + pv


def _log2_sigmoid_pair(z2):
    l1 = jnp.log2(1.0 + jnp.exp2(-jnp.abs(z2)))
    return jnp.minimum(z2, 0.0) - l1, jnp.minimum(-z2, 0.0) - l1


def _softmax_init(tq, width):
    return (jnp.full((2, tq, 1), NEG_INF, F32), jnp.zeros((2, tq, 1), F32), jnp.zeros((2, tq, width), F32))


def _sb_step(z, vb, strict_upper, carry, mask=None):
    run, acc = carry
    log_beta, log_rest = _log2_sigmoid_pair(z)
    if mask is not None:
        log_beta = jnp.where(mask, log_beta, NEG_INF)
        log_rest = jnp.where(mask, log_rest, 0.0)
    hi, lo = _split_bf16(log_rest, 2)
    later = _dot(hi, strict_upper) + _dot(lo, strict_upper)
    a = jnp.exp2(log_beta + later + run)
    acc = acc + _dot(a.astype(BF16), vb)
    run = run + jnp.sum(log_rest, axis=-1, keepdims=True)
    return run, acc


def _strict_upper(tk):
    r = lax.broadcasted_iota(jnp.int32, (tk, tk), 0)
    c = lax.broadcasted_iota(jnp.int32, (tk, tk), 1)
    return (r > c).astype(BF16)


HEAD_W = LANES
VT_ROWS = HEAD_DIM + 16
SB_DEAD_LOG2 = -160.0


def _pipeline_ahead(stage, first, count, cur, nxt):
    if first < count:
        stage(first, cur)
    elif nxt is not None:
        stage(first - count, nxt)


def _flash_prompt_kernel(q_ref, k_ref, vt_ref, o_ref, s_ref, m_ref, acc_ref, *, tq, tk, heads, chunk):
    i = pl.program_id(2)
    sub = tq // tk
    key = lax.broadcasted_iota(jnp.int32, (tk, tq), 0)
    query = i * tq + lax.broadcasted_iota(jnp.int32, (tk, tq), 1)
    ones = jnp.ones((VT_ROWS - HEAD_DIM, tk), BF16)
    m_ref[...] = jnp.full(m_ref.shape, NEG_INF, F32)
    acc_ref[...] = jnp.zeros_like(acc_ref)

    def scores(h, kt):
        rows = pl.ds(pl.multiple_of(kt * tk, tk), tk)
        lanes = slice(h * HEAD_W, (h + 1) * HEAD_W)
        s_ref[h] = _dot_nt(k_ref[0, rows, lanes], q_ref[0, :, lanes])

    def absorb(h, kt, masked):
        s_t = s_ref[h]
        if masked:
            s_t = jnp.where(((kt * tk + key) // chunk) <= (query // chunk), s_t, NEG_INF)
        m = m_ref[h]
        m_new = jnp.maximum(m, jnp.max(s_t, axis=0, keepdims=True))
        p_t = jnp.exp2(s_t - m_new).astype(BF16)
        pv = jnp.concatenate([_dot(vt_ref[kt, h], p_t), _dot(ones, p_t)], axis=0)
        acc_ref[h] = jnp.exp2(m - m_new) * acc_ref[h] + pv
        m_ref[h] = m_new

    scores(0, 0)
    scores(1, 0)

    def body(kt, carry):
        for h in range(heads):
            _pipeline_ahead(scores, h + 2, heads, kt, kt + 1)
            absorb(h, kt, False)
        return carry

    lax.fori_loop(0, sub * i, body, 0)
    for s in range(sub):
        kt = sub * i + s
        for h in range(heads):
            _pipeline_ahead(scores, h + 2, heads, kt, kt + 1 if s + 1 < sub else None)
            absorb(h, kt, True)
    for j in range(heads // 2):
        a0, a1 = acc_ref[2 * j], acc_ref[2 * j + 1]
        o_t = jnp.concatenate([a0[:HEAD_DIM] / a0[HEAD_DIM:HEAD_DIM + 1],
                               a1[:HEAD_DIM] / a1[HEAD_DIM:HEAD_DIM + 1]], axis=0)
        o_ref[0, :, j * PAIR_W:(j + 1) * PAIR_W] = jnp.transpose(o_t).astype(o_ref.dtype)


def _flash_prompt(qx, kx, vt4, tq, heads_per_step, chunk, name):
    B, S, W = qx.shape
    hs = heads_per_step
    tk = vt4.shape[3]
    nk = S // tk
    return pl.pallas_call(
        functools.partial(_flash_prompt_kernel, tq=tq, tk=tk, heads=hs, chunk=chunk),
        grid=(B, W // (HEAD_W * hs), S // tq),
        in_specs=[pl.BlockSpec((1, tq, HEAD_W * hs), lambda b, g, i: (b, i, g)),
                  pl.BlockSpec((1, S, HEAD_W * hs), lambda b, g, i: (b, 0, g)),
                  pl.BlockSpec((nk, hs, HEAD_DIM, tk), lambda b, g, i: (b, g, 0, 0))],
        out_specs=pl.BlockSpec((1, tq, HEAD_DIM * hs), lambda b, g, i: (b, i, g)),
        out_shape=jax.ShapeDtypeStruct((B, S, W // HEAD_W * HEAD_DIM), BF16),
        scratch_shapes=[pltpu.VMEM((hs, tk, tq), F32), pltpu.VMEM((hs, 1, tq), F32),
                        pltpu.VMEM((hs, VT_ROWS, tq), F32)],
        compiler_params=_cparams("parallel", "parallel", "arbitrary"),
        name=name,
    )(qx, kx, vt4)


def _pad_heads(a, extras=()):
    B, S, W = a.shape
    heads = W // HEAD_DIM
    parts = [a.reshape(B, S, heads, HEAD_DIM)] + list(extras)
    used = sum(p.shape[-1] for p in parts)
    parts.append(jnp.zeros((B, S, heads, HEAD_W - used), a.dtype))
    return jnp.concatenate(parts, axis=-1).reshape(B, S, heads * HEAD_W)


def _sb_prompt_kernel(q_ref, k_ref, vt_ref, o_ref, z_ref, lw_ref, tot_ref, run_ref, acc_ref, *, tq, tk, heads):
    i = pl.program_id(1)
    sub = tq // tk
    key = lax.broadcasted_iota(jnp.int32, (tk, tq), 0)
    query = lax.broadcasted_iota(jnp.int32, (tk, tq), 1)
    r = lax.broadcasted_iota(jnp.int32, (tk, tk), 0)
    c = lax.broadcasted_iota(jnp.int32, (tk, tk), 1)
    after = (c > r).astype(BF16)
    run_ref[...] = jnp.zeros_like(run_ref)
    acc_ref[...] = jnp.zeros_like(acc_ref)

    def logits(h, kt):
        rows = pl.ds(pl.multiple_of(kt * tk, tk), tk)
        lanes = slice(h * HEAD_W, (h + 1) * HEAD_W)
        z_ref[h] = _dot_nt(k_ref[0, rows, lanes], q_ref[0, :, lanes])

    def log_weights(masked, h, kt):
        log_beta, log_rest = _log2_sigmoid_pair(z_ref[h])
        if masked:
            before = (kt * tk + key) < (i * tq + query)
            log_beta = jnp.where(before, log_beta, NEG_INF)
            log_rest = jnp.where(before, log_rest, 0.0)
        hi, lo = _split_bf16(log_rest, 2)
        later = _dot(after, hi) + _dot(after, lo)
        lw_ref[h] = log_beta + later
        tot_ref[h] = later[0:1] + log_rest[0:1]

    def accumulate(h, kt):
        run = run_ref[h]
        a_t = jnp.exp2(lw_ref[h] + run).astype(BF16)
        acc_ref[h] += _dot(vt_ref[kt, h], a_t)
        run_ref[h] = run + tot_ref[h]

    def step(kt, masked, nxt, nxt_masked):
        for h in range(heads):
            _pipeline_ahead(logits, h + 2, heads, kt, nxt)
            if h + 1 < heads:
                log_weights(masked, h + 1, kt)
            elif nxt is not None:
                log_weights(nxt_masked, 0, nxt)
            accumulate(h, kt)

    unmasked = sub * i
    first = unmasked + sub - 1
    logits(0, first)
    logits(1, first)
    log_weights(True, 0, first)
    for s in range(sub):
        kt = first - s
        if s + 1 < sub:
            step(kt, True, kt - 1, True)
        else:
            step(kt, True, jnp.maximum(kt - 1, 0), False)

    def alive():
        return (jnp.max(run_ref[...]) > SB_DEAD_LOG2).astype(jnp.int32)

    def body(carry):
        kt, _ = carry
        step(kt, False, jnp.maximum(kt - 1, 0), False)
        return kt - 1, alive()

    lax.while_loop(lambda c: jnp.logical_and(c[0] >= 0, c[1] > 0), body, (unmasked - 1, alive()))

    for j in range(heads // 2):
        o_t = jnp.concatenate([acc_ref[2 * j], acc_ref[2 * j + 1]], axis=0)
        o_ref[0, :, j * PAIR_W:(j + 1) * PAIR_W] = jnp.transpose(o_t).astype(o_ref.dtype)


def _sb_prompt(qx, kx, vt4, tq):
    B, S, W = qx.shape
    heads = W // HEAD_W
    tk = vt4.shape[3]
    return pl.pallas_call(
        functools.partial(_sb_prompt_kernel, tq=tq, tk=tk, heads=heads),
        grid=(B, S // tq),
        in_specs=[pl.BlockSpec((1, tq, W), lambda b, i: (b, i, 0)),
                  pl.BlockSpec((1, S, W), lambda b, i: (b, 0, 0)),
                  pl.BlockSpec((S // tk, heads, HEAD_DIM, tk), lambda b, i: (b, 0, 0, 0))],
        out_specs=pl.BlockSpec((1, tq, heads * HEAD_DIM), lambda b, i: (b, i, 0)),
        out_shape=jax.ShapeDtypeStruct((B, S, heads * HEAD_DIM), BF16),
        scratch_shapes=[pltpu.VMEM((heads, tk, tq), F32), pltpu.VMEM((heads, tk, tq), F32),
                        pltpu.VMEM((heads, 1, tq), F32), pltpu.VMEM((heads, 1, tq), F32),
                        pltpu.VMEM((heads, HEAD_DIM, tq), F32)],
        compiler_params=_cparams("parallel", "arbitrary"),
        name="sb_prompt",
    )(qx, kx, vt4)


def _prep_ab_weights(w_in, b_f):
    D = w_in.shape[0]
    H = b_f.shape[0]
    hw = (w_in.shape[1] - H) // 6
    main = jnp.concatenate([w_in[:, :3 * hw], w_in[:, 3 * hw + H:]], axis=1)
    wf = jnp.zeros((D, LANES), w_in.dtype).at[:, :H].set(w_in[:, 3 * hw:3 * hw + H])
    bf = jnp.zeros((1, LANES), F32).at[0, :H].set(b_f.astype(F32))
    return jnp.concatenate([main, wf], axis=1).astype(BF16), bf


def _fox_prompt_operands(q, k, lf, H):
    B, S, W = q.shape
    lt = jnp.transpose(lf[:, :H].reshape(B, S, H), (0, 2, 1))
    cum = jnp.transpose(_cumsum_rows(lt), (0, 2, 1)) * LOG2E
    c = jnp.stack(_split_bf16_trunc(cum, 3), axis=-1)
    one = jnp.ones_like(c)
    return _pad_heads(q, [c, one]), _pad_heads(k, [one, -c])


def _fox_sample_kernel(q_ref, knt_ref, vnt_ref, kct_ref, vct_ref, cq_ref, ckp_ref, ckn_ref, o_ref,
                       s_ref, m_ref, l_ref, acc_ref, *, n, heads):
    kt = pl.program_id(1)

    @pl.when(kt == 0)
    def _():
        m_ref[...] = jnp.full(m_ref.shape, NEG_INF, F32)
        l_ref[...] = jnp.zeros_like(l_ref)
        acc_ref[...] = jnp.zeros_like(acc_ref)

    def absorb(h, s, v_t, ck, mask):
        s = s + (cq_ref[0, h] - ck)
        if mask is not None:
            s = jnp.where(mask, s, NEG_INF)
        m = m_ref[h]
        m_new = jnp.maximum(m, jnp.max(s, axis=-1, keepdims=True))
        alpha = jnp.exp2(m - m_new)
        p = jnp.exp2(s - m_new)
        l_ref[h] = alpha * l_ref[h] + jnp.sum(p, axis=-1, keepdims=True)
        acc_ref[h] = alpha * acc_ref[h] + _dot_nt(p.astype(BF16), v_t)
        m_ref[h] = m_new

    def scores(h, _=None):
        s_ref[h] = _dot(q_ref[0, h], kct_ref[0, h].astype(BF16))

    scores(0)
    scores(1)
    for h in range(heads):
        _pipeline_ahead(scores, h + 2, heads, None, None)
        absorb(h, s_ref[h], vct_ref[0, h].astype(BF16), ckp_ref[0, h:h + 1, :], None)

    @pl.when(kt == pl.num_programs(1) - 1)
    def _():
        row = lax.broadcasted_iota(jnp.int32, (n, n), 0)
        col = lax.broadcasted_iota(jnp.int32, (n, n), 1)
        for h in range(heads):
            absorb(h, _dot(q_ref[0, h], knt_ref[0, h]), vnt_ref[0, h], ckn_ref[0, h:h + 1, :], col <= row)
            o_ref[0, h] = (acc_ref[h] / l_ref[h]).astype(o_ref.dtype)


def _fox_sample(q, knt, vnt, kct, vct, cq, ckp, ckn, tk):
    DB, H, n, dh = q.shape
    P = kct.shape[-1]
    per_b = lambda a: pl.BlockSpec((1,) + a.shape[1:], lambda b, j: (b,) + (0,) * (a.ndim - 1))
    cache = pl.BlockSpec((1, H, dh, tk), lambda b, j: (b, 0, 0, j))
    return pl.pallas_call(
        functools.partial(_fox_sample_kernel, n=n, heads=H),
        grid=(DB, P // tk),
        in_specs=[per_b(q), per_b(knt), per_b(vnt), cache, cache, per_b(cq),
                  pl.BlockSpec((1, H, tk), lambda b, j: (b, 0, j)), per_b(ckn)],
        out_specs=per_b(q),
        out_shape=jax.ShapeDtypeStruct(q.shape, BF16),
        scratch_shapes=[pltpu.VMEM((H, n, tk), F32), pltpu.VMEM((H, n, 1), F32), pltpu.VMEM((H, n, 1), F32),
                        pltpu.VMEM((H, n, dh), F32)],
        compiler_params=_cparams("parallel", "arbitrary"),
        name="fox_sample",
    )(q, knt, vnt, kct, vct, cq, ckp, ckn)


def _sb_sample_kernel(q_ref, knt_ref, vnt_ref, kct_ref, vct_ref, o_ref, z_ref, run_ref, acc_ref, *, n, heads, sub):
    kt = pl.program_id(1)
    upper = _strict_upper(sub)

    def absorb(h, z, v_t, upper_m, width, mask=None):
        log_beta, log_rest = _log2_sigmoid_pair(z)
        if mask is not None:
            log_beta = jnp.where(mask, log_beta, NEG_INF)
            log_rest = jnp.where(mask, log_rest, 0.0)
        hi, lo = _split_bf16(log_rest, 2)
        run = run_ref[h]
        parts = []
        for c in reversed(range(z.shape[1] // width)):
            keys = slice(c * width, (c + 1) * width)
            later = _dot(hi[:, keys], upper_m) + _dot(lo[:, keys], upper_m)
            parts.append(jnp.exp2(log_beta[:, keys] + later + run).astype(BF16))
            run = run + jnp.sum(log_rest[:, keys], axis=-1, keepdims=True)
        a = parts[0] if len(parts) == 1 else jnp.concatenate(parts[::-1], axis=1)
        acc_ref[h] += _dot_nt(a, v_t)
        run_ref[h] = run

    def logits(h, _=None):
        z_ref[h] = _dot(q_ref[0, h], kct_ref[0, h].astype(BF16))

    @pl.when(kt == 0)
    def _():
        row = lax.broadcasted_iota(jnp.int32, (n, n), 0)
        col = lax.broadcasted_iota(jnp.int32, (n, n), 1)
        upper_n = _strict_upper(n)
        run_ref[...] = jnp.zeros_like(run_ref)
        acc_ref[...] = jnp.zeros_like(acc_ref)
        for h in range(heads):
            absorb(h, _dot(q_ref[0, h], knt_ref[0, h]), vnt_ref[0, h], upper_n, n, col < row)

    logits(0)
    logits(1)
    for h in range(heads):
        _pipeline_ahead(logits, h + 2, heads, None, None)
        absorb(h, z_ref[h], vct_ref[0, h].astype(BF16), upper, sub)

    @pl.when(kt == pl.num_programs(1) - 1)
    def _():
        o_ref[0] = acc_ref[...].astype(o_ref.dtype)


def _sb_sample(q, knt, vnt, kct, vct, tk, sub):
    DB, H, n, dh = q.shape
    P = kct.shape[-1]
    nk = P // tk
    per_b = lambda a: pl.BlockSpec((1,) + a.shape[1:], lambda b, j: (b,) + (0,) * (a.ndim - 1))
    cache = pl.BlockSpec((1, H, dh, tk), lambda b, j: (b, 0, 0, nk - 1 - j))
    return pl.pallas_call(
        functools.partial(_sb_sample_kernel, n=n, heads=H, sub=sub),
        grid=(DB, nk),
        in_specs=[per_b(q), per_b(knt), per_b(vnt), cache, cache],
        out_specs=per_b(q),
        out_shape=jax.ShapeDtypeStruct(q.shape, BF16),
        scratch_shapes=[pltpu.VMEM((H, n, tk), F32), pltpu.VMEM((H, n, 1), F32), pltpu.VMEM((H, n, dh), F32)],
        compiler_params=_cparams("parallel", "arbitrary"),
        name="sb_sample",
    )(q, knt, vnt, kct, vct)


def _fox_sample_cum(lf_new, lf_past_t):
    DB, H, P = lf_past_t.shape
    n = lf_new.shape[1]
    L = -(-(P + n) // LANES) * LANES
    both = jnp.concatenate([lf_past_t.astype(F32), jnp.transpose(lf_new, (0, 2, 1)),
                            jnp.zeros((DB, H, L - P - n), F32)], axis=2)
    cum_t = _cumsum_rows(both) * LOG2E
    ckn = cum_t[:, :, P:P + n]
    return ckn[..., None], cum_t[:, :, :P], ckn


def _heads_major(a, heads, transpose_rows):
    DB, n, _ = a.shape
    a4 = a.reshape(DB, n, heads, -1)
    return jnp.transpose(a4, (0, 2, 3, 1) if transpose_rows else (0, 2, 1, 3))


ROUTER_ROWS = 48


def _layer_norm(y, g, b):
    mu = jnp.mean(y, axis=-1, keepdims=True)
    yc = y - mu
    var = jnp.mean(yc * yc, axis=-1, keepdims=True)
    return yc * lax.rsqrt(var + LN_EPS) * g + b


def _first_argmax(v, ridx):
    vmax = jnp.max(v, axis=0, keepdims=True)
    idx = jnp.min(jnp.where(v == vmax, ridx, v.shape[0]), axis=0, keepdims=True)
    return vmax, idx


def _mix_out_kernel(o_ref, w_ref, x_ref, g_ref, b_ref, wrh_ref, wrl_ref, rb_ref, h_ref, ids_ref, wts_ref,
                    *, alpha, n_groups):
    h = _layer_norm(alpha * x_ref[...] + _dot(o_ref[...], w_ref[...]), g_ref[...], b_ref[...])
    h_ref[...] = h
    hh, hl = _split_bf16(h, 2)
    wrh = wrh_ref[...]
    low = _dot_nt(wrl_ref[...], hh)
    lg = _dot_nt(wrh, hh) + (_dot_nt(wrh, hl) + (low[:ROUTER_ROWS] + low[ROUTER_ROWS:])) + rb_ref[...]
    tm = lg.shape[1]
    ridx = lax.broadcasted_iota(jnp.int32, (8, tm), 0)
    g = jnp.where(ridx < n_groups, lg[0:8], NEG_INF)
    gmax, gidx = _first_argmax(g, ridx)
    gate = 1.0 / jnp.sum(jnp.exp(g - gmax), axis=0, keepdims=True)
    esel = lg[8:16]
    for gg in range(1, n_groups):
        esel = jnp.where(gidx == gg, lg[8 + 8 * gg:16 + 8 * gg], esel)
    v1, i1 = _first_argmax(esel, ridx)
    v2, i2 = _first_argmax(jnp.where(ridx == i1, NEG_INF, esel), ridx)
    t = jnp.exp(v2 - v1)
    w1 = 1.0 / (1.0 + t)
    ids_ref[...] = jnp.where(ridx == 0, gidx * 8 + i1, jnp.where(ridx == 1, gidx * 8 + i2, 0))
    wts_ref[...] = jnp.where(ridx == 0, gate * w1, jnp.where(ridx == 1, gate * (t * w1), 0.0))


def _mix_out(o, w, x, g, b, wrh, wrl, rb, alpha, n_groups):
    T, D = x.shape
    tm = _pick_tile(T, 256, LANES)
    const = lambda a: pl.BlockSpec(a.shape, lambda i: (0,) * a.ndim)
    rb_t = jnp.broadcast_to(rb, (ROUTER_ROWS, tm))
    return pl.pallas_call(
        functools.partial(_mix_out_kernel, alpha=alpha, n_groups=n_groups),
        grid=(T // tm,),
        in_specs=[pl.BlockSpec((tm, o.shape[1]), lambda i: (i, 0)), const(w),
                  pl.BlockSpec((tm, D), lambda i: (i, 0)), const(g), const(b), const(wrh), const(wrl), const(rb_t)],
        out_specs=[pl.BlockSpec((tm, D), lambda i: (i, 0)), pl.BlockSpec((8, tm), lambda i: (0, i)),
                   pl.BlockSpec((8, tm), lambda i: (0, i))],
        out_shape=[jax.ShapeDtypeStruct((T, D), F32), jax.ShapeDtypeStruct((8, T), jnp.int32),
                   jax.ShapeDtypeStruct((8, T), F32)],
        compiler_params=_cparams("parallel"),
        name="mix_out_ln_router",
    )(o, w, x, g, b, wrh, wrl, rb_t)


def _prep_router(w_group, b_group, w_router, b_router):
    D, G = w_group.shape
    E = w_router.shape[-1]
    wr = jnp.zeros((ROUTER_ROWS, D), F32)
    wr = wr.at[:G].set(w_group.T.astype(F32))
    wr = wr.at[8:8 + G * E].set(jnp.transpose(w_router, (0, 2, 1)).reshape(G * E, D).astype(F32))
    rb = jnp.zeros((ROUTER_ROWS, 1), F32)
    rb = rb.at[:G, 0].set(b_group.astype(F32)).at[8:8 + G * E, 0].set(b_router.reshape(-1).astype(F32))
    hi, mid, lo = _split_bf16_trunc(wr, 3)
    return hi, jnp.concatenate([mid, lo], axis=0), rb


def _gather_rows(idx_ref, n, src_hbm, dst, sem):
    def body(r, carry):
        pltpu.make_async_copy(src_hbm.at[pl.ds(idx_ref[0, 0, r], 1)], dst.at[pl.ds(r, 1)], sem).start()
        return carry
    lax.fori_loop(0, n, body, 0, unroll=8)


def _wait_rows(n, src_hbm, dst, sem):
    pltpu.make_async_copy(src_hbm.at[pl.ds(0, n)], dst, sem).wait()


def _moe_experts_kernel(te_ref, tv_ref, src_ref, nxt_ref, x_hbm, wg_ref, wu_ref, wd_ref, y_ref, xbuf, sem, *, tm):
    i = pl.program_id(0)
    nt = pl.num_programs(0)
    slot = i % 2

    @pl.when(jnp.logical_and(i == 0, tv_ref[0] > 0))
    def _():
        _gather_rows(src_ref, tm, x_hbm, xbuf.at[0], sem.at[0])

    @pl.when(jnp.logical_and(i + 1 < nt, tv_ref[jnp.minimum(i + 1, nt - 1)] > 0))
    def _():
        _gather_rows(nxt_ref, tm, x_hbm, xbuf.at[1 - slot], sem.at[1 - slot])

    @pl.when(tv_ref[i] > 0)
    def _():
        _wait_rows(tm, x_hbm, xbuf.at[slot], sem.at[slot])
        xb = xbuf[slot].astype(BF16)
        a = _dot(xb, wg_ref[0].astype(BF16))
        u = _dot(xb, wu_ref[0].astype(BF16))
        hid = (a / (1.0 + jnp.exp(-a))) * u
        y_ref[...] = _dot(hid.astype(BF16), wd_ref[0].astype(BF16))

    @pl.when(tv_ref[i] == 0)
    def _():
        y_ref[...] = jnp.zeros_like(y_ref)


def _moe_experts(x, w_gate, w_up, w_down, tile_expert, tile_valid, src, tm):
    T, D = x.shape
    F = w_gate.shape[-1]
    NT = tile_expert.shape[0]
    grid_spec = pltpu.PrefetchScalarGridSpec(
        num_scalar_prefetch=2,
        grid=(NT,),
        in_specs=[pl.BlockSpec((1, 1, tm), lambda i, te, tv: (i, 0, 0), memory_space=pltpu.SMEM),
                  pl.BlockSpec((1, 1, tm), lambda i, te, tv: (jnp.minimum(i + 1, NT - 1), 0, 0),
                               memory_space=pltpu.SMEM),
                  pl.BlockSpec(memory_space=pl.ANY),
                  pl.BlockSpec((1, D, F), lambda i, te, tv: (te[i], 0, 0)),
                  pl.BlockSpec((1, D, F), lambda i, te, tv: (te[i], 0, 0)),
                  pl.BlockSpec((1, F, D), lambda i, te, tv: (te[i], 0, 0))],
        out_specs=pl.BlockSpec((tm, D), lambda i, te, tv: (i, 0)),
        scratch_shapes=[pltpu.VMEM((2, tm, D), F32), pltpu.SemaphoreType.DMA((2,))],
    )
    return pl.pallas_call(
        functools.partial(_moe_experts_kernel, tm=tm),
        grid_spec=grid_spec,
        out_shape=jax.ShapeDtypeStruct((NT * tm, D), F32),
        compiler_params=_cparams("arbitrary"),
        name="moe_experts",
    )(tile_expert, tile_valid, src, src, x, w_gate, w_up, w_down)


def _moe_combine_kernel(pos_ref, nxt_ref, ys_hbm, h_ref, w_ref, g_ref, b_ref, o_ref, buf, sem, *, tm, alpha):
    i = pl.program_id(0)
    nt = pl.num_programs(0)
    slot = i % 2

    @pl.when(i == 0)
    def _():
        _gather_rows(pos_ref, 2 * tm, ys_hbm, buf.at[0], sem.at[0])

    @pl.when(i + 1 < nt)
    def _():
        _gather_rows(nxt_ref, 2 * tm, ys_hbm, buf.at[1 - slot], sem.at[1 - slot])

    _wait_rows(2 * tm, ys_hbm, buf.at[slot], sem.at[slot])
    w = w_ref[...]
    y = alpha * h_ref[...] + (w[:, 0:1] * buf[slot, 0:tm] + w[:, 1:2] * buf[slot, tm:2 * tm])
    o_ref[...] = _layer_norm(y, g_ref[...], b_ref[...])


def _moe_combine(ys, h, pos, wts, g, b, alpha, tm):
    T, D = h.shape
    nt = T // tm
    const = lambda a: pl.BlockSpec(a.shape, lambda i: (0,) * a.ndim)
    return pl.pallas_call(
        functools.partial(_moe_combine_kernel, tm=tm, alpha=alpha),
        grid=(nt,),
        in_specs=[pl.BlockSpec((1, 1, 2 * tm), lambda i: (i, 0, 0), memory_space=pltpu.SMEM),
                  pl.BlockSpec((1, 1, 2 * tm), lambda i: (jnp.minimum(i + 1, nt - 1), 0, 0),
                               memory_space=pltpu.SMEM),
                  pl.BlockSpec(memory_space=pl.ANY),
                  pl.BlockSpec((tm, D), lambda i: (i, 0)),
                  pl.BlockSpec((tm, 2), lambda i: (i, 0)), const(g), const(b)],
        out_specs=pl.BlockSpec((tm, D), lambda i: (i, 0)),
        out_shape=jax.ShapeDtypeStruct((T, D), F32),
        scratch_shapes=[pltpu.VMEM((2, 2 * tm, D), F32), pltpu.SemaphoreType.DMA((2,))],
        compiler_params=_cparams("arbitrary"),
        name="moe_combine_ln",
    )(pos, pos, ys, h, wts, g, b)


def _route(ids, n_experts, tm):
    T = ids.shape[1]
    flat = ids.reshape(-1)
    iota = jnp.arange(2 * T, dtype=jnp.int32)
    sorted_e, order = lax.sort((flat, iota), num_keys=1, is_stable=True)
    _, inverse = lax.sort((order, iota), num_keys=1)
    experts = jnp.arange(n_experts, dtype=jnp.int32)
    counts = jnp.sum((flat[:, None] == experts[None, :]).astype(jnp.int32), axis=0)
    padded = (counts + tm - 1) // tm * tm
    ends = jnp.cumsum(padded)
    shift = (ends - padded) - (jnp.cumsum(counts) - counts)
    NT = (2 * T + n_experts * (tm - 1)) // tm
    tile_start = jnp.arange(NT, dtype=jnp.int32) * tm
    tile_expert = jnp.minimum(jnp.sum((tile_start[:, None] >= ends[None, :]).astype(jnp.int32), axis=1),
                              n_experts - 1)
    tile_valid = (tile_start < ends[-1]).astype(jnp.int32)
    pos = (inverse + shift[flat]).reshape(2, T)
    row = jnp.arange(NT * tm, dtype=jnp.int32)
    src = (order % T)[jnp.clip(row - jnp.repeat(shift[tile_expert], tm), 0, 2 * T - 1)]
    return tile_expert, tile_valid, src.reshape(NT, 1, tm), pos


MLA_PAIR_W = 2 * LANES
QK_NOPE = 64
QK_ROPE = 32


def _mla_proj_kernel(h_ref, wdn_ref, gq_ref, gkv_ref, wq_ref, wqr_ref, wk_ref, wv_ref, cos_ref, sin_ref,
                     ckv_ref, kr_ref, qcat_ref, kcat_ref, vt_ref, *, q_lora, kv_lora, npairs, scale):
    z = _dot(h_ref[...].astype(BF16), wdn_ref[...])
    cq = z[:, :q_lora]
    ckv = z[:, q_lora:q_lora + kv_lora]
    o = q_lora + kv_lora
    kr_raw = z[:, o:o + HEAD_W]
    kr_rot = z[:, o + HEAD_W:o + 2 * HEAD_W]
    cq = cq * lax.rsqrt(jnp.mean(cq * cq, axis=-1, keepdims=True) + RMS_EPS) * gq_ref[...]
    ckv = ckv * lax.rsqrt(jnp.mean(ckv * ckv, axis=-1, keepdims=True) + RMS_EPS) * gkv_ref[...]
    ckv_ref[...] = ckv
    cos = cos_ref[...]
    sin = sin_ref[...]
    kr_tile = kr_raw * cos + kr_rot * sin
    kr_ref[...] = kr_tile[:, QK_NOPE:QK_NOPE + QK_ROPE]
    cqb = cq.astype(BF16)
    ckb = ckv.astype(BF16)
    cos2 = jnp.concatenate([cos, cos], axis=1)
    sin2 = jnp.concatenate([sin, sin], axis=1)
    kr2 = jnp.concatenate([kr_tile, kr_tile], axis=1)
    for p in range(npairs):
        lanes = slice(p * MLA_PAIR_W, (p + 1) * MLA_PAIR_W)
        q = _dot(cqb, wq_ref[:, lanes]) * cos2 + _dot(cqb, wqr_ref[:, lanes]) * sin2
        qcat_ref[:, lanes] = (q * scale).astype(BF16)
        kcat_ref[:, lanes] = (_dot(ckb, wk_ref[:, lanes]) + kr2).astype(BF16)
    vt_ref[0] = _dot_nt(wv_ref[...], ckb).reshape(vt_ref.shape[1:]).astype(BF16)


def _mla_proj(h, wdn, gq, gkv, wq, wqr, wk, wv, cos_t, sin_t, table_block, tm):
    T, D = h.shape
    q_lora, kv_lora = gq.shape[1], gkv.shape[1]
    npairs = wq.shape[1] // MLA_PAIR_W
    const = lambda a: pl.BlockSpec(a.shape, lambda i: (0,) * a.ndim)
    row = lambda w_: pl.BlockSpec((tm, w_), lambda i: (i, 0))
    table = pl.BlockSpec((tm, HEAD_W), lambda i: (table_block(i), 0))
    return pl.pallas_call(
        functools.partial(_mla_proj_kernel, q_lora=q_lora, kv_lora=kv_lora, npairs=npairs,
                          scale=(QK_NOPE + QK_ROPE) ** -0.5 * LOG2E),
        grid=(T // tm,),
        in_specs=[row(D), const(wdn), const(gq), const(gkv), const(wq), const(wqr), const(wk), const(wv),
                  table, table],
        out_specs=[row(kv_lora), row(QK_ROPE), row(wq.shape[1]), row(wk.shape[1]),
                   pl.BlockSpec((1, 2 * npairs, HEAD_DIM, tm), lambda i: (i, 0, 0, 0))],
        out_shape=[jax.ShapeDtypeStruct((T, kv_lora), F32), jax.ShapeDtypeStruct((T, QK_ROPE), F32),
                   jax.ShapeDtypeStruct((T, wq.shape[1]), BF16), jax.ShapeDtypeStruct((T, wk.shape[1]), BF16),
                   jax.ShapeDtypeStruct((T // tm, 2 * npairs, HEAD_DIM, tm), BF16)],
        compiler_params=_cparams("parallel"),
        name="mla_proj",
    )(h, wdn, gq, gkv, wq, wqr, wk, wv, cos_t, sin_t)


def _rot_half(w):
    half = w.shape[-1] // 2
    return jnp.concatenate([-w[..., half:], w[..., :half]], axis=-1)


def _prep_mla_weights(w_down, w_uq, w_ukv, heads, q_lora, kv_lora):
    D = w_down.shape[0]
    tail = HEAD_W - QK_NOPE - QK_ROPE
    w_kr = w_down[:, q_lora + kv_lora:]
    slot = lambda w: jnp.concatenate([jnp.zeros((D, QK_NOPE), w.dtype), w, jnp.zeros((D, tail), w.dtype)], axis=1)
    wdn = jnp.concatenate([w_down[:, :q_lora + kv_lora], slot(w_kr), slot(_rot_half(w_kr))], axis=1)
    wq3 = w_uq.reshape(q_lora, heads, QK_NOPE + QK_ROPE)
    nope, ropew = wq3[..., :QK_NOPE], wq3[..., QK_NOPE:]
    zpad = jnp.zeros((q_lora, heads, tail), w_uq.dtype)
    wq = jnp.concatenate([nope, ropew, zpad], axis=-1)
    wqr = jnp.concatenate([jnp.zeros_like(nope), _rot_half(ropew), zpad], axis=-1)
    wkv3 = w_ukv.reshape(kv_lora, heads, QK_NOPE + HEAD_DIM)
    w_uk, w_uv = wkv3[..., :QK_NOPE], wkv3[..., QK_NOPE:]
    wk = jnp.concatenate([w_uk, jnp.zeros((kv_lora, heads, HEAD_W - QK_NOPE), w_ukv.dtype)], axis=-1)
    wv = jnp.transpose(w_uv.reshape(kv_lora, heads * HEAD_DIM))
    b16 = lambda a: a.astype(BF16)
    return (b16(wdn), b16(wq.reshape(q_lora, -1)), b16(wqr.reshape(q_lora, -1)), b16(wk.reshape(kv_lora, -1)),
            b16(wv), b16(jnp.transpose(w_uk, (1, 2, 0))), b16(jnp.transpose(w_uv, (1, 0, 2))))


def _rope_tables(pos):
    half = QK_ROPE // 2
    inv_freq = ROPE_BASE ** (-jnp.arange(half, dtype=F32) / half)
    ang = pos.astype(F32)[:, None] * inv_freq[None, :]
    n = pos.shape[0]
    pad = jnp.zeros((n, HEAD_W - QK_NOPE - QK_ROPE), F32)
    cos = jnp.concatenate([jnp.ones((n, QK_NOPE), F32)] + [jnp.cos(ang)] * 2 + [pad], axis=1)
    sin = jnp.concatenate([jnp.zeros((n, QK_NOPE), F32)] + [jnp.sin(ang)] * 2 + [pad], axis=1)
    return cos, sin


def _mla_sample_queries(qcat, heads):
    DB, n, _ = qcat.shape
    q4 = qcat.reshape(DB, n, heads, HEAD_W)
    rows = lambda a: jnp.transpose(a, (0, 2, 1, 3)).reshape(DB, heads * n, a.shape[-1])
    return rows(q4[..., :QK_NOPE]), rows(q4[..., QK_NOPE:QK_NOPE + QK_ROPE])


def _mla_sample_kernel(qn_ref, qr_ref, wuk_ref, wuv_ref, cc_ref, rc_ref, cn_ref, rn_ref, o_ref,
                       qlat_ref, m_ref, l_ref, acc_ref, *, n, heads):
    kt = pl.program_id(1)

    @pl.when(kt == 0)
    def _():
        for h in range(heads):
            rows = slice(h * n, (h + 1) * n)
            qlat_ref[rows, :] = _dot(qn_ref[0, rows, :], wuk_ref[h]).astype(BF16)
        m_ref[...] = jnp.full(m_ref.shape, NEG_INF, F32)
        l_ref[...] = jnp.zeros_like(l_ref)
        acc_ref[...] = jnp.zeros_like(acc_ref)

    def update(ckv, kr_t):
        s = _dot_nt(qlat_ref[...], ckv) + _dot(qr_ref[0], kr_t)
        m, l, acc = _softmax_step(s[None], ckv, (m_ref[...], l_ref[...], acc_ref[...]))
        m_ref[...] = m
        l_ref[...] = l
        acc_ref[...] = acc

    update(cc_ref[0].astype(BF16), rc_ref[0].astype(BF16))

    @pl.when(kt == pl.num_programs(1) - 1)
    def _():
        update(cn_ref[0], rn_ref[0])
        o_lat = (acc_ref[0] / l_ref[0]).astype(BF16)
        for h in range(heads):
            o_ref[0, :, h * HEAD_DIM:(h + 1) * HEAD_DIM] = _dot(o_lat[h * n:(h + 1) * n], wuv_ref[h]).astype(o_ref.dtype)


def _mla_sample(qn, qr, wuk, wuv, ckv_c, kr_c, ckv_n, kr_n, n, tk):
    DB, R, _ = qn.shape
    heads = R // n
    P, C = ckv_c.shape[1], ckv_c.shape[2]
    const = lambda a: pl.BlockSpec(a.shape, lambda b, j: (0,) * a.ndim)
    per_b = lambda a: pl.BlockSpec((1,) + a.shape[1:], lambda b, j: (b, 0, 0))
    return pl.pallas_call(
        functools.partial(_mla_sample_kernel, n=n, heads=heads),
        grid=(DB, P // tk),
        in_specs=[per_b(qn), per_b(qr), const(wuk), const(wuv),
                  pl.BlockSpec((1, tk, C), lambda b, j: (b, j, 0)),
                  pl.BlockSpec((1, QK_ROPE, tk), lambda b, j: (b, 0, j)),
                  per_b(ckv_n), per_b(kr_n)],
        out_specs=pl.BlockSpec((1, n, heads * HEAD_DIM), lambda b, j: (b, 0, 0)),
        out_shape=jax.ShapeDtypeStruct((DB, n, heads * HEAD_DIM), BF16),
        scratch_shapes=[pltpu.VMEM((R, C), BF16), pltpu.VMEM((1, R, 1), F32), pltpu.VMEM((1, R, 1), F32),
                        pltpu.VMEM((1, R, C), F32)],
        compiler_params=_cparams("parallel", "arbitrary"),
        name="mla_sample",
    )(qn, qr, wuk, wuv, ckv_c, kr_c, ckv_n, kr_n)


def _moe_layer(h, ids, wts, w_gate, w_up, w_down, g, b, alpha, tm):
    T, D = h.shape
    n_experts = w_gate.shape[0] * w_gate.shape[1]
    tile_expert, tile_valid, src, pos = _route(ids[:2], n_experts, tm)
    flat3 = lambda w: w.reshape((n_experts,) + w.shape[2:])
    ys = _moe_experts(h, flat3(w_gate), flat3(w_up), flat3(w_down), tile_expert, tile_valid, src, tm)
    pos_t = jnp.transpose(pos.reshape(2, T // tm, tm), (1, 0, 2)).reshape(T // tm, 1, 2 * tm)
    return _moe_combine(ys, h, pos_t, jnp.transpose(wts[:2]), g, b, alpha, tm)


TOKEN_TILE = 256
FLASH_Q_TILE = 512
FLASH_KEY_TILE = 256
SB_Q_TILE = 256
SB_KEY_TILE = 128
CACHE_TILE = 1024
SB_SUB_TILE = 256


def kernel(x_prompt, x_sample, cache_fox_k, cache_fox_v, cache_fox_logf, cache_sb_k, cache_sb_v, cache_mla_ckv, cache_mla_krope, ab_w_in, ab_b_forget, ab_w_out, mla_w_down, mla_g_q, mla_g_kv, mla_w_uq, mla_w_ukv, mla_w_out, moe_w_group, moe_b_group, moe_w_router, moe_b_router, moe_w_gate, moe_w_up, moe_w_down, ln_g, ln_b):
    B, S, D = x_prompt.shape
    DB, n, _ = x_sample.shape
    P = cache_fox_k.shape[2]
    TP, TS = B * S, DB * n
    depth = ln_g.shape[0]
    n_groups = moe_w_group.shape[-1]
    assert depth == 2 and ab_w_in.shape[0] == 1 and mla_w_down.shape[0] == 1
    assert S % FLASH_Q_TILE == 0 and S % SB_Q_TILE == 0 and TP % TOKEN_TILE == 0 and TS % TOKEN_TILE == 0 and TOKEN_TILE % n == 0
    assert P % CACHE_TILE == 0 and P % CHUNK == 0 and n == CHUNK
    alpha = (2 * depth) ** 0.25
    tk = CACHE_TILE

    x = jnp.concatenate([x_prompt.reshape(TP, D), x_sample.reshape(TS, D)], axis=0)
    prompt3 = lambda a: a[:TP].reshape(B, S, -1)
    sample3 = lambda a: a[TP:].reshape(DB, n, -1)

    def ffn(o, w_out, resid, layer):
        wrh, wrl, rb = _prep_router(moe_w_group[layer], moe_b_group[layer], moe_w_router[layer], moe_b_router[layer])
        h, ids, wts = _mix_out(o, w_out.astype(BF16), resid, ln_g[layer, 0][None], ln_b[layer, 0][None],
                               wrh, wrl, rb, alpha, n_groups)
        return _moe_layer(h, ids, wts, moe_w_gate[layer], moe_w_up[layer], moe_w_down[layer],
                          ln_g[layer, 1][None], ln_b[layer, 1][None], alpha, TOKEN_TILE)

    fox_heads = ab_b_forget.shape[1]
    hw = (ab_w_in.shape[2] - fox_heads) // 6
    w_ab, b_forget = _prep_ab_weights(ab_w_in[0], ab_b_forget[0])
    seg = lambda j: w_ab[:, j * hw:(j + 1) * hw]
    w_tok = jnp.concatenate([seg(0), seg(1), seg(3), seg(4), w_ab[:, 6 * hw:]], axis=1)
    w_t = jnp.transpose(jnp.concatenate([seg(1), seg(2), seg(4), seg(5)], axis=1))
    (qa_p, ka_p, qb_p, kb_p, lf_p, kat, vat, kbt, vbt, vat16, vbt16) = _ab_proj_prompt(
        x_prompt.reshape(TP, D), w_tok, b_forget, w_t, B, S, fox_heads, SB_KEY_TILE)
    bsw = lambda a: a.reshape(B, S, -1)
    qx, kx = _fox_prompt_operands(bsw(qa_p), bsw(ka_p), lf_p, fox_heads)
    o_fox_p = _flash_prompt(qx, kx, vat16, FLASH_Q_TILE, fox_heads, 1, "fox_prompt")
    o_sb_p = _sb_prompt(_pad_heads(bsw(qb_p)), _pad_heads(bsw(kb_p)), vbt16, SB_Q_TILE)
    qa, ka, va, qb, kb, vb, ka16, va16, kb16, vb16, lf = _ab_proj(x_sample.reshape(TS, D), w_ab, b_forget, hw)
    dbn = lambda a: a.reshape(DB, n, -1)
    lf_s = dbn(lf)[:, :, :fox_heads]
    cq_s, ck_past, ck_new = _fox_sample_cum(lf_s, jnp.transpose(cache_fox_logf[0], (0, 2, 1)))
    cache_t = lambda c: jnp.transpose(c[0], (0, 2, 3, 1))
    hm = lambda a, t=False: _heads_major(dbn(a), fox_heads, t)
    o_fox_s = _fox_sample(hm(qa), hm(ka16, True), hm(va16, True), cache_t(cache_fox_k), cache_t(cache_fox_v),
                          cq_s, ck_past, ck_new, tk)
    o_sb_s = _sb_sample(hm(qb), hm(kb16, True), hm(vb16, True), cache_t(cache_sb_k), cache_t(cache_sb_v),
                        tk, SB_SUB_TILE)
    tokens_major = lambda a: jnp.transpose(a, (0, 2, 1, 3)).reshape(TS, hw)
    o = jnp.concatenate([jnp.concatenate([o_fox_p, o_sb_p], axis=-1).reshape(TP, 2 * hw),
                         jnp.concatenate([tokens_major(o_fox_s), tokens_major(o_sb_s)], axis=-1)], axis=0)
    x = ffn(o, ab_w_out[0], x, 0)

    q_lora, kv_lora = mla_g_q.shape[1], mla_g_kv.shape[1]
    heads = mla_w_uq.shape[2] // (QK_NOPE + QK_ROPE)
    wdn, wq, wqr, wk, wv, wuk_t, wuv = _prep_mla_weights(mla_w_down[0], mla_w_uq[0], mla_w_ukv[0], heads, q_lora, kv_lora)
    tm = TOKEN_TILE
    pos = jnp.concatenate([jnp.arange(S, dtype=jnp.int32), P + jnp.arange(tm, dtype=jnp.int32) % n])
    cos_t, sin_t = _rope_tables(pos)
    blocks_per_seq, prompt_blocks = S // tm, TP // tm
    table_block = lambda i: jnp.where(i < prompt_blocks, i % blocks_per_seq, blocks_per_seq)
    assert tm == FLASH_KEY_TILE
    ckv, kr, qcat, kcat, vt = _mla_proj(x, wdn, mla_g_q[0][None], mla_g_kv[0][None], wq, wqr, wk, wv,
                                        cos_t, sin_t, table_block, tm)
    o_p = _flash_prompt(prompt3(qcat), prompt3(kcat), vt, FLASH_Q_TILE, 8, CHUNK, "mla_prompt")
    qn, qr = _mla_sample_queries(sample3(qcat), heads)
    o_s = _mla_sample(qn, qr, wuk_t, wuv, cache_mla_ckv[0], jnp.transpose(cache_mla_krope[0], (0, 2, 1)),
                      sample3(ckv).astype(BF16), jnp.transpose(sample3(kr), (0, 2, 1)).astype(BF16), n, tk)
    o = jnp.concatenate([o_p.reshape(TP, -1), o_s.reshape(TS, -1)], axis=0)
    x = ffn(o, mla_w_out[0], x, 1)

    rows_p = lambda a: jnp.transpose(a, (0, 3, 1, 2))[None]
    rows_s = lambda a: a.reshape(1, DB, n, fox_heads, hw // fox_heads)
    pr, sr = slice(0, TP), slice(TP, TP + TS)
    return (x[pr].reshape(B, S, D), x[sr].reshape(DB, n, D),
            rows_p(kat), rows_p(vat), lf_p[:, :fox_heads].reshape(1, B, S, fox_heads), rows_p(kbt), rows_p(vbt),
            ckv[pr].reshape(1, B, S, kv_lora), kr[pr].reshape(1, B, S, QK_ROPE),
            rows_s(ka), rows_s(va), lf[:, :fox_heads].reshape(1, DB, n, fox_heads), rows_s(kb), rows_s(vb),
            ckv[sr].reshape(1, DB, n, kv_lora), kr[sr].reshape(1, DB, n, QK_ROPE))
```

```python
import functools

import jax
import jax.numpy as jnp
from jax import lax
from jax.experimental import pallas as pl
from jax.experimental.pallas import tpu as pltpu

F32 = jnp.float32
BF16 = jnp.bfloat16
NEG_INF = -1e30
LOG2E = 1.4426950408889634

LANES = 128
HEAD_DIM = 64
PAIR_W = 2 * HEAD_DIM
CHUNK = 64
LN_EPS = 1e-5
RMS_EPS = 1e-6
ROPE_BASE = 10000.0
VMEM_LIMIT = 56 * 1024 * 1024


def _cparams(*sem):
    return pltpu.CompilerParams(dimension_semantics=sem, vmem_limit_bytes=VMEM_LIMIT)


def _dot(a, b):
    return jnp.dot(a, b, preferred_element_type=F32)


def _dot_nt(a, b):
    return lax.dot_general(a, b, (((1,), (1,)), ((), ())), preferred_element_type=F32)


def _split_bf16(x, parts):
    out = []
    r = x
    for _ in range(parts):
        h = r.astype(BF16)
        out.append(h)
        r = r - h.astype(F32)
    return out


def _split_bf16_trunc(x, parts):
    out = []
    r = x
    for _ in range(parts):
        bits = lax.bitcast_convert_type(r, jnp.uint32) & jnp.uint32(0xFFFF0000)
        h = lax.bitcast_convert_type(bits, F32)
        out.append(h.astype(BF16))
        r = r - h
    return out


def _log_sigmoid(x):
    return jnp.minimum(x, 0.0) - jnp.log(1.0 + jnp.exp(-jnp.abs(x)))


def _pick_tile(n, pref, mult=8):
    t = min(pref, n)
    while n % t or t % mult:
        t -= 1
    return t


def _ab_proj_kernel(x_ref, w_ref, bf_ref, qa_ref, ka_ref, va_ref, qb_ref, kb_ref, vb_ref,
                    ka16_ref, va16_ref, kb16_ref, vb16_ref, lf_ref, *, hw, qscale):
    xb = x_ref[...].astype(BF16)

    def seg(j):
        return _dot(xb, w_ref[:, j * hw:(j + 1) * hw])

    qa_ref[...] = (seg(0) * qscale).astype(BF16)
    z = seg(1)
    ka_ref[...] = z
    ka16_ref[...] = z.astype(BF16)
    z = seg(2)
    va_ref[...] = z
    va16_ref[...] = z.astype(BF16)
    qb_ref[...] = (seg(3) * qscale).astype(BF16)
    z = seg(4)
    kb_ref[...] = z
    kb16_ref[...] = z.astype(BF16)
    z = seg(5)
    vb_ref[...] = z
    vb16_ref[...] = z.astype(BF16)
    f = _dot(xb, w_ref[:, 6 * hw:6 * hw + LANES]) + bf_ref[...]
    lf_ref[...] = _log_sigmoid(f)


def _ab_proj(x, w, bf, hw):
    T, D = x.shape
    tm = _pick_tile(T, 256)
    row = lambda w_: pl.BlockSpec((tm, w_), lambda i: (i, 0))
    f32o = jax.ShapeDtypeStruct((T, hw), F32)
    b16o = jax.ShapeDtypeStruct((T, hw), BF16)
    return pl.pallas_call(
        functools.partial(_ab_proj_kernel, hw=hw, qscale=HEAD_DIM ** -0.5 * LOG2E),
        grid=(T // tm,),
        in_specs=[row(D), pl.BlockSpec(w.shape, lambda i: (0, 0)), pl.BlockSpec(bf.shape, lambda i: (0, 0))],
        out_specs=[row(hw)] * 10 + [row(LANES)],
        out_shape=[b16o, f32o, f32o, b16o, f32o, f32o, b16o, b16o, b16o, b16o,
                   jax.ShapeDtypeStruct((T, LANES), F32)],
        compiler_params=_cparams("parallel"),
        name="ab_proj",
    )(x, w, bf)


def _ab_proj_prompt_kernel(x_ref, w_ref, bf_ref, wt_ref, qone_ref, qa_ref, ka_ref, qb_ref, kb_ref, lf_ref,
                           kat_ref, vat_ref, kbt_ref, vbt_ref, vat16_ref, vbt16_ref, *, hw, heads, qscale, sb_tk):
    xb = x_ref[...].astype(BF16)
    tm = xb.shape[0]
    wide = heads * HEAD_W

    def seg(j):
        return _dot(xb, w_ref[:, j * wide:(j + 1) * wide])

    def seg_t(j):
        return _dot_nt(wt_ref[j * hw:(j + 1) * hw, :], xb).reshape(heads, hw // heads, tm)

    qa_ref[...] = (seg(0) * qscale + qone_ref[...]).astype(BF16)
    ka_ref[...] = seg(1).astype(BF16)
    qb_ref[...] = (seg(2) * qscale).astype(BF16)
    kb_ref[...] = seg(3).astype(BF16)
    lf_ref[...] = _log_sigmoid(_dot(xb, w_ref[:, 4 * wide:4 * wide + LANES]) + bf_ref[...])
    kat_ref[0] = seg_t(0)
    z = seg_t(1)
    vat_ref[0] = z
    vat16_ref[0, :, :HEAD_DIM, :] = z.astype(BF16)
    vat16_ref[0, :, HEAD_DIM:, :] = jnp.ones((heads, VT_ROWS - HEAD_DIM, tm), BF16)
    kbt_ref[0] = seg_t(2)
    z = seg_t(3)
    vbt_ref[0] = z
    for c in range(tm // sb_tk):
        vbt16_ref[c] = z[:, :, c * sb_tk:(c + 1) * sb_tk].astype(BF16)


def _ab_proj_prompt(x, w_tok, bf, w_t, q_ones, B, S, heads, sb_tk):
    TP, D = x.shape
    hw = w_t.shape[0] // 4
    wide = heads * HEAD_W
    tm = FLASH_KEY_TILE
    nj = S // tm
    const = lambda a: pl.BlockSpec(a.shape, lambda b, j: (0,) * a.ndim)
    row = lambda w_: pl.BlockSpec((tm, w_), lambda b, j: (b * nj + j, 0))
    t_spec = pl.BlockSpec((1, heads, hw // heads, tm), lambda b, j: (b, 0, 0, j))
    b16 = jax.ShapeDtypeStruct((TP, wide), BF16)
    t32 = jax.ShapeDtypeStruct((B, heads, hw // heads, S), F32)
    return pl.pallas_call(
        functools.partial(_ab_proj_prompt_kernel, hw=hw, heads=heads, qscale=HEAD_DIM ** -0.5 * LOG2E, sb_tk=sb_tk),
        grid=(B, nj),
        in_specs=[row(D), const(w_tok), const(bf), const(w_t), const(q_ones)],
        out_specs=[row(wide)] * 4 + [row(LANES)] + [t_spec] * 4 + [
            pl.BlockSpec((1, heads, VT_ROWS, tm), lambda b, j: (b * nj + j, 0, 0, 0)),
            pl.BlockSpec((tm // sb_tk, heads, hw // heads, sb_tk), lambda b, j: (b * nj + j, 0, 0, 0))],
        out_shape=[b16] * 4 + [jax.ShapeDtypeStruct((TP, LANES), F32)] + [t32] * 4 + [
            jax.ShapeDtypeStruct((TP // tm, heads, VT_ROWS, tm), BF16),
            jax.ShapeDtypeStruct((TP // sb_tk, heads, hw // heads, sb_tk), BF16)],
        compiler_params=_cparams("parallel", "parallel"),
        name="ab_proj_prompt",
    )(x, w_tok, bf, w_t, q_ones)


def _cumsum_kernel(x_ref, o_ref, carry_ref, *, tl):
    @pl.when(pl.program_id(1) == 0)
    def _():
        carry_ref[...] = jnp.zeros_like(carry_ref)

    x = x_ref[0]
    r = lax.broadcasted_iota(jnp.int32, (tl, tl), 0)
    c = lax.broadcasted_iota(jnp.int32, (tl, tl), 1)
    upper = (r <= c).astype(BF16)
    parts = jnp.concatenate(_split_bf16(x, 4), axis=0)
    y = _dot(parts, upper)
    cum = (y[0:8] + y[8:16]) + (y[16:24] + y[24:32]) + carry_ref[:, 0:1]
    o_ref[0] = cum
    carry_ref[...] = jnp.broadcast_to(cum[:, tl - 1:tl], carry_ref.shape)


def _cumsum_rows(x):
    B, H, L = x.shape
    tl = _pick_tile(L, 512, LANES)
    return pl.pallas_call(
        functools.partial(_cumsum_kernel, tl=tl),
        grid=(B, L // tl),
        in_specs=[pl.BlockSpec((1, H, tl), lambda b, j: (b, 0, j))],
        out_specs=pl.BlockSpec((1, H, tl), lambda b, j: (b, 0, j)),
        out_shape=jax.ShapeDtypeStruct((B, H, L), F32),
        scratch_shapes=[pltpu.VMEM((H, LANES), F32)],
        compiler_params=_cparams("parallel", "arbitrary"),
        name="cumsum_rows",
    )(x)


def _stack_pair(qp):
    lo = lax.broadcasted_iota(jnp.int32, qp.shape, 1) < HEAD_DIM
    zero = jnp.zeros_like(qp)
    return jnp.concatenate([jnp.where(lo, qp, zero), jnp.where(lo, zero, qp)], axis=0)


def _unstack_pair(o):
    lo = lax.broadcasted_iota(jnp.int32, o.shape[1:], 1) < HEAD_DIM
    return jnp.where(lo, o[0], o[1])


def _softmax_step(s, vb, carry):
    m, l, acc = carry
    two, tq, tk = s.shape
    m_new = jnp.maximum(m, jnp.max(s, axis=-1, keepdims=True))
    alpha = jnp.exp2(m - m_new)
    p = jnp.exp2(s - m_new)
    l = alpha * l + jnp.sum(p, axis=-1, keepdims=True)
    pv = _dot(p.reshape(two * tq, tk).astype(BF16), vb).reshape(two, tq, vb.shape[-1])
    return m_new, l, alpha * acc + pv


def _log2_sigmoid_pair(z2):
    l1 = jnp.log2(1.0 + jnp.exp2(-jnp.abs(z2)))
    return jnp.minimum(z2, 0.0) - l1, jnp.minimum(-z2, 0.0) - l1


def _softmax_init(tq, width):
    return (jnp.full((2, tq, 1), NEG_INF, F32), jnp.zeros((2, tq, 1), F32), jnp.zeros((2, tq, width), F32))


def _sb_step(z, vb, strict_upper, carry, mask=None):
    run, acc = carry
    log_beta, log_rest = _log2_sigmoid_pair(z)
    if mask is not None:
        log_beta = jnp.where(mask, log_beta, NEG_INF)
        log_rest = jnp.where(mask, log_rest, 0.0)
    hi, lo = _split_bf16(log_rest, 2)
    later = _dot(hi, strict_upper) + _dot(lo, strict_upper)
    a = jnp.exp2(log_beta + later + run)
    acc = acc + _dot(a.astype(BF16), vb)
    run = run + jnp.sum(log_rest, axis=-1, keepdims=True)
    return run, acc


def _strict_upper(tk):
    r = lax.broadcasted_iota(jnp.int32, (tk, tk), 0)
    c = lax.broadcasted_iota(jnp.int32, (tk, tk), 1)
    return (r > c).astype(BF16)


HEAD_W = LANES
VT_ROWS = HEAD_DIM + 16
SB_DEAD_LOG2 = -160.0


def _pipeline_ahead(stage, first, count, cur, nxt):
    if first < count:
        stage(first, cur)
    elif nxt is not None:
        stage(first - count, nxt)


def _flash_prompt_kernel(q_ref, k_ref, vt_ref, into_ref, o_ref, s_ref, m_ref, acc_ref, *, tq, tk, heads, chunk):
    del into_ref
    i = pl.program_id(2)
    sub = tq // tk
    key = lax.broadcasted_iota(jnp.int32, (tk, tq), 0)
    query = lax.broadcasted_iota(jnp.int32, (tk, tq), 1)
    m_ref[...] = jnp.full(m_ref.shape, NEG_INF, F32)
    acc_ref[...] = jnp.zeros_like(acc_ref)

    def scores(h, kt):
        rows = pl.ds(pl.multiple_of(kt * tk, tk), tk)
        lanes = slice(h * HEAD_W, (h + 1) * HEAD_W)
        s_ref[h] = _dot_nt(k_ref[0, rows, lanes], q_ref[0, :, lanes])

    def absorb(h, kt, visible):
        s_t = s_ref[h]
        if visible is not None:
            s_t = jnp.where(visible, s_t, NEG_INF)
        m = m_ref[h]
        m_new = jnp.maximum(m, jnp.max(s_t, axis=0, keepdims=True))
        p_t = jnp.exp2(s_t - m_new).astype(BF16)
        pv = _dot(vt_ref[kt, h], p_t)
        acc_ref[h] = jnp.exp2(m - m_new) * acc_ref[h] + pv
        m_ref[h] = m_new

    scores(0, 0)
    scores(1, 0)

    def body(kt, carry):
        for h in range(heads):
            _pipeline_ahead(scores, h + 2, heads, kt, kt + 1)
            absorb(h, kt, None)
        return carry

    lax.fori_loop(0, sub * i, body, 0)
    for s in range(sub):
        kt = sub * i + s
        visible = ((s * tk + key) // chunk) <= (query // chunk)
        for h in range(heads):
            _pipeline_ahead(scores, h + 2, heads, kt, kt + 1 if s + 1 < sub else None)
            absorb(h, kt, visible)
    for j in range(heads // 2):
        a0, a1 = acc_ref[2 * j], acc_ref[2 * j + 1]
        o_t = jnp.concatenate([a0[:HEAD_DIM] / a0[HEAD_DIM:HEAD_DIM + 1],
                               a1[:HEAD_DIM] / a1[HEAD_DIM:HEAD_DIM + 1]], axis=0)
        o_ref[:, j * PAIR_W:(j + 1) * PAIR_W] = jnp.transpose(o_t).astype(o_ref.dtype)


def _flash_prompt(qx, kx, vt4, tq, heads_per_step, chunk, name, into):
    B, S, W = qx.shape
    nq = S // tq
    hs = heads_per_step
    tk = vt4.shape[3]
    nk = S // tk
    return pl.pallas_call(
        functools.partial(_flash_prompt_kernel, tq=tq, tk=tk, heads=hs, chunk=chunk),
        grid=(B, W // (HEAD_W * hs), S // tq),
        in_specs=[pl.BlockSpec((1, tq, HEAD_W * hs), lambda b, g, i: (b, i, g)),
                  pl.BlockSpec((1, S, HEAD_W * hs), lambda b, g, i: (b, 0, g)),
                  pl.BlockSpec((nk, hs, VT_ROWS, tk), lambda b, g, i: (b, g, 0, 0)),
                  pl.BlockSpec(memory_space=pl.ANY)],
        out_specs=pl.BlockSpec((tq, HEAD_DIM * hs), lambda b, g, i: (b * nq + i, g)),
        out_shape=jax.ShapeDtypeStruct(into.shape, into.dtype),
        input_output_aliases={3: 0},
        scratch_shapes=[pltpu.VMEM((hs, tk, tq), F32), pltpu.VMEM((hs, 1, tq), F32),
                        pltpu.VMEM((hs, VT_ROWS, tq), F32)],
        compiler_params=_cparams("parallel", "parallel", "arbitrary"),
        name=name,
    )(qx, kx, vt4, into)


def _sb_prompt_kernel(q_ref, k_ref, vt_ref, into_ref, o_ref, z_ref, lw_ref, tot_ref, run_ref, acc_ref,
                      *, tq, tk, heads):
    del into_ref
    i = pl.program_id(1)
    sub = tq // tk
    key = lax.broadcasted_iota(jnp.int32, (tk, tq), 0)
    query = lax.broadcasted_iota(jnp.int32, (tk, tq), 1)
    r = lax.broadcasted_iota(jnp.int32, (tk, tk), 0)
    c = lax.broadcasted_iota(jnp.int32, (tk, tk), 1)
    after = (c > r).astype(BF16)
    run_ref[...] = jnp.zeros_like(run_ref)
    acc_ref[...] = jnp.zeros_like(acc_ref)

    def logits(h, kt):
        rows = pl.ds(pl.multiple_of(kt * tk, tk), tk)
        lanes = slice(h * HEAD_W, (h + 1) * HEAD_W)
        z_ref[h] = _dot_nt(k_ref[0, rows, lanes], q_ref[0, :, lanes])

    def log_weights(before, h, kt):
        log_beta, log_rest = _log2_sigmoid_pair(z_ref[h])
        if before is not None:
            log_beta = jnp.where(before, log_beta, NEG_INF)
            log_rest = jnp.where(before, log_rest, 0.0)
        hi, lo = _split_bf16(log_rest, 2)
        later = _dot(after, hi) + _dot(after, lo)
        lw_ref[h] = log_beta + later
        tot_ref[h] = later[0:1] + log_rest[0:1]

    def accumulate(h, kt):
        run = run_ref[h]
        a_t = jnp.exp2(lw_ref[h] + run).astype(BF16)
        acc_ref[h] += _dot(vt_ref[kt, h], a_t)
        run_ref[h] = run + tot_ref[h]

    def step(kt, masked, nxt, nxt_masked):
        for h in range(heads):
            _pipeline_ahead(logits, h + 2, heads, kt, nxt)
            if h + 1 < heads:
                log_weights(masked, h + 1, kt)
            elif nxt is not None:
                log_weights(nxt_masked, 0, nxt)
            accumulate(h, kt)

    unmasked = sub * i
    first = unmasked + sub - 1
    masks = [((sub - 1 - s) * tk + key) < query for s in range(sub)]
    logits(0, first)
    logits(1, first)
    log_weights(masks[0], 0, first)
    for s in range(sub):
        kt = first - s
        if s + 1 < sub:
            step(kt, masks[s], kt - 1, masks[s + 1])
        else:
            step(kt, masks[s], jnp.maximum(kt - 1, 0), None)

    def alive():
        return (jnp.max(run_ref[...]) > SB_DEAD_LOG2).astype(jnp.int32)

    def body(carry):
        kt, _ = carry
        step(kt, None, jnp.maximum(kt - 1, 0), None)
        return kt - 1, alive()

    lax.while_loop(lambda c: jnp.logical_and(c[0] >= 0, c[1] > 0), body, (unmasked - 1, alive()))

    for j in range(heads // 2):
        o_t = jnp.concatenate([acc_ref[2 * j], acc_ref[2 * j + 1]], axis=0)
        o_ref[:, j * PAIR_W:(j + 1) * PAIR_W] = jnp.transpose(o_t).astype(o_ref.dtype)


def _sb_prompt(qx, kx, vt4, tq, into, lane_block):
    B, S, W = qx.shape
    heads = W // HEAD_W
    tk = vt4.shape[3]
    nq = S // tq
    return pl.pallas_call(
        functools.partial(_sb_prompt_kernel, tq=tq, tk=tk, heads=heads),
        grid=(B, nq),
        in_specs=[pl.BlockSpec((1, tq, W), lambda b, i: (b, i, 0)),
                  pl.BlockSpec((1, S, W), lambda b, i: (b, 0, 0)),
                  pl.BlockSpec((S // tk, heads, HEAD_DIM, tk), lambda b, i: (b, 0, 0, 0)),
                  pl.BlockSpec(memory_space=pl.ANY)],
        out_specs=pl.BlockSpec((tq, heads * HEAD_DIM), lambda b, i: (b * nq + i, lane_block)),
        out_shape=jax.ShapeDtypeStruct(into.shape, into.dtype),
        input_output_aliases={3: 0},
        scratch_shapes=[pltpu.VMEM((heads, tk, tq), F32), pltpu.VMEM((heads, tk, tq), F32),
                        pltpu.VMEM((heads, 1, tq), F32), pltpu.VMEM((heads, 1, tq), F32),
                        pltpu.VMEM((heads, HEAD_DIM, tq), F32)],
        compiler_params=_cparams("parallel", "arbitrary"),
        name="sb_prompt",
    )(qx, kx, vt4, into)


def _prep_ab_weights(w_in, b_f):
    D = w_in.shape[0]
    H = b_f.shape[0]
    hw = (w_in.shape[1] - H) // 6
    main = jnp.concatenate([w_in[:, :3 * hw], w_in[:, 3 * hw + H:]], axis=1)
    wf = jnp.zeros((D, LANES), w_in.dtype).at[:, :H].set(w_in[:, 3 * hw:3 * hw + H])
    bf = jnp.zeros((1, LANES), F32).at[0, :H].set(b_f.astype(F32))
    return jnp.concatenate([main, wf], axis=1).astype(BF16), bf


def _prompt_ab_weights(w_ab, hw, heads):
    D = w_ab.shape[0]
    seg = lambda j: w_ab[:, j * hw:(j + 1) * hw]
    pad = lambda w: jnp.concatenate([w.reshape(D, heads, hw // heads),
                                     jnp.zeros((D, heads, HEAD_W - hw // heads), w.dtype)], axis=-1).reshape(D, -1)
    w_tok = jnp.concatenate([pad(seg(0)), pad(seg(1)), pad(seg(3)), pad(seg(4)), w_ab[:, 6 * hw:]], axis=1)
    w_t = jnp.transpose(jnp.concatenate([seg(1), seg(2), seg(4), seg(5)], axis=1))
    return w_tok, w_t


DECAY_TERMS = 3


def _insert_decay_kernel(k_ref, c_ref, sel_ref, o_ref):
    o_ref[...] = (k_ref[...].astype(F32) + _dot(c_ref[...], sel_ref[...])).astype(BF16)


def _fox_insert_decay(kx, lf, B, S, H):
    lt = jnp.transpose(lf[:, :H].reshape(B, S, H), (0, 2, 1))
    cum = jnp.transpose(_cumsum_rows(lt), (0, 2, 1)) * LOG2E
    parts = jnp.stack(_split_bf16_trunc(-cum, DECAY_TERMS), axis=-1).reshape(B * S, H * DECAY_TERMS)
    c = jnp.zeros((B * S, LANES), BF16).at[:, :H * DECAY_TERMS].set(parts)
    src = jnp.arange(LANES)[:, None]
    dst = jnp.arange(H * HEAD_W)[None, :]
    sel = jnp.logical_and(src < H * DECAY_TERMS,
                          dst == (src // DECAY_TERMS) * HEAD_W + HEAD_DIM + src % DECAY_TERMS).astype(BF16)
    tm = _pick_tile(B * S, 512)
    return pl.pallas_call(
        _insert_decay_kernel,
        grid=(B * S // tm,),
        in_specs=[pl.BlockSpec((tm, H * HEAD_W), lambda i: (i, 0)), pl.BlockSpec((tm, LANES), lambda i: (i, 0)),
                  pl.BlockSpec(sel.shape, lambda i: (0, 0))],
        out_specs=pl.BlockSpec((tm, H * HEAD_W), lambda i: (i, 0)),
        out_shape=jax.ShapeDtypeStruct(kx.shape, BF16),
        input_output_aliases={0: 0},
        compiler_params=_cparams("parallel"),
        name="fox_insert_decay",
    )(kx, c, sel)


def _query_decay_ones(H):
    lane = jnp.arange(H * HEAD_W) % HEAD_W
    return jnp.logical_and(lane >= HEAD_DIM, lane < HEAD_DIM + DECAY_TERMS).astype(F32)[None]


def _fox_sample_kernel(q_ref, knt_ref, vnt_ref, kct_ref, vct_ref, cq_ref, ckp_ref, ckn_ref, o_ref,
                       s_ref, m_ref, l_ref, acc_ref, *, n, heads):
    kt = pl.program_id(1)

    @pl.when(kt == 0)
    def _():
        m_ref[...] = jnp.full(m_ref.shape, NEG_INF, F32)
        l_ref[...] = jnp.zeros_like(l_ref)
        acc_ref[...] = jnp.zeros_like(acc_ref)

    def absorb(h, s, v_t, ck, mask):
        s = s + (cq_ref[0, h] - ck)
        if mask is not None:
            s = jnp.where(mask, s, NEG_INF)
        m = m_ref[h]
        m_new = jnp.maximum(m, jnp.max(s, axis=-1, keepdims=True))
        alpha = jnp.exp2(m - m_new)
        p = jnp.exp2(s - m_new)
        l_ref[h] = alpha * l_ref[h] + jnp.sum(p, axis=-1, keepdims=True)
        acc_ref[h] = alpha * acc_ref[h] + _dot_nt(p.astype(BF16), v_t)
        m_ref[h] = m_new

    def scores(h, _=None):
        s_ref[h] = _dot(q_ref[0, h], kct_ref[0, h].astype(BF16))

    scores(0)
    scores(1)
    for h in range(heads):
        _pipeline_ahead(scores, h + 2, heads, None, None)
        absorb(h, s_ref[h], vct_ref[0, h].astype(BF16), ckp_ref[0, h:h + 1, :], None)

    @pl.when(kt == pl.num_programs(1) - 1)
    def _():
        row = lax.broadcasted_iota(jnp.int32, (n, n), 0)
        col = lax.broadcasted_iota(jnp.int32, (n, n), 1)
        for h in range(heads):
            absorb(h, _dot(q_ref[0, h], knt_ref[0, h]), vnt_ref[0, h], ckn_ref[0, h:h + 1, :], col <= row)
            o_ref[0, h] = (acc_ref[h] / l_ref[h]).astype(o_ref.dtype)


def _fox_sample(q, knt, vnt, kct, vct, cq, ckp, ckn, tk):
    DB, H, n, dh = q.shape
    P = kct.shape[-1]
    per_b = lambda a: pl.BlockSpec((1,) + a.shape[1:], lambda b, j: (b,) + (0,) * (a.ndim - 1))
    cache = pl.BlockSpec((1, H, dh, tk), lambda b, j: (b, 0, 0, j))
    return pl.pallas_call(
        functools.partial(_fox_sample_kernel, n=n, heads=H),
        grid=(DB, P // tk),
        in_specs=[per_b(q), per_b(knt), per_b(vnt), cache, cache, per_b(cq),
                  pl.BlockSpec((1, H, tk), lambda b, j: (b, 0, j)), per_b(ckn)],
        out_specs=per_b(q),
        out_shape=jax.ShapeDtypeStruct(q.shape, BF16),
        scratch_shapes=[pltpu.VMEM((H, n, tk), F32), pltpu.VMEM((H, n, 1), F32), pltpu.VMEM((H, n, 1), F32),
                        pltpu.VMEM((H, n, dh), F32)],
        compiler_params=_cparams("parallel", "arbitrary"),
        name="fox_sample",
    )(q, knt, vnt, kct, vct, cq, ckp, ckn)


def _sb_sample_kernel(q_ref, knt_ref, vnt_ref, kct_ref, vct_ref, o_ref, z_ref, run_ref, acc_ref, *, n, heads, sub):
    kt = pl.program_id(1)
    upper = _strict_upper(sub)

    def absorb(h, z, v_t, upper_m, width, mask=None):
        log_beta, log_rest = _log2_sigmoid_pair(z)
        if mask is not None:
            log_beta = jnp.where(mask, log_beta, NEG_INF)
            log_rest = jnp.where(mask, log_rest, 0.0)
        hi, lo = _split_bf16(log_rest, 2)
        run = run_ref[h]
        parts = []
        for c in reversed(range(z.shape[1] // width)):
            keys = slice(c * width, (c + 1) * width)
            later = _dot(hi[:, keys], upper_m) + _dot(lo[:, keys], upper_m)
            parts.append(jnp.exp2(log_beta[:, keys] + later + run).astype(BF16))
            run = run + jnp.sum(log_rest[:, keys], axis=-1, keepdims=True)
        a = parts[0] if len(parts) == 1 else jnp.concatenate(parts[::-1], axis=1)
        acc_ref[h] += _dot_nt(a, v_t)
        run_ref[h] = run

    def logits(h, _=None):
        z_ref[h] = _dot(q_ref[0, h], kct_ref[0, h].astype(BF16))

    @pl.when(kt == 0)
    def _():
        row = lax.broadcasted_iota(jnp.int32, (n, n), 0)
        col = lax.broadcasted_iota(jnp.int32, (n, n), 1)
        upper_n = _strict_upper(n)
        run_ref[...] = jnp.zeros_like(run_ref)
        acc_ref[...] = jnp.zeros_like(acc_ref)
        for h in range(heads):
            absorb(h, _dot(q_ref[0, h], knt_ref[0, h]), vnt_ref[0, h], upper_n, n, col < row)

    logits(0)
    logits(1)
    for h in range(heads):
        _pipeline_ahead(logits, h + 2, heads, None, None)
        absorb(h, z_ref[h], vct_ref[0, h].astype(BF16), upper, sub)

    @pl.when(kt == pl.num_programs(1) - 1)
    def _():
        o_ref[0] = acc_ref[...].astype(o_ref.dtype)


def _sb_sample(q, knt, vnt, kct, vct, tk, sub):
    DB, H, n, dh = q.shape
    P = kct.shape[-1]
    nk = P // tk
    per_b = lambda a: pl.BlockSpec((1,) + a.shape[1:], lambda b, j: (b,) + (0,) * (a.ndim - 1))
    cache = pl.BlockSpec((1, H, dh, tk), lambda b, j: (b, 0, 0, nk - 1 - j))
    return pl.pallas_call(
        functools.partial(_sb_sample_kernel, n=n, heads=H, sub=sub),
        grid=(DB, nk),
        in_specs=[per_b(q), per_b(knt), per_b(vnt), cache, cache],
        out_specs=per_b(q),
        out_shape=jax.ShapeDtypeStruct(q.shape, BF16),
        scratch_shapes=[pltpu.VMEM((H, n, tk), F32), pltpu.VMEM((H, n, 1), F32), pltpu.VMEM((H, n, dh), F32)],
        compiler_params=_cparams("parallel", "arbitrary"),
        name="sb_sample",
    )(q, knt, vnt, kct, vct)


def _fox_sample_cum(lf_new, lf_past_t):
    DB, H, P = lf_past_t.shape
    n = lf_new.shape[1]
    L = -(-(P + n) // LANES) * LANES
    both = jnp.concatenate([lf_past_t.astype(F32), jnp.transpose(lf_new, (0, 2, 1)),
                            jnp.zeros((DB, H, L - P - n), F32)], axis=2)
    cum_t = _cumsum_rows(both) * LOG2E
    ckn = cum_t[:, :, P:P + n]
    return ckn[..., None], cum_t[:, :, :P], ckn


def _heads_major(a, heads, transpose_rows):
    DB, n, _ = a.shape
    a4 = a.reshape(DB, n, heads, -1)
    return jnp.transpose(a4, (0, 2, 3, 1) if transpose_rows else (0, 2, 1, 3))


ROUTER_ROWS = 48


def _layer_norm(y, g, b):
    mu = jnp.mean(y, axis=-1, keepdims=True)
    yc = y - mu
    var = jnp.mean(yc * yc, axis=-1, keepdims=True)
    return yc * lax.rsqrt(var + LN_EPS) * g + b


def _first_argmax(v, ridx):
    vmax = jnp.max(v, axis=0, keepdims=True)
    idx = jnp.min(jnp.where(v == vmax, ridx, v.shape[0]), axis=0, keepdims=True)
    return vmax, idx


def _two_part_specs(tm, width, head_tiles):
    return (pl.BlockSpec((tm, width), lambda i: (jnp.minimum(i, head_tiles - 1), 0)),
            pl.BlockSpec((tm, width), lambda i: (jnp.maximum(i - head_tiles, 0), 0)))


def _mix_out_kernel(o_ref, w_ref, xh_ref, xt_ref, g_ref, b_ref, wrh_ref, wrl_ref, rb_ref, h_ref, ids_ref, wts_ref,
                    *, alpha, n_groups, head_tiles):
    x = jnp.where(pl.program_id(0) < head_tiles, xh_ref[...], xt_ref[...])
    h = _layer_norm(alpha * x + _dot(o_ref[...], w_ref[...]), g_ref[...], b_ref[...])
    h_ref[...] = h
    hh, hl = _split_bf16(h, 2)
    wrh = wrh_ref[...]
    low = _dot_nt(wrl_ref[...], hh)
    lg = _dot_nt(wrh, hh) + (_dot_nt(wrh, hl) + (low[:ROUTER_ROWS] + low[ROUTER_ROWS:])) + rb_ref[...]
    tm = lg.shape[1]
    ridx = lax.broadcasted_iota(jnp.int32, (8, tm), 0)
    g = jnp.where(ridx < n_groups, lg[0:8], NEG_INF)
    gmax, gidx = _first_argmax(g, ridx)
    gate = 1.0 / jnp.sum(jnp.exp(g - gmax), axis=0, keepdims=True)
    esel = lg[8:16]
    for gg in range(1, n_groups):
        esel = jnp.where(gidx == gg, lg[8 + 8 * gg:16 + 8 * gg], esel)
    v1, i1 = _first_argmax(esel, ridx)
    v2, i2 = _first_argmax(jnp.where(ridx == i1, NEG_INF, esel), ridx)
    t = jnp.exp(v2 - v1)
    w1 = 1.0 / (1.0 + t)
    ids_ref[...] = jnp.where(ridx == 0, gidx * 8 + i1, jnp.where(ridx == 1, gidx * 8 + i2, 0))
    wts_ref[...] = jnp.where(ridx == 0, gate * w1, jnp.where(ridx == 1, gate * (t * w1), 0.0))


def _mix_out(o, w, x_head, x_tail, g, b, wrh, wrl, rb, alpha, n_groups, tm):
    D = x_head.shape[1]
    T = x_head.shape[0] + x_tail.shape[0]
    head_tiles = x_head.shape[0] // tm
    const = lambda a: pl.BlockSpec(a.shape, lambda i: (0,) * a.ndim)
    rb_t = jnp.broadcast_to(rb, (ROUTER_ROWS, tm))
    return pl.pallas_call(
        functools.partial(_mix_out_kernel, alpha=alpha, n_groups=n_groups, head_tiles=head_tiles),
        grid=(T // tm,),
        in_specs=[pl.BlockSpec((tm, o.shape[1]), lambda i: (i, 0)), const(w),
                  *_two_part_specs(tm, D, head_tiles), const(g), const(b), const(wrh), const(wrl), const(rb_t)],
        out_specs=[pl.BlockSpec((tm, D), lambda i: (i, 0)), pl.BlockSpec((8, tm), lambda i: (0, i)),
                   pl.BlockSpec((8, tm), lambda i: (0, i))],
        out_shape=[jax.ShapeDtypeStruct((T, D), F32), jax.ShapeDtypeStruct((8, T), jnp.int32),
                   jax.ShapeDtypeStruct((8, T), F32)],
        compiler_params=_cparams("parallel"),
        name="mix_out_ln_router",
    )(o, w, x_head, x_tail, g, b, wrh, wrl, rb_t)


def _prep_router(w_group, b_group, w_router, b_router):
    D, G = w_group.shape
    E = w_router.shape[-1]
    wr = jnp.zeros((ROUTER_ROWS, D), F32)
    wr = wr.at[:G].set(w_group.T.astype(F32))
    wr = wr.at[8:8 + G * E].set(jnp.transpose(w_router, (0, 2, 1)).reshape(G * E, D).astype(F32))
    rb = jnp.zeros((ROUTER_ROWS, 1), F32)
    rb = rb.at[:G, 0].set(b_group.astype(F32)).at[8:8 + G * E, 0].set(b_router.reshape(-1).astype(F32))
    hi, mid, lo = _split_bf16_trunc(wr, 3)
    return hi, jnp.concatenate([mid, lo], axis=0), rb


def _gather_rows(idx_ref, n, src_hbm, dst, sem):
    def body(r, carry):
        pltpu.make_async_copy(src_hbm.at[pl.ds(idx_ref[0, 0, r], 1)], dst.at[pl.ds(r, 1)], sem).start()
        return carry
    lax.fori_loop(0, n, body, 0, unroll=8)


def _wait_rows(n, src_hbm, dst, sem):
    pltpu.make_async_copy(src_hbm.at[pl.ds(0, n)], dst, sem).wait()


def _moe_experts_kernel(te_ref, tv_ref, src_ref, nxt_ref, x_hbm, wg_ref, wu_ref, wd_ref, y_ref, xbuf, sem, *, tm):
    i = pl.program_id(0)
    nt = pl.num_programs(0)
    slot = i % 2

    @pl.when(jnp.logical_and(i == 0, tv_ref[0] > 0))
    def _():
        _gather_rows(src_ref, tm, x_hbm, xbuf.at[0], sem.at[0])

    @pl.when(jnp.logical_and(i + 1 < nt, tv_ref[jnp.minimum(i + 1, nt - 1)] > 0))
    def _():
        _gather_rows(nxt_ref, tm, x_hbm, xbuf.at[1 - slot], sem.at[1 - slot])

    @pl.when(tv_ref[i] > 0)
    def _():
        _wait_rows(tm, x_hbm, xbuf.at[slot], sem.at[slot])
        xb = xbuf[slot].astype(BF16)
        a = _dot(xb, wg_ref[0].astype(BF16))
        u = _dot(xb, wu_ref[0].astype(BF16))
        hid = (a / (1.0 + jnp.exp(-a))) * u
        y_ref[...] = _dot(hid.astype(BF16), wd_ref[0].astype(BF16))

    @pl.when(tv_ref[i] == 0)
    def _():
        y_ref[...] = jnp.zeros_like(y_ref)


def _moe_experts(x, w_gate, w_up, w_down, tile_expert, tile_valid, src, tm):
    T, D = x.shape
    F = w_gate.shape[-1]
    NT = tile_expert.shape[0]
    grid_spec = pltpu.PrefetchScalarGridSpec(
        num_scalar_prefetch=2,
        grid=(NT,),
        in_specs=[pl.BlockSpec((1, 1, tm), lambda i, te, tv: (i, 0, 0), memory_space=pltpu.SMEM),
                  pl.BlockSpec((1, 1, tm), lambda i, te, tv: (jnp.minimum(i + 1, NT - 1), 0, 0),
                               memory_space=pltpu.SMEM),
                  pl.BlockSpec(memory_space=pl.ANY),
                  pl.BlockSpec((1, D, F), lambda i, te, tv: (te[i], 0, 0)),
                  pl.BlockSpec((1, D, F), lambda i, te, tv: (te[i], 0, 0)),
                  pl.BlockSpec((1, F, D), lambda i, te, tv: (te[i], 0, 0))],
        out_specs=pl.BlockSpec((tm, D), lambda i, te, tv: (i, 0)),
        scratch_shapes=[pltpu.VMEM((2, tm, D), F32), pltpu.SemaphoreType.DMA((2,))],
    )
    return pl.pallas_call(
        functools.partial(_moe_experts_kernel, tm=tm),
        grid_spec=grid_spec,
        out_shape=jax.ShapeDtypeStruct((NT * tm, D), F32),
        compiler_params=_cparams("arbitrary"),
        name="moe_experts",
    )(tile_expert, tile_valid, src, src, x, w_gate, w_up, w_down)


def _moe_combine_kernel(pos_ref, nxt_ref, ys_hbm, h_ref, w_ref, g_ref, b_ref, oh_ref, ot_ref, buf, sem,
                        *, tm, alpha, head_tiles):
    i = pl.program_id(0)
    nt = pl.num_programs(0)
    slot = i % 2

    @pl.when(i == 0)
    def _():
        _gather_rows(pos_ref, 2 * tm, ys_hbm, buf.at[0], sem.at[0])

    @pl.when(i + 1 < nt)
    def _():
        _gather_rows(nxt_ref, 2 * tm, ys_hbm, buf.at[1 - slot], sem.at[1 - slot])

    _wait_rows(2 * tm, ys_hbm, buf.at[slot], sem.at[slot])
    w = w_ref[...]
    y = _layer_norm(alpha * h_ref[...] + (w[:, 0:1] * buf[slot, 0:tm] + w[:, 1:2] * buf[slot, tm:2 * tm]),
                    g_ref[...], b_ref[...])

    @pl.when(i < head_tiles)
    def _():
        oh_ref[...] = y

    @pl.when(i >= head_tiles)
    def _():
        ot_ref[...] = y


def _moe_combine(ys, h, pos, wts, g, b, alpha, tm, head_rows):
    T, D = h.shape
    nt = T // tm
    head_tiles = head_rows // tm
    const = lambda a: pl.BlockSpec(a.shape, lambda i: (0,) * a.ndim)
    return pl.pallas_call(
        functools.partial(_moe_combine_kernel, tm=tm, alpha=alpha, head_tiles=head_tiles),
        grid=(nt,),
        in_specs=[pl.BlockSpec((1, 1, 2 * tm), lambda i: (i, 0, 0), memory_space=pltpu.SMEM),
                  pl.BlockSpec((1, 1, 2 * tm), lambda i: (jnp.minimum(i + 1, nt - 1), 0, 0),
                               memory_space=pltpu.SMEM),
                  pl.BlockSpec(memory_space=pl.ANY),
                  pl.BlockSpec((tm, D), lambda i: (i, 0)),
                  pl.BlockSpec((tm, 2), lambda i: (i, 0)), const(g), const(b)],
        out_specs=list(_two_part_specs(tm, D, head_tiles)),
        out_shape=[jax.ShapeDtypeStruct((head_rows, D), F32), jax.ShapeDtypeStruct((T - head_rows, D), F32)],
        scratch_shapes=[pltpu.VMEM((2, 2 * tm, D), F32), pltpu.SemaphoreType.DMA((2,))],
        compiler_params=_cparams("arbitrary"),
        name="moe_combine_ln",
    )(pos, pos, ys, h, wts, g, b)


def _route(ids, n_experts, tm):
    T = ids.shape[1]
    flat = ids.reshape(-1)
    iota = jnp.arange(2 * T, dtype=jnp.int32)
    sorted_e, order = lax.sort((flat, iota), num_keys=1, is_stable=True)
    _, inverse = lax.sort((order, iota), num_keys=1)
    experts = jnp.arange(n_experts, dtype=jnp.int32)
    counts = jnp.sum((flat[:, None] == experts[None, :]).astype(jnp.int32), axis=0)
    padded = (counts + tm - 1) // tm * tm
    ends = jnp.cumsum(padded)
    shift = (ends - padded) - (jnp.cumsum(counts) - counts)
    NT = (2 * T + n_experts * (tm - 1)) // tm
    tile_start = jnp.arange(NT, dtype=jnp.int32) * tm
    tile_expert = jnp.minimum(jnp.sum((tile_start[:, None] >= ends[None, :]).astype(jnp.int32), axis=1),
                              n_experts - 1)
    tile_valid = (tile_start < ends[-1]).astype(jnp.int32)
    pos = (inverse + shift[flat]).reshape(2, T)
    row = jnp.arange(NT * tm, dtype=jnp.int32)
    src = (order % T)[jnp.clip(row - jnp.repeat(shift[tile_expert], tm), 0, 2 * T - 1)]
    return tile_expert, tile_valid, src.reshape(NT, 1, tm), pos


MLA_PAIR_W = 2 * LANES
QK_NOPE = 64
QK_ROPE = 32


def _mla_proj_kernel(hh_ref, ht_ref, wdn_ref, gq_ref, gkv_ref, wq_ref, wqr_ref, wk_ref, wv_ref, cos_ref, sin_ref,
                     ckv_ref, kr_ref, qcat_ref, kcat_ref, vt_ref, *, q_lora, kv_lora, npairs, scale, head_tiles):
    h = jnp.where(pl.program_id(0) < head_tiles, hh_ref[...], ht_ref[...])
    z = _dot(h.astype(BF16), wdn_ref[...])
    cq = z[:, :q_lora]
    ckv = z[:, q_lora:q_lora + kv_lora]
    o = q_lora + kv_lora
    kr_raw = z[:, o:o + HEAD_W]
    kr_rot = z[:, o + HEAD_W:o + 2 * HEAD_W]
    cq = cq * lax.rsqrt(jnp.mean(cq * cq, axis=-1, keepdims=True) + RMS_EPS) * gq_ref[...]
    ckv = ckv * lax.rsqrt(jnp.mean(ckv * ckv, axis=-1, keepdims=True) + RMS_EPS) * gkv_ref[...]
    ckv_ref[...] = ckv
    cos = cos_ref[...]
    sin = sin_ref[...]
    kr_tile = kr_raw * cos + kr_rot * sin
    kr_ref[...] = kr_tile[:, QK_NOPE:QK_NOPE + QK_ROPE]
    cqb = cq.astype(BF16)
    ckb = ckv.astype(BF16)
    cos2 = jnp.concatenate([cos, cos], axis=1)
    sin2 = jnp.concatenate([sin, sin], axis=1)
    kr2 = jnp.concatenate([kr_tile, kr_tile], axis=1)
    for p in range(npairs):
        lanes = slice(p * MLA_PAIR_W, (p + 1) * MLA_PAIR_W)
        q = _dot(cqb, wq_ref[:, lanes]) * cos2 + _dot(cqb, wqr_ref[:, lanes]) * sin2
        qcat_ref[:, lanes] = (q * scale).astype(BF16)
        kcat_ref[:, lanes] = (_dot(ckb, wk_ref[:, lanes]) + kr2).astype(BF16)
    heads = 2 * npairs
    vt = _dot_nt(wv_ref[...], ckb).reshape(heads, HEAD_DIM, ckb.shape[0])
    vt_ref[0, :, :HEAD_DIM, :] = vt.astype(BF16)
    vt_ref[0, :, HEAD_DIM:, :] = jnp.ones((heads, VT_ROWS - HEAD_DIM, ckb.shape[0]), BF16)


def _mla_proj(h_head, h_tail, wdn, gq, gkv, wq, wqr, wk, wv, cos_t, sin_t, table_block, tm):
    D = h_head.shape[1]
    T = h_head.shape[0] + h_tail.shape[0]
    head_tiles = h_head.shape[0] // tm
    q_lora, kv_lora = gq.shape[1], gkv.shape[1]
    npairs = wq.shape[1] // MLA_PAIR_W
    const = lambda a: pl.BlockSpec(a.shape, lambda i: (0,) * a.ndim)
    row = lambda w_: pl.BlockSpec((tm, w_), lambda i: (i, 0))
    table = pl.BlockSpec((tm, HEAD_W), lambda i: (table_block(i), 0))
    return pl.pallas_call(
        functools.partial(_mla_proj_kernel, q_lora=q_lora, kv_lora=kv_lora, npairs=npairs,
                          scale=(QK_NOPE + QK_ROPE) ** -0.5 * LOG2E, head_tiles=head_tiles),
        grid=(T // tm,),
        in_specs=[*_two_part_specs(tm, D, head_tiles), const(wdn), const(gq), const(gkv), const(wq), const(wqr),
                  const(wk), const(wv), table, table],
        out_specs=[row(kv_lora), row(QK_ROPE), row(wq.shape[1]), row(wk.shape[1]),
                   pl.BlockSpec((1, 2 * npairs, VT_ROWS, tm), lambda i: (i, 0, 0, 0))],
        out_shape=[jax.ShapeDtypeStruct((T, kv_lora), F32), jax.ShapeDtypeStruct((T, QK_ROPE), F32),
                   jax.ShapeDtypeStruct((T, wq.shape[1]), BF16), jax.ShapeDtypeStruct((T, wk.shape[1]), BF16),
                   jax.ShapeDtypeStruct((T // tm, 2 * npairs, VT_ROWS, tm), BF16)],
        compiler_params=_cparams("parallel"),
        name="mla_proj",
    )(h_head, h_tail, wdn, gq, gkv, wq, wqr, wk, wv, cos_t, sin_t)


def _rot_half(w):
    half = w.shape[-1] // 2
    return jnp.concatenate([-w[..., half:], w[..., :half]], axis=-1)


def _prep_mla_weights(w_down, w_uq, w_ukv, heads, q_lora, kv_lora):
    D = w_down.shape[0]
    tail = HEAD_W - QK_NOPE - QK_ROPE
    w_kr = w_down[:, q_lora + kv_lora:]
    slot = lambda w: jnp.concatenate([jnp.zeros((D, QK_NOPE), w.dtype), w, jnp.zeros((D, tail), w.dtype)], axis=1)
    wdn = jnp.concatenate([w_down[:, :q_lora + kv_lora], slot(w_kr), slot(_rot_half(w_kr))], axis=1)
    wq3 = w_uq.reshape(q_lora, heads, QK_NOPE + QK_ROPE)
    nope, ropew = wq3[..., :QK_NOPE], wq3[..., QK_NOPE:]
    zpad = jnp.zeros((q_lora, heads, tail), w_uq.dtype)
    wq = jnp.concatenate([nope, ropew, zpad], axis=-1)
    wqr = jnp.concatenate([jnp.zeros_like(nope), _rot_half(ropew), zpad], axis=-1)
    wkv3 = w_ukv.reshape(kv_lora, heads, QK_NOPE + HEAD_DIM)
    w_uk, w_uv = wkv3[..., :QK_NOPE], wkv3[..., QK_NOPE:]
    wk = jnp.concatenate([w_uk, jnp.zeros((kv_lora, heads, HEAD_W - QK_NOPE), w_ukv.dtype)], axis=-1)
    wv = jnp.transpose(w_uv.reshape(kv_lora, heads * HEAD_DIM))
    b16 = lambda a: a.astype(BF16)
    return (b16(wdn), b16(wq.reshape(q_lora, -1)), b16(wqr.reshape(q_lora, -1)), b16(wk.reshape(kv_lora, -1)),
            b16(wv), b16(jnp.transpose(w_uk, (1, 2, 0))), b16(jnp.transpose(w_uv, (1, 0, 2))))


def _rope_tables(pos):
    half = QK_ROPE // 2
    inv_freq = ROPE_BASE ** (-jnp.arange(half, dtype=F32) / half)
    ang = pos.astype(F32)[:, None] * inv_freq[None, :]
    n = pos.shape[0]
    pad = jnp.zeros((n, HEAD_W - QK_NOPE - QK_ROPE), F32)
    cos = jnp.concatenate([jnp.ones((n, QK_NOPE), F32)] + [jnp.cos(ang)] * 2 + [pad], axis=1)
    sin = jnp.concatenate([jnp.zeros((n, QK_NOPE), F32)] + [jnp.sin(ang)] * 2 + [pad], axis=1)
    return cos, sin


def _mla_sample_queries(qcat, heads):
    DB, n, _ = qcat.shape
    q4 = qcat.reshape(DB, n, heads, HEAD_W)
    rows = lambda a: jnp.transpose(a, (0, 2, 1, 3)).reshape(DB, heads * n, a.shape[-1])
    return rows(q4[..., :QK_NOPE]), rows(q4[..., QK_NOPE:QK_NOPE + QK_ROPE])


def _mla_sample_kernel(qn_ref, qr_ref, wuk_ref, wuv_ref, cc_ref, rc_ref, cn_ref, rn_ref, o_ref,
                       qlat_ref, m_ref, l_ref, acc_ref, *, n, heads):
    kt = pl.program_id(1)

    @pl.when(kt == 0)
    def _():
        for h in range(heads):
            rows = slice(h * n, (h + 1) * n)
            qlat_ref[rows, :] = _dot(qn_ref[0, rows, :], wuk_ref[h]).astype(BF16)
        m_ref[...] = jnp.full(m_ref.shape, NEG_INF, F32)
        l_ref[...] = jnp.zeros_like(l_ref)
        acc_ref[...] = jnp.zeros_like(acc_ref)

    def update(ckv, kr_t):
        s = _dot_nt(qlat_ref[...], ckv) + _dot(qr_ref[0], kr_t)
        m, l, acc = _softmax_step(s[None], ckv, (m_ref[...], l_ref[...], acc_ref[...]))
        m_ref[...] = m
        l_ref[...] = l
        acc_ref[...] = acc

    update(cc_ref[0].astype(BF16), rc_ref[0].astype(BF16))

    @pl.when(kt == pl.num_programs(1) - 1)
    def _():
        update(cn_ref[0], rn_ref[0])
        o_lat = (acc_ref[0] / l_ref[0]).astype(BF16)
        for h in range(heads):
            o_ref[0, :, h * HEAD_DIM:(h + 1) * HEAD_DIM] = _dot(o_lat[h * n:(h + 1) * n], wuv_ref[h]).astype(o_ref.dtype)


def _mla_sample(qn, qr, wuk, wuv, ckv_c, kr_c, ckv_n, kr_n, n, tk):
    DB, R, _ = qn.shape
    heads = R // n
    P, C = ckv_c.shape[1], ckv_c.shape[2]
    const = lambda a: pl.BlockSpec(a.shape, lambda b, j: (0,) * a.ndim)
    per_b = lambda a: pl.BlockSpec((1,) + a.shape[1:], lambda b, j: (b, 0, 0))
    return pl.pallas_call(
        functools.partial(_mla_sample_kernel, n=n, heads=heads),
        grid=(DB, P // tk),
        in_specs=[per_b(qn), per_b(qr), const(wuk), const(wuv),
                  pl.BlockSpec((1, tk, C), lambda b, j: (b, j, 0)),
                  pl.BlockSpec((1, QK_ROPE, tk), lambda b, j: (b, 0, j)),
                  per_b(ckv_n), per_b(kr_n)],
        out_specs=pl.BlockSpec((1, n, heads * HEAD_DIM), lambda b, j: (b, 0, 0)),
        out_shape=jax.ShapeDtypeStruct((DB, n, heads * HEAD_DIM), BF16),
        scratch_shapes=[pltpu.VMEM((R, C), BF16), pltpu.VMEM((1, R, 1), F32), pltpu.VMEM((1, R, 1), F32),
                        pltpu.VMEM((1, R, C), F32)],
        compiler_params=_cparams("parallel", "arbitrary"),
        name="mla_sample",
    )(qn, qr, wuk, wuv, ckv_c, kr_c, ckv_n, kr_n)


def _moe_layer(h, ids, wts, w_gate, w_up, w_down, g, b, alpha, tm, head_rows):
    T, D = h.shape
    n_experts = w_gate.shape[0] * w_gate.shape[1]
    tile_expert, tile_valid, src, pos = _route(ids[:2], n_experts, tm)
    flat3 = lambda w: w.reshape((n_experts,) + w.shape[2:])
    ys = _moe_experts(h, flat3(w_gate), flat3(w_up), flat3(w_down), tile_expert, tile_valid, src, tm)
    pos_t = jnp.transpose(pos.reshape(2, T // tm, tm), (1, 0, 2)).reshape(T // tm, 1, 2 * tm)
    return _moe_combine(ys, h, pos_t, jnp.transpose(wts[:2]), g, b, alpha, tm, head_rows)


TOKEN_TILE = 256
FLASH_Q_TILE = 512
FLASH_KEY_TILE = 256
SB_Q_TILE = 256
SB_KEY_TILE = 128
CACHE_TILE = 1024
SB_SUB_TILE = 256


def kernel(x_prompt, x_sample, cache_fox_k, cache_fox_v, cache_fox_logf, cache_sb_k, cache_sb_v, cache_mla_ckv, cache_mla_krope, ab_w_in, ab_b_forget, ab_w_out, mla_w_down, mla_g_q, mla_g_kv, mla_w_uq, mla_w_ukv, mla_w_out, moe_w_group, moe_b_group, moe_w_router, moe_b_router, moe_w_gate, moe_w_up, moe_w_down, ln_g, ln_b):
    B, S, D = x_prompt.shape
    DB, n, _ = x_sample.shape
    P = cache_fox_k.shape[2]
    TP, TS = B * S, DB * n
    depth = ln_g.shape[0]
    n_groups = moe_w_group.shape[-1]
    assert depth == 2 and ab_w_in.shape[0] == 1 and mla_w_down.shape[0] == 1
    assert S % FLASH_Q_TILE == 0 and S % SB_Q_TILE == 0 and TP % TOKEN_TILE == 0 and TS % TOKEN_TILE == 0 and TOKEN_TILE % n == 0
    assert P % CACHE_TILE == 0 and P % CHUNK == 0 and n == CHUNK
    alpha = (2 * depth) ** 0.25
    tk = CACHE_TILE

    xp, xs = x_prompt.reshape(TP, D), x_sample.reshape(TS, D)
    prompt3 = lambda a: a[:TP].reshape(B, S, -1)
    sample3 = lambda a: a[TP:].reshape(DB, n, -1)

    def ffn(o, w_out, resid, layer):
        wrh, wrl, rb = _prep_router(moe_w_group[layer], moe_b_group[layer], moe_w_router[layer], moe_b_router[layer])
        h, ids, wts = _mix_out(o, w_out.astype(BF16), *resid, ln_g[layer, 0][None], ln_b[layer, 0][None],
                               wrh, wrl, rb, alpha, n_groups, TOKEN_TILE)
        return _moe_layer(h, ids, wts, moe_w_gate[layer], moe_w_up[layer], moe_w_down[layer],
                          ln_g[layer, 1][None], ln_b[layer, 1][None], alpha, TOKEN_TILE, TP)

    fox_heads = ab_b_forget.shape[1]
    hw = (ab_w_in.shape[2] - fox_heads) // 6
    w_ab, b_forget = _prep_ab_weights(ab_w_in[0], ab_b_forget[0])
    w_tok, w_t = _prompt_ab_weights(w_ab, hw, fox_heads)
    (qa_p, ka_p, qb_p, kb_p, lf_p, kat, vat, kbt, vbt, vat16, vbt16) = _ab_proj_prompt(
        xp, w_tok, b_forget, w_t, _query_decay_ones(fox_heads), B, S, fox_heads, SB_KEY_TILE)
    bsw = lambda a: a.reshape(B, S, -1)
    ka_p = _fox_insert_decay(ka_p, lf_p, B, S, fox_heads)
    qa, ka, va, qb, kb, vb, ka16, va16, kb16, vb16, lf = _ab_proj(xs, w_ab, b_forget, hw)
    dbn = lambda a: a.reshape(DB, n, -1)
    lf_s = dbn(lf)[:, :, :fox_heads]
    cq_s, ck_past, ck_new = _fox_sample_cum(lf_s, jnp.transpose(cache_fox_logf[0], (0, 2, 1)))
    cache_t = lambda c: jnp.transpose(c[0], (0, 2, 3, 1))
    hm = lambda a, t=False: _heads_major(dbn(a), fox_heads, t)
    o_fox_s = _fox_sample(hm(qa), hm(ka16, True), hm(va16, True), cache_t(cache_fox_k), cache_t(cache_fox_v),
                          cq_s, ck_past, ck_new, tk)
    o_sb_s = _sb_sample(hm(qb), hm(kb16, True), hm(vb16, True), cache_t(cache_sb_k), cache_t(cache_sb_v),
                        tk, SB_SUB_TILE)
    tokens_major = lambda a: jnp.transpose(a, (0, 2, 1, 3)).reshape(TS, hw)
    o = jnp.pad(jnp.concatenate([tokens_major(o_fox_s), tokens_major(o_sb_s)], axis=-1), ((TP, 0), (0, 0)))
    o = _flash_prompt(bsw(qa_p), bsw(ka_p), vat16, FLASH_Q_TILE, fox_heads, 1, "fox_prompt", o)
    o = _sb_prompt(bsw(qb_p), bsw(kb_p), vbt16, SB_Q_TILE, o, 1)
    xp, xs = ffn(o, ab_w_out[0], (xp, xs), 0)

    q_lora, kv_lora = mla_g_q.shape[1], mla_g_kv.shape[1]
    heads = mla_w_uq.shape[2] // (QK_NOPE + QK_ROPE)
    wdn, wq, wqr, wk, wv, wuk_t, wuv = _prep_mla_weights(mla_w_down[0], mla_w_uq[0], mla_w_ukv[0], heads, q_lora, kv_lora)
    tm = TOKEN_TILE
    pos = jnp.concatenate([jnp.arange(S, dtype=jnp.int32), P + jnp.arange(tm, dtype=jnp.int32) % n])
    cos_t, sin_t = _rope_tables(pos)
    blocks_per_seq, prompt_blocks = S // tm, TP // tm
    table_block = lambda i: jnp.where(i < prompt_blocks, i % blocks_per_seq, blocks_per_seq)
    assert tm == FLASH_KEY_TILE
    ckv, kr, qcat, kcat, vt = _mla_proj(xp, xs, wdn, mla_g_q[0][None], mla_g_kv[0][None], wq, wqr, wk, wv,
                                        cos_t, sin_t, table_block, tm)
    qn, qr = _mla_sample_queries(sample3(qcat), heads)
    o_s = _mla_sample(qn, qr, wuk_t, wuv, cache_mla_ckv[0], jnp.transpose(cache_mla_krope[0], (0, 2, 1)),
                      sample3(ckv).astype(BF16), jnp.transpose(sample3(kr), (0, 2, 1)).astype(BF16), n, tk)
    o = jnp.pad(o_s.reshape(TS, -1), ((TP, 0), (0, 0)))
    o = _flash_prompt(prompt3(qcat), prompt3(kcat), vt, FLASH_Q_TILE, 8, CHUNK, "mla_prompt", o)
    xp, xs = ffn(o, mla_w_out[0], (xp, xs), 1)

    rows_p = lambda a: jnp.transpose(a, (0, 3, 1, 2))[None]
    rows_s = lambda a: a.reshape(1, DB, n, fox_heads, hw // fox_heads)
    pr, sr = slice(0, TP), slice(TP, TP + TS)
    return (xp.reshape(B, S, D), xs.reshape(DB, n, D),
            rows_p(kat), rows_p(vat), lf_p[:, :fox_heads].reshape(1, B, S, fox_heads), rows_p(kbt), rows_p(vbt),
            ckv[pr].reshape(1, B, S, kv_lora), kr[pr].reshape(1, B, S, QK_ROPE),
            rows_s(ka), rows_s(va), lf[:, :fox_heads].reshape(1, DB, n, fox_heads), rows_s(kb), rows_s(vb),
            ckv[sr].reshape(1, DB, n, kv_lora), kr[sr].reshape(1, DB, n, QK_ROPE))
```

```python
import functools

import jax
import jax.numpy as jnp
from jax import lax
from jax.experimental import pallas as pl
from jax.experimental.pallas import tpu as pltpu

F32 = jnp.float32
BF16 = jnp.bfloat16
NEG_INF = -1e30
LOG2E = 1.4426950408889634

LANES = 128
HEAD_DIM = 64
PAIR_W = 2 * HEAD_DIM
CHUNK = 64
LN_EPS = 1e-5
RMS_EPS = 1e-6
ROPE_BASE = 10000.0
VMEM_LIMIT = 56 * 1024 * 1024


def _cparams(*sem):
    return pltpu.CompilerParams(dimension_semantics=sem, vmem_limit_bytes=VMEM_LIMIT)


def _dot(a, b):
    return jnp.dot(a, b, preferred_element_type=F32)


def _dot_nt(a, b):
    return lax.dot_general(a, b, (((1,), (1,)), ((), ())), preferred_element_type=F32)


def _split_bf16(x, parts):
    out = []
    r = x
    for _ in range(parts):
        h = r.astype(BF16)
        out.append(h)
        r = r - h.astype(F32)
    return out


def _split_bf16_trunc(x, parts):
    out = []
    r = x
    for _ in range(parts):
        bits = lax.bitcast_convert_type(r, jnp.uint32) & jnp.uint32(0xFFFF0000)
        h = lax.bitcast_convert_type(bits, F32)
        out.append(h.astype(BF16))
        r = r - h
    return out


def _log_sigmoid(x):
    return jnp.minimum(x, 0.0) - jnp.log(1.0 + jnp.exp(-jnp.abs(x)))


def _pick_tile(n, pref, mult=8):
    t = min(pref, n)
    while n % t or t % mult:
        t -= 1
    return t


def _ab_proj_kernel(x_ref, w_ref, bf_ref, qa_ref, ka_ref, va_ref, qb_ref, kb_ref, vb_ref,
                    ka16_ref, va16_ref, kb16_ref, vb16_ref, lf_ref, *, hw, qscale):
    xb = x_ref[...].astype(BF16)

    def seg(j):
        return _dot(xb, w_ref[:, j * hw:(j + 1) * hw])

    qa_ref[...] = (seg(0) * qscale).astype(BF16)
    z = seg(1)
    ka_ref[...] = z
    ka16_ref[...] = z.astype(BF16)
    z = seg(2)
    va_ref[...] = z
    va16_ref[...] = z.astype(BF16)
    qb_ref[...] = (seg(3) * qscale).astype(BF16)
    z = seg(4)
    kb_ref[...] = z
    kb16_ref[...] = z.astype(BF16)
    z = seg(5)
    vb_ref[...] = z
    vb16_ref[...] = z.astype(BF16)
    f = _dot(xb, w_ref[:, 6 * hw:6 * hw + LANES]) + bf_ref[...]
    lf_ref[...] = _log_sigmoid(f)


def _ab_proj(x, w, bf, hw):
    T, D = x.shape
    tm = _pick_tile(T, 256)
    row = lambda w_: pl.BlockSpec((tm, w_), lambda i: (i, 0))
    f32o = jax.ShapeDtypeStruct((T, hw), F32)
    b16o = jax.ShapeDtypeStruct((T, hw), BF16)
    return pl.pallas_call(
        functools.partial(_ab_proj_kernel, hw=hw, qscale=HEAD_DIM ** -0.5 * LOG2E),
        grid=(T // tm,),
        in_specs=[row(D), pl.BlockSpec(w.shape, lambda i: (0, 0)), pl.BlockSpec(bf.shape, lambda i: (0, 0))],
        out_specs=[row(hw)] * 10 + [row(LANES)],
        out_shape=[b16o, f32o, f32o, b16o, f32o, f32o, b16o, b16o, b16o, b16o,
                   jax.ShapeDtypeStruct((T, LANES), F32)],
        compiler_params=_cparams("parallel"),
        name="ab_proj",
    )(x, w, bf)


def _ab_proj_prompt_kernel(x_ref, w_ref, bf_ref, wt_ref, qone_ref, qa_ref, ka_ref, qb_ref, kb_ref, lf_ref,
                           kat_ref, vat_ref, kbt_ref, vbt_ref, vat16_ref, vbt16_ref, *, hw, heads, qscale, sb_tk):
    xb = x_ref[...].astype(BF16)
    tm = xb.shape[0]
    wide = heads * HEAD_W

    def seg(j):
        return _dot(xb, w_ref[:, j * wide:(j + 1) * wide])

    def seg_t(j):
        return _dot_nt(wt_ref[j * hw:(j + 1) * hw, :], xb).reshape(heads, hw // heads, tm)

    qa_ref[...] = (seg(0) * qscale + qone_ref[...]).astype(BF16)
    ka_ref[...] = seg(1).astype(BF16)
    qb_ref[...] = (seg(2) * qscale).astype(BF16)
    kb_ref[...] = seg(3).astype(BF16)
    lf_ref[...] = _log_sigmoid(_dot(xb, w_ref[:, 4 * wide:4 * wide + LANES]) + bf_ref[...])
    kat_ref[0] = seg_t(0)
    z = seg_t(1)
    vat_ref[0] = z
    vat16_ref[0, :, :HEAD_DIM, :] = z.astype(BF16)
    vat16_ref[0, :, HEAD_DIM:, :] = jnp.ones((heads, VT_ROWS - HEAD_DIM, tm), BF16)
    kbt_ref[0] = seg_t(2)
    z = seg_t(3)
    vbt_ref[0] = z
    for c in range(tm // sb_tk):
        vbt16_ref[c] = z[:, :, c * sb_tk:(c + 1) * sb_tk].astype(BF16)


def _ab_proj_prompt(x, w_tok, bf, w_t, q_ones, B, S, heads, sb_tk):
    TP, D = x.shape
    hw = w_t.shape[0] // 4
    wide = heads * HEAD_W
    tm = FLASH_KEY_TILE
    nj = S // tm
    const = lambda a: pl.BlockSpec(a.shape, lambda b, j: (0,) * a.ndim)
    row = lambda w_: pl.BlockSpec((tm, w_), lambda b, j: (b * nj + j, 0))
    t_spec = pl.BlockSpec((1, heads, hw // heads, tm), lambda b, j: (b, 0, 0, j))
    b16 = jax.ShapeDtypeStruct((TP, wide), BF16)
    t32 = jax.ShapeDtypeStruct((B, heads, hw // heads, S), F32)
    return pl.pallas_call(
        functools.partial(_ab_proj_prompt_kernel, hw=hw, heads=heads, qscale=HEAD_DIM ** -0.5 * LOG2E, sb_tk=sb_tk),
        grid=(B, nj),
        in_specs=[row(D), const(w_tok), const(bf), const(w_t), const(q_ones)],
        out_specs=[row(wide)] * 4 + [row(LANES)] + [t_spec] * 4 + [
            pl.BlockSpec((1, heads, VT_ROWS, tm), lambda b, j: (b * nj + j, 0, 0, 0)),
            pl.BlockSpec((tm // sb_tk, heads, hw // heads, sb_tk), lambda b, j: (b * nj + j, 0, 0, 0))],
        out_shape=[b16] * 4 + [jax.ShapeDtypeStruct((TP, LANES), F32)] + [t32] * 4 + [
            jax.ShapeDtypeStruct((TP // tm, heads, VT_ROWS, tm), BF16),
            jax.ShapeDtypeStruct((TP // sb_tk, heads, hw // heads, sb_tk), BF16)],
        compiler_params=_cparams("parallel", "parallel"),
        name="ab_proj_prompt",
    )(x, w_tok, bf, w_t, q_ones)


def _cumsum_kernel(x_ref, o_ref, carry_ref, *, tl):
    @pl.when(pl.program_id(0) == 0)
    def _():
        carry_ref[...] = jnp.zeros_like(carry_ref)

    x = x_ref[...]
    rows = x.shape[0]
    r = lax.broadcasted_iota(jnp.int32, (tl, tl), 0)
    c = lax.broadcasted_iota(jnp.int32, (tl, tl), 1)
    upper = (r <= c).astype(BF16)
    parts = jnp.concatenate(_split_bf16(x, 4), axis=0)
    y = _dot(parts, upper)
    cum = (y[0:rows] + y[rows:2 * rows]) + (y[2 * rows:3 * rows] + y[3 * rows:]) + carry_ref[:, 0:1]
    o_ref[...] = cum
    carry_ref[...] = jnp.broadcast_to(cum[:, tl - 1:tl], carry_ref.shape)


def _cumsum_rows(x):
    B, H, L = x.shape
    tl = _pick_tile(L, 512, LANES)
    out = pl.pallas_call(
        functools.partial(_cumsum_kernel, tl=tl),
        grid=(L // tl,),
        in_specs=[pl.BlockSpec((B * H, tl), lambda j: (0, j))],
        out_specs=pl.BlockSpec((B * H, tl), lambda j: (0, j)),
        out_shape=jax.ShapeDtypeStruct((B * H, L), F32),
        scratch_shapes=[pltpu.VMEM((B * H, LANES), F32)],
        compiler_params=_cparams("arbitrary"),
        name="cumsum_rows",
    )(x.reshape(B * H, L))
    return out.reshape(B, H, L)


def _stack_pair(qp):
    lo = lax.broadcasted_iota(jnp.int32, qp.shape, 1) < HEAD_DIM
    zero = jnp.zeros_like(qp)
    return jnp.concatenate([jnp.where(lo, qp, zero), jnp.where(lo, zero, qp)], axis=0)


def _unstack_pair(o):
    lo = lax.broadcasted_iota(jnp.int32, o.shape[1:], 1) < HEAD_DIM
    return jnp.where(lo, o[0], o[1])


def _softmax_step(s, vb, carry):
    m, l, acc = carry
    two, tq, tk = s.shape
    m_new = jnp.maximum(m, jnp.max(s, axis=-1, keepdims=True))
    alpha = jnp.exp2(m - m_new)
    p = jnp.exp2(s - m_new)
    l = alpha * l + jnp.sum(p, axis=-1, keepdims=True)
    pv = _dot(p.reshape(two * tq, tk).astype(BF16), vb).reshape(two, tq, vb.shape[-1])
    return m_new, l, alpha * acc + pv


def _log2_sigmoid_pair(z2):
    l1 = jnp.log2(1.0 + jnp.exp2(-jnp.abs(z2)))
    return jnp.minimum(z2, 0.0) - l1, jnp.minimum(-z2, 0.0) - l1


def _softmax_init(tq, width):
    return (jnp.full((2, tq, 1), NEG_INF, F32), jnp.zeros((2, tq, 1), F32), jnp.zeros((2, tq, width), F32))


def _sb_step(z, vb, strict_upper, carry, mask=None):
    run, acc = carry
    log_beta, log_rest = _log2_sigmoid_pair(z)
    if mask is not None:
        log_beta = jnp.where(mask, log_beta, NEG_INF)
        log_rest = jnp.where(mask, log_rest, 0.0)
    hi, lo = _split_bf16(log_rest, 2)
    later = _dot(hi, strict_upper) + _dot(lo, strict_upper)
    a = jnp.exp2(log_beta + later + run)
    acc = acc + _dot(a.astype(BF16), vb)
    run = run + jnp.sum(log_rest, axis=-1, keepdims=True)
    return run, acc


def _strict_upper(tk):
    r = lax.broadcasted_iota(jnp.int32, (tk, tk), 0)
    c = lax.broadcasted_iota(jnp.int32, (tk, tk), 1)
    return (r > c).astype(BF16)


HEAD_W = LANES
VT_ROWS = HEAD_DIM + 16
SB_DEAD_LOG2 = -160.0


def _pipeline_ahead(stage, first, count, cur, nxt):
    if first < count:
        stage(first, cur)
    elif nxt is not None:
        stage(first - count, nxt)


def _flash_prompt_kernel(q_ref, k_ref, vt_ref, into_ref, o_ref, s_ref, m_ref, acc_ref, *, tq, tk, heads, chunk):
    del into_ref
    i = pl.program_id(2)
    sub = tq // tk
    key = lax.broadcasted_iota(jnp.int32, (tk, tq), 0)
    query = lax.broadcasted_iota(jnp.int32, (tk, tq), 1)
    m_ref[...] = jnp.full(m_ref.shape, NEG_INF, F32)
    acc_ref[...] = jnp.zeros_like(acc_ref)

    def scores(h, kt):
        rows = pl.ds(pl.multiple_of(kt * tk, tk), tk)
        lanes = slice(h * HEAD_W, (h + 1) * HEAD_W)
        s_ref[h] = _dot_nt(k_ref[0, rows, lanes], q_ref[0, :, lanes])

    def absorb(h, kt, visible):
        s_t = s_ref[h]
        if visible is not None:
            s_t = jnp.where(visible, s_t, NEG_INF)
        m = m_ref[h]
        m_new = jnp.maximum(m, jnp.max(s_t, axis=0, keepdims=True))
        p_t = jnp.exp2(s_t - m_new).astype(BF16)
        pv = _dot(vt_ref[kt, h], p_t)
        acc_ref[h] = jnp.exp2(m - m_new) * acc_ref[h] + pv
        m_ref[h] = m_new

    scores(0, 0)
    scores(1, 0)

    def body(kt, carry):
        for h in range(heads):
            _pipeline_ahead(scores, h + 2, heads, kt, kt + 1)
            absorb(h, kt, None)
        return carry

    lax.fori_loop(0, sub * i, body, 0)
    for s in range(sub):
        kt = sub * i + s
        visible = ((s * tk + key) // chunk) <= (query // chunk)
        for h in range(heads):
            _pipeline_ahead(scores, h + 2, heads, kt, kt + 1 if s + 1 < sub else None)
            absorb(h, kt, visible)
    for j in range(heads // 2):
        a0, a1 = acc_ref[2 * j], acc_ref[2 * j + 1]
        o_t = jnp.concatenate([a0[:HEAD_DIM] / a0[HEAD_DIM:HEAD_DIM + 1],
                               a1[:HEAD_DIM] / a1[HEAD_DIM:HEAD_DIM + 1]], axis=0)
        o_ref[:, j * PAIR_W:(j + 1) * PAIR_W] = jnp.transpose(o_t).astype(o_ref.dtype)


def _flash_prompt(qx, kx, vt4, tq, heads_per_step, chunk, name, into):
    B, S, W = qx.shape
    nq = S // tq
    hs = heads_per_step
    tk = vt4.shape[3]
    nk = S // tk
    return pl.pallas_call(
        functools.partial(_flash_prompt_kernel, tq=tq, tk=tk, heads=hs, chunk=chunk),
        grid=(B, W // (HEAD_W * hs), S // tq),
        in_specs=[pl.BlockSpec((1, tq, HEAD_W * hs), lambda b, g, i: (b, i, g)),
                  pl.BlockSpec((1, S, HEAD_W * hs), lambda b, g, i: (b, 0, g)),
                  pl.BlockSpec((nk, hs, VT_ROWS, tk), lambda b, g, i: (b, g, 0, 0)),
                  pl.BlockSpec(memory_space=pl.ANY)],
        out_specs=pl.BlockSpec((tq, HEAD_DIM * hs), lambda b, g, i: (b * nq + i, g)),
        out_shape=jax.ShapeDtypeStruct(into.shape, into.dtype),
        input_output_aliases={3: 0},
        scratch_shapes=[pltpu.VMEM((hs, tk, tq), F32), pltpu.VMEM((hs, 1, tq), F32),
                        pltpu.VMEM((hs, VT_ROWS, tq), F32)],
        compiler_params=_cparams("parallel", "parallel", "arbitrary"),
        name=name,
    )(qx, kx, vt4, into)


def _sb_prompt_kernel(q_ref, k_ref, vt_ref, into_ref, o_ref, z_ref, lw_ref, tot_ref, run_ref, acc_ref,
                      *, tq, tk, heads):
    del into_ref
    i = pl.program_id(1)
    sub = tq // tk
    key = lax.broadcasted_iota(jnp.int32, (tk, tq), 0)
    query = lax.broadcasted_iota(jnp.int32, (tk, tq), 1)
    r = lax.broadcasted_iota(jnp.int32, (tk, tk), 0)
    c = lax.broadcasted_iota(jnp.int32, (tk, tk), 1)
    after = (c > r).astype(BF16)
    run_ref[...] = jnp.zeros_like(run_ref)
    acc_ref[...] = jnp.zeros_like(acc_ref)

    def logits(h, kt):
        rows = pl.ds(pl.multiple_of(kt * tk, tk), tk)
        lanes = slice(h * HEAD_W, (h + 1) * HEAD_W)
        z_ref[h] = _dot_nt(k_ref[0, rows, lanes], q_ref[0, :, lanes])

    def log_weights(before, h, kt):
        log_beta, log_rest = _log2_sigmoid_pair(z_ref[h])
        if before is not None:
            log_beta = jnp.where(before, log_beta, NEG_INF)
            log_rest = jnp.where(before, log_rest, 0.0)
        hi, lo = _split_bf16(log_rest, 2)
        later = _dot(after, hi) + _dot(after, lo)
        lw_ref[h] = log_beta + later
        tot_ref[h] = later[0:1] + log_rest[0:1]

    def accumulate(h, kt):
        run = run_ref[h]
        a_t = jnp.exp2(lw_ref[h] + run).astype(BF16)
        acc_ref[h] += _dot(vt_ref[kt, h], a_t)
        run_ref[h] = run + tot_ref[h]

    def step(kt, masked, nxt, nxt_masked):
        for h in range(heads):
            _pipeline_ahead(logits, h + 2, heads, kt, nxt)
            if h + 1 < heads:
                log_weights(masked, h + 1, kt)
            elif nxt is not None:
                log_weights(nxt_masked, 0, nxt)
            accumulate(h, kt)

    unmasked = sub * i
    first = unmasked + sub - 1
    masks = [((sub - 1 - s) * tk + key) < query for s in range(sub)]
    logits(0, first)
    logits(1, first)
    log_weights(masks[0], 0, first)
    for s in range(sub):
        kt = first - s
        if s + 1 < sub:
            step(kt, masks[s], kt - 1, masks[s + 1])
        else:
            step(kt, masks[s], jnp.maximum(kt - 1, 0), None)

    def alive():
        return (jnp.max(run_ref[...]) > SB_DEAD_LOG2).astype(jnp.int32)

    def body(carry):
        kt, _ = carry
        step(kt, None, jnp.maximum(kt - 1, 0), None)
        return kt - 1, alive()

    lax.while_loop(lambda c: jnp.logical_and(c[0] >= 0, c[1] > 0), body, (unmasked - 1, alive()))

    for j in range(heads // 2):
        o_t = jnp.concatenate([acc_ref[2 * j], acc_ref[2 * j + 1]], axis=0)
        o_ref[:, j * PAIR_W:(j + 1) * PAIR_W] = jnp.transpose(o_t).astype(o_ref.dtype)


def _sb_prompt(qx, kx, vt4, tq, into, lane_block):
    B, S, W = qx.shape
    heads = W // HEAD_W
    tk = vt4.shape[3]
    nq = S // tq
    return pl.pallas_call(
        functools.partial(_sb_prompt_kernel, tq=tq, tk=tk, heads=heads),
        grid=(B, nq),
        in_specs=[pl.BlockSpec((1, tq, W), lambda b, i: (b, i, 0)),
                  pl.BlockSpec((1, S, W), lambda b, i: (b, 0, 0)),
                  pl.BlockSpec((S // tk, heads, HEAD_DIM, tk), lambda b, i: (b, 0, 0, 0)),
                  pl.BlockSpec(memory_space=pl.ANY)],
        out_specs=pl.BlockSpec((tq, heads * HEAD_DIM), lambda b, i: (b * nq + i, lane_block)),
        out_shape=jax.ShapeDtypeStruct(into.shape, into.dtype),
        input_output_aliases={3: 0},
        scratch_shapes=[pltpu.VMEM((heads, tk, tq), F32), pltpu.VMEM((heads, tk, tq), F32),
                        pltpu.VMEM((heads, 1, tq), F32), pltpu.VMEM((heads, 1, tq), F32),
                        pltpu.VMEM((heads, HEAD_DIM, tq), F32)],
        compiler_params=_cparams("parallel", "arbitrary"),
        name="sb_prompt",
    )(qx, kx, vt4, into)


def _prep_ab_weights(w_in, b_f):
    D = w_in.shape[0]
    H = b_f.shape[0]
    hw = (w_in.shape[1] - H) // 6
    main = jnp.concatenate([w_in[:, :3 * hw], w_in[:, 3 * hw + H:]], axis=1)
    wf = jnp.zeros((D, LANES), w_in.dtype).at[:, :H].set(w_in[:, 3 * hw:3 * hw + H])
    bf = jnp.zeros((1, LANES), F32).at[0, :H].set(b_f.astype(F32))
    return jnp.concatenate([main, wf], axis=1).astype(BF16), bf


def _prompt_ab_weights(w_ab, hw, heads):
    D = w_ab.shape[0]
    seg = lambda j: w_ab[:, j * hw:(j + 1) * hw]
    pad = lambda w: jnp.concatenate([w.reshape(D, heads, hw // heads),
                                     jnp.zeros((D, heads, HEAD_W - hw // heads), w.dtype)], axis=-1).reshape(D, -1)
    w_tok = jnp.concatenate([pad(seg(0)), pad(seg(1)), pad(seg(3)), pad(seg(4)), w_ab[:, 6 * hw:]], axis=1)
    w_t = jnp.transpose(jnp.concatenate([seg(1), seg(2), seg(4), seg(5)], axis=1))
    return w_tok, w_t


DECAY_TERMS = 3


def _insert_decay_kernel(k_ref, c_ref, sel_ref, o_ref):
    terms = sum(_dot(part, sel_ref[j]) for j, part in enumerate(_split_bf16(c_ref[...], DECAY_TERMS)))
    o_ref[...] = (k_ref[...].astype(F32) + terms).astype(BF16)


def _fox_insert_decay(kx, lf, B, S, H):
    lt = jnp.transpose(lf[:, :H].reshape(B, S, H), (0, 2, 1))
    cum = jnp.transpose(_cumsum_rows(lt), (0, 2, 1)).reshape(B * S, H) * (-LOG2E)
    c = jnp.pad(cum, ((0, 0), (0, LANES - H)))
    src = jnp.arange(LANES)[None, :, None]
    dst = jnp.arange(H * HEAD_W)[None, None, :]
    term = jnp.arange(DECAY_TERMS)[:, None, None]
    sel = jnp.logical_and(src < H, dst == src * HEAD_W + HEAD_DIM + term).astype(BF16)
    tm = _pick_tile(B * S, 512)
    return pl.pallas_call(
        _insert_decay_kernel,
        grid=(B * S // tm,),
        in_specs=[pl.BlockSpec((tm, H * HEAD_W), lambda i: (i, 0)), pl.BlockSpec((tm, LANES), lambda i: (i, 0)),
                  pl.BlockSpec(sel.shape, lambda i: (0, 0, 0))],
        out_specs=pl.BlockSpec((tm, H * HEAD_W), lambda i: (i, 0)),
        out_shape=jax.ShapeDtypeStruct(kx.shape, BF16),
        input_output_aliases={0: 0},
        compiler_params=_cparams("parallel"),
        name="fox_insert_decay",
    )(kx, c, sel)


def _query_decay_ones(H):
    lane = jnp.arange(H * HEAD_W) % HEAD_W
    return jnp.logical_and(lane >= HEAD_DIM, lane < HEAD_DIM + DECAY_TERMS).astype(F32)[None]


def _fox_sample_kernel(q_ref, knt_ref, vnt_ref, kct_ref, vct_ref, cq_ref, ckp_ref, ckn_ref, o_ref,
                       s_ref, m_ref, l_ref, acc_ref, *, n, heads):
    kt = pl.program_id(1)

    @pl.when(kt == 0)
    def _():
        m_ref[...] = jnp.full(m_ref.shape, NEG_INF, F32)
        l_ref[...] = jnp.zeros_like(l_ref)
        acc_ref[...] = jnp.zeros_like(acc_ref)

    def absorb(h, s, v_t, ck, mask):
        s = s + (cq_ref[0, h] - ck)
        if mask is not None:
            s = jnp.where(mask, s, NEG_INF)
        m = m_ref[h]
        m_new = jnp.maximum(m, jnp.max(s, axis=-1, keepdims=True))
        alpha = jnp.exp2(m - m_new)
        p = jnp.exp2(s - m_new)
        l_ref[h] = alpha * l_ref[h] + jnp.sum(p, axis=-1, keepdims=True)
        acc_ref[h] = alpha * acc_ref[h] + _dot_nt(p.astype(BF16), v_t)
        m_ref[h] = m_new

    def scores(h, _=None):
        s_ref[h] = _dot(q_ref[0, h], kct_ref[0, h].astype(BF16))

    scores(0)
    scores(1)
    for h in range(heads):
        _pipeline_ahead(scores, h + 2, heads, None, None)
        absorb(h, s_ref[h], vct_ref[0, h].astype(BF16), ckp_ref[0, h:h + 1, :], None)

    @pl.when(kt == pl.num_programs(1) - 1)
    def _():
        row = lax.broadcasted_iota(jnp.int32, (n, n), 0)
        col = lax.broadcasted_iota(jnp.int32, (n, n), 1)
        for h in range(heads):
            absorb(h, _dot(q_ref[0, h], knt_ref[0, h]), vnt_ref[0, h], ckn_ref[0, h:h + 1, :], col <= row)
            o_ref[0, h] = (acc_ref[h] / l_ref[h]).astype(o_ref.dtype)


def _fox_sample(q, knt, vnt, kct, vct, cq, ckp, ckn, tk):
    DB, H, n, dh = q.shape
    P = kct.shape[-1]
    per_b = lambda a: pl.BlockSpec((1,) + a.shape[1:], lambda b, j: (b,) + (0,) * (a.ndim - 1))
    cache = pl.BlockSpec((1, H, dh, tk), lambda b, j: (b, 0, 0, j))
    return pl.pallas_call(
        functools.partial(_fox_sample_kernel, n=n, heads=H),
        grid=(DB, P // tk),
        in_specs=[per_b(q), per_b(knt), per_b(vnt), cache, cache, per_b(cq),
                  pl.BlockSpec((1, H, tk), lambda b, j: (b, 0, j)), per_b(ckn)],
        out_specs=per_b(q),
        out_shape=jax.ShapeDtypeStruct(q.shape, BF16),
        scratch_shapes=[pltpu.VMEM((H, n, tk), F32), pltpu.VMEM((H, n, 1), F32), pltpu.VMEM((H, n, 1), F32),
                        pltpu.VMEM((H, n, dh), F32)],
        compiler_params=_cparams("parallel", "arbitrary"),
        name="fox_sample",
    )(q, knt, vnt, kct, vct, cq, ckp, ckn)


def _sb_sample_kernel(q_ref, knt_ref, vnt_ref, kct_ref, vct_ref, o_ref, z_ref, run_ref, acc_ref, *, n, heads, sub):
    kt = pl.program_id(1)
    upper = _strict_upper(sub)

    def absorb(h, z, v_t, upper_m, width, mask=None):
        log_beta, log_rest = _log2_sigmoid_pair(z)
        if mask is not None:
            log_beta = jnp.where(mask, log_beta, NEG_INF)
            log_rest = jnp.where(mask, log_rest, 0.0)
        hi, lo = _split_bf16(log_rest, 2)
        run = run_ref[h]
        parts = []
        for c in reversed(range(z.shape[1] // width)):
            keys = slice(c * width, (c + 1) * width)
            later = _dot(hi[:, keys], upper_m) + _dot(lo[:, keys], upper_m)
            parts.append(jnp.exp2(log_beta[:, keys] + later + run).astype(BF16))
            run = run + jnp.sum(log_rest[:, keys], axis=-1, keepdims=True)
        a = parts[0] if len(parts) == 1 else jnp.concatenate(parts[::-1], axis=1)
        acc_ref[h] += _dot_nt(a, v_t)
        run_ref[h] = run

    def logits(h, _=None):
        z_ref[h] = _dot(q_ref[0, h], kct_ref[0, h].astype(BF16))

    @pl.when(kt == 0)
    def _():
        row = lax.broadcasted_iota(jnp.int32, (n, n), 0)
        col = lax.broadcasted_iota(jnp.int32, (n, n), 1)
        upper_n = _strict_upper(n)
        run_ref[...] = jnp.zeros_like(run_ref)
        acc_ref[...] = jnp.zeros_like(acc_ref)
        for h in range(heads):
            absorb(h, _dot(q_ref[0, h], knt_ref[0, h]), vnt_ref[0, h], upper_n, n, col < row)

    logits(0)
    logits(1)
    for h in range(heads):
        _pipeline_ahead(logits, h + 2, heads, None, None)
        absorb(h, z_ref[h], vct_ref[0, h].astype(BF16), upper, sub)

    @pl.when(kt == pl.num_programs(1) - 1)
    def _():
        o_ref[0] = acc_ref[...].astype(o_ref.dtype)


def _sb_sample(q, knt, vnt, kct, vct, tk, sub):
    DB, H, n, dh = q.shape
    P = kct.shape[-1]
    nk = P // tk
    per_b = lambda a: pl.BlockSpec((1,) + a.shape[1:], lambda b, j: (b,) + (0,) * (a.ndim - 1))
    cache = pl.BlockSpec((1, H, dh, tk), lambda b, j: (b, 0, 0, nk - 1 - j))
    return pl.pallas_call(
        functools.partial(_sb_sample_kernel, n=n, heads=H, sub=sub),
        grid=(DB, nk),
        in_specs=[per_b(q), per_b(knt), per_b(vnt), cache, cache],
        out_specs=per_b(q),
        out_shape=jax.ShapeDtypeStruct(q.shape, BF16),
        scratch_shapes=[pltpu.VMEM((H, n, tk), F32), pltpu.VMEM((H, n, 1), F32), pltpu.VMEM((H, n, dh), F32)],
        compiler_params=_cparams("parallel", "arbitrary"),
        name="sb_sample",
    )(q, knt, vnt, kct, vct)


def _fox_sample_cum(lf_new, lf_past_t):
    DB, H, P = lf_past_t.shape
    n = lf_new.shape[1]
    L = -(-(P + n) // LANES) * LANES
    both = jnp.concatenate([lf_past_t.astype(F32), jnp.transpose(lf_new, (0, 2, 1)),
                            jnp.zeros((DB, H, L - P - n), F32)], axis=2)
    cum_t = _cumsum_rows(both) * LOG2E
    ckn = cum_t[:, :, P:P + n]
    return ckn[..., None], cum_t[:, :, :P], ckn


def _heads_major(a, heads, transpose_rows):
    DB, n, _ = a.shape
    a4 = a.reshape(DB, n, heads, -1)
    return jnp.transpose(a4, (0, 2, 3, 1) if transpose_rows else (0, 2, 1, 3))


ROUTER_ROWS = 48


def _layer_norm(y, g, b):
    mu = jnp.mean(y, axis=-1, keepdims=True)
    yc = y - mu
    var = jnp.mean(yc * yc, axis=-1, keepdims=True)
    return yc * lax.rsqrt(var + LN_EPS) * g + b


def _first_argmax(v, ridx):
    vmax = jnp.max(v, axis=0, keepdims=True)
    idx = jnp.min(jnp.where(v == vmax, ridx, v.shape[0]), axis=0, keepdims=True)
    return vmax, idx


def _two_part_specs(tm, width, head_tiles):
    return (pl.BlockSpec((tm, width), lambda i: (jnp.minimum(i, head_tiles - 1), 0)),
            pl.BlockSpec((tm, width), lambda i: (jnp.maximum(i - head_tiles, 0), 0)))


def _mix_out_kernel(o_ref, w_ref, xh_ref, xt_ref, g_ref, b_ref, wrh_ref, wrl_ref, rb_ref, h_ref, ids_ref, wts_ref,
                    *, alpha, n_groups, head_tiles):
    x = jnp.where(pl.program_id(0) < head_tiles, xh_ref[...], xt_ref[...])
    h = _layer_norm(alpha * x + _dot(o_ref[...], w_ref[...]), g_ref[...], b_ref[...])
    h_ref[...] = h
    hh, hl = _split_bf16(h, 2)
    wrh = wrh_ref[...]
    low = _dot_nt(wrl_ref[...], hh)
    lg = _dot_nt(wrh, hh) + (_dot_nt(wrh, hl) + (low[:ROUTER_ROWS] + low[ROUTER_ROWS:])) + rb_ref[...]
    tm = lg.shape[1]
    ridx = lax.broadcasted_iota(jnp.int32, (8, tm), 0)
    g = jnp.where(ridx < n_groups, lg[0:8], NEG_INF)
    gmax, gidx = _first_argmax(g, ridx)
    gate = 1.0 / jnp.sum(jnp.exp(g - gmax), axis=0, keepdims=True)
    esel = lg[8:16]
    for gg in range(1, n_groups):
        esel = jnp.where(gidx == gg, lg[8 + 8 * gg:16 + 8 * gg], esel)
    v1, i1 = _first_argmax(esel, ridx)
    v2, i2 = _first_argmax(jnp.where(ridx == i1, NEG_INF, esel), ridx)
    t = jnp.exp(v2 - v1)
    w1 = 1.0 / (1.0 + t)
    ids_ref[...] = jnp.where(ridx == 0, gidx * 8 + i1, jnp.where(ridx == 1, gidx * 8 + i2, 0))
    wts_ref[...] = jnp.where(ridx == 0, gate * w1, jnp.where(ridx == 1, gate * (t * w1), 0.0))


def _mix_out(o, w, x_head, x_tail, g, b, wrh, wrl, rb, alpha, n_groups, tm):
    D = x_head.shape[1]
    T = x_head.shape[0] + x_tail.shape[0]
    head_tiles = x_head.shape[0] // tm
    const = lambda a: pl.BlockSpec(a.shape, lambda i: (0,) * a.ndim)
    rb_t = jnp.broadcast_to(rb, (ROUTER_ROWS, tm))
    return pl.pallas_call(
        functools.partial(_mix_out_kernel, alpha=alpha, n_groups=n_groups, head_tiles=head_tiles),
        grid=(T // tm,),
        in_specs=[pl.BlockSpec((tm, o.shape[1]), lambda i: (i, 0)), const(w),
                  *_two_part_specs(tm, D, head_tiles), const(g), const(b), const(wrh), const(wrl), const(rb_t)],
        out_specs=[pl.BlockSpec((tm, D), lambda i: (i, 0)), pl.BlockSpec((8, tm), lambda i: (0, i)),
                   pl.BlockSpec((8, tm), lambda i: (0, i))],
        out_shape=[jax.ShapeDtypeStruct((T, D), F32), jax.ShapeDtypeStruct((8, T), jnp.int32),
                   jax.ShapeDtypeStruct((8, T), F32)],
        compiler_params=_cparams("parallel"),
        name="mix_out_ln_router",
    )(o, w, x_head, x_tail, g, b, wrh, wrl, rb_t)


def _prep_router(w_group, b_group, w_router, b_router):
    D, G = w_group.shape
    E = w_router.shape[-1]
    wr = jnp.zeros((ROUTER_ROWS, D), F32)
    wr = wr.at[:G].set(w_group.T.astype(F32))
    wr = wr.at[8:8 + G * E].set(jnp.transpose(w_router, (0, 2, 1)).reshape(G * E, D).astype(F32))
    rb = jnp.zeros((ROUTER_ROWS, 1), F32)
    rb = rb.at[:G, 0].set(b_group.astype(F32)).at[8:8 + G * E, 0].set(b_router.reshape(-1).astype(F32))
    hi, mid, lo = _split_bf16_trunc(wr, 3)
    return hi, jnp.concatenate([mid, lo], axis=0), rb


def _gather_rows(idx_ref, n, src_hbm, dst, sem):
    for r in range(n):
        pltpu.make_async_copy(src_hbm.at[pl.ds(idx_ref[0, 0, r], 1)], dst.at[pl.ds(r, 1)], sem).start()


def _wait_rows(n, src_hbm, dst, sem):
    pltpu.make_async_copy(src_hbm.at[pl.ds(0, n)], dst, sem).wait()


def _moe_experts_kernel(te_ref, tv_ref, src_ref, nxt_ref, x_hbm, wg_ref, wu_ref, wd_ref, y_ref, xbuf, sem, *, tm):
    i = pl.program_id(0)
    nt = pl.num_programs(0)
    slot = i % 2

    @pl.when(jnp.logical_and(i == 0, tv_ref[0] > 0))
    def _():
        _gather_rows(src_ref, tm, x_hbm, xbuf.at[0], sem.at[0])

    @pl.when(jnp.logical_and(i + 1 < nt, tv_ref[jnp.minimum(i + 1, nt - 1)] > 0))
    def _():
        _gather_rows(nxt_ref, tm, x_hbm, xbuf.at[1 - slot], sem.at[1 - slot])

    @pl.when(tv_ref[i] > 0)
    def _():
        _wait_rows(tm, x_hbm, xbuf.at[slot], sem.at[slot])
        xb = xbuf[slot].astype(BF16)
        a = _dot(xb, wg_ref[0].astype(BF16))
        u = _dot(xb, wu_ref[0].astype(BF16))
        hid = (a / (1.0 + jnp.exp(-a))) * u
        y_ref[...] = _dot(hid.astype(BF16), wd_ref[0].astype(BF16))

    @pl.when(tv_ref[i] == 0)
    def _():
        y_ref[...] = jnp.zeros_like(y_ref)


def _moe_experts(x, w_gate, w_up, w_down, tile_expert, tile_valid, src, tm):
    T, D = x.shape
    F = w_gate.shape[-1]
    NT = tile_expert.shape[0]
    grid_spec = pltpu.PrefetchScalarGridSpec(
        num_scalar_prefetch=2,
        grid=(NT,),
        in_specs=[pl.BlockSpec((1, 1, tm), lambda i, te, tv: (i, 0, 0), memory_space=pltpu.SMEM),
                  pl.BlockSpec((1, 1, tm), lambda i, te, tv: (jnp.minimum(i + 1, NT - 1), 0, 0),
                               memory_space=pltpu.SMEM),
                  pl.BlockSpec(memory_space=pl.ANY),
                  pl.BlockSpec((1, D, F), lambda i, te, tv: (te[i], 0, 0)),
                  pl.BlockSpec((1, D, F), lambda i, te, tv: (te[i], 0, 0)),
                  pl.BlockSpec((1, F, D), lambda i, te, tv: (te[i], 0, 0))],
        out_specs=pl.BlockSpec((tm, D), lambda i, te, tv: (i, 0)),
        scratch_shapes=[pltpu.VMEM((2, tm, D), F32), pltpu.SemaphoreType.DMA((2,))],
    )
    return pl.pallas_call(
        functools.partial(_moe_experts_kernel, tm=tm),
        grid_spec=grid_spec,
        out_shape=jax.ShapeDtypeStruct((NT * tm, D), F32),
        compiler_params=_cparams("arbitrary"),
        name="moe_experts",
    )(tile_expert, tile_valid, src, src, x, w_gate, w_up, w_down)


def _moe_combine_kernel(pos_ref, nxt_ref, ys_hbm, h_ref, w_ref, g_ref, b_ref, oh_ref, ot_ref, buf, sem,
                        *, tm, alpha, head_tiles):
    i = pl.program_id(0)
    nt = pl.num_programs(0)
    slot = i % 2

    @pl.when(i == 0)
    def _():
        _gather_rows(pos_ref, 2 * tm, ys_hbm, buf.at[0], sem.at[0])

    @pl.when(i + 1 < nt)
    def _():
        _gather_rows(nxt_ref, 2 * tm, ys_hbm, buf.at[1 - slot], sem.at[1 - slot])

    _wait_rows(2 * tm, ys_hbm, buf.at[slot], sem.at[slot])
    w = w_ref[...]
    y = _layer_norm(alpha * h_ref[...] + (w[:, 0:1] * buf[slot, 0:tm] + w[:, 1:2] * buf[slot, tm:2 * tm]),
                    g_ref[...], b_ref[...])

    @pl.when(i < head_tiles)
    def _():
        oh_ref[...] = y

    @pl.when(i >= head_tiles)
    def _():
        ot_ref[...] = y


def _moe_combine(ys, h, pos, wts, g, b, alpha, tm, head_rows):
    T, D = h.shape
    nt = T // tm
    head_tiles = head_rows // tm
    const = lambda a: pl.BlockSpec(a.shape, lambda i: (0,) * a.ndim)
    return pl.pallas_call(
        functools.partial(_moe_combine_kernel, tm=tm, alpha=alpha, head_tiles=head_tiles),
        grid=(nt,),
        in_specs=[pl.BlockSpec((1, 1, 2 * tm), lambda i: (i, 0, 0), memory_space=pltpu.SMEM),
                  pl.BlockSpec((1, 1, 2 * tm), lambda i: (jnp.minimum(i + 1, nt - 1), 0, 0),
                               memory_space=pltpu.SMEM),
                  pl.BlockSpec(memory_space=pl.ANY),
                  pl.BlockSpec((tm, D), lambda i: (i, 0)),
                  pl.BlockSpec((tm, 2), lambda i: (i, 0)), const(g), const(b)],
        out_specs=list(_two_part_specs(tm, D, head_tiles)),
        out_shape=[jax.ShapeDtypeStruct((head_rows, D), F32), jax.ShapeDtypeStruct((T - head_rows, D), F32)],
        scratch_shapes=[pltpu.VMEM((2, 2 * tm, D), F32), pltpu.SemaphoreType.DMA((2,))],
        compiler_params=_cparams("arbitrary"),
        name="moe_combine_ln",
    )(pos, pos, ys, h, wts, g, b)


def _route(ids, n_experts, tm):
    T = ids.shape[1]
    flat = ids.reshape(-1)
    iota = jnp.arange(2 * T, dtype=jnp.int32)
    sorted_e, order = lax.sort((flat, iota), num_keys=1, is_stable=True)
    _, inverse = lax.sort((order, iota), num_keys=1)
    experts = jnp.arange(n_experts, dtype=jnp.int32)
    counts = jnp.sum((flat[:, None] == experts[None, :]).astype(jnp.int32), axis=0)
    padded = (counts + tm - 1) // tm * tm
    ends = jnp.cumsum(padded)
    shift = (ends - padded) - (jnp.cumsum(counts) - counts)
    NT = (2 * T + n_experts * (tm - 1)) // tm
    tile_start = jnp.arange(NT, dtype=jnp.int32) * tm
    tile_expert = jnp.minimum(jnp.sum((tile_start[:, None] >= ends[None, :]).astype(jnp.int32), axis=1),
                              n_experts - 1)
    tile_valid = (tile_start < ends[-1]).astype(jnp.int32)
    pos = (inverse + shift[flat]).reshape(2, T)
    row = jnp.arange(NT * tm, dtype=jnp.int32)
    src = (order % T)[jnp.clip(row - jnp.repeat(shift[tile_expert], tm), 0, 2 * T - 1)]
    return tile_expert, tile_valid, src.reshape(NT, 1, tm), pos


MLA_PAIR_W = 2 * LANES
QK_NOPE = 64
QK_ROPE = 32


def _mla_proj_kernel(hh_ref, ht_ref, wdn_ref, gq_ref, gkv_ref, wq_ref, wqr_ref, wk_ref, wv_ref, cos_ref, sin_ref,
                     ckv_ref, kr_ref, qcat_ref, kcat_ref, vt_ref, *, q_lora, kv_lora, npairs, scale, head_tiles):
    h = jnp.where(pl.program_id(0) < head_tiles, hh_ref[...], ht_ref[...])
    z = _dot(h.astype(BF16), wdn_ref[...])
    cq = z[:, :q_lora]
    ckv = z[:, q_lora:q_lora + kv_lora]
    o = q_lora + kv_lora
    kr_raw = z[:, o:o + HEAD_W]
    kr_rot = z[:, o + HEAD_W:o + 2 * HEAD_W]
    cq = cq * lax.rsqrt(jnp.mean(cq * cq, axis=-1, keepdims=True) + RMS_EPS) * gq_ref[...]
    ckv = ckv * lax.rsqrt(jnp.mean(ckv * ckv, axis=-1, keepdims=True) + RMS_EPS) * gkv_ref[...]
    ckv_ref[...] = ckv
    cos = cos_ref[...]
    sin = sin_ref[...]
    kr_tile = kr_raw * cos + kr_rot * sin
    kr_ref[...] = kr_tile[:, QK_NOPE:QK_NOPE + QK_ROPE]
    cqb = cq.astype(BF16)
    ckb = ckv.astype(BF16)
    cos2 = jnp.concatenate([cos, cos], axis=1)
    sin2 = jnp.concatenate([sin, sin], axis=1)
    kr2 = jnp.concatenate([kr_tile, kr_tile], axis=1)
    for p in range(npairs):
        lanes = slice(p * MLA_PAIR_W, (p + 1) * MLA_PAIR_W)
        q = _dot(cqb, wq_ref[:, lanes]) * cos2 + _dot(cqb, wqr_ref[:, lanes]) * sin2
        qcat_ref[:, lanes] = (q * scale).astype(BF16)
        kcat_ref[:, lanes] = (_dot(ckb, wk_ref[:, lanes]) + kr2).astype(BF16)
    heads = 2 * npairs
    vt = _dot_nt(wv_ref[...], ckb).reshape(heads, HEAD_DIM, ckb.shape[0])
    vt_ref[0, :, :HEAD_DIM, :] = vt.astype(BF16)
    vt_ref[0, :, HEAD_DIM:, :] = jnp.ones((heads, VT_ROWS - HEAD_DIM, ckb.shape[0]), BF16)


def _mla_proj(h_head, h_tail, wdn, gq, gkv, wq, wqr, wk, wv, cos_t, sin_t, table_block, tm):
    D = h_head.shape[1]
    T = h_head.shape[0] + h_tail.shape[0]
    head_tiles = h_head.shape[0] // tm
    q_lora, kv_lora = gq.shape[1], gkv.shape[1]
    npairs = wq.shape[1] // MLA_PAIR_W
    const = lambda a: pl.BlockSpec(a.shape, lambda i: (0,) * a.ndim)
    row = lambda w_: pl.BlockSpec((tm, w_), lambda i: (i, 0))
    table = pl.BlockSpec((tm, HEAD_W), lambda i: (table_block(i), 0))
    return pl.pallas_call(
        functools.partial(_mla_proj_kernel, q_lora=q_lora, kv_lora=kv_lora, npairs=npairs,
                          scale=(QK_NOPE + QK_ROPE) ** -0.5 * LOG2E, head_tiles=head_tiles),
        grid=(T // tm,),
        in_specs=[*_two_part_specs(tm, D, head_tiles), const(wdn), const(gq), const(gkv), const(wq), const(wqr),
                  const(wk), const(wv), table, table],
        out_specs=[row(kv_lora), row(QK_ROPE), row(wq.shape[1]), row(wk.shape[1]),
                   pl.BlockSpec((1, 2 * npairs, VT_ROWS, tm), lambda i: (i, 0, 0, 0))],
        out_shape=[jax.ShapeDtypeStruct((T, kv_lora), F32), jax.ShapeDtypeStruct((T, QK_ROPE), F32),
                   jax.ShapeDtypeStruct((T, wq.shape[1]), BF16), jax.ShapeDtypeStruct((T, wk.shape[1]), BF16),
                   jax.ShapeDtypeStruct((T // tm, 2 * npairs, VT_ROWS, tm), BF16)],
        compiler_params=_cparams("parallel"),
        name="mla_proj",
    )(h_head, h_tail, wdn, gq, gkv, wq, wqr, wk, wv, cos_t, sin_t)


def _rot_half(w):
    half = w.shape[-1] // 2
    return jnp.concatenate([-w[..., half:], w[..., :half]], axis=-1)


def _prep_mla_weights(w_down, w_uq, w_ukv, heads, q_lora, kv_lora):
    D = w_down.shape[0]
    tail = HEAD_W - QK_NOPE - QK_ROPE
    w_kr = w_down[:, q_lora + kv_lora:]
    slot = lambda w: jnp.concatenate([jnp.zeros((D, QK_NOPE), w.dtype), w, jnp.zeros((D, tail), w.dtype)], axis=1)
    wdn = jnp.concatenate([w_down[:, :q_lora + kv_lora], slot(w_kr), slot(_rot_half(w_kr))], axis=1)
    wq3 = w_uq.reshape(q_lora, heads, QK_NOPE + QK_ROPE)
    nope, ropew = wq3[..., :QK_NOPE], wq3[..., QK_NOPE:]
    zpad = jnp.zeros((q_lora, heads, tail), w_uq.dtype)
    wq = jnp.concatenate([nope, ropew, zpad], axis=-1)
    wqr = jnp.concatenate([jnp.zeros_like(nope), _rot_half(ropew), zpad], axis=-1)
    wkv3 = w_ukv.reshape(kv_lora, heads, QK_NOPE + HEAD_DIM)
    w_uk, w_uv = wkv3[..., :QK_NOPE], wkv3[..., QK_NOPE:]
    wk = jnp.concatenate([w_uk, jnp.zeros((kv_lora, heads, HEAD_W - QK_NOPE), w_ukv.dtype)], axis=-1)
    wv = jnp.transpose(w_uv.reshape(kv_lora, heads * HEAD_DIM))
    b16 = lambda a: a.astype(BF16)
    return (b16(wdn), b16(wq.reshape(q_lora, -1)), b16(wqr.reshape(q_lora, -1)), b16(wk.reshape(kv_lora, -1)),
            b16(wv), b16(jnp.transpose(w_uk, (1, 2, 0))), b16(jnp.transpose(w_uv, (1, 0, 2))))


def _rope_tables(pos):
    half = QK_ROPE // 2
    inv_freq = ROPE_BASE ** (-jnp.arange(half, dtype=F32) / half)
    ang = pos.astype(F32)[:, None] * inv_freq[None, :]
    n = pos.shape[0]
    pad = jnp.zeros((n, HEAD_W - QK_NOPE - QK_ROPE), F32)
    cos = jnp.concatenate([jnp.ones((n, QK_NOPE), F32)] + [jnp.cos(ang)] * 2 + [pad], axis=1)
    sin = jnp.concatenate([jnp.zeros((n, QK_NOPE), F32)] + [jnp.sin(ang)] * 2 + [pad], axis=1)
    return cos, sin


def _mla_sample_queries(qcat, heads):
    DB, n, _ = qcat.shape
    q4 = qcat.reshape(DB, n, heads, HEAD_W)
    rows = lambda a: jnp.transpose(a, (0, 2, 1, 3)).reshape(DB, heads * n, a.shape[-1])
    return rows(q4[..., :QK_NOPE]), rows(q4[..., QK_NOPE:QK_NOPE + QK_ROPE])


def _mla_sample_kernel(qn_ref, qr_ref, wuk_ref, wuv_ref, cc_ref, rc_ref, cn_ref, rn_ref, o_ref,
                       qlat_ref, m_ref, l_ref, acc_ref, *, n, heads):
    kt = pl.program_id(1)

    @pl.when(kt == 0)
    def _():
        for h in range(heads):
            rows = slice(h * n, (h + 1) * n)
            qlat_ref[rows, :] = _dot(qn_ref[0, rows, :], wuk_ref[h]).astype(BF16)
        m_ref[...] = jnp.full(m_ref.shape, NEG_INF, F32)
        l_ref[...] = jnp.zeros_like(l_ref)
        acc_ref[...] = jnp.zeros_like(acc_ref)

    def update(ckv, kr_t):
        s = _dot_nt(qlat_ref[...], ckv) + _dot(qr_ref[0], kr_t)
        m, l, acc = _softmax_step(s[None], ckv, (m_ref[...], l_ref[...], acc_ref[...]))
        m_ref[...] = m
        l_ref[...] = l
        acc_ref[...] = acc

    update(cc_ref[0].astype(BF16), rc_ref[0].astype(BF16))

    @pl.when(kt == pl.num_programs(1) - 1)
    def _():
        update(cn_ref[0], rn_ref[0])
        o_lat = (acc_ref[0] / l_ref[0]).astype(BF16)
        for h in range(heads):
            o_ref[0, :, h * HEAD_DIM:(h + 1) * HEAD_DIM] = _dot(o_lat[h * n:(h + 1) * n], wuv_ref[h]).astype(o_ref.dtype)


def _mla_sample(qn, qr, wuk, wuv, ckv_c, kr_c, ckv_n, kr_n, n, tk):
    DB, R, _ = qn.shape
    heads = R // n
    P, C = ckv_c.shape[1], ckv_c.shape[2]
    const = lambda a: pl.BlockSpec(a.shape, lambda b, j: (0,) * a.ndim)
    per_b = lambda a: pl.BlockSpec((1,) + a.shape[1:], lambda b, j: (b, 0, 0))
    return pl.pallas_call(
        functools.partial(_mla_sample_kernel, n=n, heads=heads),
        grid=(DB, P // tk),
        in_specs=[per_b(qn), per_b(qr), const(wuk), const(wuv),
                  pl.BlockSpec((1, tk, C), lambda b, j: (b, j, 0)),
                  pl.BlockSpec((1, QK_ROPE, tk), lambda b, j: (b, 0, j)),
                  per_b(ckv_n), per_b(kr_n)],
        out_specs=pl.BlockSpec((1, n, heads * HEAD_DIM), lambda b, j: (b, 0, 0)),
        out_shape=jax.ShapeDtypeStruct((DB, n, heads * HEAD_DIM), BF16),
        scratch_shapes=[pltpu.VMEM((R, C), BF16), pltpu.VMEM((1, R, 1), F32), pltpu.VMEM((1, R, 1), F32),
                        pltpu.VMEM((1, R, C), F32)],
        compiler_params=_cparams("parallel", "arbitrary"),
        name="mla_sample",
    )(qn, qr, wuk, wuv, ckv_c, kr_c, ckv_n, kr_n)


def _moe_layer(h, ids, wts, w_gate, w_up, w_down, g, b, alpha, tm, head_rows):
    T, D = h.shape
    n_experts = w_gate.shape[0] * w_gate.shape[1]
    tile_expert, tile_valid, src, pos = _route(ids[:2], n_experts, tm)
    flat3 = lambda w: w.reshape((n_experts,) + w.shape[2:])
    ys = _moe_experts(h, flat3(w_gate), flat3(w_up), flat3(w_down), tile_expert, tile_valid, src, tm)
    pos_t = jnp.transpose(pos.reshape(2, T // tm, tm), (1, 0, 2)).reshape(T // tm, 1, 2 * tm)
    return _moe_combine(ys, h, pos_t, jnp.transpose(wts[:2]), g, b, alpha, tm, head_rows)


TOKEN_TILE = 256
FLASH_Q_TILE = 512
FLASH_KEY_TILE = 256
SB_Q_TILE = 256
SB_KEY_TILE = 128
CACHE_TILE = 1024
SB_SUB_TILE = 256


def kernel(x_prompt, x_sample, cache_fox_k, cache_fox_v, cache_fox_logf, cache_sb_k, cache_sb_v, cache_mla_ckv, cache_mla_krope, ab_w_in, ab_b_forget, ab_w_out, mla_w_down, mla_g_q, mla_g_kv, mla_w_uq, mla_w_ukv, mla_w_out, moe_w_group, moe_b_group, moe_w_router, moe_b_router, moe_w_gate, moe_w_up, moe_w_down, ln_g, ln_b):
    B, S, D = x_prompt.shape
    DB, n, _ = x_sample.shape
    P = cache_fox_k.shape[2]
    TP, TS = B * S, DB * n
    depth = ln_g.shape[0]
    n_groups = moe_w_group.shape[-1]
    assert depth == 2 and ab_w_in.shape[0] == 1 and mla_w_down.shape[0] == 1
    assert S % FLASH_Q_TILE == 0 and S % SB_Q_TILE == 0 and TP % TOKEN_TILE == 0 and TS % TOKEN_TILE == 0 and TOKEN_TILE % n == 0
    assert P % CACHE_TILE == 0 and P % CHUNK == 0 and n == CHUNK
    alpha = (2 * depth) ** 0.25
    tk = CACHE_TILE

    xp, xs = x_prompt.reshape(TP, D), x_sample.reshape(TS, D)
    prompt3 = lambda a: a[:TP].reshape(B, S, -1)
    sample3 = lambda a: a[TP:].reshape(DB, n, -1)

    def ffn(o, w_out, resid, layer):
        wrh, wrl, rb = _prep_router(moe_w_group[layer], moe_b_group[layer], moe_w_router[layer], moe_b_router[layer])
        h, ids, wts = _mix_out(o, w_out.astype(BF16), *resid, ln_g[layer, 0][None], ln_b[layer, 0][None],
                               wrh, wrl, rb, alpha, n_groups, TOKEN_TILE)
        return _moe_layer(h, ids, wts, moe_w_gate[layer], moe_w_up[layer], moe_w_down[layer],
                          ln_g[layer, 1][None], ln_b[layer, 1][None], alpha, TOKEN_TILE, TP)

    fox_heads = ab_b_forget.shape[1]
    hw = (ab_w_in.shape[2] - fox_heads) // 6
    w_ab, b_forget = _prep_ab_weights(ab_w_in[0], ab_b_forget[0])
    w_tok, w_t = _prompt_ab_weights(w_ab, hw, fox_heads)
    (qa_p, ka_p, qb_p, kb_p, lf_p, kat, vat, kbt, vbt, vat16, vbt16) = _ab_proj_prompt(
        xp, w_tok, b_forget, w_t, _query_decay_ones(fox_heads), B, S, fox_heads, SB_KEY_TILE)
    bsw = lambda a: a.reshape(B, S, -1)
    ka_p = _fox_insert_decay(ka_p, lf_p, B, S, fox_heads)
    qa, ka, va, qb, kb, vb, ka16, va16, kb16, vb16, lf = _ab_proj(xs, w_ab, b_forget, hw)
    dbn = lambda a: a.reshape(DB, n, -1)
    lf_s = dbn(lf)[:, :, :fox_heads]
    cq_s, ck_past, ck_new = _fox_sample_cum(lf_s, jnp.transpose(cache_fox_logf[0], (0, 2, 1)))
    cache_t = lambda c: jnp.transpose(c[0], (0, 2, 3, 1))
    hm = lambda a, t=False: _heads_major(dbn(a), fox_heads, t)
    o_fox_s = _fox_sample(hm(qa), hm(ka16, True), hm(va16, True), cache_t(cache_fox_k), cache_t(cache_fox_v),
                          cq_s, ck_past, ck_new, tk)
    o_sb_s = _sb_sample(hm(qb), hm(kb16, True), hm(vb16, True), cache_t(cache_sb_k), cache_t(cache_sb_v),
                        tk, SB_SUB_TILE)
    tokens_major = lambda a: jnp.transpose(a, (0, 2, 1, 3)).reshape(TS, hw)
    o = jnp.pad(jnp.concatenate([tokens_major(o_fox_s), tokens_major(o_sb_s)], axis=-1), ((TP, 0), (0, 0)))
    o = _flash_prompt(bsw(qa_p), bsw(ka_p), vat16, FLASH_Q_TILE, fox_heads, 1, "fox_prompt", o)
    o = _sb_prompt(bsw(qb_p), bsw(kb_p), vbt16, SB_Q_TILE, o, 1)
    xp, xs = ffn(o, ab_w_out[0], (xp, xs), 0)

    q_lora, kv_lora = mla_g_q.shape[1], mla_g_kv.shape[1]
    heads = mla_w_uq.shape[2] // (QK_NOPE + QK_ROPE)
    wdn, wq, wqr, wk, wv, wuk_t, wuv = _prep_mla_weights(mla_w_down[0], mla_w_uq[0], mla_w_ukv[0], heads, q_lora, kv_lora)
    tm = TOKEN_TILE
    pos = jnp.concatenate([jnp.arange(S, dtype=jnp.int32), P + jnp.arange(tm, dtype=jnp.int32) % n])
    cos_t, sin_t = _rope_tables(pos)
    blocks_per_seq, prompt_blocks = S // tm, TP // tm
    table_block = lambda i: jnp.where(i < prompt_blocks, i % blocks_per_seq, blocks_per_seq)
    assert tm == FLASH_KEY_TILE
    ckv, kr, qcat, kcat, vt = _mla_proj(xp, xs, wdn, mla_g_q[0][None], mla_g_kv[0][None], wq, wqr, wk, wv,
                                        cos_t, sin_t, table_block, tm)
    qn, qr = _mla_sample_queries(sample3(qcat), heads)
    o_s = _mla_sample(qn, qr, wuk_t, wuv, cache_mla_ckv[0], jnp.transpose(cache_mla_krope[0], (0, 2, 1)),
                      sample3(ckv).astype(BF16), jnp.transpose(sample3(kr), (0, 2, 1)).astype(BF16), n, tk)
    o = jnp.pad(o_s.reshape(TS, -1), ((TP, 0), (0, 0)))
    o = _flash_prompt(prompt3(qcat), prompt3(kcat), vt, FLASH_Q_TILE, 8, CHUNK, "mla_prompt", o)
    xp, xs = ffn(o, mla_w_out[0], (xp, xs), 1)

    rows_p = lambda a: jnp.transpose(a, (0, 3, 1, 2))[None]
    rows_s = lambda a: a.reshape(1, DB, n, fox_heads, hw // fox_heads)
    pr, sr = slice(0, TP), slice(TP, TP + TS)
    return (xp.reshape(B, S, D), xs.reshape(DB, n, D),
            rows_p(kat), rows_p(vat), lf_p[:, :fox_heads].reshape(1, B, S, fox_heads), rows_p(kbt), rows_p(vbt),
            ckv[pr].reshape(1, B, S, kv_lora), kr[pr].reshape(1, B, S, QK_ROPE),
            rows_s(ka), rows_s(va), lf[:, :fox_heads].reshape(1, DB, n, fox_heads), rows_s(kb), rows_s(vb),
            ckv[sr].reshape(1, DB, n, kv_lora), kr[sr].reshape(1, DB, n, QK_ROPE))
```

```python
import functools

import jax
import jax.numpy as jnp
from jax import lax
from jax.experimental import pallas as pl
from jax.experimental.pallas import tpu as pltpu

F32 = jnp.float32
BF16 = jnp.bfloat16
NEG_INF = -1e30
LOG2E = 1.4426950408889634

LANES = 128
HEAD_DIM = 64
PAIR_W = 2 * HEAD_DIM
CHUNK = 64
LN_EPS = 1e-5
RMS_EPS = 1e-6
ROPE_BASE = 10000.0
VMEM_LIMIT = 56 * 1024 * 1024


def _cparams(*sem):
    return pltpu.CompilerParams(dimension_semantics=sem, vmem_limit_bytes=VMEM_LIMIT)


def _dot(a, b):
    return jnp.dot(a, b, preferred_element_type=F32)


def _dot_nt(a, b):
    return lax.dot_general(a, b, (((1,), (1,)), ((), ())), preferred_element_type=F32)


def _split_bf16(x, parts):
    out = []
    r = x
    for _ in range(parts):
        h = r.astype(BF16)
        out.append(h)
        r = r - h.astype(F32)
    return out


def _split_bf16_trunc(x, parts):
    out = []
    r = x
    for _ in range(parts):
        bits = lax.bitcast_convert_type(r, jnp.uint32) & jnp.uint32(0xFFFF0000)
        h = lax.bitcast_convert_type(bits, F32)
        out.append(h.astype(BF16))
        r = r - h
    return out


def _log_sigmoid(x):
    return jnp.minimum(x, 0.0) - jnp.log(1.0 + jnp.exp(-jnp.abs(x)))


def _pick_tile(n, pref, mult=8):
    t = min(pref, n)
    while n % t or t % mult:
        t -= 1
    return t


def _ab_proj_kernel(x_ref, w_ref, bf_ref, qa_ref, ka_ref, va_ref, qb_ref, kb_ref, vb_ref,
                    ka16_ref, va16_ref, kb16_ref, vb16_ref, lf_ref, *, hw, qscale):
    xb = x_ref[...].astype(BF16)

    def seg(j):
        return _dot(xb, w_ref[:, j * hw:(j + 1) * hw])

    qa_ref[...] = (seg(0) * qscale).astype(BF16)
    z = seg(1)
    ka_ref[...] = z
    ka16_ref[...] = z.astype(BF16)
    z = seg(2)
    va_ref[...] = z
    va16_ref[...] = z.astype(BF16)
    qb_ref[...] = (seg(3) * qscale).astype(BF16)
    z = seg(4)
    kb_ref[...] = z
    kb16_ref[...] = z.astype(BF16)
    z = seg(5)
    vb_ref[...] = z
    vb16_ref[...] = z.astype(BF16)
    f = _dot(xb, w_ref[:, 6 * hw:6 * hw + LANES]) + bf_ref[...]
    lf_ref[...] = _log_sigmoid(f)


def _ab_proj(x, w, bf, hw):
    T, D = x.shape
    tm = _pick_tile(T, 256)
    row = lambda w_: pl.BlockSpec((tm, w_), lambda i: (i, 0))
    f32o = jax.ShapeDtypeStruct((T, hw), F32)
    b16o = jax.ShapeDtypeStruct((T, hw), BF16)
    return pl.pallas_call(
        functools.partial(_ab_proj_kernel, hw=hw, qscale=HEAD_DIM ** -0.5 * LOG2E),
        grid=(T // tm,),
        in_specs=[row(D), pl.BlockSpec(w.shape, lambda i: (0, 0)), pl.BlockSpec(bf.shape, lambda i: (0, 0))],
        out_specs=[row(hw)] * 10 + [row(LANES)],
        out_shape=[b16o, f32o, f32o, b16o, f32o, f32o, b16o, b16o, b16o, b16o,
                   jax.ShapeDtypeStruct((T, LANES), F32)],
        compiler_params=_cparams("parallel"),
        name="ab_proj",
    )(x, w, bf)


def _ab_proj_prompt_kernel(x_ref, w_ref, bf_ref, wt_ref, qone_ref, qa_ref, ka_ref, qb_ref, kb_ref, lf_ref,
                           kat_ref, vat_ref, kbt_ref, vbt_ref, vat16_ref, vbt16_ref, *, hw, heads, qscale, sb_tk):
    xb = x_ref[...].astype(BF16)
    tm = xb.shape[0]
    wide = heads * HEAD_W

    def seg(j):
        return _dot(xb, w_ref[:, j * wide:(j + 1) * wide])

    def seg_t(j):
        return _dot_nt(wt_ref[j * hw:(j + 1) * hw, :], xb).reshape(heads, hw // heads, tm)

    qa_ref[...] = (seg(0) * qscale + qone_ref[...]).astype(BF16)
    ka_ref[...] = seg(1).astype(BF16)
    qb_ref[...] = (seg(2) * qscale).astype(BF16)
    kb_ref[...] = seg(3).astype(BF16)
    lf_ref[...] = _log_sigmoid(_dot(xb, w_ref[:, 4 * wide:4 * wide + LANES]) + bf_ref[...])
    kat_ref[0] = seg_t(0)
    z = seg_t(1)
    vat_ref[0] = z
    vat16_ref[0, :, :HEAD_DIM, :] = z.astype(BF16)
    vat16_ref[0, :, HEAD_DIM:, :] = jnp.ones((heads, VT_ROWS - HEAD_DIM, tm), BF16)
    kbt_ref[0] = seg_t(2)
    z = seg_t(3)
    vbt_ref[0] = z
    for c in range(tm // sb_tk):
        vbt16_ref[c] = z[:, :, c * sb_tk:(c + 1) * sb_tk].astype(BF16)


def _ab_proj_prompt(x, w_tok, bf, w_t, q_ones, B, S, heads, sb_tk):
    TP, D = x.shape
    hw = w_t.shape[0] // 4
    wide = heads * HEAD_W
    tm = FLASH_KEY_TILE
    nj = S // tm
    const = lambda a: pl.BlockSpec(a.shape, lambda b, j: (0,) * a.ndim)
    row = lambda w_: pl.BlockSpec((tm, w_), lambda b, j: (b * nj + j, 0))
    t_spec = pl.BlockSpec((1, heads, hw // heads, tm), lambda b, j: (b, 0, 0, j))
    b16 = jax.ShapeDtypeStruct((TP, wide), BF16)
    t32 = jax.ShapeDtypeStruct((B, heads, hw // heads, S), F32)
    return pl.pallas_call(
        functools.partial(_ab_proj_prompt_kernel, hw=hw, heads=heads, qscale=HEAD_DIM ** -0.5 * LOG2E, sb_tk=sb_tk),
        grid=(B, nj),
        in_specs=[row(D), const(w_tok), const(bf), const(w_t), const(q_ones)],
        out_specs=[row(wide)] * 4 + [row(LANES)] + [t_spec] * 4 + [
            pl.BlockSpec((1, heads, VT_ROWS, tm), lambda b, j: (b * nj + j, 0, 0, 0)),
            pl.BlockSpec((tm // sb_tk, heads, hw // heads, sb_tk), lambda b, j: (b * nj + j, 0, 0, 0))],
        out_shape=[b16] * 4 + [jax.ShapeDtypeStruct((TP, LANES), F32)] + [t32] * 4 + [
            jax.ShapeDtypeStruct((TP // tm, heads, VT_ROWS, tm), BF16),
            jax.ShapeDtypeStruct((TP // sb_tk, heads, hw // heads, sb_tk), BF16)],
        compiler_params=_cparams("parallel", "parallel"),
        name="ab_proj_prompt",
    )(x, w_tok, bf, w_t, q_ones)


def _cumsum_kernel(x_ref, o_ref, carry_ref, *, tl):
    @pl.when(pl.program_id(0) == 0)
    def _():
        carry_ref[...] = jnp.zeros_like(carry_ref)

    x = x_ref[...]
    rows = x.shape[0]
    r = lax.broadcasted_iota(jnp.int32, (tl, tl), 0)
    c = lax.broadcasted_iota(jnp.int32, (tl, tl), 1)
    upper = (r <= c).astype(BF16)
    parts = jnp.concatenate(_split_bf16(x, 4), axis=0)
    y = _dot(parts, upper)
    cum = (y[0:rows] + y[rows:2 * rows]) + (y[2 * rows:3 * rows] + y[3 * rows:]) + carry_ref[:, 0:1]
    o_ref[...] = cum
    carry_ref[...] = jnp.broadcast_to(cum[:, tl - 1:tl], carry_ref.shape)


def _cumsum_rows(x):
    B, H, L = x.shape
    tl = _pick_tile(L, 512, LANES)
    out = pl.pallas_call(
        functools.partial(_cumsum_kernel, tl=tl),
        grid=(L // tl,),
        in_specs=[pl.BlockSpec((B * H, tl), lambda j: (0, j))],
        out_specs=pl.BlockSpec((B * H, tl), lambda j: (0, j)),
        out_shape=jax.ShapeDtypeStruct((B * H, L), F32),
        scratch_shapes=[pltpu.VMEM((B * H, LANES), F32)],
        compiler_params=_cparams("arbitrary"),
        name="cumsum_rows",
    )(x.reshape(B * H, L))
    return out.reshape(B, H, L)


def _stack_pair(qp):
    lo = lax.broadcasted_iota(jnp.int32, qp.shape, 1) < HEAD_DIM
    zero = jnp.zeros_like(qp)
    return jnp.concatenate([jnp.where(lo, qp, zero), jnp.where(lo, zero, qp)], axis=0)


def _unstack_pair(o):
    lo = lax.broadcasted_iota(jnp.int32, o.shape[1:], 1) < HEAD_DIM
    return jnp.where(lo, o[0], o[1])


def _softmax_step(s, vb, carry):
    m, l, acc = carry
    two, tq, tk = s.shape
    m_new = jnp.maximum(m, jnp.max(s, axis=-1, keepdims=True))
    alpha = jnp.exp2(m - m_new)
    p = jnp.exp2(s - m_new)
    l = alpha * l + jnp.sum(p, axis=-1, keepdims=True)
    pv = _dot(p.reshape(two * tq, tk).astype(BF16), vb).reshape(two, tq, vb.shape[-1])
    return m_new, l, alpha * acc + pv


def _log2_sigmoid_pair(z2):
    l1 = jnp.log2(1.0 + jnp.exp2(-jnp.abs(z2)))
    return jnp.minimum(z2, 0.0) - l1, jnp.minimum(-z2, 0.0) - l1


def _softmax_init(tq, width):
    return (jnp.full((2, tq, 1), NEG_INF, F32), jnp.zeros((2, tq, 1), F32), jnp.zeros((2, tq, width), F32))


def _sb_step(z, vb, strict_upper, carry, mask=None):
    run, acc = carry
    log_beta, log_rest = _log2_sigmoid_pair(z)
    if mask is not None:
        log_beta = jnp.where(mask, log_beta, NEG_INF)
        log_rest = jnp.where(mask, log_rest, 0.0)
    hi, lo = _split_bf16(log_rest, 2)
    later = _dot(hi, strict_upper) + _dot(lo, strict_upper)
    a = jnp.exp2(log_beta + later + run)
    acc = acc + _dot(a.astype(BF16), vb)
    run = run + jnp.sum(log_rest, axis=-1, keepdims=True)
    return run, acc


def _strict_upper(tk):
    r = lax.broadcasted_iota(jnp.int32, (tk, tk), 0)
    c = lax.broadcasted_iota(jnp.int32, (tk, tk), 1)
    return (r > c).astype(BF16)


HEAD_W = LANES
VT_ROWS = HEAD_DIM + 16
SB_DEAD_LOG2 = -160.0


def _pipeline_ahead(stage, first, count, cur, nxt):
    if first < count:
        stage(first, cur)
    elif nxt is not None:
        stage(first - count, nxt)


def _flash_prompt_kernel(q_ref, k_ref, vt_ref, o_ref, s_ref, m_ref, acc_ref, *, tq, tk, heads, chunk):
    i = pl.program_id(2)
    sub = tq // tk
    key = lax.broadcasted_iota(jnp.int32, (tk, tq), 0)
    query = lax.broadcasted_iota(jnp.int32, (tk, tq), 1)
    m_ref[...] = jnp.full(m_ref.shape, NEG_INF, F32)
    acc_ref[...] = jnp.zeros_like(acc_ref)

    def scores(h, kt):
        rows = pl.ds(pl.multiple_of(kt * tk, tk), tk)
        lanes = slice(h * HEAD_W, (h + 1) * HEAD_W)
        s_ref[h] = _dot_nt(k_ref[rows, lanes], q_ref[:, lanes])

    def absorb(h, kt, visible):
        s_t = s_ref[h]
        if visible is not None:
            s_t = jnp.where(visible, s_t, NEG_INF)
        m = m_ref[h]
        m_new = jnp.maximum(m, jnp.max(s_t, axis=0, keepdims=True))
        p_t = jnp.exp2(s_t - m_new).astype(BF16)
        pv = _dot(vt_ref[kt, h], p_t)
        acc_ref[h] = jnp.exp2(m - m_new) * acc_ref[h] + pv
        m_ref[h] = m_new

    scores(0, 0)
    scores(1, 0)

    def body(kt, carry):
        for h in range(heads):
            _pipeline_ahead(scores, h + 2, heads, kt, kt + 1)
            absorb(h, kt, None)
        return carry

    lax.fori_loop(0, sub * i, body, 0)
    for s in range(sub):
        kt = sub * i + s
        visible = ((s * tk + key) // chunk) <= (query // chunk)
        for h in range(heads):
            _pipeline_ahead(scores, h + 2, heads, kt, kt + 1 if s + 1 < sub else None)
            absorb(h, kt, visible)
    for j in range(heads // 2):
        a0, a1 = acc_ref[2 * j], acc_ref[2 * j + 1]
        o_t = jnp.concatenate([a0[:HEAD_DIM] / a0[HEAD_DIM:HEAD_DIM + 1],
                               a1[:HEAD_DIM] / a1[HEAD_DIM:HEAD_DIM + 1]], axis=0)
        o_ref[:, j * PAIR_W:(j + 1) * PAIR_W] = jnp.transpose(o_t).astype(o_ref.dtype)


def _flash_prompt(qx, kx, vt4, B, S, tq, heads_per_step, chunk, name):
    W = qx.shape[1]
    nq = S // tq
    hs = heads_per_step
    tk = vt4.shape[3]
    nk = S // tk
    return pl.pallas_call(
        functools.partial(_flash_prompt_kernel, tq=tq, tk=tk, heads=hs, chunk=chunk),
        grid=(B, W // (HEAD_W * hs), S // tq),
        in_specs=[pl.BlockSpec((tq, HEAD_W * hs), lambda b, g, i: (b * nq + i, g)),
                  pl.BlockSpec((S, HEAD_W * hs), lambda b, g, i: (b, g)),
                  pl.BlockSpec((nk, hs, VT_ROWS, tk), lambda b, g, i: (b, g, 0, 0))],
        out_specs=pl.BlockSpec((tq, HEAD_DIM * hs), lambda b, g, i: (b * nq + i, g)),
        out_shape=jax.ShapeDtypeStruct((B * S, W // HEAD_W * HEAD_DIM), BF16),
        scratch_shapes=[pltpu.VMEM((hs, tk, tq), F32), pltpu.VMEM((hs, 1, tq), F32),
                        pltpu.VMEM((hs, VT_ROWS, tq), F32)],
        compiler_params=_cparams("parallel", "parallel", "arbitrary"),
        name=name,
    )(qx, kx, vt4)


def _sb_prompt_kernel(q_ref, k_ref, vt_ref, o_ref, z_ref, lw_ref, tot_ref, run_ref, acc_ref, *, tq, tk, heads):
    i = pl.program_id(1)
    sub = tq // tk
    key = lax.broadcasted_iota(jnp.int32, (tk, tq), 0)
    query = lax.broadcasted_iota(jnp.int32, (tk, tq), 1)
    r = lax.broadcasted_iota(jnp.int32, (tk, tk), 0)
    c = lax.broadcasted_iota(jnp.int32, (tk, tk), 1)
    after = (c > r).astype(BF16)
    run_ref[...] = jnp.zeros_like(run_ref)
    acc_ref[...] = jnp.zeros_like(acc_ref)

    def logits(h, kt):
        rows = pl.ds(pl.multiple_of(kt * tk, tk), tk)
        lanes = slice(h * HEAD_W, (h + 1) * HEAD_W)
        z_ref[h] = _dot_nt(k_ref[rows, lanes], q_ref[:, lanes])

    def log_weights(before, h, kt):
        log_beta, log_rest = _log2_sigmoid_pair(z_ref[h])
        if before is not None:
            log_beta = jnp.where(before, log_beta, NEG_INF)
            log_rest = jnp.where(before, log_rest, 0.0)
        hi, lo = _split_bf16(log_rest, 2)
        later = _dot(after, hi) + _dot(after, lo)
        lw_ref[h] = log_beta + later
        tot_ref[h] = later[0:1] + log_rest[0:1]

    def accumulate(h, kt):
        run = run_ref[h]
        a_t = jnp.exp2(lw_ref[h] + run).astype(BF16)
        acc_ref[h] += _dot(vt_ref[kt, h], a_t)
        run_ref[h] = run + tot_ref[h]

    def step(kt, masked, nxt, nxt_masked):
        for h in range(heads):
            _pipeline_ahead(logits, h + 2, heads, kt, nxt)
            if h + 1 < heads:
                log_weights(masked, h + 1, kt)
            elif nxt is not None:
                log_weights(nxt_masked, 0, nxt)
            accumulate(h, kt)

    unmasked = sub * i
    first = unmasked + sub - 1
    masks = [((sub - 1 - s) * tk + key) < query for s in range(sub)]
    logits(0, first)
    logits(1, first)
    log_weights(masks[0], 0, first)
    for s in range(sub):
        kt = first - s
        if s + 1 < sub:
            step(kt, masks[s], kt - 1, masks[s + 1])
        else:
            step(kt, masks[s], jnp.maximum(kt - 1, 0), None)

    def alive():
        return (jnp.max(run_ref[...]) > SB_DEAD_LOG2).astype(jnp.int32)

    def body(carry):
        kt, _ = carry
        step(kt, None, jnp.maximum(kt - 1, 0), None)
        return kt - 1, alive()

    lax.while_loop(lambda c: jnp.logical_and(c[0] >= 0, c[1] > 0), body, (unmasked - 1, alive()))

    for j in range(heads // 2):
        o_t = jnp.concatenate([acc_ref[2 * j], acc_ref[2 * j + 1]], axis=0)
        o_ref[:, j * PAIR_W:(j + 1) * PAIR_W] = jnp.transpose(o_t).astype(o_ref.dtype)


def _sb_prompt(qx, kx, vt4, B, S, tq):
    W = qx.shape[1]
    heads = W // HEAD_W
    tk = vt4.shape[3]
    nq = S // tq
    return pl.pallas_call(
        functools.partial(_sb_prompt_kernel, tq=tq, tk=tk, heads=heads),
        grid=(B, nq),
        in_specs=[pl.BlockSpec((tq, W), lambda b, i: (b * nq + i, 0)),
                  pl.BlockSpec((S, W), lambda b, i: (b, 0)),
                  pl.BlockSpec((S // tk, heads, HEAD_DIM, tk), lambda b, i: (b, 0, 0, 0))],
        out_specs=pl.BlockSpec((tq, heads * HEAD_DIM), lambda b, i: (b * nq + i, 0)),
        out_shape=jax.ShapeDtypeStruct((B * S, heads * HEAD_DIM), BF16),
        scratch_shapes=[pltpu.VMEM((heads, tk, tq), F32), pltpu.VMEM((heads, tk, tq), F32),
                        pltpu.VMEM((heads, 1, tq), F32), pltpu.VMEM((heads, 1, tq), F32),
                        pltpu.VMEM((heads, HEAD_DIM, tq), F32)],
        compiler_params=_cparams("parallel", "arbitrary"),
        name="sb_prompt",
    )(qx, kx, vt4)


def _prep_ab_weights(w_in, b_f):
    D = w_in.shape[0]
    H = b_f.shape[0]
    hw = (w_in.shape[1] - H) // 6
    main = jnp.concatenate([w_in[:, :3 * hw], w_in[:, 3 * hw + H:]], axis=1)
    wf = jnp.zeros((D, LANES), w_in.dtype).at[:, :H].set(w_in[:, 3 * hw:3 * hw + H])
    bf = jnp.zeros((1, LANES), F32).at[0, :H].set(b_f.astype(F32))
    return jnp.concatenate([main, wf], axis=1).astype(BF16), bf


def _prompt_ab_weights(w_ab, hw, heads):
    D = w_ab.shape[0]
    seg = lambda j: w_ab[:, j * hw:(j + 1) * hw]
    pad = lambda w: jnp.concatenate([w.reshape(D, heads, hw // heads),
                                     jnp.zeros((D, heads, HEAD_W - hw // heads), w.dtype)], axis=-1).reshape(D, -1)
    w_tok = jnp.concatenate([pad(seg(0)), pad(seg(1)), pad(seg(3)), pad(seg(4)), w_ab[:, 6 * hw:]], axis=1)
    w_t = jnp.transpose(jnp.concatenate([seg(1), seg(2), seg(4), seg(5)], axis=1))
    return w_tok, w_t


DECAY_TERMS = 3


def _insert_decay_kernel(k_ref, c_ref, sel_ref, o_ref):
    terms = sum(_dot(part, sel_ref[j]) for j, part in enumerate(_split_bf16(c_ref[...], DECAY_TERMS)))
    o_ref[...] = (k_ref[...].astype(F32) + terms).astype(BF16)


def _fox_insert_decay(kx, lf, B, S, H):
    lt = jnp.transpose(lf[:, :H].reshape(B, S, H), (0, 2, 1))
    cum = jnp.transpose(_cumsum_rows(lt), (0, 2, 1)).reshape(B * S, H) * (-LOG2E)
    c = jnp.pad(cum, ((0, 0), (0, LANES - H)))
    src = jnp.arange(LANES)[None, :, None]
    dst = jnp.arange(H * HEAD_W)[None, None, :]
    term = jnp.arange(DECAY_TERMS)[:, None, None]
    sel = jnp.logical_and(src < H, dst == src * HEAD_W + HEAD_DIM + term).astype(BF16)
    tm = _pick_tile(B * S, 512)
    return pl.pallas_call(
        _insert_decay_kernel,
        grid=(B * S // tm,),
        in_specs=[pl.BlockSpec((tm, H * HEAD_W), lambda i: (i, 0)), pl.BlockSpec((tm, LANES), lambda i: (i, 0)),
                  pl.BlockSpec(sel.shape, lambda i: (0, 0, 0))],
        out_specs=pl.BlockSpec((tm, H * HEAD_W), lambda i: (i, 0)),
        out_shape=jax.ShapeDtypeStruct(kx.shape, BF16),
        input_output_aliases={0: 0},
        compiler_params=_cparams("parallel"),
        name="fox_insert_decay",
    )(kx, c, sel)


def _query_decay_ones(H):
    lane = jnp.arange(H * HEAD_W) % HEAD_W
    return jnp.logical_and(lane >= HEAD_DIM, lane < HEAD_DIM + DECAY_TERMS).astype(F32)[None]


def _fox_sample_kernel(q_ref, knt_ref, vnt_ref, kct_ref, vct_ref, cq_ref, ckp_ref, ckn_ref, o_ref,
                       s_ref, m_ref, l_ref, acc_ref, *, n, heads):
    kt = pl.program_id(1)

    @pl.when(kt == 0)
    def _():
        m_ref[...] = jnp.full(m_ref.shape, NEG_INF, F32)
        l_ref[...] = jnp.zeros_like(l_ref)
        acc_ref[...] = jnp.zeros_like(acc_ref)

    def absorb(h, s, v_t, ck, mask):
        s = s + (cq_ref[0, h] - ck)
        if mask is not None:
            s = jnp.where(mask, s, NEG_INF)
        m = m_ref[h]
        m_new = jnp.maximum(m, jnp.max(s, axis=-1, keepdims=True))
        alpha = jnp.exp2(m - m_new)
        p = jnp.exp2(s - m_new)
        l_ref[h] = alpha * l_ref[h] + jnp.sum(p, axis=-1, keepdims=True)
        acc_ref[h] = alpha * acc_ref[h] + _dot_nt(p.astype(BF16), v_t)
        m_ref[h] = m_new

    def scores(h, _=None):
        s_ref[h] = _dot(q_ref[0, h], kct_ref[0, h].astype(BF16))

    scores(0)
    scores(1)
    for h in range(heads):
        _pipeline_ahead(scores, h + 2, heads, None, None)
        absorb(h, s_ref[h], vct_ref[0, h].astype(BF16), ckp_ref[0, h:h + 1, :], None)

    @pl.when(kt == pl.num_programs(1) - 1)
    def _():
        row = lax.broadcasted_iota(jnp.int32, (n, n), 0)
        col = lax.broadcasted_iota(jnp.int32, (n, n), 1)
        for h in range(heads):
            absorb(h, _dot(q_ref[0, h], knt_ref[0, h]), vnt_ref[0, h], ckn_ref[0, h:h + 1, :], col <= row)
            o_ref[0, h] = (acc_ref[h] / l_ref[h]).astype(o_ref.dtype)


def _fox_sample(q, knt, vnt, kct, vct, cq, ckp, ckn, tk):
    DB, H, n, dh = q.shape
    P = kct.shape[-1]
    per_b = lambda a: pl.BlockSpec((1,) + a.shape[1:], lambda b, j: (b,) + (0,) * (a.ndim - 1))
    cache = pl.BlockSpec((1, H, dh, tk), lambda b, j: (b, 0, 0, j))
    return pl.pallas_call(
        functools.partial(_fox_sample_kernel, n=n, heads=H),
        grid=(DB, P // tk),
        in_specs=[per_b(q), per_b(knt), per_b(vnt), cache, cache, per_b(cq),
                  pl.BlockSpec((1, H, tk), lambda b, j: (b, 0, j)), per_b(ckn)],
        out_specs=per_b(q),
        out_shape=jax.ShapeDtypeStruct(q.shape, BF16),
        scratch_shapes=[pltpu.VMEM((H, n, tk), F32), pltpu.VMEM((H, n, 1), F32), pltpu.VMEM((H, n, 1), F32),
                        pltpu.VMEM((H, n, dh), F32)],
        compiler_params=_cparams("parallel", "arbitrary"),
        name="fox_sample",
    )(q, knt, vnt, kct, vct, cq, ckp, ckn)


def _sb_sample_kernel(q_ref, knt_ref, vnt_ref, kct_ref, vct_ref, o_ref, z_ref, run_ref, acc_ref, *, n, heads, sub):
    kt = pl.program_id(1)
    upper = _strict_upper(sub)

    def absorb(h, z, v_t, upper_m, width, mask=None):
        log_beta, log_rest = _log2_sigmoid_pair(z)
        if mask is not None:
            log_beta = jnp.where(mask, log_beta, NEG_INF)
            log_rest = jnp.where(mask, log_rest, 0.0)
        hi, lo = _split_bf16(log_rest, 2)
        run = run_ref[h]
        parts = []
        for c in reversed(range(z.shape[1] // width)):
            keys = slice(c * width, (c + 1) * width)
            later = _dot(hi[:, keys], upper_m) + _dot(lo[:, keys], upper_m)
            parts.append(jnp.exp2(log_beta[:, keys] + later + run).astype(BF16))
            run = run + jnp.sum(log_rest[:, keys], axis=-1, keepdims=True)
        a = parts[0] if len(parts) == 1 else jnp.concatenate(parts[::-1], axis=1)
        acc_ref[h] += _dot_nt(a, v_t)
        run_ref[h] = run

    def logits(h, _=None):
        z_ref[h] = _dot(q_ref[0, h], kct_ref[0, h].astype(BF16))

    @pl.when(kt == 0)
    def _():
        row = lax.broadcasted_iota(jnp.int32, (n, n), 0)
        col = lax.broadcasted_iota(jnp.int32, (n, n), 1)
        upper_n = _strict_upper(n)
        run_ref[...] = jnp.zeros_like(run_ref)
        acc_ref[...] = jnp.zeros_like(acc_ref)
        for h in range(heads):
            absorb(h, _dot(q_ref[0, h], knt_ref[0, h]), vnt_ref[0, h], upper_n, n, col < row)

    logits(0)
    logits(1)
    for h in range(heads):
        _pipeline_ahead(logits, h + 2, heads, None, None)
        absorb(h, z_ref[h], vct_ref[0, h].astype(BF16), upper, sub)

    @pl.when(kt == pl.num_programs(1) - 1)
    def _():
        o_ref[0] = acc_ref[...].astype(o_ref.dtype)


def _sb_sample(q, knt, vnt, kct, vct, tk, sub):
    DB, H, n, dh = q.shape
    P = kct.shape[-1]
    nk = P // tk
    per_b = lambda a: pl.BlockSpec((1,) + a.shape[1:], lambda b, j: (b,) + (0,) * (a.ndim - 1))
    cache = pl.BlockSpec((1, H, dh, tk), lambda b, j: (b, 0, 0, nk - 1 - j))
    return pl.pallas_call(
        functools.partial(_sb_sample_kernel, n=n, heads=H, sub=sub),
        grid=(DB, nk),
        in_specs=[per_b(q), per_b(knt), per_b(vnt), cache, cache],
        out_specs=per_b(q),
        out_shape=jax.ShapeDtypeStruct(q.shape, BF16),
        scratch_shapes=[pltpu.VMEM((H, n, tk), F32), pltpu.VMEM((H, n, 1), F32), pltpu.VMEM((H, n, dh), F32)],
        compiler_params=_cparams("parallel", "arbitrary"),
        name="sb_sample",
    )(q, knt, vnt, kct, vct)


def _fox_sample_cum(lf_new, lf_past_t):
    DB, H, P = lf_past_t.shape
    n = lf_new.shape[1]
    L = -(-(P + n) // LANES) * LANES
    both = jnp.concatenate([lf_past_t.astype(F32), jnp.transpose(lf_new, (0, 2, 1)),
                            jnp.zeros((DB, H, L - P - n), F32)], axis=2)
    cum_t = _cumsum_rows(both) * LOG2E
    ckn = cum_t[:, :, P:P + n]
    return ckn[..., None], cum_t[:, :, :P], ckn


def _heads_major(a, heads, transpose_rows):
    DB, n, _ = a.shape
    a4 = a.reshape(DB, n, heads, -1)
    return jnp.transpose(a4, (0, 2, 3, 1) if transpose_rows else (0, 2, 1, 3))


ROUTER_ROWS = 48


def _layer_norm(y, g, b):
    mu = jnp.mean(y, axis=-1, keepdims=True)
    yc = y - mu
    var = jnp.mean(yc * yc, axis=-1, keepdims=True)
    return yc * lax.rsqrt(var + LN_EPS) * g + b


def _first_argmax(v, ridx):
    vmax = jnp.max(v, axis=0, keepdims=True)
    idx = jnp.min(jnp.where(v == vmax, ridx, v.shape[0]), axis=0, keepdims=True)
    return vmax, idx


def _two_part_specs(tm, width, head_tiles):
    return (pl.BlockSpec((tm, width), lambda i: (jnp.minimum(i, head_tiles - 1), 0)),
            pl.BlockSpec((tm, width), lambda i: (jnp.maximum(i - head_tiles, 0), 0)))


def _mix_out_kernel(*refs, alpha, n_groups, head_tiles, n_parts):
    o_parts, ot_ref = refs[:n_parts], refs[n_parts]
    w_ref, xh_ref, xt_ref, g_ref, b_ref, wrh_ref, wrl_ref, rb_ref, h_ref, ids_ref, wts_ref = refs[n_parts + 1:]
    in_head = pl.program_id(0) < head_tiles
    x = jnp.where(in_head, xh_ref[...], xt_ref[...])
    o_head = o_parts[0][...] if n_parts == 1 else jnp.concatenate([r[...] for r in o_parts], axis=1)
    o = jnp.where(in_head, o_head, ot_ref[...])
    h = _layer_norm(alpha * x + _dot(o, w_ref[...]), g_ref[...], b_ref[...])
    h_ref[...] = h
    hh, hl = _split_bf16(h, 2)
    wrh = wrh_ref[...]
    low = _dot_nt(wrl_ref[...], hh)
    lg = _dot_nt(wrh, hh) + (_dot_nt(wrh, hl) + (low[:ROUTER_ROWS] + low[ROUTER_ROWS:])) + rb_ref[...]
    tm = lg.shape[1]
    ridx = lax.broadcasted_iota(jnp.int32, (8, tm), 0)
    g = jnp.where(ridx < n_groups, lg[0:8], NEG_INF)
    gmax, gidx = _first_argmax(g, ridx)
    gate = 1.0 / jnp.sum(jnp.exp(g - gmax), axis=0, keepdims=True)
    esel = lg[8:16]
    for gg in range(1, n_groups):
        esel = jnp.where(gidx == gg, lg[8 + 8 * gg:16 + 8 * gg], esel)
    v1, i1 = _first_argmax(esel, ridx)
    v2, i2 = _first_argmax(jnp.where(ridx == i1, NEG_INF, esel), ridx)
    t = jnp.exp(v2 - v1)
    w1 = 1.0 / (1.0 + t)
    ids_ref[...] = jnp.where(ridx == 0, gidx * 8 + i1, jnp.where(ridx == 1, gidx * 8 + i2, 0))
    wts_ref[...] = jnp.where(ridx == 0, gate * w1, jnp.where(ridx == 1, gate * (t * w1), 0.0))


def _mix_out(o_head_parts, o_tail, w, x_head, x_tail, g, b, wrh, wrl, rb, alpha, n_groups, tm):
    D = x_head.shape[1]
    T = x_head.shape[0] + x_tail.shape[0]
    head_tiles = x_head.shape[0] // tm
    const = lambda a: pl.BlockSpec(a.shape, lambda i: (0,) * a.ndim)
    rb_t = jnp.broadcast_to(rb, (ROUTER_ROWS, tm))
    head_spec = lambda width: _two_part_specs(tm, width, head_tiles)[0]
    return pl.pallas_call(
        functools.partial(_mix_out_kernel, alpha=alpha, n_groups=n_groups, head_tiles=head_tiles,
                          n_parts=len(o_head_parts)),
        grid=(T // tm,),
        in_specs=[*[head_spec(p.shape[1]) for p in o_head_parts], _two_part_specs(tm, o_tail.shape[1], head_tiles)[1],
                  const(w), *_two_part_specs(tm, D, head_tiles), const(g), const(b), const(wrh), const(wrl),
                  const(rb_t)],
        out_specs=[pl.BlockSpec((tm, D), lambda i: (i, 0)), pl.BlockSpec((8, tm), lambda i: (0, i)),
                   pl.BlockSpec((8, tm), lambda i: (0, i))],
        out_shape=[jax.ShapeDtypeStruct((T, D), F32), jax.ShapeDtypeStruct((8, T), jnp.int32),
                   jax.ShapeDtypeStruct((8, T), F32)],
        compiler_params=_cparams("parallel"),
        name="mix_out_ln_router",
    )(*o_head_parts, o_tail, w, x_head, x_tail, g, b, wrh, wrl, rb_t)


def _prep_router(w_group, b_group, w_router, b_router):
    D, G = w_group.shape
    E = w_router.shape[-1]
    wr = jnp.zeros((ROUTER_ROWS, D), F32)
    wr = wr.at[:G].set(w_group.T.astype(F32))
    wr = wr.at[8:8 + G * E].set(jnp.transpose(w_router, (0, 2, 1)).reshape(G * E, D).astype(F32))
    rb = jnp.zeros((ROUTER_ROWS, 1), F32)
    rb = rb.at[:G, 0].set(b_group.astype(F32)).at[8:8 + G * E, 0].set(b_router.reshape(-1).astype(F32))
    hi, mid, lo = _split_bf16_trunc(wr, 3)
    return hi, jnp.concatenate([mid, lo], axis=0), rb


def _gather_rows(idx_ref, n, src_hbm, dst, sem):
    for r in range(n):
        pltpu.make_async_copy(src_hbm.at[pl.ds(idx_ref[0, 0, r], 1)], dst.at[pl.ds(r, 1)], sem).start()


def _wait_rows(n, src_hbm, dst, sem):
    pltpu.make_async_copy(src_hbm.at[pl.ds(0, n)], dst, sem).wait()


def _moe_experts_kernel(te_ref, tv_ref, src_ref, nxt_ref, x_hbm, wg_ref, wu_ref, wd_ref, y_ref, xbuf, sem, *, tm):
    i = pl.program_id(0)
    nt = pl.num_programs(0)
    slot = i % 2

    @pl.when(jnp.logical_and(i == 0, tv_ref[0] > 0))
    def _():
        _gather_rows(src_ref, tm, x_hbm, xbuf.at[0], sem.at[0])

    @pl.when(jnp.logical_and(i + 1 < nt, tv_ref[jnp.minimum(i + 1, nt - 1)] > 0))
    def _():
        _gather_rows(nxt_ref, tm, x_hbm, xbuf.at[1 - slot], sem.at[1 - slot])

    @pl.when(tv_ref[i] > 0)
    def _():
        _wait_rows(tm, x_hbm, xbuf.at[slot], sem.at[slot])
        xb = xbuf[slot].astype(BF16)
        a = _dot(xb, wg_ref[0].astype(BF16))
        u = _dot(xb, wu_ref[0].astype(BF16))
        hid = (a / (1.0 + jnp.exp(-a))) * u
        y_ref[...] = _dot(hid.astype(BF16), wd_ref[0].astype(BF16))

    @pl.when(tv_ref[i] == 0)
    def _():
        y_ref[...] = jnp.zeros_like(y_ref)


def _moe_experts(x, w_gate, w_up, w_down, tile_expert, tile_valid, src, tm):
    T, D = x.shape
    F = w_gate.shape[-1]
    NT = tile_expert.shape[0]
    grid_spec = pltpu.PrefetchScalarGridSpec(
        num_scalar_prefetch=2,
        grid=(NT,),
        in_specs=[pl.BlockSpec((1, 1, tm), lambda i, te, tv: (i, 0, 0), memory_space=pltpu.SMEM),
                  pl.BlockSpec((1, 1, tm), lambda i, te, tv: (jnp.minimum(i + 1, NT - 1), 0, 0),
                               memory_space=pltpu.SMEM),
                  pl.BlockSpec(memory_space=pl.ANY),
                  pl.BlockSpec((1, D, F), lambda i, te, tv: (te[i], 0, 0)),
                  pl.BlockSpec((1, D, F), lambda i, te, tv: (te[i], 0, 0)),
                  pl.BlockSpec((1, F, D), lambda i, te, tv: (te[i], 0, 0))],
        out_specs=pl.BlockSpec((tm, D), lambda i, te, tv: (i, 0)),
        scratch_shapes=[pltpu.VMEM((2, tm, D), F32), pltpu.SemaphoreType.DMA((2,))],
    )
    return pl.pallas_call(
        functools.partial(_moe_experts_kernel, tm=tm),
        grid_spec=grid_spec,
        out_shape=jax.ShapeDtypeStruct((NT * tm, D), F32),
        compiler_params=_cparams("arbitrary"),
        name="moe_experts",
    )(tile_expert, tile_valid, src, src, x, w_gate, w_up, w_down)


def _moe_combine_kernel(pos_ref, nxt_ref, ys_hbm, h_ref, w_ref, g_ref, b_ref, oh_ref, ot_ref, buf, sem,
                        *, tm, alpha, head_tiles):
    i = pl.program_id(0)
    nt = pl.num_programs(0)
    slot = i % 2

    @pl.when(i == 0)
    def _():
        _gather_rows(pos_ref, 2 * tm, ys_hbm, buf.at[0], sem.at[0])

    @pl.when(i + 1 < nt)
    def _():
        _gather_rows(nxt_ref, 2 * tm, ys_hbm, buf.at[1 - slot], sem.at[1 - slot])

    _wait_rows(2 * tm, ys_hbm, buf.at[slot], sem.at[slot])
    w = w_ref[...]
    y = _layer_norm(alpha * h_ref[...] + (w[:, 0:1] * buf[slot, 0:tm] + w[:, 1:2] * buf[slot, tm:2 * tm]),
                    g_ref[...], b_ref[...])

    @pl.when(i < head_tiles)
    def _():
        oh_ref[...] = y

    @pl.when(i >= head_tiles)
    def _():
        ot_ref[...] = y


def _moe_combine(ys, h, pos, wts, g, b, alpha, tm, head_rows):
    T, D = h.shape
    nt = T // tm
    head_tiles = head_rows // tm
    const = lambda a: pl.BlockSpec(a.shape, lambda i: (0,) * a.ndim)
    return pl.pallas_call(
        functools.partial(_moe_combine_kernel, tm=tm, alpha=alpha, head_tiles=head_tiles),
        grid=(nt,),
        in_specs=[pl.BlockSpec((1, 1, 2 * tm), lambda i: (i, 0, 0), memory_space=pltpu.SMEM),
                  pl.BlockSpec((1, 1, 2 * tm), lambda i: (jnp.minimum(i + 1, nt - 1), 0, 0),
                               memory_space=pltpu.SMEM),
                  pl.BlockSpec(memory_space=pl.ANY),
                  pl.BlockSpec((tm, D), lambda i: (i, 0)),
                  pl.BlockSpec((tm, 2), lambda i: (i, 0)), const(g), const(b)],
        out_specs=list(_two_part_specs(tm, D, head_tiles)),
        out_shape=[jax.ShapeDtypeStruct((head_rows, D), F32), jax.ShapeDtypeStruct((T - head_rows, D), F32)],
        scratch_shapes=[pltpu.VMEM((2, 2 * tm, D), F32), pltpu.SemaphoreType.DMA((2,))],
        compiler_params=_cparams("arbitrary"),
        name="moe_combine_ln",
    )(pos, pos, ys, h, wts, g, b)


def _route(ids, n_experts, tm):
    T = ids.shape[1]
    flat = ids.reshape(-1)
    iota = jnp.arange(2 * T, dtype=jnp.int32)
    sorted_e, order = lax.sort((flat, iota), num_keys=1, is_stable=True)
    _, inverse = lax.sort((order, iota), num_keys=1)
    experts = jnp.arange(n_experts, dtype=jnp.int32)
    counts = jnp.sum((flat[:, None] == experts[None, :]).astype(jnp.int32), axis=0)
    padded = (counts + tm - 1) // tm * tm
    ends = jnp.cumsum(padded)
    shift = (ends - padded) - (jnp.cumsum(counts) - counts)
    NT = (2 * T + n_experts * (tm - 1)) // tm
    tile_start = jnp.arange(NT, dtype=jnp.int32) * tm
    tile_expert = jnp.minimum(jnp.sum((tile_start[:, None] >= ends[None, :]).astype(jnp.int32), axis=1),
                              n_experts - 1)
    tile_valid = (tile_start < ends[-1]).astype(jnp.int32)
    pos = (inverse + shift[flat]).reshape(2, T)
    row = jnp.arange(NT * tm, dtype=jnp.int32)
    src = (order % T)[jnp.clip(row - jnp.repeat(shift[tile_expert], tm), 0, 2 * T - 1)]
    return tile_expert, tile_valid, src.reshape(NT, 1, tm), pos


MLA_PAIR_W = 2 * LANES
MLA_SAMPLE_CHUNK = 256
QK_NOPE = 64
QK_ROPE = 32


def _mla_proj_kernel(hh_ref, ht_ref, wdn_ref, gq_ref, gkv_ref, wq_ref, wqr_ref, wk_ref, wv_ref, cos_ref, sin_ref,
                     ckv_ref, kr_ref, qcat_ref, kcat_ref, vt_ref, *, q_lora, kv_lora, npairs, scale, head_tiles):
    h = jnp.where(pl.program_id(0) < head_tiles, hh_ref[...], ht_ref[...])
    z = _dot(h.astype(BF16), wdn_ref[...])
    cq = z[:, :q_lora]
    ckv = z[:, q_lora:q_lora + kv_lora]
    o = q_lora + kv_lora
    kr_raw = z[:, o:o + HEAD_W]
    kr_rot = z[:, o + HEAD_W:o + 2 * HEAD_W]
    cq = cq * lax.rsqrt(jnp.mean(cq * cq, axis=-1, keepdims=True) + RMS_EPS) * gq_ref[...]
    ckv = ckv * lax.rsqrt(jnp.mean(ckv * ckv, axis=-1, keepdims=True) + RMS_EPS) * gkv_ref[...]
    ckv_ref[...] = ckv
    cos = cos_ref[...]
    sin = sin_ref[...]
    kr_tile = kr_raw * cos + kr_rot * sin
    kr_ref[...] = kr_tile[:, QK_NOPE:QK_NOPE + QK_ROPE]
    cqb = cq.astype(BF16)
    ckb = ckv.astype(BF16)
    cos2 = jnp.concatenate([cos, cos], axis=1)
    sin2 = jnp.concatenate([sin, sin], axis=1)
    kr2 = jnp.concatenate([kr_tile, kr_tile], axis=1)
    for p in range(npairs):
        lanes = slice(p * MLA_PAIR_W, (p + 1) * MLA_PAIR_W)
        q = _dot(cqb, wq_ref[:, lanes]) * cos2 + _dot(cqb, wqr_ref[:, lanes]) * sin2
        qcat_ref[:, lanes] = (q * scale).astype(BF16)
        kcat_ref[:, lanes] = (_dot(ckb, wk_ref[:, lanes]) + kr2).astype(BF16)
    heads = 2 * npairs
    vt = _dot_nt(wv_ref[...], ckb).reshape(heads, HEAD_DIM, ckb.shape[0])
    vt_ref[0, :, :HEAD_DIM, :] = vt.astype(BF16)
    vt_ref[0, :, HEAD_DIM:, :] = jnp.ones((heads, VT_ROWS - HEAD_DIM, ckb.shape[0]), BF16)


def _mla_proj(h_head, h_tail, wdn, gq, gkv, wq, wqr, wk, wv, cos_t, sin_t, table_block, tm):
    D = h_head.shape[1]
    T = h_head.shape[0] + h_tail.shape[0]
    head_tiles = h_head.shape[0] // tm
    q_lora, kv_lora = gq.shape[1], gkv.shape[1]
    npairs = wq.shape[1] // MLA_PAIR_W
    const = lambda a: pl.BlockSpec(a.shape, lambda i: (0,) * a.ndim)
    row = lambda w_: pl.BlockSpec((tm, w_), lambda i: (i, 0))
    table = pl.BlockSpec((tm, HEAD_W), lambda i: (table_block(i), 0))
    return pl.pallas_call(
        functools.partial(_mla_proj_kernel, q_lora=q_lora, kv_lora=kv_lora, npairs=npairs,
                          scale=(QK_NOPE + QK_ROPE) ** -0.5 * LOG2E, head_tiles=head_tiles),
        grid=(T // tm,),
        in_specs=[*_two_part_specs(tm, D, head_tiles), const(wdn), const(gq), const(gkv), const(wq), const(wqr),
                  const(wk), const(wv), table, table],
        out_specs=[row(kv_lora), row(QK_ROPE), row(wq.shape[1]), row(wk.shape[1]),
                   pl.BlockSpec((1, 2 * npairs, VT_ROWS, tm), lambda i: (i, 0, 0, 0))],
        out_shape=[jax.ShapeDtypeStruct((T, kv_lora), F32), jax.ShapeDtypeStruct((T, QK_ROPE), F32),
                   jax.ShapeDtypeStruct((T, wq.shape[1]), BF16), jax.ShapeDtypeStruct((T, wk.shape[1]), BF16),
                   jax.ShapeDtypeStruct((T // tm, 2 * npairs, VT_ROWS, tm), BF16)],
        compiler_params=_cparams("parallel"),
        name="mla_proj",
    )(h_head, h_tail, wdn, gq, gkv, wq, wqr, wk, wv, cos_t, sin_t)


def _rot_half(w):
    half = w.shape[-1] // 2
    return jnp.concatenate([-w[..., half:], w[..., :half]], axis=-1)


def _prep_mla_weights(w_down, w_uq, w_ukv, heads, q_lora, kv_lora):
    D = w_down.shape[0]
    tail = HEAD_W - QK_NOPE - QK_ROPE
    w_kr = w_down[:, q_lora + kv_lora:]
    slot = lambda w: jnp.concatenate([jnp.zeros((D, QK_NOPE), w.dtype), w, jnp.zeros((D, tail), w.dtype)], axis=1)
    wdn = jnp.concatenate([w_down[:, :q_lora + kv_lora], slot(w_kr), slot(_rot_half(w_kr))], axis=1)
    wq3 = w_uq.reshape(q_lora, heads, QK_NOPE + QK_ROPE)
    nope, ropew = wq3[..., :QK_NOPE], wq3[..., QK_NOPE:]
    zpad = jnp.zeros((q_lora, heads, tail), w_uq.dtype)
    wq = jnp.concatenate([nope, ropew, zpad], axis=-1)
    wqr = jnp.concatenate([jnp.zeros_like(nope), _rot_half(ropew), zpad], axis=-1)
    wkv3 = w_ukv.reshape(kv_lora, heads, QK_NOPE + HEAD_DIM)
    w_uk, w_uv = wkv3[..., :QK_NOPE], wkv3[..., QK_NOPE:]
    wk = jnp.concatenate([w_uk, jnp.zeros((kv_lora, heads, HEAD_W - QK_NOPE), w_ukv.dtype)], axis=-1)
    wv = jnp.transpose(w_uv.reshape(kv_lora, heads * HEAD_DIM))
    b16 = lambda a: a.astype(BF16)
    return (b16(wdn), b16(wq.reshape(q_lora, -1)), b16(wqr.reshape(q_lora, -1)), b16(wk.reshape(kv_lora, -1)),
            b16(wv), b16(jnp.transpose(w_uk, (1, 2, 0))), b16(jnp.transpose(w_uv, (1, 0, 2))))


def _rope_tables(pos):
    half = QK_ROPE // 2
    inv_freq = ROPE_BASE ** (-jnp.arange(half, dtype=F32) / half)
    ang = pos.astype(F32)[:, None] * inv_freq[None, :]
    n = pos.shape[0]
    pad = jnp.zeros((n, HEAD_W - QK_NOPE - QK_ROPE), F32)
    cos = jnp.concatenate([jnp.ones((n, QK_NOPE), F32)] + [jnp.cos(ang)] * 2 + [pad], axis=1)
    sin = jnp.concatenate([jnp.zeros((n, QK_NOPE), F32)] + [jnp.sin(ang)] * 2 + [pad], axis=1)
    return cos, sin


def _mla_sample_queries(qcat, heads):
    DB, n, _ = qcat.shape
    q4 = qcat.reshape(DB, n, heads, HEAD_W)
    rows = lambda a: jnp.transpose(a, (0, 2, 1, 3)).reshape(DB, heads * n, a.shape[-1])
    return rows(q4[..., :QK_NOPE]), rows(q4[..., QK_NOPE:QK_NOPE + QK_ROPE])


def _mla_sample_kernel(qn_ref, qr_ref, wuk_ref, wuv_ref, cc_ref, rc_ref, cn_ref, rn_ref, o_ref,
                       qlat_ref, s_ref, m_ref, l_ref, acc_ref, *, n, heads):
    kt = pl.program_id(1)

    @pl.when(kt == 0)
    def _():
        for h in range(heads):
            rows = slice(h * n, (h + 1) * n)
            qlat_ref[rows, :] = _dot(qn_ref[0, rows, :], wuk_ref[h]).astype(BF16)
        m_ref[...] = jnp.full(m_ref.shape, NEG_INF, F32)
        l_ref[...] = jnp.zeros_like(l_ref)
        acc_ref[...] = jnp.zeros_like(acc_ref)

    def scores(rows, ckv, kr_t):
        return _dot_nt(qlat_ref[rows, :], ckv) + _dot(qr_ref[0, rows, :], kr_t)

    def absorb(rows, s, ckv):
        m, l, acc = _softmax_step(s[None], ckv, (m_ref[:, rows], l_ref[:, rows], acc_ref[:, rows]))
        m_ref[:, rows] = m
        l_ref[:, rows] = l
        acc_ref[:, rows] = acc

    def update(ckv, kr_t):
        absorb(slice(None), scores(slice(None), ckv, kr_t), ckv)

    ckv = cc_ref[0].astype(BF16)
    kr_t = rc_ref[0].astype(BF16)
    chunk_rows = s_ref.shape[1]
    chunks = [slice(c * chunk_rows, (c + 1) * chunk_rows) for c in range(s_ref.shape[0])]
    s_ref[0] = scores(chunks[0], ckv, kr_t)
    for c, rows in enumerate(chunks):
        if c + 1 < len(chunks):
            s_ref[c + 1] = scores(chunks[c + 1], ckv, kr_t)
        absorb(rows, s_ref[c], ckv)

    @pl.when(kt == pl.num_programs(1) - 1)
    def _():
        update(cn_ref[0], rn_ref[0])
        o_lat = (acc_ref[0] / l_ref[0]).astype(BF16)
        for h in range(heads):
            o_ref[0, :, h * HEAD_DIM:(h + 1) * HEAD_DIM] = _dot(o_lat[h * n:(h + 1) * n], wuv_ref[h]).astype(o_ref.dtype)


def _mla_sample(qn, qr, wuk, wuv, ckv_c, kr_c, ckv_n, kr_n, n, tk):
    DB, R, _ = qn.shape
    heads = R // n
    P, C = ckv_c.shape[1], ckv_c.shape[2]
    const = lambda a: pl.BlockSpec(a.shape, lambda b, j: (0,) * a.ndim)
    per_b = lambda a: pl.BlockSpec((1,) + a.shape[1:], lambda b, j: (b, 0, 0))
    return pl.pallas_call(
        functools.partial(_mla_sample_kernel, n=n, heads=heads),
        grid=(DB, P // tk),
        in_specs=[per_b(qn), per_b(qr), const(wuk), const(wuv),
                  pl.BlockSpec((1, tk, C), lambda b, j: (b, j, 0)),
                  pl.BlockSpec((1, QK_ROPE, tk), lambda b, j: (b, 0, j)),
                  per_b(ckv_n), per_b(kr_n)],
        out_specs=pl.BlockSpec((1, n, heads * HEAD_DIM), lambda b, j: (b, 0, 0)),
        out_shape=jax.ShapeDtypeStruct((DB, n, heads * HEAD_DIM), BF16),
        scratch_shapes=[pltpu.VMEM((R, C), BF16), pltpu.VMEM((R // MLA_SAMPLE_CHUNK, MLA_SAMPLE_CHUNK, tk), F32),
                        pltpu.VMEM((1, R, 1), F32), pltpu.VMEM((1, R, 1), F32), pltpu.VMEM((1, R, C), F32)],
        compiler_params=_cparams("parallel", "arbitrary"),
        name="mla_sample",
    )(qn, qr, wuk, wuv, ckv_c, kr_c, ckv_n, kr_n)


def _moe_layer(h, ids, wts, w_gate, w_up, w_down, layer, g, b, alpha, tm, head_rows):
    T, D = h.shape
    n_experts = w_gate.shape[1] * w_gate.shape[2]
    tile_expert, tile_valid, src, pos = _route(ids[:2], n_experts, tm)
    flat3 = lambda w: w.reshape((-1,) + w.shape[3:])
    ys = _moe_experts(h, flat3(w_gate), flat3(w_up), flat3(w_down), tile_expert + layer * n_experts, tile_valid,
                      src, tm)
    pos_t = jnp.transpose(pos.reshape(2, T // tm, tm), (1, 0, 2)).reshape(T // tm, 1, 2 * tm)
    return _moe_combine(ys, h, pos_t, jnp.transpose(wts[:2]), g, b, alpha, tm, head_rows)


TOKEN_TILE = 256
FLASH_Q_TILE = 512
FLASH_KEY_TILE = 256
SB_Q_TILE = 256
SB_KEY_TILE = 128
CACHE_TILE = 1024
SB_SUB_TILE = 256


def kernel(x_prompt, x_sample, cache_fox_k, cache_fox_v, cache_fox_logf, cache_sb_k, cache_sb_v, cache_mla_ckv, cache_mla_krope, ab_w_in, ab_b_forget, ab_w_out, mla_w_down, mla_g_q, mla_g_kv, mla_w_uq, mla_w_ukv, mla_w_out, moe_w_group, moe_b_group, moe_w_router, moe_b_router, moe_w_gate, moe_w_up, moe_w_down, ln_g, ln_b):
    B, S, D = x_prompt.shape
    DB, n, _ = x_sample.shape
    P = cache_fox_k.shape[2]
    TP, TS = B * S, DB * n
    depth = ln_g.shape[0]
    n_groups = moe_w_group.shape[-1]
    assert depth == 2 and ab_w_in.shape[0] == 1 and mla_w_down.shape[0] == 1
    assert S % FLASH_Q_TILE == 0 and S % SB_Q_TILE == 0 and TP % TOKEN_TILE == 0 and TS % TOKEN_TILE == 0 and TOKEN_TILE % n == 0
    assert P % CACHE_TILE == 0 and P % CHUNK == 0 and n == CHUNK
    alpha = (2 * depth) ** 0.25
    tk = CACHE_TILE

    xp, xs = x_prompt.reshape(TP, D), x_sample.reshape(TS, D)
    sample3 = lambda a: a[TP:].reshape(DB, n, -1)

    def ffn(o_head_parts, o_tail, w_out, resid, layer):
        wrh, wrl, rb = _prep_router(moe_w_group[layer], moe_b_group[layer], moe_w_router[layer], moe_b_router[layer])
        h, ids, wts = _mix_out(o_head_parts, o_tail, w_out.astype(BF16), *resid, ln_g[layer, 0][None],
                               ln_b[layer, 0][None], wrh, wrl, rb, alpha, n_groups, TOKEN_TILE)
        return _moe_layer(h, ids, wts, moe_w_gate, moe_w_up, moe_w_down, layer,
                          ln_g[layer, 1][None], ln_b[layer, 1][None], alpha, TOKEN_TILE, TP)

    fox_heads = ab_b_forget.shape[1]
    hw = (ab_w_in.shape[2] - fox_heads) // 6
    w_ab, b_forget = _prep_ab_weights(ab_w_in[0], ab_b_forget[0])
    w_tok, w_t = _prompt_ab_weights(w_ab, hw, fox_heads)
    (qa_p, ka_p, qb_p, kb_p, lf_p, kat, vat, kbt, vbt, vat16, vbt16) = _ab_proj_prompt(
        xp, w_tok, b_forget, w_t, _query_decay_ones(fox_heads), B, S, fox_heads, SB_KEY_TILE)
    ka_p = _fox_insert_decay(ka_p, lf_p, B, S, fox_heads)
    qa, ka, va, qb, kb, vb, ka16, va16, kb16, vb16, lf = _ab_proj(xs, w_ab, b_forget, hw)
    dbn = lambda a: a.reshape(DB, n, -1)
    lf_s = dbn(lf)[:, :, :fox_heads]
    cq_s, ck_past, ck_new = _fox_sample_cum(lf_s, jnp.transpose(cache_fox_logf[0], (0, 2, 1)))
    cache_t = lambda c: jnp.transpose(c[0], (0, 2, 3, 1))
    hm = lambda a, t=False: _heads_major(dbn(a), fox_heads, t)
    o_fox_s = _fox_sample(hm(qa), hm(ka16, True), hm(va16, True), cache_t(cache_fox_k), cache_t(cache_fox_v),
                          cq_s, ck_past, ck_new, tk)
    o_sb_s = _sb_sample(hm(qb), hm(kb16, True), hm(vb16, True), cache_t(cache_sb_k), cache_t(cache_sb_v),
                        tk, SB_SUB_TILE)
    tokens_major = lambda a: jnp.transpose(a, (0, 2, 1, 3)).reshape(TS, hw)
    o_tail = jnp.concatenate([tokens_major(o_fox_s), tokens_major(o_sb_s)], axis=-1)
    o_fox_p = _flash_prompt(qa_p, ka_p, vat16, B, S, FLASH_Q_TILE, fox_heads, 1, "fox_prompt")
    o_sb_p = _sb_prompt(qb_p, kb_p, vbt16, B, S, SB_Q_TILE)
    xp, xs = ffn((o_fox_p, o_sb_p), o_tail, ab_w_out[0], (xp, xs), 0)

    q_lora, kv_lora = mla_g_q.shape[1], mla_g_kv.shape[1]
    heads = mla_w_uq.shape[2] // (QK_NOPE + QK_ROPE)
    wdn, wq, wqr, wk, wv, wuk_t, wuv = _prep_mla_weights(mla_w_down[0], mla_w_uq[0], mla_w_ukv[0], heads, q_lora, kv_lora)
    tm = TOKEN_TILE
    pos = jnp.concatenate([jnp.arange(S, dtype=jnp.int32), P + jnp.arange(tm, dtype=jnp.int32) % n])
    cos_t, sin_t = _rope_tables(pos)
    blocks_per_seq, prompt_blocks = S // tm, TP // tm
    table_block = lambda i: jnp.where(i < prompt_blocks, i % blocks_per_seq, blocks_per_seq)
    assert tm == FLASH_KEY_TILE
    ckv, kr, qcat, kcat, vt = _mla_proj(xp, xs, wdn, mla_g_q[0][None], mla_g_kv[0][None], wq, wqr, wk, wv,
                                        cos_t, sin_t, table_block, tm)
    qn, qr = _mla_sample_queries(sample3(qcat), heads)
    o_s = _mla_sample(qn, qr, wuk_t, wuv, cache_mla_ckv[0], jnp.transpose(cache_mla_krope[0], (0, 2, 1)),
                      sample3(ckv).astype(BF16), jnp.transpose(sample3(kr), (0, 2, 1)).astype(BF16), n, tk)
    o_p = _flash_prompt(qcat, kcat, vt, B, S, FLASH_Q_TILE, 8, CHUNK, "mla_prompt")
    xp, xs = ffn((o_p,), o_s.reshape(TS, -1), mla_w_out[0], (xp, xs), 1)

    rows_p = lambda a: jnp.transpose(a, (0, 3, 1, 2))[None]
    rows_s = lambda a: a.reshape(1, DB, n, fox_heads, hw // fox_heads)
    pr, sr = slice(0, TP), slice(TP, TP + TS)
    return (xp.reshape(B, S, D), xs.reshape(DB, n, D),
            rows_p(kat), rows_p(vat), lf_p[:, :fox_heads].reshape(1, B, S, fox_heads), rows_p(kbt), rows_p(vbt),
            ckv[pr].reshape(1, B, S, kv_lora), kr[pr].reshape(1, B, S, QK_ROPE),
            rows_s(ka), rows_s(va), lf[:, :fox_heads].reshape(1, DB, n, fox_heads), rows_s(kb), rows_s(vb),
            ckv[sr].reshape(1, DB, n, kv_lora), kr[sr].reshape(1, DB, n, QK_ROPE))
```

```python
import functools

import jax
import jax.numpy as jnp
from jax import lax
from jax.experimental import pallas as pl
from jax.experimental.pallas import tpu as pltpu

F32 = jnp.float32
BF16 = jnp.bfloat16
NEG_INF = -1e30
LOG2E = 1.4426950408889634

LANES = 128
HEAD_DIM = 64
PAIR_W = 2 * HEAD_DIM
CHUNK = 64
LN_EPS = 1e-5
RMS_EPS = 1e-6
ROPE_BASE = 10000.0
VMEM_LIMIT = 56 * 1024 * 1024


def _cparams(*sem):
    return pltpu.CompilerParams(dimension_semantics=sem, vmem_limit_bytes=VMEM_LIMIT)


def _dot(a, b):
    return jnp.dot(a, b, preferred_element_type=F32)


def _dot_nt(a, b):
    return lax.dot_general(a, b, (((1,), (1,)), ((), ())), preferred_element_type=F32)


def _split_bf16(x, parts):
    out = []
    r = x
    for _ in range(parts):
        h = r.astype(BF16)
        out.append(h)
        r = r - h.astype(F32)
    return out


def _split_bf16_trunc(x, parts):
    out = []
    r = x
    for _ in range(parts):
        bits = lax.bitcast_convert_type(r, jnp.uint32) & jnp.uint32(0xFFFF0000)
        h = lax.bitcast_convert_type(bits, F32)
        out.append(h.astype(BF16))
        r = r - h
    return out


def _log_sigmoid(x):
    return jnp.minimum(x, 0.0) - jnp.log(1.0 + jnp.exp(-jnp.abs(x)))


def _pick_tile(n, pref, mult=8):
    t = min(pref, n)
    while n % t or t % mult:
        t -= 1
    return t


def _ab_proj_kernel(x_ref, w_ref, bf_ref, qa_ref, ka_ref, va_ref, qb_ref, kb_ref, vb_ref,
                    ka16_ref, va16_ref, kb16_ref, vb16_ref, lf_ref, *, hw, qscale):
    xb = x_ref[...].astype(BF16)

    def seg(j):
        return _dot(xb, w_ref[:, j * hw:(j + 1) * hw])

    qa_ref[...] = (seg(0) * qscale).astype(BF16)
    z = seg(1)
    ka_ref[...] = z
    ka16_ref[...] = z.astype(BF16)
    z = seg(2)
    va_ref[...] = z
    va16_ref[...] = z.astype(BF16)
    qb_ref[...] = (seg(3) * qscale).astype(BF16)
    z = seg(4)
    kb_ref[...] = z
    kb16_ref[...] = z.astype(BF16)
    z = seg(5)
    vb_ref[...] = z
    vb16_ref[...] = z.astype(BF16)
    f = _dot(xb, w_ref[:, 6 * hw:6 * hw + LANES]) + bf_ref[...]
    lf_ref[...] = _log_sigmoid(f)


def _ab_proj(x, w, bf, hw):
    T, D = x.shape
    tm = _pick_tile(T, 256)
    row = lambda w_: pl.BlockSpec((tm, w_), lambda i: (i, 0))
    f32o = jax.ShapeDtypeStruct((T, hw), F32)
    b16o = jax.ShapeDtypeStruct((T, hw), BF16)
    return pl.pallas_call(
        functools.partial(_ab_proj_kernel, hw=hw, qscale=HEAD_DIM ** -0.5 * LOG2E),
        grid=(T // tm,),
        in_specs=[row(D), pl.BlockSpec(w.shape, lambda i: (0, 0)), pl.BlockSpec(bf.shape, lambda i: (0, 0))],
        out_specs=[row(hw)] * 10 + [row(LANES)],
        out_shape=[b16o, f32o, f32o, b16o, f32o, f32o, b16o, b16o, b16o, b16o,
                   jax.ShapeDtypeStruct((T, LANES), F32)],
        compiler_params=_cparams("parallel"),
        name="ab_proj",
    )(x, w, bf)


def _ab_proj_prompt_kernel(x_ref, w_ref, bf_ref, wt_ref, qone_ref, qa_ref, ka_ref, qb_ref, kb_ref, lf_ref,
                           kat_ref, vat_ref, kbt_ref, vbt_ref, vat16_ref, vbt16_ref, *, hw, heads, qscale, sb_tk):
    xb = x_ref[...].astype(BF16)
    tm = xb.shape[0]
    wide = heads * HEAD_W

    def seg(j):
        return _dot(xb, w_ref[:, j * wide:(j + 1) * wide])

    def seg_t(j):
        return _dot_nt(wt_ref[j * hw:(j + 1) * hw, :], xb).reshape(heads, hw // heads, tm)

    qa_ref[...] = (seg(0) * qscale + qone_ref[...]).astype(BF16)
    ka_ref[...] = seg(1).astype(BF16)
    qb_ref[...] = (seg(2) * qscale).astype(BF16)
    kb_ref[...] = seg(3).astype(BF16)
    lf_ref[...] = _log_sigmoid(_dot(xb, w_ref[:, 4 * wide:4 * wide + LANES]) + bf_ref[...])
    kat_ref[0] = seg_t(0)
    z = seg_t(1)
    vat_ref[0] = z
    vat16_ref[0, :, :HEAD_DIM, :] = z.astype(BF16)
    vat16_ref[0, :, HEAD_DIM:, :] = jnp.ones((heads, VT_ROWS - HEAD_DIM, tm), BF16)
    kbt_ref[0] = seg_t(2)
    z = seg_t(3)
    vbt_ref[0] = z
    for c in range(tm // sb_tk):
        vbt16_ref[c] = z[:, :, c * sb_tk:(c + 1) * sb_tk].astype(BF16)


def _ab_proj_prompt(x, w_tok, bf, w_t, q_ones, B, S, heads, sb_tk):
    TP, D = x.shape
    hw = w_t.shape[0] // 4
    wide = heads * HEAD_W
    tm = FLASH_KEY_TILE
    nj = S // tm
    const = lambda a: pl.BlockSpec(a.shape, lambda b, j: (0,) * a.ndim)
    row = lambda w_: pl.BlockSpec((tm, w_), lambda b, j: (b * nj + j, 0))
    t_spec = pl.BlockSpec((1, heads, hw // heads, tm), lambda b, j: (b, 0, 0, j))
    b16 = jax.ShapeDtypeStruct((TP, wide), BF16)
    t32 = jax.ShapeDtypeStruct((B, heads, hw // heads, S), F32)
    return pl.pallas_call(
        functools.partial(_ab_proj_prompt_kernel, hw=hw, heads=heads, qscale=HEAD_DIM ** -0.5 * LOG2E, sb_tk=sb_tk),
        grid=(B, nj),
        in_specs=[row(D), const(w_tok), const(bf), const(w_t), const(q_ones)],
        out_specs=[row(wide)] * 4 + [row(LANES)] + [t_spec] * 4 + [
            pl.BlockSpec((1, heads, VT_ROWS, tm), lambda b, j: (b * nj + j, 0, 0, 0)),
            pl.BlockSpec((tm // sb_tk, heads, hw // heads, sb_tk), lambda b, j: (b * nj + j, 0, 0, 0))],
        out_shape=[b16] * 4 + [jax.ShapeDtypeStruct((TP, LANES), F32)] + [t32] * 4 + [
            jax.ShapeDtypeStruct((TP // tm, heads, VT_ROWS, tm), BF16),
            jax.ShapeDtypeStruct((TP // sb_tk, heads, hw // heads, sb_tk), BF16)],
        compiler_params=_cparams("parallel", "parallel"),
        name="ab_proj_prompt",
    )(x, w_tok, bf, w_t, q_ones)


def _cumsum_kernel(x_ref, o_ref, carry_ref, *, tl):
    @pl.when(pl.program_id(0) == 0)
    def _():
        carry_ref[...] = jnp.zeros_like(carry_ref)

    x = x_ref[...]
    rows = x.shape[0]
    r = lax.broadcasted_iota(jnp.int32, (tl, tl), 0)
    c = lax.broadcasted_iota(jnp.int32, (tl, tl), 1)
    upper = (r <= c).astype(BF16)
    parts = jnp.concatenate(_split_bf16(x, 4), axis=0)
    y = _dot(parts, upper)
    cum = (y[0:rows] + y[rows:2 * rows]) + (y[2 * rows:3 * rows] + y[3 * rows:]) + carry_ref[:, 0:1]
    o_ref[...] = cum
    carry_ref[...] = jnp.broadcast_to(cum[:, tl - 1:tl], carry_ref.shape)


def _cumsum_rows(x):
    B, H, L = x.shape
    tl = _pick_tile(L, 512, LANES)
    out = pl.pallas_call(
        functools.partial(_cumsum_kernel, tl=tl),
        grid=(L // tl,),
        in_specs=[pl.BlockSpec((B * H, tl), lambda j: (0, j))],
        out_specs=pl.BlockSpec((B * H, tl), lambda j: (0, j)),
        out_shape=jax.ShapeDtypeStruct((B * H, L), F32),
        scratch_shapes=[pltpu.VMEM((B * H, LANES), F32)],
        compiler_params=_cparams("arbitrary"),
        name="cumsum_rows",
    )(x.reshape(B * H, L))
    return out.reshape(B, H, L)


def _stack_pair(qp):
    lo = lax.broadcasted_iota(jnp.int32, qp.shape, 1) < HEAD_DIM
    zero = jnp.zeros_like(qp)
    return jnp.concatenate([jnp.where(lo, qp, zero), jnp.where(lo, zero, qp)], axis=0)


def _unstack_pair(o):
    lo = lax.broadcasted_iota(jnp.int32, o.shape[1:], 1) < HEAD_DIM
    return jnp.where(lo, o[0], o[1])


def _softmax_step(s, vb, carry):
    m, l, acc = carry
    two, tq, tk = s.shape
    m_new = jnp.maximum(m, jnp.max(s, axis=-1, keepdims=True))
    alpha = jnp.exp2(m - m_new)
    p = jnp.exp2(s - m_new)
    l = alpha * l + jnp.sum(p, axis=-1, keepdims=True)
    pv = _dot(p.reshape(two * tq, tk).astype(BF16), vb).reshape(two, tq, vb.shape[-1])
    return m_new, l, alpha * acc + pv


def _log2_sigmoid_pair(z2):
    l1 = jnp.log2(1.0 + jnp.exp2(-jnp.abs(z2)))
    return jnp.minimum(z2, 0.0) - l1, jnp.minimum(-z2, 0.0) - l1


def _softmax_init(tq, width):
    return (jnp.full((2, tq, 1), NEG_INF, F32), jnp.zeros((2, tq, 1), F32), jnp.zeros((2, tq, width), F32))


def _sb_step(z, vb, strict_upper, carry, mask=None):
    run, acc = carry
    log_beta, log_rest = _log2_sigmoid_pair(z)
    if mask is not None:
        log_beta = jnp.where(mask, log_beta, NEG_INF)
        log_rest = jnp.where(mask, log_rest, 0.0)
    hi, lo = _split_bf16(log_rest, 2)
    later = _dot(hi, strict_upper) + _dot(lo, strict_upper)
    a = jnp.exp2(log_beta + later + run)
    acc = acc + _dot(a.astype(BF16), vb)
    run = run + jnp.sum(log_rest, axis=-1, keepdims=True)
    return run, acc


def _strict_upper(tk):
    r = lax.broadcasted_iota(jnp.int32, (tk, tk), 0)
    c = lax.broadcasted_iota(jnp.int32, (tk, tk), 1)
    return (r > c).astype(BF16)


HEAD_W = LANES
VT_ROWS = HEAD_DIM + 16
SB_DEAD_LOG2 = -160.0


def _pipeline_ahead(stage, first, count, cur, nxt):
    if first < count:
        stage(first, cur)
    elif nxt is not None:
        stage(first - count, nxt)


def _flash_prompt_kernel(q_ref, k_ref, vt_ref, o_ref, s_ref, m_ref, acc_ref, *, tq, tk, heads, chunk):
    i = pl.program_id(2)
    sub = tq // tk
    key = lax.broadcasted_iota(jnp.int32, (tk, tq), 0)
    query = lax.broadcasted_iota(jnp.int32, (tk, tq), 1)
    m_ref[...] = jnp.full(m_ref.shape, NEG_INF, F32)
    acc_ref[...] = jnp.zeros_like(acc_ref)

    def scores(h, kt):
        rows = pl.ds(pl.multiple_of(kt * tk, tk), tk)
        lanes = slice(h * HEAD_W, (h + 1) * HEAD_W)
        s_ref[h] = _dot_nt(k_ref[rows, lanes], q_ref[:, lanes])

    def absorb(h, kt, visible):
        s_t = s_ref[h]
        if visible is not None:
            s_t = jnp.where(visible, s_t, NEG_INF)
        m = m_ref[h]
        m_new = jnp.maximum(m, jnp.max(s_t, axis=0, keepdims=True))
        p_t = jnp.exp2(s_t - m_new).astype(BF16)
        pv = _dot(vt_ref[kt, h], p_t)
        acc_ref[h] = jnp.exp2(m - m_new) * acc_ref[h] + pv
        m_ref[h] = m_new

    scores(0, 0)
    scores(1, 0)

    def body(kt, carry):
        for h in range(heads):
            _pipeline_ahead(scores, h + 2, heads, kt, kt + 1)
            absorb(h, kt, None)
        return carry

    lax.fori_loop(0, sub * i, body, 0)
    for s in range(sub):
        kt = sub * i + s
        visible = ((s * tk + key) // chunk) <= (query // chunk)
        for h in range(heads):
            _pipeline_ahead(scores, h + 2, heads, kt, kt + 1 if s + 1 < sub else None)
            absorb(h, kt, visible)
    for j in range(heads // 2):
        a0, a1 = acc_ref[2 * j], acc_ref[2 * j + 1]
        o_t = jnp.concatenate([a0[:HEAD_DIM] / a0[HEAD_DIM:HEAD_DIM + 1],
                               a1[:HEAD_DIM] / a1[HEAD_DIM:HEAD_DIM + 1]], axis=0)
        o_ref[:, j * PAIR_W:(j + 1) * PAIR_W] = jnp.transpose(o_t).astype(o_ref.dtype)


def _flash_prompt(qx, kx, vt4, B, S, tq, heads_per_step, chunk, name):
    W = qx.shape[1]
    nq = S // tq
    hs = heads_per_step
    tk = vt4.shape[3]
    nk = S // tk
    return pl.pallas_call(
        functools.partial(_flash_prompt_kernel, tq=tq, tk=tk, heads=hs, chunk=chunk),
        grid=(B, W // (HEAD_W * hs), S // tq),
        in_specs=[pl.BlockSpec((tq, HEAD_W * hs), lambda b, g, i: (b * nq + i, g)),
                  pl.BlockSpec((S, HEAD_W * hs), lambda b, g, i: (b, g)),
                  pl.BlockSpec((nk, hs, VT_ROWS, tk), lambda b, g, i: (b, g, 0, 0))],
        out_specs=pl.BlockSpec((tq, HEAD_DIM * hs), lambda b, g, i: (b * nq + i, g)),
        out_shape=jax.ShapeDtypeStruct((B * S, W // HEAD_W * HEAD_DIM), BF16),
        scratch_shapes=[pltpu.VMEM((hs, tk, tq), F32), pltpu.VMEM((hs, 1, tq), F32),
                        pltpu.VMEM((hs, VT_ROWS, tq), F32)],
        compiler_params=_cparams("parallel", "parallel", "arbitrary"),
        name=name,
    )(qx, kx, vt4)


def _sb_prompt_kernel(q_ref, k_ref, vt_ref, o_ref, z_ref, lw_ref, tot_ref, run_ref, acc_ref, *, tq, tk, heads):
    i = pl.program_id(1)
    sub = tq // tk
    key = lax.broadcasted_iota(jnp.int32, (tk, tq), 0)
    query = lax.broadcasted_iota(jnp.int32, (tk, tq), 1)
    r = lax.broadcasted_iota(jnp.int32, (tk, tk), 0)
    c = lax.broadcasted_iota(jnp.int32, (tk, tk), 1)
    after = (c > r).astype(BF16)
    run_ref[...] = jnp.zeros_like(run_ref)
    acc_ref[...] = jnp.zeros_like(acc_ref)

    def logits(h, kt):
        rows = pl.ds(pl.multiple_of(kt * tk, tk), tk)
        lanes = slice(h * HEAD_W, (h + 1) * HEAD_W)
        z_ref[h] = _dot_nt(k_ref[rows, lanes], q_ref[:, lanes])

    def log_weights(before, h, kt):
        log_beta, log_rest = _log2_sigmoid_pair(z_ref[h])
        if before is not None:
            log_beta = jnp.where(before, log_beta, NEG_INF)
            log_rest = jnp.where(before, log_rest, 0.0)
        hi, lo = _split_bf16(log_rest, 2)
        later = _dot(after, hi) + _dot(after, lo)
        lw_ref[h] = log_beta + later
        tot_ref[h] = later[0:1] + log_rest[0:1]

    def accumulate(h, kt):
        run = run_ref[h]
        a_t = jnp.exp2(lw_ref[h] + run).astype(BF16)
        acc_ref[h] += _dot(vt_ref[kt, h], a_t)
        run_ref[h] = run + tot_ref[h]

    def step(kt, masked, nxt, nxt_masked):
        for h in range(heads):
            _pipeline_ahead(logits, h + 2, heads, kt, nxt)
            if h + 1 < heads:
                log_weights(masked, h + 1, kt)
            elif nxt is not None:
                log_weights(nxt_masked, 0, nxt)
            accumulate(h, kt)

    unmasked = sub * i
    first = unmasked + sub - 1
    masks = [((sub - 1 - s) * tk + key) < query for s in range(sub)]
    logits(0, first)
    logits(1, first)
    log_weights(masks[0], 0, first)
    for s in range(sub):
        kt = first - s
        if s + 1 < sub:
            step(kt, masks[s], kt - 1, masks[s + 1])
        else:
            step(kt, masks[s], jnp.maximum(kt - 1, 0), None)

    def alive():
        return (jnp.max(run_ref[...]) > SB_DEAD_LOG2).astype(jnp.int32)

    def body(carry):
        kt, _ = carry
        step(kt, None, jnp.maximum(kt - 1, 0), None)
        return kt - 1, alive()

    lax.while_loop(lambda c: jnp.logical_and(c[0] >= 0, c[1] > 0), body, (unmasked - 1, alive()))

    for j in range(heads // 2):
        o_t = jnp.concatenate([acc_ref[2 * j], acc_ref[2 * j + 1]], axis=0)
        o_ref[:, j * PAIR_W:(j + 1) * PAIR_W] = jnp.transpose(o_t).astype(o_ref.dtype)


def _sb_prompt(qx, kx, vt4, B, S, tq):
    W = qx.shape[1]
    heads = W // HEAD_W
    tk = vt4.shape[3]
    nq = S // tq
    return pl.pallas_call(
        functools.partial(_sb_prompt_kernel, tq=tq, tk=tk, heads=heads),
        grid=(B, nq),
        in_specs=[pl.BlockSpec((tq, W), lambda b, i: (b * nq + i, 0)),
                  pl.BlockSpec((S, W), lambda b, i: (b, 0)),
                  pl.BlockSpec((S // tk, heads, HEAD_DIM, tk), lambda b, i: (b, 0, 0, 0))],
        out_specs=pl.BlockSpec((tq, heads * HEAD_DIM), lambda b, i: (b * nq + i, 0)),
        out_shape=jax.ShapeDtypeStruct((B * S, heads * HEAD_DIM), BF16),
        scratch_shapes=[pltpu.VMEM((heads, tk, tq), F32), pltpu.VMEM((heads, tk, tq), F32),
                        pltpu.VMEM((heads, 1, tq), F32), pltpu.VMEM((heads, 1, tq), F32),
                        pltpu.VMEM((heads, HEAD_DIM, tq), F32)],
        compiler_params=_cparams("parallel", "arbitrary"),
        name="sb_prompt",
    )(qx, kx, vt4)


def _prep_ab_weights(w_in, b_f):
    D = w_in.shape[0]
    H = b_f.shape[0]
    hw = (w_in.shape[1] - H) // 6
    main = jnp.concatenate([w_in[:, :3 * hw], w_in[:, 3 * hw + H:]], axis=1)
    wf = jnp.zeros((D, LANES), w_in.dtype).at[:, :H].set(w_in[:, 3 * hw:3 * hw + H])
    bf = jnp.zeros((1, LANES), F32).at[0, :H].set(b_f.astype(F32))
    return jnp.concatenate([main, wf], axis=1).astype(BF16), bf


def _prompt_ab_weights(w_ab, hw, heads):
    D = w_ab.shape[0]
    seg = lambda j: w_ab[:, j * hw:(j + 1) * hw]
    pad = lambda w: jnp.concatenate([w.reshape(D, heads, hw // heads),
                                     jnp.zeros((D, heads, HEAD_W - hw // heads), w.dtype)], axis=-1).reshape(D, -1)
    w_tok = jnp.concatenate([pad(seg(0)), pad(seg(1)), pad(seg(3)), pad(seg(4)), w_ab[:, 6 * hw:]], axis=1)
    w_t = jnp.transpose(jnp.concatenate([seg(1), seg(2), seg(4), seg(5)], axis=1))
    return w_tok, w_t


DECAY_TERMS = 3


def _insert_decay_kernel(k_ref, c_ref, sel_ref, o_ref):
    terms = sum(_dot(part, sel_ref[j]) for j, part in enumerate(_split_bf16(c_ref[...], DECAY_TERMS)))
    o_ref[...] = (k_ref[...].astype(F32) + terms).astype(BF16)


def _fox_insert_decay(kx, lf, B, S, H):
    lt = jnp.transpose(lf[:, :H].reshape(B, S, H), (0, 2, 1))
    cum = jnp.transpose(_cumsum_rows(lt), (0, 2, 1)).reshape(B * S, H) * (-LOG2E)
    c = jnp.pad(cum, ((0, 0), (0, LANES - H)))
    src = jnp.arange(LANES)[None, :, None]
    dst = jnp.arange(H * HEAD_W)[None, None, :]
    term = jnp.arange(DECAY_TERMS)[:, None, None]
    sel = jnp.logical_and(src < H, dst == src * HEAD_W + HEAD_DIM + term).astype(BF16)
    tm = _pick_tile(B * S, 512)
    return pl.pallas_call(
        _insert_decay_kernel,
        grid=(B * S // tm,),
        in_specs=[pl.BlockSpec((tm, H * HEAD_W), lambda i: (i, 0)), pl.BlockSpec((tm, LANES), lambda i: (i, 0)),
                  pl.BlockSpec(sel.shape, lambda i: (0, 0, 0))],
        out_specs=pl.BlockSpec((tm, H * HEAD_W), lambda i: (i, 0)),
        out_shape=jax.ShapeDtypeStruct(kx.shape, BF16),
        input_output_aliases={0: 0},
        compiler_params=_cparams("parallel"),
        name="fox_insert_decay",
    )(kx, c, sel)


def _query_decay_ones(H):
    lane = jnp.arange(H * HEAD_W) % HEAD_W
    return jnp.logical_and(lane >= HEAD_DIM, lane < HEAD_DIM + DECAY_TERMS).astype(F32)[None]


def _fox_sample_kernel(q_ref, knt_ref, vnt_ref, kct_ref, vct_ref, cq_ref, ckp_ref, ckn_ref, o_ref,
                       s_ref, m_ref, l_ref, acc_ref, *, n, heads):
    kt = pl.program_id(1)

    @pl.when(kt == 0)
    def _():
        m_ref[...] = jnp.full(m_ref.shape, NEG_INF, F32)
        l_ref[...] = jnp.zeros_like(l_ref)
        acc_ref[...] = jnp.zeros_like(acc_ref)

    def absorb(h, s, v_t, ck, mask):
        s = s + (cq_ref[0, h] - ck)
        if mask is not None:
            s = jnp.where(mask, s, NEG_INF)
        m = m_ref[h]
        m_new = jnp.maximum(m, jnp.max(s, axis=-1, keepdims=True))
        alpha = jnp.exp2(m - m_new)
        p = jnp.exp2(s - m_new)
        l_ref[h] = alpha * l_ref[h] + jnp.sum(p, axis=-1, keepdims=True)
        acc_ref[h] = alpha * acc_ref[h] + _dot_nt(p.astype(BF16), v_t)
        m_ref[h] = m_new

    def scores(h, _=None):
        s_ref[h] = _dot(q_ref[0, h], kct_ref[0, h].astype(BF16))

    scores(0)
    scores(1)
    for h in range(heads):
        _pipeline_ahead(scores, h + 2, heads, None, None)
        absorb(h, s_ref[h], vct_ref[0, h].astype(BF16), ckp_ref[0, h:h + 1, :], None)

    @pl.when(kt == pl.num_programs(1) - 1)
    def _():
        row = lax.broadcasted_iota(jnp.int32, (n, n), 0)
        col = lax.broadcasted_iota(jnp.int32, (n, n), 1)
        for h in range(heads):
            absorb(h, _dot(q_ref[0, h], knt_ref[0, h]), vnt_ref[0, h], ckn_ref[0, h:h + 1, :], col <= row)
            o_ref[0, h] = (acc_ref[h] / l_ref[h]).astype(o_ref.dtype)


def _fox_sample(q, knt, vnt, kct, vct, cq, ckp, ckn, tk):
    DB, H, n, dh = q.shape
    P = kct.shape[-1]
    per_b = lambda a: pl.BlockSpec((1,) + a.shape[1:], lambda b, j: (b,) + (0,) * (a.ndim - 1))
    cache = pl.BlockSpec((1, H, dh, tk), lambda b, j: (b, 0, 0, j))
    return pl.pallas_call(
        functools.partial(_fox_sample_kernel, n=n, heads=H),
        grid=(DB, P // tk),
        in_specs=[per_b(q), per_b(knt), per_b(vnt), cache, cache, per_b(cq),
                  pl.BlockSpec((1, H, tk), lambda b, j: (b, 0, j)), per_b(ckn)],
        out_specs=per_b(q),
        out_shape=jax.ShapeDtypeStruct(q.shape, BF16),
        scratch_shapes=[pltpu.VMEM((H, n, tk), F32), pltpu.VMEM((H, n, 1), F32), pltpu.VMEM((H, n, 1), F32),
                        pltpu.VMEM((H, n, dh), F32)],
        compiler_params=_cparams("parallel", "arbitrary"),
        name="fox_sample",
    )(q, knt, vnt, kct, vct, cq, ckp, ckn)


def _sb_sample_kernel(q_ref, knt_ref, vnt_ref, kct_ref, vct_ref, o_ref, z_ref, run_ref, acc_ref, *, n, heads, sub):
    kt = pl.program_id(1)
    upper = _strict_upper(sub)

    def absorb(h, z, v_t, upper_m, width, mask=None):
        log_beta, log_rest = _log2_sigmoid_pair(z)
        if mask is not None:
            log_beta = jnp.where(mask, log_beta, NEG_INF)
            log_rest = jnp.where(mask, log_rest, 0.0)
        hi, lo = _split_bf16(log_rest, 2)
        run = run_ref[h]
        parts = []
        for c in reversed(range(z.shape[1] // width)):
            keys = slice(c * width, (c + 1) * width)
            later = _dot(hi[:, keys], upper_m) + _dot(lo[:, keys], upper_m)
            parts.append(jnp.exp2(log_beta[:, keys] + later + run).astype(BF16))
            run = run + jnp.sum(log_rest[:, keys], axis=-1, keepdims=True)
        a = parts[0] if len(parts) == 1 else jnp.concatenate(parts[::-1], axis=1)
        acc_ref[h] += _dot_nt(a, v_t)
        run_ref[h] = run

    def logits(h, _=None):
        z_ref[h] = _dot(q_ref[0, h], kct_ref[0, h].astype(BF16))

    @pl.when(kt == 0)
    def _():
        row = lax.broadcasted_iota(jnp.int32, (n, n), 0)
        col = lax.broadcasted_iota(jnp.int32, (n, n), 1)
        upper_n = _strict_upper(n)
        run_ref[...] = jnp.zeros_like(run_ref)
        acc_ref[...] = jnp.zeros_like(acc_ref)
        for h in range(heads):
            absorb(h, _dot(q_ref[0, h], knt_ref[0, h]), vnt_ref[0, h], upper_n, n, col < row)

    logits(0)
    logits(1)
    for h in range(heads):
        _pipeline_ahead(logits, h + 2, heads, None, None)
        absorb(h, z_ref[h], vct_ref[0, h].astype(BF16), upper, sub)

    @pl.when(kt == pl.num_programs(1) - 1)
    def _():
        o_ref[0] = acc_ref[...].astype(o_ref.dtype)


def _sb_sample(q, knt, vnt, kct, vct, tk, sub):
    DB, H, n, dh = q.shape
    P = kct.shape[-1]
    nk = P // tk
    per_b = lambda a: pl.BlockSpec((1,) + a.shape[1:], lambda b, j: (b,) + (0,) * (a.ndim - 1))
    cache = pl.BlockSpec((1, H, dh, tk), lambda b, j: (b, 0, 0, nk - 1 - j))
    return pl.pallas_call(
        functools.partial(_sb_sample_kernel, n=n, heads=H, sub=sub),
        grid=(DB, nk),
        in_specs=[per_b(q), per_b(knt), per_b(vnt), cache, cache],
        out_specs=per_b(q),
        out_shape=jax.ShapeDtypeStruct(q.shape, BF16),
        scratch_shapes=[pltpu.VMEM((H, n, tk), F32), pltpu.VMEM((H, n, 1), F32), pltpu.VMEM((H, n, dh), F32)],
        compiler_params=_cparams("parallel", "arbitrary"),
        name="sb_sample",
    )(q, knt, vnt, kct, vct)


def _fox_sample_cum(lf_new, lf_past_t):
    DB, H, P = lf_past_t.shape
    n = lf_new.shape[1]
    L = -(-(P + n) // LANES) * LANES
    both = jnp.concatenate([lf_past_t.astype(F32), jnp.transpose(lf_new, (0, 2, 1)),
                            jnp.zeros((DB, H, L - P - n), F32)], axis=2)
    cum_t = _cumsum_rows(both) * LOG2E
    ckn = cum_t[:, :, P:P + n]
    return ckn[..., None], cum_t[:, :, :P], ckn


def _heads_major(a, heads, transpose_rows):
    DB, n, _ = a.shape
    a4 = a.reshape(DB, n, heads, -1)
    return jnp.transpose(a4, (0, 2, 3, 1) if transpose_rows else (0, 2, 1, 3))


ROUTER_ROWS = 48


def _layer_norm(y, g, b):
    mu = jnp.mean(y, axis=-1, keepdims=True)
    yc = y - mu
    var = jnp.mean(yc * yc, axis=-1, keepdims=True)
    return yc * lax.rsqrt(var + LN_EPS) * g + b


def _first_argmax(v, ridx):
    vmax = jnp.max(v, axis=0, keepdims=True)
    idx = jnp.min(jnp.where(v == vmax, ridx, v.shape[0]), axis=0, keepdims=True)
    return vmax, idx


def _two_part_specs(tm, width, head_tiles):
    return (pl.BlockSpec((tm, width), lambda i: (jnp.minimum(i, head_tiles - 1), 0)),
            pl.BlockSpec((tm, width), lambda i: (jnp.maximum(i - head_tiles, 0), 0)))


def _mix_out_kernel(*refs, alpha, n_groups, head_tiles, n_parts):
    o_parts, ot_ref = refs[:n_parts], refs[n_parts]
    w_ref, xh_ref, xt_ref, g_ref, b_ref, wrh_ref, wrl_ref, rb_ref, h_ref, ids_ref, wts_ref = refs[n_parts + 1:]
    in_head = pl.program_id(0) < head_tiles
    x = jnp.where(in_head, xh_ref[...], xt_ref[...])
    o_head = o_parts[0][...] if n_parts == 1 else jnp.concatenate([r[...] for r in o_parts], axis=1)
    o = jnp.where(in_head, o_head, ot_ref[...])
    h = _layer_norm(alpha * x + _dot(o, w_ref[...]), g_ref[...], b_ref[...])
    h_ref[...] = h
    hh, hl = _split_bf16(h, 2)
    R = ROUTER_ROWS
    wr = _dot_nt(wrl_ref[...], hh)
    lg = wr[:R] + (_dot_nt(wrh_ref[...], hl) + (wr[R:2 * R] + wr[2 * R:])) + rb_ref[...]
    tm = lg.shape[1]
    ridx = lax.broadcasted_iota(jnp.int32, (8, tm), 0)
    g = jnp.where(ridx < n_groups, lg[0:8], NEG_INF)
    gmax, gidx = _first_argmax(g, ridx)
    gate = 1.0 / jnp.sum(jnp.exp(g - gmax), axis=0, keepdims=True)
    esel = lg[8:16]
    for gg in range(1, n_groups):
        esel = jnp.where(gidx == gg, lg[8 + 8 * gg:16 + 8 * gg], esel)
    v1, i1 = _first_argmax(esel, ridx)
    v2, i2 = _first_argmax(jnp.where(ridx == i1, NEG_INF, esel), ridx)
    t = jnp.exp(v2 - v1)
    w1 = 1.0 / (1.0 + t)
    ids_ref[...] = jnp.where(ridx == 0, gidx * 8 + i1, jnp.where(ridx == 1, gidx * 8 + i2, 0))
    wts_ref[...] = jnp.where(ridx == 0, gate * w1, jnp.where(ridx == 1, gate * (t * w1), 0.0))


def _mix_out(o_head_parts, o_tail, w, x_head, x_tail, g, b, wrh, wrl, rb, alpha, n_groups, tm):
    D = x_head.shape[1]
    T = x_head.shape[0] + x_tail.shape[0]
    head_tiles = x_head.shape[0] // tm
    const = lambda a: pl.BlockSpec(a.shape, lambda i: (0,) * a.ndim)
    rb_t = jnp.broadcast_to(rb, (ROUTER_ROWS, tm))
    head_spec = lambda width: _two_part_specs(tm, width, head_tiles)[0]
    return pl.pallas_call(
        functools.partial(_mix_out_kernel, alpha=alpha, n_groups=n_groups, head_tiles=head_tiles,
                          n_parts=len(o_head_parts)),
        grid=(T // tm,),
        in_specs=[*[head_spec(p.shape[1]) for p in o_head_parts], _two_part_specs(tm, o_tail.shape[1], head_tiles)[1],
                  const(w), *_two_part_specs(tm, D, head_tiles), const(g), const(b), const(wrh), const(wrl),
                  const(rb_t)],
        out_specs=[pl.BlockSpec((tm, D), lambda i: (i, 0)), pl.BlockSpec((8, tm), lambda i: (0, i)),
                   pl.BlockSpec((8, tm), lambda i: (0, i))],
        out_shape=[jax.ShapeDtypeStruct((T, D), F32), jax.ShapeDtypeStruct((8, T), jnp.int32),
                   jax.ShapeDtypeStruct((8, T), F32)],
        compiler_params=_cparams("parallel"),
        name="mix_out_ln_router",
    )(*o_head_parts, o_tail, w, x_head, x_tail, g, b, wrh, wrl, rb_t)


def _prep_router(w_group, b_group, w_router, b_router):
    D, G = w_group.shape
    E = w_router.shape[-1]
    wr = jnp.zeros((ROUTER_ROWS, D), F32)
    wr = wr.at[:G].set(w_group.T.astype(F32))
    wr = wr.at[8:8 + G * E].set(jnp.transpose(w_router, (0, 2, 1)).reshape(G * E, D).astype(F32))
    rb = jnp.zeros((ROUTER_ROWS, 1), F32)
    rb = rb.at[:G, 0].set(b_group.astype(F32)).at[8:8 + G * E, 0].set(b_router.reshape(-1).astype(F32))
    hi, mid, lo = _split_bf16_trunc(wr, 3)
    return hi, jnp.concatenate([hi, mid, lo], axis=0), rb


def _gather_rows(idx_ref, lo, hi, src_hbm, dst, sem):
    for r in range(lo, hi):
        pltpu.make_async_copy(src_hbm.at[pl.ds(idx_ref[0, 0, r], 1)], dst.at[pl.ds(r, 1)], sem).start()


def _wait_rows(n, src_hbm, dst, sem):
    pltpu.make_async_copy(src_hbm.at[pl.ds(0, n)], dst, sem).wait()


def _moe_experts_kernel(te_ref, tv_ref, src_ref, nxt_ref, x_hbm, wg_ref, wu_ref, wd_ref, y_ref, xbuf, sem, *, tm):
    i = pl.program_id(0)
    nt = pl.num_programs(0)
    slot = i % 2

    @pl.when(jnp.logical_and(i == 0, tv_ref[0] > 0))
    def _():
        _gather_rows(src_ref, 0, tm, x_hbm, xbuf.at[0], sem.at[0])

    @pl.when(tv_ref[i] > 0)
    def _():
        nxt = lambda q: _gather_rows(nxt_ref, q * tm // 4, (q + 1) * tm // 4, x_hbm, xbuf.at[1 - slot],
                                     sem.at[1 - slot])
        _wait_rows(tm, x_hbm, xbuf.at[slot], sem.at[slot])
        xb = xbuf[slot].astype(BF16)
        nxt(0)
        a = _dot(xb, wg_ref[0].astype(BF16))
        nxt(1)
        u = _dot(xb, wu_ref[0].astype(BF16))
        nxt(2)
        hid = (a / (1.0 + jnp.exp(-a))) * u
        y_ref[...] = _dot(hid.astype(BF16), wd_ref[0].astype(BF16))
        nxt(3)

    @pl.when(tv_ref[i] == 0)
    def _():
        y_ref[...] = jnp.zeros_like(y_ref)

    @pl.when(jnp.logical_and(tv_ref[i] == 0, tv_ref[jnp.maximum(i - 1, 0)] > 0))
    def _():
        _wait_rows(tm, x_hbm, xbuf.at[slot], sem.at[slot])

    @pl.when(jnp.logical_and(i == nt - 1, tv_ref[i] > 0))
    def _():
        _wait_rows(tm, x_hbm, xbuf.at[1 - slot], sem.at[1 - slot])


def _moe_experts(x, w_gate, w_up, w_down, tile_expert, tile_valid, src, tm):
    T, D = x.shape
    F = w_gate.shape[-1]
    NT = tile_expert.shape[0]
    grid_spec = pltpu.PrefetchScalarGridSpec(
        num_scalar_prefetch=2,
        grid=(NT,),
        in_specs=[pl.BlockSpec((1, 1, tm), lambda i, te, tv: (i, 0, 0), memory_space=pltpu.SMEM),
                  pl.BlockSpec((1, 1, tm), lambda i, te, tv: (i + 1, 0, 0), memory_space=pltpu.SMEM),
                  pl.BlockSpec(memory_space=pl.ANY),
                  pl.BlockSpec((1, D, F), lambda i, te, tv: (te[i], 0, 0)),
                  pl.BlockSpec((1, D, F), lambda i, te, tv: (te[i], 0, 0)),
                  pl.BlockSpec((1, F, D), lambda i, te, tv: (te[i], 0, 0))],
        out_specs=pl.BlockSpec((tm, D), lambda i, te, tv: (i, 0)),
        scratch_shapes=[pltpu.VMEM((2, tm, D), F32), pltpu.SemaphoreType.DMA((2,))],
    )
    return pl.pallas_call(
        functools.partial(_moe_experts_kernel, tm=tm),
        grid_spec=grid_spec,
        out_shape=jax.ShapeDtypeStruct((NT * tm, D), F32),
        compiler_params=_cparams("arbitrary"),
        name="moe_experts",
    )(tile_expert, tile_valid, src, src, x, w_gate, w_up, w_down)


def _moe_combine_kernel(pos_ref, nxt_ref, ys_hbm, h_ref, w_ref, g_ref, b_ref, oh_ref, ot_ref, buf, sem,
                        *, tm, alpha, head_tiles):
    i = pl.program_id(0)
    nt = pl.num_programs(0)
    slot = i % 2

    @pl.when(i == 0)
    def _():
        _gather_rows(pos_ref, 0, 2 * tm, ys_hbm, buf.at[0], sem.at[0])

    _wait_rows(2 * tm, ys_hbm, buf.at[slot], sem.at[slot])
    _gather_rows(nxt_ref, 0, tm, ys_hbm, buf.at[1 - slot], sem.at[1 - slot])
    w = w_ref[...]
    y = _layer_norm(alpha * h_ref[...] + (w[:, 0:1] * buf[slot, 0:tm] + w[:, 1:2] * buf[slot, tm:2 * tm]),
                    g_ref[...], b_ref[...])
    _gather_rows(nxt_ref, tm, 2 * tm, ys_hbm, buf.at[1 - slot], sem.at[1 - slot])

    @pl.when(i == nt - 1)
    def _():
        _wait_rows(2 * tm, ys_hbm, buf.at[1 - slot], sem.at[1 - slot])

    @pl.when(i < head_tiles)
    def _():
        oh_ref[...] = y

    @pl.when(i >= head_tiles)
    def _():
        ot_ref[...] = y


def _moe_combine(ys, h, pos, wts, g, b, alpha, tm, head_rows):
    T, D = h.shape
    nt = T // tm
    head_tiles = head_rows // tm
    const = lambda a: pl.BlockSpec(a.shape, lambda i: (0,) * a.ndim)
    return pl.pallas_call(
        functools.partial(_moe_combine_kernel, tm=tm, alpha=alpha, head_tiles=head_tiles),
        grid=(nt,),
        in_specs=[pl.BlockSpec((1, 1, 2 * tm), lambda i: (i, 0, 0), memory_space=pltpu.SMEM),
                  pl.BlockSpec((1, 1, 2 * tm), lambda i: (i + 1, 0, 0), memory_space=pltpu.SMEM),
                  pl.BlockSpec(memory_space=pl.ANY),
                  pl.BlockSpec((tm, D), lambda i: (i, 0)),
                  pl.BlockSpec((tm, 2), lambda i: (i, 0)), const(g), const(b)],
        out_specs=list(_two_part_specs(tm, D, head_tiles)),
        out_shape=[jax.ShapeDtypeStruct((head_rows, D), F32), jax.ShapeDtypeStruct((T - head_rows, D), F32)],
        scratch_shapes=[pltpu.VMEM((2, 2 * tm, D), F32), pltpu.SemaphoreType.DMA((2,))],
        compiler_params=_cparams("arbitrary"),
        name="moe_combine_ln",
    )(pos, pos, ys, h, wts, g, b)


def _route(ids, n_experts, tm):
    T = ids.shape[1]
    flat = ids.reshape(-1)
    iota = jnp.arange(2 * T, dtype=jnp.int32)
    sorted_e, order = lax.sort((flat, iota), num_keys=1, is_stable=True)
    _, inverse = lax.sort((order, iota), num_keys=1)
    experts = jnp.arange(n_experts, dtype=jnp.int32)
    counts = jnp.sum((flat[:, None] == experts[None, :]).astype(jnp.int32), axis=0)
    padded = (counts + tm - 1) // tm * tm
    ends = jnp.cumsum(padded)
    shift = (ends - padded) - (jnp.cumsum(counts) - counts)
    NT = (2 * T + n_experts * (tm - 1)) // tm
    tile_start = jnp.arange(NT, dtype=jnp.int32) * tm
    tile_expert = jnp.minimum(jnp.sum((tile_start[:, None] >= ends[None, :]).astype(jnp.int32), axis=1),
                              n_experts - 1)
    tile_valid = (tile_start < ends[-1]).astype(jnp.int32)
    pos = (inverse + shift[flat]).reshape(2, T)
    row = jnp.arange((NT + 1) * tm, dtype=jnp.int32)
    row_shift = jnp.repeat(shift[jnp.concatenate([tile_expert, tile_expert[-1:]])], tm)
    src = (order % T)[jnp.clip(row - row_shift, 0, 2 * T - 1)]
    return tile_expert, tile_valid, src.reshape(NT + 1, 1, tm), pos


MLA_PAIR_W = 2 * LANES
MLA_SAMPLE_CHUNK = 256
QK_NOPE = 64
QK_ROPE = 32


def _mla_proj_kernel(hh_ref, ht_ref, wdn_ref, gq_ref, gkv_ref, wq_ref, wqr_ref, wk_ref, wv_ref, cos_ref, sin_ref,
                     ckv_ref, kr_ref, qcat_ref, kcat_ref, vt_ref, *, q_lora, kv_lora, npairs, scale, head_tiles):
    h = jnp.where(pl.program_id(0) < head_tiles, hh_ref[...], ht_ref[...])
    z = _dot(h.astype(BF16), wdn_ref[...])
    cq = z[:, :q_lora]
    ckv = z[:, q_lora:q_lora + kv_lora]
    o = q_lora + kv_lora
    kr_raw = z[:, o:o + HEAD_W]
    kr_rot = z[:, o + HEAD_W:o + 2 * HEAD_W]
    cq = cq * lax.rsqrt(jnp.mean(cq * cq, axis=-1, keepdims=True) + RMS_EPS) * gq_ref[...]
    ckv = ckv * lax.rsqrt(jnp.mean(ckv * ckv, axis=-1, keepdims=True) + RMS_EPS) * gkv_ref[...]
    ckv_ref[...] = ckv
    cos = cos_ref[...]
    sin = sin_ref[...]
    kr_tile = kr_raw * cos + kr_rot * sin
    kr_ref[...] = kr_tile[:, QK_NOPE:QK_NOPE + QK_ROPE]
    cqb = cq.astype(BF16)
    ckb = ckv.astype(BF16)
    cos2 = jnp.concatenate([cos, cos], axis=1)
    sin2 = jnp.concatenate([sin, sin], axis=1)
    kr2 = jnp.concatenate([kr_tile, kr_tile], axis=1)
    for p in range(npairs):
        lanes = slice(p * MLA_PAIR_W, (p + 1) * MLA_PAIR_W)
        q = _dot(cqb, wq_ref[:, lanes]) * cos2 + _dot(cqb, wqr_ref[:, lanes]) * sin2
        qcat_ref[:, lanes] = (q * scale).astype(BF16)
        kcat_ref[:, lanes] = (_dot(ckb, wk_ref[:, lanes]) + kr2).astype(BF16)
    heads = 2 * npairs
    vt = _dot_nt(wv_ref[...], ckb).reshape(heads, HEAD_DIM, ckb.shape[0])
    vt_ref[0, :, :HEAD_DIM, :] = vt.astype(BF16)
    vt_ref[0, :, HEAD_DIM:, :] = jnp.ones((heads, VT_ROWS - HEAD_DIM, ckb.shape[0]), BF16)


def _mla_proj(h_head, h_tail, wdn, gq, gkv, wq, wqr, wk, wv, cos_t, sin_t, table_block, tm):
    D = h_head.shape[1]
    T = h_head.shape[0] + h_tail.shape[0]
    head_tiles = h_head.shape[0] // tm
    q_lora, kv_lora = gq.shape[1], gkv.shape[1]
    npairs = wq.shape[1] // MLA_PAIR_W
    const = lambda a: pl.BlockSpec(a.shape, lambda i: (0,) * a.ndim)
    row = lambda w_: pl.BlockSpec((tm, w_), lambda i: (i, 0))
    table = pl.BlockSpec((tm, HEAD_W), lambda i: (table_block(i), 0))
    return pl.pallas_call(
        functools.partial(_mla_proj_kernel, q_lora=q_lora, kv_lora=kv_lora, npairs=npairs,
                          scale=(QK_NOPE + QK_ROPE) ** -0.5 * LOG2E, head_tiles=head_tiles),
        grid=(T // tm,),
        in_specs=[*_two_part_specs(tm, D, head_tiles), const(wdn), const(gq), const(gkv), const(wq), const(wqr),
                  const(wk), const(wv), table, table],
        out_specs=[row(kv_lora), row(QK_ROPE), row(wq.shape[1]), row(wk.shape[1]),
                   pl.BlockSpec((1, 2 * npairs, VT_ROWS, tm), lambda i: (i, 0, 0, 0))],
        out_shape=[jax.ShapeDtypeStruct((T, kv_lora), F32), jax.ShapeDtypeStruct((T, QK_ROPE), F32),
                   jax.ShapeDtypeStruct((T, wq.shape[1]), BF16), jax.ShapeDtypeStruct((T, wk.shape[1]), BF16),
                   jax.ShapeDtypeStruct((T // tm, 2 * npairs, VT_ROWS, tm), BF16)],
        compiler_params=_cparams("parallel"),
        name="mla_proj",
    )(h_head, h_tail, wdn, gq, gkv, wq, wqr, wk, wv, cos_t, sin_t)


def _rot_half(w):
    half = w.shape[-1] // 2
    return jnp.concatenate([-w[..., half:], w[..., :half]], axis=-1)


def _prep_mla_weights(w_down, w_uq, w_ukv, heads, q_lora, kv_lora):
    D = w_down.shape[0]
    tail = HEAD_W - QK_NOPE - QK_ROPE
    w_kr = w_down[:, q_lora + kv_lora:]
    slot = lambda w: jnp.concatenate([jnp.zeros((D, QK_NOPE), w.dtype), w, jnp.zeros((D, tail), w.dtype)], axis=1)
    wdn = jnp.concatenate([w_down[:, :q_lora + kv_lora], slot(w_kr), slot(_rot_half(w_kr))], axis=1)
    wq3 = w_uq.reshape(q_lora, heads, QK_NOPE + QK_ROPE)
    nope, ropew = wq3[..., :QK_NOPE], wq3[..., QK_NOPE:]
    zpad = jnp.zeros((q_lora, heads, tail), w_uq.dtype)
    wq = jnp.concatenate([nope, ropew, zpad], axis=-1)
    wqr = jnp.concatenate([jnp.zeros_like(nope), _rot_half(ropew), zpad], axis=-1)
    wkv3 = w_ukv.reshape(kv_lora, heads, QK_NOPE + HEAD_DIM)
    w_uk, w_uv = wkv3[..., :QK_NOPE], wkv3[..., QK_NOPE:]
    wk = jnp.concatenate([w_uk, jnp.zeros((kv_lora, heads, HEAD_W - QK_NOPE), w_ukv.dtype)], axis=-1)
    wv = jnp.transpose(w_uv.reshape(kv_lora, heads * HEAD_DIM))
    b16 = lambda a: a.astype(BF16)
    return (b16(wdn), b16(wq.reshape(q_lora, -1)), b16(wqr.reshape(q_lora, -1)), b16(wk.reshape(kv_lora, -1)),
            b16(wv), b16(jnp.transpose(w_uk, (1, 2, 0))), b16(jnp.transpose(w_uv, (1, 0, 2))))


def _rope_tables(pos):
    half = QK_ROPE // 2
    inv_freq = ROPE_BASE ** (-jnp.arange(half, dtype=F32) / half)
    ang = pos.astype(F32)[:, None] * inv_freq[None, :]
    n = pos.shape[0]
    pad = jnp.zeros((n, HEAD_W - QK_NOPE - QK_ROPE), F32)
    cos = jnp.concatenate([jnp.ones((n, QK_NOPE), F32)] + [jnp.cos(ang)] * 2 + [pad], axis=1)
    sin = jnp.concatenate([jnp.zeros((n, QK_NOPE), F32)] + [jnp.sin(ang)] * 2 + [pad], axis=1)
    return cos, sin


def _mla_sample_queries(qcat, heads):
    DB, n, _ = qcat.shape
    q4 = qcat.reshape(DB, n, heads, HEAD_W)
    rows = lambda a: jnp.transpose(a, (0, 2, 1, 3)).reshape(DB, heads * n, a.shape[-1])
    return rows(q4[..., :QK_NOPE]), rows(q4[..., QK_NOPE:QK_NOPE + QK_ROPE])


def _mla_sample_kernel(qn_ref, qr_ref, wuk_ref, wuv_ref, cc_ref, rc_ref, cn_ref, rn_ref, o_ref,
                       qlat_ref, s_ref, m_ref, l_ref, acc_ref, *, n, heads):
    kt = pl.program_id(1)

    @pl.when(kt == 0)
    def _():
        for h in range(heads):
            rows = slice(h * n, (h + 1) * n)
            qlat_ref[rows, :] = _dot(qn_ref[0, rows, :], wuk_ref[h]).astype(BF16)
        m_ref[...] = jnp.full(m_ref.shape, NEG_INF, F32)
        l_ref[...] = jnp.zeros_like(l_ref)
        acc_ref[...] = jnp.zeros_like(acc_ref)

    def scores(rows, ckv, kr_t):
        return _dot_nt(qlat_ref[rows, :], ckv) + _dot(qr_ref[0, rows, :], kr_t)

    def absorb(rows, s, ckv):
        m, l, acc = _softmax_step(s[None], ckv, (m_ref[:, rows], l_ref[:, rows], acc_ref[:, rows]))
        m_ref[:, rows] = m
        l_ref[:, rows] = l
        acc_ref[:, rows] = acc

    def update(ckv, kr_t):
        absorb(slice(None), scores(slice(None), ckv, kr_t), ckv)

    ckv = cc_ref[0].astype(BF16)
    kr_t = rc_ref[0].astype(BF16)
    chunk_rows = s_ref.shape[1]
    chunks = [slice(c * chunk_rows, (c + 1) * chunk_rows) for c in range(s_ref.shape[0])]
    s_ref[0] = scores(chunks[0], ckv, kr_t)
    for c, rows in enumerate(chunks):
        if c + 1 < len(chunks):
            s_ref[c + 1] = scores(chunks[c + 1], ckv, kr_t)
        absorb(rows, s_ref[c], ckv)

    @pl.when(kt == pl.num_programs(1) - 1)
    def _():
        update(cn_ref[0], rn_ref[0])
        o_lat = (acc_ref[0] / l_ref[0]).astype(BF16)
        for h in range(heads):
            o_ref[0, :, h * HEAD_DIM:(h + 1) * HEAD_DIM] = _dot(o_lat[h * n:(h + 1) * n], wuv_ref[h]).astype(o_ref.dtype)


def _mla_sample(qn, qr, wuk, wuv, ckv_c, kr_c, ckv_n, kr_n, n, tk):
    DB, R, _ = qn.shape
    heads = R // n
    P, C = ckv_c.shape[1], ckv_c.shape[2]
    const = lambda a: pl.BlockSpec(a.shape, lambda b, j: (0,) * a.ndim)
    per_b = lambda a: pl.BlockSpec((1,) + a.shape[1:], lambda b, j: (b, 0, 0))
    return pl.pallas_call(
        functools.partial(_mla_sample_kernel, n=n, heads=heads),
        grid=(DB, P // tk),
        in_specs=[per_b(qn), per_b(qr), const(wuk), const(wuv),
                  pl.BlockSpec((1, tk, C), lambda b, j: (b, j, 0)),
                  pl.BlockSpec((1, QK_ROPE, tk), lambda b, j: (b, 0, j)),
                  per_b(ckv_n), per_b(kr_n)],
        out_specs=pl.BlockSpec((1, n, heads * HEAD_DIM), lambda b, j: (b, 0, 0)),
        out_shape=jax.ShapeDtypeStruct((DB, n, heads * HEAD_DIM), BF16),
        scratch_shapes=[pltpu.VMEM((R, C), BF16), pltpu.VMEM((R // MLA_SAMPLE_CHUNK, MLA_SAMPLE_CHUNK, tk), F32),
                        pltpu.VMEM((1, R, 1), F32), pltpu.VMEM((1, R, 1), F32), pltpu.VMEM((1, R, C), F32)],
        compiler_params=_cparams("parallel", "arbitrary"),
        name="mla_sample",
    )(qn, qr, wuk, wuv, ckv_c, kr_c, ckv_n, kr_n)


def _moe_layer(h, ids, wts, w_gate, w_up, w_down, layer, g, b, alpha, tm, head_rows):
    T, D = h.shape
    n_experts = w_gate.shape[1] * w_gate.shape[2]
    tile_expert, tile_valid, src, pos = _route(ids[:2], n_experts, tm)
    flat3 = lambda w: w.reshape((-1,) + w.shape[3:])
    ys = _moe_experts(h, flat3(w_gate), flat3(w_up), flat3(w_down), tile_expert + layer * n_experts, tile_valid,
                      src, tm)
    pos_t = jnp.transpose(pos.reshape(2, T // tm, tm), (1, 0, 2)).reshape(T // tm, 1, 2 * tm)
    pos_t = jnp.pad(pos_t, ((0, 1), (0, 0), (0, 0)))
    return _moe_combine(ys, h, pos_t, jnp.transpose(wts[:2]), g, b, alpha, tm, head_rows)


TOKEN_TILE = 256
FLASH_Q_TILE = 512
FLASH_KEY_TILE = 256
SB_Q_TILE = 256
SB_KEY_TILE = 128
CACHE_TILE = 1024
SB_SUB_TILE = 256


def kernel(x_prompt, x_sample, cache_fox_k, cache_fox_v, cache_fox_logf, cache_sb_k, cache_sb_v, cache_mla_ckv, cache_mla_krope, ab_w_in, ab_b_forget, ab_w_out, mla_w_down, mla_g_q, mla_g_kv, mla_w_uq, mla_w_ukv, mla_w_out, moe_w_group, moe_b_group, moe_w_router, moe_b_router, moe_w_gate, moe_w_up, moe_w_down, ln_g, ln_b):
    B, S, D = x_prompt.shape
    DB, n, _ = x_sample.shape
    P = cache_fox_k.shape[2]
    TP, TS = B * S, DB * n
    depth = ln_g.shape[0]
    n_groups = moe_w_group.shape[-1]
    assert depth == 2 and ab_w_in.shape[0] == 1 and mla_w_down.shape[0] == 1
    assert S % FLASH_Q_TILE == 0 and S % SB_Q_TILE == 0 and TP % TOKEN_TILE == 0 and TS % TOKEN_TILE == 0 and TOKEN_TILE % n == 0
    assert P % CACHE_TILE == 0 and P % CHUNK == 0 and n == CHUNK
    alpha = (2 * depth) ** 0.25
    tk = CACHE_TILE

    xp, xs = x_prompt.reshape(TP, D), x_sample.reshape(TS, D)
    sample3 = lambda a: a[TP:].reshape(DB, n, -1)

    def ffn(o_head_parts, o_tail, w_out, resid, layer):
        wrh, wrl, rb = _prep_router(moe_w_group[layer], moe_b_group[layer], moe_w_router[layer], moe_b_router[layer])
        h, ids, wts = _mix_out(o_head_parts, o_tail, w_out.astype(BF16), *resid, ln_g[layer, 0][None],
                               ln_b[layer, 0][None], wrh, wrl, rb, alpha, n_groups, TOKEN_TILE)
        return _moe_layer(h, ids, wts, moe_w_gate, moe_w_up, moe_w_down, layer,
                          ln_g[layer, 1][None], ln_b[layer, 1][None], alpha, TOKEN_TILE, TP)

    fox_heads = ab_b_forget.shape[1]
    hw = (ab_w_in.shape[2] - fox_heads) // 6
    w_ab, b_forget = _prep_ab_weights(ab_w_in[0], ab_b_forget[0])
    w_tok, w_t = _prompt_ab_weights(w_ab, hw, fox_heads)
    (qa_p, ka_p, qb_p, kb_p, lf_p, kat, vat, kbt, vbt, vat16, vbt16) = _ab_proj_prompt(
        xp, w_tok, b_forget, w_t, _query_decay_ones(fox_heads), B, S, fox_heads, SB_KEY_TILE)
    ka_p = _fox_insert_decay(ka_p, lf_p, B, S, fox_heads)
    qa, ka, va, qb, kb, vb, ka16, va16, kb16, vb16, lf = _ab_proj(xs, w_ab, b_forget, hw)
    dbn = lambda a: a.reshape(DB, n, -1)
    lf_s = dbn(lf)[:, :, :fox_heads]
    cq_s, ck_past, ck_new = _fox_sample_cum(lf_s, jnp.transpose(cache_fox_logf[0], (0, 2, 1)))
    cache_t = lambda c: jnp.transpose(c[0], (0, 2, 3, 1))
    hm = lambda a, t=False: _heads_major(dbn(a), fox_heads, t)
    o_fox_s = _fox_sample(hm(qa), hm(ka16, True), hm(va16, True), cache_t(cache_fox_k), cache_t(cache_fox_v),
                          cq_s, ck_past, ck_new, tk)
    o_sb_s = _sb_sample(hm(qb), hm(kb16, True), hm(vb16, True), cache_t(cache_sb_k), cache_t(cache_sb_v),
                        tk, SB_SUB_TILE)
    tokens_major = lambda a: jnp.transpose(a, (0, 2, 1, 3)).reshape(TS, hw)
    o_tail = jnp.concatenate([tokens_major(o_fox_s), tokens_major(o_sb_s)], axis=-1)
    o_fox_p = _flash_prompt(qa_p, ka_p, vat16, B, S, FLASH_Q_TILE, fox_heads, 1, "fox_prompt")
    o_sb_p = _sb_prompt(qb_p, kb_p, vbt16, B, S, SB_Q_TILE)
    xp, xs = ffn((o_fox_p, o_sb_p), o_tail, ab_w_out[0], (xp, xs), 0)

    q_lora, kv_lora = mla_g_q.shape[1], mla_g_kv.shape[1]
    heads = mla_w_uq.shape[2] // (QK_NOPE + QK_ROPE)
    wdn, wq, wqr, wk, wv, wuk_t, wuv = _prep_mla_weights(mla_w_down[0], mla_w_uq[0], mla_w_ukv[0], heads, q_lora, kv_lora)
    tm = TOKEN_TILE
    pos = jnp.concatenate([jnp.arange(S, dtype=jnp.int32), P + jnp.arange(tm, dtype=jnp.int32) % n])
    cos_t, sin_t = _rope_tables(pos)
    blocks_per_seq, prompt_blocks = S // tm, TP // tm
    table_block = lambda i: jnp.where(i < prompt_blocks, i % blocks_per_seq, blocks_per_seq)
    assert tm == FLASH_KEY_TILE
    ckv, kr, qcat, kcat, vt = _mla_proj(xp, xs, wdn, mla_g_q[0][None], mla_g_kv[0][None], wq, wqr, wk, wv,
                                        cos_t, sin_t, table_block, tm)
    qn, qr = _mla_sample_queries(sample3(qcat), heads)
    o_s = _mla_sample(qn, qr, wuk_t, wuv, cache_mla_ckv[0], jnp.transpose(cache_mla_krope[0], (0, 2, 1)),
                      sample3(ckv).astype(BF16), jnp.transpose(sample3(kr), (0, 2, 1)).astype(BF16), n, tk)
    o_p = _flash_prompt(qcat, kcat, vt, B, S, FLASH_Q_TILE, 8, CHUNK, "mla_prompt")
    xp, xs = ffn((o_p,), o_s.reshape(TS, -1), mla_w_out[0], (xp, xs), 1)

    rows_p = lambda a: jnp.transpose(a, (0, 3, 1, 2))[None]
    rows_s = lambda a: a.reshape(1, DB, n, fox_heads, hw // fox_heads)
    pr, sr = slice(0, TP), slice(TP, TP + TS)
    return (xp.reshape(B, S, D), xs.reshape(DB, n, D),
            rows_p(kat), rows_p(vat), lf_p[:, :fox_heads].reshape(1, B, S, fox_heads), rows_p(kbt), rows_p(vbt),
            ckv[pr].reshape(1, B, S, kv_lora), kr[pr].reshape(1, B, S, QK_ROPE),
            rows_s(ka), rows_s(va), lf[:, :fox_heads].reshape(1, DB, n, fox_heads), rows_s(kb), rows_s(vb),
            ckv[sr].reshape(1, DB, n, kv_lora), kr[sr].reshape(1, DB, n, QK_ROPE))
```

```python
import functools

import jax
import jax.numpy as jnp
from jax import lax
from jax.experimental import pallas as pl
from jax.experimental.pallas import tpu as pltpu

F32 = jnp.float32
BF16 = jnp.bfloat16
NEG_INF = -1e30
LOG2E = 1.4426950408889634

LANES = 128
HEAD_DIM = 64
PAIR_W = 2 * HEAD_DIM
CHUNK = 64
LN_EPS = 1e-5
RMS_EPS = 1e-6
ROPE_BASE = 10000.0
VMEM_LIMIT = 56 * 1024 * 1024


def _cparams(*sem):
    return pltpu.CompilerParams(dimension_semantics=sem, vmem_limit_bytes=VMEM_LIMIT)


def _dot(a, b):
    return jnp.dot(a, b, preferred_element_type=F32)


def _dot_nt(a, b):
    return lax.dot_general(a, b, (((1,), (1,)), ((), ())), preferred_element_type=F32)


def _split_bf16(x, parts):
    out = []
    r = x
    for _ in range(parts):
        h = r.astype(BF16)
        out.append(h)
        r = r - h.astype(F32)
    return out


def _split_bf16_trunc(x, parts):
    out = []
    r = x
    for _ in range(parts):
        bits = lax.bitcast_convert_type(r, jnp.uint32) & jnp.uint32(0xFFFF0000)
        h = lax.bitcast_convert_type(bits, F32)
        out.append(h.astype(BF16))
        r = r - h
    return out


def _log_sigmoid(x):
    return jnp.minimum(x, 0.0) - jnp.log(1.0 + jnp.exp(-jnp.abs(x)))


def _pick_tile(n, pref, mult=8):
    t = min(pref, n)
    while n % t or t % mult:
        t -= 1
    return t


def _ab_proj_kernel(x_ref, w_ref, bf_ref, qa_ref, ka_ref, va_ref, qb_ref, kb_ref, vb_ref,
                    ka16_ref, va16_ref, kb16_ref, vb16_ref, lf_ref, *, hw, qscale):
    xb = x_ref[...].astype(BF16)

    def seg(j):
        return _dot(xb, w_ref[:, j * hw:(j + 1) * hw])

    qa_ref[...] = (seg(0) * qscale).astype(BF16)
    z = seg(1)
    ka_ref[...] = z
    ka16_ref[...] = z.astype(BF16)
    z = seg(2)
    va_ref[...] = z
    va16_ref[...] = z.astype(BF16)
    qb_ref[...] = (seg(3) * qscale).astype(BF16)
    z = seg(4)
    kb_ref[...] = z
    kb16_ref[...] = z.astype(BF16)
    z = seg(5)
    vb_ref[...] = z
    vb16_ref[...] = z.astype(BF16)
    f = _dot(xb, w_ref[:, 6 * hw:6 * hw + LANES]) + bf_ref[...]
    lf_ref[...] = _log_sigmoid(f)


def _ab_proj(x, w, bf, hw):
    T, D = x.shape
    tm = _pick_tile(T, 256)
    row = lambda w_: pl.BlockSpec((tm, w_), lambda i: (i, 0))
    f32o = jax.ShapeDtypeStruct((T, hw), F32)
    b16o = jax.ShapeDtypeStruct((T, hw), BF16)
    return pl.pallas_call(
        functools.partial(_ab_proj_kernel, hw=hw, qscale=HEAD_DIM ** -0.5 * LOG2E),
        grid=(T // tm,),
        in_specs=[row(D), pl.BlockSpec(w.shape, lambda i: (0, 0)), pl.BlockSpec(bf.shape, lambda i: (0, 0))],
        out_specs=[row(hw)] * 10 + [row(LANES)],
        out_shape=[b16o, f32o, f32o, b16o, f32o, f32o, b16o, b16o, b16o, b16o,
                   jax.ShapeDtypeStruct((T, LANES), F32)],
        compiler_params=_cparams("parallel"),
        name="ab_proj",
    )(x, w, bf)


def _ab_proj_prompt_kernel(x_ref, w_ref, bf_ref, wt_ref, qone_ref, qa_ref, ka_ref, qb_ref, kb_ref, lf_ref,
                           kat_ref, vat_ref, kbt_ref, vbt_ref, vat16_ref, vbt16_ref, *, hw, heads, qscale, sb_tk):
    xb = x_ref[...].astype(BF16)
    tm = xb.shape[0]
    wide = heads * HEAD_W

    def seg(j):
        return _dot(xb, w_ref[:, j * wide:(j + 1) * wide])

    def seg_t(j):
        return _dot_nt(wt_ref[j * hw:(j + 1) * hw, :], xb).reshape(heads, hw // heads, tm)

    qa_ref[...] = (seg(0) * qscale + qone_ref[...]).astype(BF16)
    ka_ref[...] = seg(1).astype(BF16)
    qb_ref[...] = (seg(2) * qscale).astype(BF16)
    kb_ref[...] = seg(3).astype(BF16)
    lf_ref[...] = _log_sigmoid(_dot(xb, w_ref[:, 4 * wide:4 * wide + LANES]) + bf_ref[...])
    kat_ref[0] = seg_t(0)
    z = seg_t(1)
    vat_ref[0] = z
    vat16_ref[0, :, :HEAD_DIM, :] = z.astype(BF16)
    vat16_ref[0, :, HEAD_DIM:, :] = jnp.ones((heads, VT_ROWS - HEAD_DIM, tm), BF16)
    kbt_ref[0] = seg_t(2)
    z = seg_t(3)
    vbt_ref[0] = z
    for c in range(tm // sb_tk):
        vbt16_ref[c] = z[:, :, c * sb_tk:(c + 1) * sb_tk].astype(BF16)


def _ab_proj_prompt(x, w_tok, bf, w_t, q_ones, B, S, heads, sb_tk):
    TP, D = x.shape
    hw = w_t.shape[0] // 4
    wide = heads * HEAD_W
    tm = FLASH_KEY_TILE
    nj = S // tm
    const = lambda a: pl.BlockSpec(a.shape, lambda b, j: (0,) * a.ndim)
    row = lambda w_: pl.BlockSpec((tm, w_), lambda b, j: (b * nj + j, 0))
    t_spec = pl.BlockSpec((1, heads, hw // heads, tm), lambda b, j: (b, 0, 0, j))
    b16 = jax.ShapeDtypeStruct((TP, wide), BF16)
    t32 = jax.ShapeDtypeStruct((B, heads, hw // heads, S), F32)
    return pl.pallas_call(
        functools.partial(_ab_proj_prompt_kernel, hw=hw, heads=heads, qscale=HEAD_DIM ** -0.5 * LOG2E, sb_tk=sb_tk),
        grid=(B, nj),
        in_specs=[row(D), const(w_tok), const(bf), const(w_t), const(q_ones)],
        out_specs=[row(wide)] * 4 + [row(LANES)] + [t_spec] * 4 + [
            pl.BlockSpec((1, heads, VT_ROWS, tm), lambda b, j: (b * nj + j, 0, 0, 0)),
            pl.BlockSpec((tm // sb_tk, heads, hw // heads, sb_tk), lambda b, j: (b * nj + j, 0, 0, 0))],
        out_shape=[b16] * 4 + [jax.ShapeDtypeStruct((TP, LANES), F32)] + [t32] * 4 + [
            jax.ShapeDtypeStruct((TP // tm, heads, VT_ROWS, tm), BF16),
            jax.ShapeDtypeStruct((TP // sb_tk, heads, hw // heads, sb_tk), BF16)],
        compiler_params=_cparams("parallel", "parallel"),
        name="ab_proj_prompt",
    )(x, w_tok, bf, w_t, q_ones)


def _cumsum_kernel(x_ref, o_ref, carry_ref, *, tl):
    @pl.when(pl.program_id(0) == 0)
    def _():
        carry_ref[...] = jnp.zeros_like(carry_ref)

    x = x_ref[...]
    rows = x.shape[0]
    r = lax.broadcasted_iota(jnp.int32, (tl, tl), 0)
    c = lax.broadcasted_iota(jnp.int32, (tl, tl), 1)
    upper = (r <= c).astype(BF16)
    parts = jnp.concatenate(_split_bf16(x, 4), axis=0)
    y = _dot(parts, upper)
    cum = (y[0:rows] + y[rows:2 * rows]) + (y[2 * rows:3 * rows] + y[3 * rows:]) + carry_ref[:, 0:1]
    o_ref[...] = cum
    carry_ref[...] = jnp.broadcast_to(cum[:, tl - 1:tl], carry_ref.shape)


def _cumsum_rows(x):
    B, H, L = x.shape
    tl = _pick_tile(L, 512, LANES)
    out = pl.pallas_call(
        functools.partial(_cumsum_kernel, tl=tl),
        grid=(L // tl,),
        in_specs=[pl.BlockSpec((B * H, tl), lambda j: (0, j))],
        out_specs=pl.BlockSpec((B * H, tl), lambda j: (0, j)),
        out_shape=jax.ShapeDtypeStruct((B * H, L), F32),
        scratch_shapes=[pltpu.VMEM((B * H, LANES), F32)],
        compiler_params=_cparams("arbitrary"),
        name="cumsum_rows",
    )(x.reshape(B * H, L))
    return out.reshape(B, H, L)


def _stack_pair(qp):
    lo = lax.broadcasted_iota(jnp.int32, qp.shape, 1) < HEAD_DIM
    zero = jnp.zeros_like(qp)
    return jnp.concatenate([jnp.where(lo, qp, zero), jnp.where(lo, zero, qp)], axis=0)


def _unstack_pair(o):
    lo = lax.broadcasted_iota(jnp.int32, o.shape[1:], 1) < HEAD_DIM
    return jnp.where(lo, o[0], o[1])


def _softmax_step(s, vb, carry):
    m, l, acc = carry
    two, tq, tk = s.shape
    m_new = jnp.maximum(m, jnp.max(s, axis=-1, keepdims=True))
    alpha = jnp.exp2(m - m_new)
    p = jnp.exp2(s - m_new)
    l = alpha * l + jnp.sum(p, axis=-1, keepdims=True)
    pv = _dot(p.reshape(two * tq, tk).astype(BF16), vb).reshape(two, tq, vb.shape[-1])
    return m_new, l, alpha * acc + pv


def _log2_sigmoid_pair(z2):
    l1 = jnp.log2(1.0 + jnp.exp2(-jnp.abs(z2)))
    return jnp.minimum(z2, 0.0) - l1, jnp.minimum(-z2, 0.0) - l1


def _softmax_init(tq, width):
    return (jnp.full((2, tq, 1), NEG_INF, F32), jnp.zeros((2, tq, 1), F32), jnp.zeros((2, tq, width), F32))


def _sb_step(z, vb, strict_upper, carry, mask=None):
    run, acc = carry
    log_beta, log_rest = _log2_sigmoid_pair(z)
    if mask is not None:
        log_beta = jnp.where(mask, log_beta, NEG_INF)
        log_rest = jnp.where(mask, log_rest, 0.0)
    hi, lo = _split_bf16(log_rest, 2)
    later = _dot(hi, strict_upper) + _dot(lo, strict_upper)
    a = jnp.exp2(log_beta + later + run)
    acc = acc + _dot(a.astype(BF16), vb)
    run = run + jnp.sum(log_rest, axis=-1, keepdims=True)
    return run, acc


def _strict_upper(tk):
    r = lax.broadcasted_iota(jnp.int32, (tk, tk), 0)
    c = lax.broadcasted_iota(jnp.int32, (tk, tk), 1)
    return (r > c).astype(BF16)


HEAD_W = LANES
VT_ROWS = HEAD_DIM + 16
SB_DEAD_LOG2 = -160.0


def _pipeline_ahead(stage, first, count, cur, nxt):
    if first < count:
        stage(first, cur)
    elif nxt is not None:
        stage(first - count, nxt)


def _flash_prompt_kernel(q_ref, k_ref, vt_ref, o_ref, s_ref, m_ref, acc_ref, *, tq, tk, heads, chunk):
    i = pl.program_id(2)
    sub = tq // tk
    key = lax.broadcasted_iota(jnp.int32, (tk, tq), 0)
    query = lax.broadcasted_iota(jnp.int32, (tk, tq), 1)
    m_ref[...] = jnp.full(m_ref.shape, NEG_INF, F32)
    acc_ref[...] = jnp.zeros_like(acc_ref)

    def scores(h, kt):
        rows = pl.ds(pl.multiple_of(kt * tk, tk), tk)
        lanes = slice(h * HEAD_W, (h + 1) * HEAD_W)
        s_ref[h] = _dot_nt(k_ref[rows, lanes], q_ref[:, lanes])

    def absorb(h, kt, visible):
        s_t = s_ref[h]
        if visible is not None:
            s_t = jnp.where(visible, s_t, NEG_INF)
        m = m_ref[h]
        m_new = jnp.maximum(m, jnp.max(s_t, axis=0, keepdims=True))
        p_t = jnp.exp2(s_t - m_new).astype(BF16)
        pv = _dot(vt_ref[kt, h], p_t)
        acc_ref[h] = jnp.exp2(m - m_new) * acc_ref[h] + pv
        m_ref[h] = m_new

    scores(0, 0)
    scores(1, 0)

    def body(kt, carry):
        for h in range(heads):
            _pipeline_ahead(scores, h + 2, heads, kt, kt + 1)
            absorb(h, kt, None)
        return carry

    lax.fori_loop(0, sub * i, body, 0)
    for s in range(sub):
        kt = sub * i + s
        visible = ((s * tk + key) // chunk) <= (query // chunk)
        for h in range(heads):
            _pipeline_ahead(scores, h + 2, heads, kt, kt + 1 if s + 1 < sub else None)
            absorb(h, kt, visible)
    for j in range(heads // 2):
        a0, a1 = acc_ref[2 * j], acc_ref[2 * j + 1]
        o_t = jnp.concatenate([a0[:HEAD_DIM] / a0[HEAD_DIM:HEAD_DIM + 1],
                               a1[:HEAD_DIM] / a1[HEAD_DIM:HEAD_DIM + 1]], axis=0)
        o_ref[:, j * PAIR_W:(j + 1) * PAIR_W] = jnp.transpose(o_t).astype(o_ref.dtype)


def _flash_prompt(qx, kx, vt4, B, S, tq, heads_per_step, chunk, name):
    W = qx.shape[1]
    nq = S // tq
    hs = heads_per_step
    tk = vt4.shape[3]
    nk = S // tk
    return pl.pallas_call(
        functools.partial(_flash_prompt_kernel, tq=tq, tk=tk, heads=hs, chunk=chunk),
        grid=(B, W // (HEAD_W * hs), S // tq),
        in_specs=[pl.BlockSpec((tq, HEAD_W * hs), lambda b, g, i: (b * nq + i, g)),
                  pl.BlockSpec((S, HEAD_W * hs), lambda b, g, i: (b, g)),
                  pl.BlockSpec((nk, hs, VT_ROWS, tk), lambda b, g, i: (b, g, 0, 0))],
        out_specs=pl.BlockSpec((tq, HEAD_DIM * hs), lambda b, g, i: (b * nq + i, g)),
        out_shape=jax.ShapeDtypeStruct((B * S, W // HEAD_W * HEAD_DIM), BF16),
        scratch_shapes=[pltpu.VMEM((hs, tk, tq), F32), pltpu.VMEM((hs, 1, tq), F32),
                        pltpu.VMEM((hs, VT_ROWS, tq), F32)],
        compiler_params=_cparams("parallel", "parallel", "arbitrary"),
        name=name,
    )(qx, kx, vt4)


def _sb_prompt_kernel(q_ref, k_ref, vt_ref, o_ref, z_ref, lw_ref, tot_ref, run_ref, acc_ref, *, tq, tk, heads):
    i = pl.program_id(1)
    sub = tq // tk
    key = lax.broadcasted_iota(jnp.int32, (tk, tq), 0)
    query = lax.broadcasted_iota(jnp.int32, (tk, tq), 1)
    r = lax.broadcasted_iota(jnp.int32, (tk, tk), 0)
    c = lax.broadcasted_iota(jnp.int32, (tk, tk), 1)
    after = (c > r).astype(BF16)
    run_ref[...] = jnp.zeros_like(run_ref)
    acc_ref[...] = jnp.zeros_like(acc_ref)

    def logits(h, kt):
        rows = pl.ds(pl.multiple_of(kt * tk, tk), tk)
        lanes = slice(h * HEAD_W, (h + 1) * HEAD_W)
        z_ref[h] = _dot_nt(k_ref[rows, lanes], q_ref[:, lanes])

    def log_weights(before, h, kt):
        log_beta, log_rest = _log2_sigmoid_pair(z_ref[h])
        if before is not None:
            log_beta = jnp.where(before, log_beta, NEG_INF)
            log_rest = jnp.where(before, log_rest, 0.0)
        hi, lo = _split_bf16(log_rest, 2)
        later = _dot(after, hi) + _dot(after, lo)
        lw_ref[h] = log_beta + later
        tot_ref[h] = later[0:1] + log_rest[0:1]

    def accumulate(h, kt):
        run = run_ref[h]
        a_t = jnp.exp2(lw_ref[h] + run).astype(BF16)
        acc_ref[h] += _dot(vt_ref[kt, h], a_t)
        run_ref[h] = run + tot_ref[h]

    def step(kt, masked, nxt, nxt_masked):
        for h in range(heads):
            _pipeline_ahead(logits, h + 2, heads, kt, nxt)
            if h + 1 < heads:
                log_weights(masked, h + 1, kt)
            elif nxt is not None:
                log_weights(nxt_masked, 0, nxt)
            accumulate(h, kt)

    unmasked = sub * i
    first = unmasked + sub - 1
    masks = [((sub - 1 - s) * tk + key) < query for s in range(sub)]
    logits(0, first)
    logits(1, first)
    log_weights(masks[0], 0, first)
    for s in range(sub):
        kt = first - s
        if s + 1 < sub:
            step(kt, masks[s], kt - 1, masks[s + 1])
        else:
            step(kt, masks[s], jnp.maximum(kt - 1, 0), None)

    def alive():
        return (jnp.max(run_ref[...]) > SB_DEAD_LOG2).astype(jnp.int32)

    def body(carry):
        kt, _ = carry
        step(kt, None, jnp.maximum(kt - 1, 0), None)
        return kt - 1, alive()

    lax.while_loop(lambda c: jnp.logical_and(c[0] >= 0, c[1] > 0), body, (unmasked - 1, alive()))

    for j in range(heads // 2):
        o_t = jnp.concatenate([acc_ref[2 * j], acc_ref[2 * j + 1]], axis=0)
        o_ref[:, j * PAIR_W:(j + 1) * PAIR_W] = jnp.transpose(o_t).astype(o_ref.dtype)


def _sb_prompt(qx, kx, vt4, B, S, tq):
    W = qx.shape[1]
    heads = W // HEAD_W
    tk = vt4.shape[3]
    nq = S // tq
    return pl.pallas_call(
        functools.partial(_sb_prompt_kernel, tq=tq, tk=tk, heads=heads),
        grid=(B, nq),
        in_specs=[pl.BlockSpec((tq, W), lambda b, i: (b * nq + i, 0)),
                  pl.BlockSpec((S, W), lambda b, i: (b, 0)),
                  pl.BlockSpec((S // tk, heads, HEAD_DIM, tk), lambda b, i: (b, 0, 0, 0))],
        out_specs=pl.BlockSpec((tq, heads * HEAD_DIM), lambda b, i: (b * nq + i, 0)),
        out_shape=jax.ShapeDtypeStruct((B * S, heads * HEAD_DIM), BF16),
        scratch_shapes=[pltpu.VMEM((heads, tk, tq), F32), pltpu.VMEM((heads, tk, tq), F32),
                        pltpu.VMEM((heads, 1, tq), F32), pltpu.VMEM((heads, 1, tq), F32),
                        pltpu.VMEM((heads, HEAD_DIM, tq), F32)],
        compiler_params=_cparams("parallel", "arbitrary"),
        name="sb_prompt",
    )(qx, kx, vt4)


def _prep_ab_weights(w_in, b_f):
    D = w_in.shape[0]
    H = b_f.shape[0]
    hw = (w_in.shape[1] - H) // 6
    main = jnp.concatenate([w_in[:, :3 * hw], w_in[:, 3 * hw + H:]], axis=1)
    wf = jnp.zeros((D, LANES), w_in.dtype).at[:, :H].set(w_in[:, 3 * hw:3 * hw + H])
    bf = jnp.zeros((1, LANES), F32).at[0, :H].set(b_f.astype(F32))
    return jnp.concatenate([main, wf], axis=1).astype(BF16), bf


def _prompt_ab_weights(w_ab, hw, heads):
    D = w_ab.shape[0]
    seg = lambda j: w_ab[:, j * hw:(j + 1) * hw]
    pad = lambda w: jnp.concatenate([w.reshape(D, heads, hw // heads),
                                     jnp.zeros((D, heads, HEAD_W - hw // heads), w.dtype)], axis=-1).reshape(D, -1)
    w_tok = jnp.concatenate([pad(seg(0)), pad(seg(1)), pad(seg(3)), pad(seg(4)), w_ab[:, 6 * hw:]], axis=1)
    w_t = jnp.transpose(jnp.concatenate([seg(1), seg(2), seg(4), seg(5)], axis=1))
    return w_tok, w_t


DECAY_TERMS = 3


def _insert_decay_kernel(k_ref, c_ref, sel_ref, o_ref):
    terms = sum(_dot(part, sel_ref[j]) for j, part in enumerate(_split_bf16(c_ref[...], DECAY_TERMS)))
    o_ref[...] = (k_ref[...].astype(F32) + terms).astype(BF16)


def _fox_insert_decay(kx, lf, B, S, H):
    lt = jnp.transpose(lf[:, :H].reshape(B, S, H), (0, 2, 1))
    cum = jnp.transpose(_cumsum_rows(lt), (0, 2, 1)).reshape(B * S, H) * (-LOG2E)
    c = jnp.pad(cum, ((0, 0), (0, LANES - H)))
    src = jnp.arange(LANES)[None, :, None]
    dst = jnp.arange(H * HEAD_W)[None, None, :]
    term = jnp.arange(DECAY_TERMS)[:, None, None]
    sel = jnp.logical_and(src < H, dst == src * HEAD_W + HEAD_DIM + term).astype(BF16)
    tm = _pick_tile(B * S, 512)
    return pl.pallas_call(
        _insert_decay_kernel,
        grid=(B * S // tm,),
        in_specs=[pl.BlockSpec((tm, H * HEAD_W), lambda i: (i, 0)), pl.BlockSpec((tm, LANES), lambda i: (i, 0)),
                  pl.BlockSpec(sel.shape, lambda i: (0, 0, 0))],
        out_specs=pl.BlockSpec((tm, H * HEAD_W), lambda i: (i, 0)),
        out_shape=jax.ShapeDtypeStruct(kx.shape, BF16),
        input_output_aliases={0: 0},
        compiler_params=_cparams("parallel"),
        name="fox_insert_decay",
    )(kx, c, sel)


def _query_decay_ones(H):
    lane = jnp.arange(H * HEAD_W) % HEAD_W
    return jnp.logical_and(lane >= HEAD_DIM, lane < HEAD_DIM + DECAY_TERMS).astype(F32)[None]


def _fox_sample_kernel(q_ref, knt_ref, vnt_ref, kct_ref, vct_ref, cq_ref, ckp_ref, ckn_ref, o_ref,
                       s_ref, m_ref, l_ref, acc_ref, *, n, heads):
    kt = pl.program_id(1)

    @pl.when(kt == 0)
    def _():
        m_ref[...] = jnp.full(m_ref.shape, NEG_INF, F32)
        l_ref[...] = jnp.zeros_like(l_ref)
        acc_ref[...] = jnp.zeros_like(acc_ref)

    def absorb(h, s, v_t, ck, mask):
        s = s + (cq_ref[0, h] - ck)
        if mask is not None:
            s = jnp.where(mask, s, NEG_INF)
        m = m_ref[h]
        m_new = jnp.maximum(m, jnp.max(s, axis=-1, keepdims=True))
        alpha = jnp.exp2(m - m_new)
        p = jnp.exp2(s - m_new)
        l_ref[h] = alpha * l_ref[h] + jnp.sum(p, axis=-1, keepdims=True)
        acc_ref[h] = alpha * acc_ref[h] + _dot_nt(p.astype(BF16), v_t)
        m_ref[h] = m_new

    def scores(h, _=None):
        s_ref[h] = _dot(q_ref[0, h], kct_ref[0, h].astype(BF16))

    scores(0)
    scores(1)
    for h in range(heads):
        _pipeline_ahead(scores, h + 2, heads, None, None)
        absorb(h, s_ref[h], vct_ref[0, h].astype(BF16), ckp_ref[0, h:h + 1, :], None)

    @pl.when(kt == pl.num_programs(1) - 1)
    def _():
        row = lax.broadcasted_iota(jnp.int32, (n, n), 0)
        col = lax.broadcasted_iota(jnp.int32, (n, n), 1)
        for h in range(heads):
            absorb(h, _dot(q_ref[0, h], knt_ref[0, h]), vnt_ref[0, h], ckn_ref[0, h:h + 1, :], col <= row)
            o_ref[0, h] = (acc_ref[h] / l_ref[h]).astype(o_ref.dtype)


def _fox_sample(q, knt, vnt, kct, vct, cq, ckp, ckn, tk):
    DB, H, n, dh = q.shape
    P = kct.shape[-1]
    per_b = lambda a: pl.BlockSpec((1,) + a.shape[1:], lambda b, j: (b,) + (0,) * (a.ndim - 1))
    cache = pl.BlockSpec((1, H, dh, tk), lambda b, j: (b, 0, 0, j))
    return pl.pallas_call(
        functools.partial(_fox_sample_kernel, n=n, heads=H),
        grid=(DB, P // tk),
        in_specs=[per_b(q), per_b(knt), per_b(vnt), cache, cache, per_b(cq),
                  pl.BlockSpec((1, H, tk), lambda b, j: (b, 0, j)), per_b(ckn)],
        out_specs=per_b(q),
        out_shape=jax.ShapeDtypeStruct(q.shape, BF16),
        scratch_shapes=[pltpu.VMEM((H, n, tk), F32), pltpu.VMEM((H, n, 1), F32), pltpu.VMEM((H, n, 1), F32),
                        pltpu.VMEM((H, n, dh), F32)],
        compiler_params=_cparams("parallel", "arbitrary"),
        name="fox_sample",
    )(q, knt, vnt, kct, vct, cq, ckp, ckn)


def _sb_sample_kernel(q_ref, knt_ref, vnt_ref, kct_ref, vct_ref, o_ref, z_ref, run_ref, acc_ref, *, n, heads, sub):
    kt = pl.program_id(1)
    upper = _strict_upper(sub)

    def absorb(h, z, v_t, upper_m, width, mask=None):
        log_beta, log_rest = _log2_sigmoid_pair(z)
        if mask is not None:
            log_beta = jnp.where(mask, log_beta, NEG_INF)
            log_rest = jnp.where(mask, log_rest, 0.0)
        hi, lo = _split_bf16(log_rest, 2)
        run = run_ref[h]
        parts = []
        for c in reversed(range(z.shape[1] // width)):
            keys = slice(c * width, (c + 1) * width)
            later = _dot(hi[:, keys], upper_m) + _dot(lo[:, keys], upper_m)
            parts.append(jnp.exp2(log_beta[:, keys] + later + run).astype(BF16))
            run = run + jnp.sum(log_rest[:, keys], axis=-1, keepdims=True)
        a = parts[0] if len(parts) == 1 else jnp.concatenate(parts[::-1], axis=1)
        acc_ref[h] += _dot_nt(a, v_t)
        run_ref[h] = run

    def logits(h, _=None):
        z_ref[h] = _dot(q_ref[0, h], kct_ref[0, h].astype(BF16))

    @pl.when(kt == 0)
    def _():
        row = lax.broadcasted_iota(jnp.int32, (n, n), 0)
        col = lax.broadcasted_iota(jnp.int32, (n, n), 1)
        upper_n = _strict_upper(n)
        run_ref[...] = jnp.zeros_like(run_ref)
        acc_ref[...] = jnp.zeros_like(acc_ref)
        for h in range(heads):
            absorb(h, _dot(q_ref[0, h], knt_ref[0, h]), vnt_ref[0, h], upper_n, n, col < row)

    @pl.when(jnp.max(run_ref[...]) > SB_DEAD_LOG2)
    def _():
        logits(0)
        logits(1)
        for h in range(heads):
            _pipeline_ahead(logits, h + 2, heads, None, None)
            absorb(h, z_ref[h], vct_ref[0, h].astype(BF16), upper, sub)

    @pl.when(kt == pl.num_programs(1) - 1)
    def _():
        o_ref[0] = acc_ref[...].astype(o_ref.dtype)


def _sb_sample(q, knt, vnt, kct, vct, tk, sub):
    DB, H, n, dh = q.shape
    P = kct.shape[-1]
    nk = P // tk
    per_b = lambda a: pl.BlockSpec((1,) + a.shape[1:], lambda b, j: (b,) + (0,) * (a.ndim - 1))
    cache = pl.BlockSpec((1, H, dh, tk), lambda b, j: (b, 0, 0, nk - 1 - j))
    return pl.pallas_call(
        functools.partial(_sb_sample_kernel, n=n, heads=H, sub=sub),
        grid=(DB, nk),
        in_specs=[per_b(q), per_b(knt), per_b(vnt), cache, cache],
        out_specs=per_b(q),
        out_shape=jax.ShapeDtypeStruct(q.shape, BF16),
        scratch_shapes=[pltpu.VMEM((H, n, tk), F32), pltpu.VMEM((H, n, 1), F32), pltpu.VMEM((H, n, dh), F32)],
        compiler_params=_cparams("parallel", "arbitrary"),
        name="sb_sample",
    )(q, knt, vnt, kct, vct)


def _fox_sample_cum(lf_new, lf_past_t):
    DB, H, P = lf_past_t.shape
    n = lf_new.shape[1]
    L = -(-(P + n) // LANES) * LANES
    both = jnp.concatenate([lf_past_t.astype(F32), jnp.transpose(lf_new, (0, 2, 1)),
                            jnp.zeros((DB, H, L - P - n), F32)], axis=2)
    cum_t = _cumsum_rows(both) * LOG2E
    ckn = cum_t[:, :, P:P + n]
    return ckn[..., None], cum_t[:, :, :P], ckn


def _heads_major(a, heads, transpose_rows):
    DB, n, _ = a.shape
    a4 = a.reshape(DB, n, heads, -1)
    return jnp.transpose(a4, (0, 2, 3, 1) if transpose_rows else (0, 2, 1, 3))


ROUTER_ROWS = 48


def _layer_norm(y, g, b):
    mu = jnp.mean(y, axis=-1, keepdims=True)
    yc = y - mu
    var = jnp.mean(yc * yc, axis=-1, keepdims=True)
    return yc * lax.rsqrt(var + LN_EPS) * g + b


def _first_argmax(v, ridx):
    vmax = jnp.max(v, axis=0, keepdims=True)
    idx = jnp.min(jnp.where(v == vmax, ridx, v.shape[0]), axis=0, keepdims=True)
    return vmax, idx


def _two_part_specs(tm, width, head_tiles):
    return (pl.BlockSpec((tm, width), lambda i: (jnp.minimum(i, head_tiles - 1), 0)),
            pl.BlockSpec((tm, width), lambda i: (jnp.maximum(i - head_tiles, 0), 0)))


def _mix_out_kernel(*refs, alpha, n_groups, head_tiles, n_parts):
    o_parts, ot_ref = refs[:n_parts], refs[n_parts]
    w_ref, xh_ref, xt_ref, g_ref, b_ref, wrh_ref, wrl_ref, rb_ref, h_ref, ids_ref, wts_ref = refs[n_parts + 1:]
    in_head = pl.program_id(0) < head_tiles
    x = jnp.where(in_head, xh_ref[...], xt_ref[...])
    o_head = o_parts[0][...] if n_parts == 1 else jnp.concatenate([r[...] for r in o_parts], axis=1)
    o = jnp.where(in_head, o_head, ot_ref[...])
    h = _layer_norm(alpha * x + _dot(o, w_ref[...]), g_ref[...], b_ref[...])
    h_ref[...] = h
    hh, hl = _split_bf16(h, 2)
    R = ROUTER_ROWS
    wr = _dot_nt(wrl_ref[...], hh)
    lg = wr[:R] + (_dot_nt(wrh_ref[...], hl) + (wr[R:2 * R] + wr[2 * R:])) + rb_ref[...]
    tm = lg.shape[1]
    ridx = lax.broadcasted_iota(jnp.int32, (8, tm), 0)
    g = jnp.where(ridx < n_groups, lg[0:8], NEG_INF)
    gmax, gidx = _first_argmax(g, ridx)
    gate = 1.0 / jnp.sum(jnp.exp(g - gmax), axis=0, keepdims=True)
    esel = lg[8:16]
    for gg in range(1, n_groups):
        esel = jnp.where(gidx == gg, lg[8 + 8 * gg:16 + 8 * gg], esel)
    v1, i1 = _first_argmax(esel, ridx)
    v2, i2 = _first_argmax(jnp.where(ridx == i1, NEG_INF, esel), ridx)
    t = jnp.exp(v2 - v1)
    w1 = 1.0 / (1.0 + t)
    ids_ref[...] = jnp.where(ridx == 0, gidx * 8 + i1, jnp.where(ridx == 1, gidx * 8 + i2, 0))
    wts_ref[...] = jnp.where(ridx == 0, gate * w1, jnp.where(ridx == 1, gate * (t * w1), 0.0))


def _mix_out(o_head_parts, o_tail, w, x_head, x_tail, g, b, wrh, wrl, rb, alpha, n_groups, tm):
    D = x_head.shape[1]
    T = x_head.shape[0] + x_tail.shape[0]
    head_tiles = x_head.shape[0] // tm
    const = lambda a: pl.BlockSpec(a.shape, lambda i: (0,) * a.ndim)
    rb_t = jnp.broadcast_to(rb, (ROUTER_ROWS, tm))
    head_spec = lambda width: _two_part_specs(tm, width, head_tiles)[0]
    return pl.pallas_call(
        functools.partial(_mix_out_kernel, alpha=alpha, n_groups=n_groups, head_tiles=head_tiles,
                          n_parts=len(o_head_parts)),
        grid=(T // tm,),
        in_specs=[*[head_spec(p.shape[1]) for p in o_head_parts], _two_part_specs(tm, o_tail.shape[1], head_tiles)[1],
                  const(w), *_two_part_specs(tm, D, head_tiles), const(g), const(b), const(wrh), const(wrl),
                  const(rb_t)],
        out_specs=[pl.BlockSpec((tm, D), lambda i: (i, 0)), pl.BlockSpec((8, tm), lambda i: (0, i)),
                   pl.BlockSpec((8, tm), lambda i: (0, i))],
        out_shape=[jax.ShapeDtypeStruct((T, D), F32), jax.ShapeDtypeStruct((8, T), jnp.int32),
                   jax.ShapeDtypeStruct((8, T), F32)],
        compiler_params=_cparams("parallel"),
        name="mix_out_ln_router",
    )(*o_head_parts, o_tail, w, x_head, x_tail, g, b, wrh, wrl, rb_t)


def _prep_router(w_group, b_group, w_router, b_router):
    D, G = w_group.shape
    E = w_router.shape[-1]
    wr = jnp.zeros((ROUTER_ROWS, D), F32)
    wr = wr.at[:G].set(w_group.T.astype(F32))
    wr = wr.at[8:8 + G * E].set(jnp.transpose(w_router, (0, 2, 1)).reshape(G * E, D).astype(F32))
    rb = jnp.zeros((ROUTER_ROWS, 1), F32)
    rb = rb.at[:G, 0].set(b_group.astype(F32)).at[8:8 + G * E, 0].set(b_router.reshape(-1).astype(F32))
    hi, mid, lo = _split_bf16_trunc(wr, 3)
    return hi, jnp.concatenate([hi, mid, lo], axis=0), rb


def _gather_rows(idx_ref, lo, hi, src_hbm, dst, sem):
    for r in range(lo, hi):
        pltpu.make_async_copy(src_hbm.at[pl.ds(idx_ref[0, 0, r], 1)], dst.at[pl.ds(r, 1)], sem).start()


def _wait_rows(n, src_hbm, dst, sem):
    pltpu.make_async_copy(src_hbm.at[pl.ds(0, n)], dst, sem).wait()


def _moe_experts_kernel(te_ref, tv_ref, src_ref, nxt_ref, x_hbm, wg_ref, wu_ref, wd_ref, y_ref, xbuf, sem, *, tm):
    i = pl.program_id(0)
    nt = pl.num_programs(0)
    slot = i % 2

    @pl.when(jnp.logical_and(i == 0, tv_ref[0] > 0))
    def _():
        _gather_rows(src_ref, 0, tm, x_hbm, xbuf.at[0], sem.at[0])

    @pl.when(jnp.logical_and(i + 1 < nt, tv_ref[jnp.minimum(i + 1, nt - 1)] > 0))
    def _():
        _gather_rows(nxt_ref, 0, tm, x_hbm, xbuf.at[1 - slot], sem.at[1 - slot])

    @pl.when(tv_ref[i] > 0)
    def _():
        _wait_rows(tm, x_hbm, xbuf.at[slot], sem.at[slot])
        xb = xbuf[slot].astype(BF16)
        a = _dot(xb, wg_ref[0].astype(BF16))
        u = _dot(xb, wu_ref[0].astype(BF16))
        hid = (a / (1.0 + jnp.exp(-a))) * u
        y_ref[...] = _dot(hid.astype(BF16), wd_ref[0].astype(BF16))

    @pl.when(tv_ref[i] == 0)
    def _():
        y_ref[...] = jnp.zeros_like(y_ref)


def _moe_experts(x, w_gate, w_up, w_down, tile_expert, tile_valid, src, tm):
    T, D = x.shape
    F = w_gate.shape[-1]
    NT = tile_expert.shape[0]
    grid_spec = pltpu.PrefetchScalarGridSpec(
        num_scalar_prefetch=2,
        grid=(NT,),
        in_specs=[pl.BlockSpec((1, 1, tm), lambda i, te, tv: (i, 0, 0), memory_space=pltpu.SMEM),
                  pl.BlockSpec((1, 1, tm), lambda i, te, tv: (i + 1, 0, 0), memory_space=pltpu.SMEM),
                  pl.BlockSpec(memory_space=pl.ANY),
                  pl.BlockSpec((1, D, F), lambda i, te, tv: (te[i], 0, 0)),
                  pl.BlockSpec((1, D, F), lambda i, te, tv: (te[i], 0, 0)),
                  pl.BlockSpec((1, F, D), lambda i, te, tv: (te[i], 0, 0))],
        out_specs=pl.BlockSpec((tm, D), lambda i, te, tv: (i, 0)),
        scratch_shapes=[pltpu.VMEM((2, tm, D), F32), pltpu.SemaphoreType.DMA((2,))],
    )
    return pl.pallas_call(
        functools.partial(_moe_experts_kernel, tm=tm),
        grid_spec=grid_spec,
        out_shape=jax.ShapeDtypeStruct((NT * tm, D), F32),
        compiler_params=_cparams("arbitrary"),
        name="moe_experts",
    )(tile_expert, tile_valid, src, src, x, w_gate, w_up, w_down)


def _moe_combine_kernel(pos_ref, nxt_ref, ys_hbm, h_ref, w_ref, g_ref, b_ref, oh_ref, ot_ref, buf, sem,
                        *, tm, alpha, head_tiles):
    i = pl.program_id(0)
    nt = pl.num_programs(0)
    slot = i % 2

    @pl.when(i == 0)
    def _():
        _gather_rows(pos_ref, 0, 2 * tm, ys_hbm, buf.at[0], sem.at[0])

    @pl.when(i + 1 < nt)
    def _():
        _gather_rows(nxt_ref, 0, 2 * tm, ys_hbm, buf.at[1 - slot], sem.at[1 - slot])

    _wait_rows(2 * tm, ys_hbm, buf.at[slot], sem.at[slot])
    w = w_ref[...]
    y = _layer_norm(alpha * h_ref[...] + (w[:, 0:1] * buf[slot, 0:tm] + w[:, 1:2] * buf[slot, tm:2 * tm]),
                    g_ref[...], b_ref[...])

    @pl.when(i < head_tiles)
    def _():
        oh_ref[...] = y

    @pl.when(i >= head_tiles)
    def _():
        ot_ref[...] = y


def _moe_combine(ys, h, pos, wts, g, b, alpha, tm, head_rows):
    T, D = h.shape
    nt = T // tm
    head_tiles = head_rows // tm
    const = lambda a: pl.BlockSpec(a.shape, lambda i: (0,) * a.ndim)
    return pl.pallas_call(
        functools.partial(_moe_combine_kernel, tm=tm, alpha=alpha, head_tiles=head_tiles),
        grid=(nt,),
        in_specs=[pl.BlockSpec((1, 1, 2 * tm), lambda i: (i, 0, 0), memory_space=pltpu.SMEM),
                  pl.BlockSpec((1, 1, 2 * tm), lambda i: (i + 1, 0, 0), memory_space=pltpu.SMEM),
                  pl.BlockSpec(memory_space=pl.ANY),
                  pl.BlockSpec((tm, D), lambda i: (i, 0)),
                  pl.BlockSpec((tm, 2), lambda i: (i, 0)), const(g), const(b)],
        out_specs=list(_two_part_specs(tm, D, head_tiles)),
        out_shape=[jax.ShapeDtypeStruct((head_rows, D), F32), jax.ShapeDtypeStruct((T - head_rows, D), F32)],
        scratch_shapes=[pltpu.VMEM((2, 2 * tm, D), F32), pltpu.SemaphoreType.DMA((2,))],
        compiler_params=_cparams("arbitrary"),
        name="moe_combine_ln",
    )(pos, pos, ys, h, wts, g, b)


def _route(ids, n_experts, tm):
    T = ids.shape[1]
    flat = ids.reshape(-1)
    iota = jnp.arange(2 * T, dtype=jnp.int32)
    sorted_e, order = lax.sort((flat, iota), num_keys=1, is_stable=True)
    _, inverse = lax.sort((order, iota), num_keys=1)
    experts = jnp.arange(n_experts, dtype=jnp.int32)
    counts = jnp.sum((flat[:, None] == experts[None, :]).astype(jnp.int32), axis=0)
    padded = (counts + tm - 1) // tm * tm
    ends = jnp.cumsum(padded)
    shift = (ends - padded) - (jnp.cumsum(counts) - counts)
    NT = (2 * T + n_experts * (tm - 1)) // tm
    tile_start = jnp.arange(NT, dtype=jnp.int32) * tm
    tile_expert = jnp.minimum(jnp.sum((tile_start[:, None] >= ends[None, :]).astype(jnp.int32), axis=1),
                              n_experts - 1)
    tile_valid = (tile_start < ends[-1]).astype(jnp.int32)
    pos = (inverse + shift[flat]).reshape(2, T)
    row = jnp.arange((NT + 1) * tm, dtype=jnp.int32)
    row_shift = jnp.repeat(shift[jnp.concatenate([tile_expert, tile_expert[-1:]])], tm)
    src = (order % T)[jnp.clip(row - row_shift, 0, 2 * T - 1)]
    return tile_expert, tile_valid, src.reshape(NT + 1, 1, tm), pos


MLA_PAIR_W = 2 * LANES
MLA_SAMPLE_CHUNK = 256
QK_NOPE = 64
QK_ROPE = 32


def _mla_proj_kernel(hh_ref, ht_ref, wdn_ref, gq_ref, gkv_ref, wq_ref, wqr_ref, wk_ref, wv_ref, cos_ref, sin_ref,
                     ckv_ref, kr_ref, qcat_ref, kcat_ref, vt_ref, *, q_lora, kv_lora, npairs, scale, head_tiles):
    h = jnp.where(pl.program_id(0) < head_tiles, hh_ref[...], ht_ref[...])
    z = _dot(h.astype(BF16), wdn_ref[...])
    cq = z[:, :q_lora]
    ckv = z[:, q_lora:q_lora + kv_lora]
    o = q_lora + kv_lora
    kr_raw = z[:, o:o + HEAD_W]
    kr_rot = z[:, o + HEAD_W:o + 2 * HEAD_W]
    cq = cq * lax.rsqrt(jnp.mean(cq * cq, axis=-1, keepdims=True) + RMS_EPS) * gq_ref[...]
    ckv = ckv * lax.rsqrt(jnp.mean(ckv * ckv, axis=-1, keepdims=True) + RMS_EPS) * gkv_ref[...]
    ckv_ref[...] = ckv
    cos = cos_ref[...]
    sin = sin_ref[...]
    kr_tile = kr_raw * cos + kr_rot * sin
    kr_ref[...] = kr_tile[:, QK_NOPE:QK_NOPE + QK_ROPE]
    cqb = cq.astype(BF16)
    ckb = ckv.astype(BF16)
    cos2 = jnp.concatenate([cos, cos], axis=1)
    sin2 = jnp.concatenate([sin, sin], axis=1)
    kr2 = jnp.concatenate([kr_tile, kr_tile], axis=1)
    for p in range(npairs):
        lanes = slice(p * MLA_PAIR_W, (p + 1) * MLA_PAIR_W)
        q = _dot(cqb, wq_ref[:, lanes]) * cos2 + _dot(cqb, wqr_ref[:, lanes]) * sin2
        qcat_ref[:, lanes] = (q * scale).astype(BF16)
        kcat_ref[:, lanes] = (_dot(ckb, wk_ref[:, lanes]) + kr2).astype(BF16)
    heads = 2 * npairs
    vt = _dot_nt(wv_ref[...], ckb).reshape(heads, HEAD_DIM, ckb.shape[0])
    vt_ref[0, :, :HEAD_DIM, :] = vt.astype(BF16)
    vt_ref[0, :, HEAD_DIM:, :] = jnp.ones((heads, VT_ROWS - HEAD_DIM, ckb.shape[0]), BF16)


def _mla_proj(h_head, h_tail, wdn, gq, gkv, wq, wqr, wk, wv, cos_t, sin_t, table_block, tm):
    D = h_head.shape[1]
    T = h_head.shape[0] + h_tail.shape[0]
    head_tiles = h_head.shape[0] // tm
    q_lora, kv_lora = gq.shape[1], gkv.shape[1]
    npairs = wq.shape[1] // MLA_PAIR_W
    const = lambda a: pl.BlockSpec(a.shape, lambda i: (0,) * a.ndim)
    row = lambda w_: pl.BlockSpec((tm, w_), lambda i: (i, 0))
    table = pl.BlockSpec((tm, HEAD_W), lambda i: (table_block(i), 0))
    return pl.pallas_call(
        functools.partial(_mla_proj_kernel, q_lora=q_lora, kv_lora=kv_lora, npairs=npairs,
                          scale=(QK_NOPE + QK_ROPE) ** -0.5 * LOG2E, head_tiles=head_tiles),
        grid=(T // tm,),
        in_specs=[*_two_part_specs(tm, D, head_tiles), const(wdn), const(gq), const(gkv), const(wq), const(wqr),
                  const(wk), const(wv), table, table],
        out_specs=[row(kv_lora), row(QK_ROPE), row(wq.shape[1]), row(wk.shape[1]),
                   pl.BlockSpec((1, 2 * npairs, VT_ROWS, tm), lambda i: (i, 0, 0, 0))],
        out_shape=[jax.ShapeDtypeStruct((T, kv_lora), F32), jax.ShapeDtypeStruct((T, QK_ROPE), F32),
                   jax.ShapeDtypeStruct((T, wq.shape[1]), BF16), jax.ShapeDtypeStruct((T, wk.shape[1]), BF16),
                   jax.ShapeDtypeStruct((T // tm, 2 * npairs, VT_ROWS, tm), BF16)],
        compiler_params=_cparams("parallel"),
        name="mla_proj",
    )(h_head, h_tail, wdn, gq, gkv, wq, wqr, wk, wv, cos_t, sin_t)


def _rot_half(w):
    half = w.shape[-1] // 2
    return jnp.concatenate([-w[..., half:], w[..., :half]], axis=-1)


def _prep_mla_weights(w_down, w_uq, w_ukv, heads, q_lora, kv_lora):
    D = w_down.shape[0]
    tail = HEAD_W - QK_NOPE - QK_ROPE
    w_kr = w_down[:, q_lora + kv_lora:]
    slot = lambda w: jnp.concatenate([jnp.zeros((D, QK_NOPE), w.dtype), w, jnp.zeros((D, tail), w.dtype)], axis=1)
    wdn = jnp.concatenate([w_down[:, :q_lora + kv_lora], slot(w_kr), slot(_rot_half(w_kr))], axis=1)
    wq3 = w_uq.reshape(q_lora, heads, QK_NOPE + QK_ROPE)
    nope, ropew = wq3[..., :QK_NOPE], wq3[..., QK_NOPE:]
    zpad = jnp.zeros((q_lora, heads, tail), w_uq.dtype)
    wq = jnp.concatenate([nope, ropew, zpad], axis=-1)
    wqr = jnp.concatenate([jnp.zeros_like(nope), _rot_half(ropew), zpad], axis=-1)
    wkv3 = w_ukv.reshape(kv_lora, heads, QK_NOPE + HEAD_DIM)
    w_uk, w_uv = wkv3[..., :QK_NOPE], wkv3[..., QK_NOPE:]
    wk = jnp.concatenate([w_uk, jnp.zeros((kv_lora, heads, HEAD_W - QK_NOPE), w_ukv.dtype)], axis=-1)
    wv = jnp.transpose(w_uv.reshape(kv_lora, heads * HEAD_DIM))
    b16 = lambda a: a.astype(BF16)
    return (b16(wdn), b16(wq.reshape(q_lora, -1)), b16(wqr.reshape(q_lora, -1)), b16(wk.reshape(kv_lora, -1)),
            b16(wv), b16(jnp.transpose(w_uk, (1, 2, 0))), b16(jnp.transpose(w_uv, (1, 0, 2))))


def _rope_tables(pos):
    half = QK_ROPE // 2
    inv_freq = ROPE_BASE ** (-jnp.arange(half, dtype=F32) / half)
    ang = pos.astype(F32)[:, None] * inv_freq[None, :]
    n = pos.shape[0]
    pad = jnp.zeros((n, HEAD_W - QK_NOPE - QK_ROPE), F32)
    cos = jnp.concatenate([jnp.ones((n, QK_NOPE), F32)] + [jnp.cos(ang)] * 2 + [pad], axis=1)
    sin = jnp.concatenate([jnp.zeros((n, QK_NOPE), F32)] + [jnp.sin(ang)] * 2 + [pad], axis=1)
    return cos, sin


def _mla_sample_queries(qcat, heads):
    DB, n, _ = qcat.shape
    q4 = qcat.reshape(DB, n, heads, HEAD_W)
    rows = lambda a: jnp.transpose(a, (0, 2, 1, 3)).reshape(DB, heads * n, a.shape[-1])
    return rows(q4[..., :QK_NOPE]), rows(q4[..., QK_NOPE:QK_NOPE + QK_ROPE])


def _mla_sample_kernel(qn_ref, qr_ref, wuk_ref, wuv_ref, cc_ref, rc_ref, cn_ref, rn_ref, o_ref,
                       qlat_ref, s_ref, m_ref, l_ref, acc_ref, *, n, heads):
    kt = pl.program_id(1)

    @pl.when(kt == 0)
    def _():
        for h in range(heads):
            rows = slice(h * n, (h + 1) * n)
            qlat_ref[rows, :] = _dot(qn_ref[0, rows, :], wuk_ref[h]).astype(BF16)
        m_ref[...] = jnp.full(m_ref.shape, NEG_INF, F32)
        l_ref[...] = jnp.zeros_like(l_ref)
        acc_ref[...] = jnp.zeros_like(acc_ref)

    def scores(rows, ckv, kr_t):
        return _dot_nt(qlat_ref[rows, :], ckv) + _dot(qr_ref[0, rows, :], kr_t)

    def absorb(rows, s, ckv):
        m, l, acc = _softmax_step(s[None], ckv, (m_ref[:, rows], l_ref[:, rows], acc_ref[:, rows]))
        m_ref[:, rows] = m
        l_ref[:, rows] = l
        acc_ref[:, rows] = acc

    def update(ckv, kr_t):
        absorb(slice(None), scores(slice(None), ckv, kr_t), ckv)

    ckv = cc_ref[0].astype(BF16)
    kr_t = rc_ref[0].astype(BF16)
    chunk_rows = s_ref.shape[1]
    chunks = [slice(c * chunk_rows, (c + 1) * chunk_rows) for c in range(s_ref.shape[0])]
    s_ref[0] = scores(chunks[0], ckv, kr_t)
    for c, rows in enumerate(chunks):
        if c + 1 < len(chunks):
            s_ref[c + 1] = scores(chunks[c + 1], ckv, kr_t)
        absorb(rows, s_ref[c], ckv)

    @pl.when(kt == pl.num_programs(1) - 1)
    def _():
        update(cn_ref[0], rn_ref[0])
        o_lat = (acc_ref[0] / l_ref[0]).astype(BF16)
        for h in range(heads):
            o_ref[0, :, h * HEAD_DIM:(h + 1) * HEAD_DIM] = _dot(o_lat[h * n:(h + 1) * n], wuv_ref[h]).astype(o_ref.dtype)


def _mla_sample(qn, qr, wuk, wuv, ckv_c, kr_c, ckv_n, kr_n, n, tk):
    DB, R, _ = qn.shape
    heads = R // n
    P, C = ckv_c.shape[1], ckv_c.shape[2]
    const = lambda a: pl.BlockSpec(a.shape, lambda b, j: (0,) * a.ndim)
    per_b = lambda a: pl.BlockSpec((1,) + a.shape[1:], lambda b, j: (b, 0, 0))
    return pl.pallas_call(
        functools.partial(_mla_sample_kernel, n=n, heads=heads),
        grid=(DB, P // tk),
        in_specs=[per_b(qn), per_b(qr), const(wuk), const(wuv),
                  pl.BlockSpec((1, tk, C), lambda b, j: (b, j, 0)),
                  pl.BlockSpec((1, QK_ROPE, tk), lambda b, j: (b, 0, j)),
                  per_b(ckv_n), per_b(kr_n)],
        out_specs=pl.BlockSpec((1, n, heads * HEAD_DIM), lambda b, j: (b, 0, 0)),
        out_shape=jax.ShapeDtypeStruct((DB, n, heads * HEAD_DIM), BF16),
        scratch_shapes=[pltpu.VMEM((R, C), BF16), pltpu.VMEM((R // MLA_SAMPLE_CHUNK, MLA_SAMPLE_CHUNK, tk), F32),
                        pltpu.VMEM((1, R, 1), F32), pltpu.VMEM((1, R, 1), F32), pltpu.VMEM((1, R, C), F32)],
        compiler_params=_cparams("parallel", "arbitrary"),
        name="mla_sample",
    )(qn, qr, wuk, wuv, ckv_c, kr_c, ckv_n, kr_n)


def _moe_layer(h, ids, wts, w_gate, w_up, w_down, layer, g, b, alpha, tm, head_rows):
    T, D = h.shape
    n_experts = w_gate.shape[1] * w_gate.shape[2]
    tile_expert, tile_valid, src, pos = _route(ids[:2], n_experts, tm)
    flat3 = lambda w: w.reshape((-1,) + w.shape[3:])
    ys = _moe_experts(h, flat3(w_gate), flat3(w_up), flat3(w_down), tile_expert + layer * n_experts, tile_valid,
                      src, tm)
    pos_t = jnp.transpose(pos.reshape(2, T // tm, tm), (1, 0, 2)).reshape(T // tm, 1, 2 * tm)
    pos_t = jnp.pad(pos_t, ((0, 1), (0, 0), (0, 0)))
    return _moe_combine(ys, h, pos_t, jnp.transpose(wts[:2]), g, b, alpha, tm, head_rows)


TOKEN_TILE = 256
FLASH_Q_TILE = 512
FLASH_KEY_TILE = 256
SB_Q_TILE = 256
SB_KEY_TILE = 128
CACHE_TILE = 1024
SB_SUB_TILE = 256


def kernel(x_prompt, x_sample, cache_fox_k, cache_fox_v, cache_fox_logf, cache_sb_k, cache_sb_v, cache_mla_ckv, cache_mla_krope, ab_w_in, ab_b_forget, ab_w_out, mla_w_down, mla_g_q, mla_g_kv, mla_w_uq, mla_w_ukv, mla_w_out, moe_w_group, moe_b_group, moe_w_router, moe_b_router, moe_w_gate, moe_w_up, moe_w_down, ln_g, ln_b):
    B, S, D = x_prompt.shape
    DB, n, _ = x_sample.shape
    P = cache_fox_k.shape[2]
    TP, TS = B * S, DB * n
    depth = ln_g.shape[0]
    n_groups = moe_w_group.shape[-1]
    assert depth == 2 and ab_w_in.shape[0] == 1 and mla_w_down.shape[0] == 1
    assert S % FLASH_Q_TILE == 0 and S % SB_Q_TILE == 0 and TP % TOKEN_TILE == 0 and TS % TOKEN_TILE == 0 and TOKEN_TILE % n == 0
    assert P % CACHE_TILE == 0 and P % CHUNK == 0 and n == CHUNK
    alpha = (2 * depth) ** 0.25
    tk = CACHE_TILE

    xp, xs = x_prompt.reshape(TP, D), x_sample.reshape(TS, D)
    sample3 = lambda a: a[TP:].reshape(DB, n, -1)

    def ffn(o_head_parts, o_tail, w_out, resid, layer):
        wrh, wrl, rb = _prep_router(moe_w_group[layer], moe_b_group[layer], moe_w_router[layer], moe_b_router[layer])
        h, ids, wts = _mix_out(o_head_parts, o_tail, w_out.astype(BF16), *resid, ln_g[layer, 0][None],
                               ln_b[layer, 0][None], wrh, wrl, rb, alpha, n_groups, TOKEN_TILE)
        return _moe_layer(h, ids, wts, moe_w_gate, moe_w_up, moe_w_down, layer,
                          ln_g[layer, 1][None], ln_b[layer, 1][None], alpha, TOKEN_TILE, TP)

    fox_heads = ab_b_forget.shape[1]
    hw = (ab_w_in.shape[2] - fox_heads) // 6
    w_ab, b_forget = _prep_ab_weights(ab_w_in[0], ab_b_forget[0])
    w_tok, w_t = _prompt_ab_weights(w_ab, hw, fox_heads)
    (qa_p, ka_p, qb_p, kb_p, lf_p, kat, vat, kbt, vbt, vat16, vbt16) = _ab_proj_prompt(
        xp, w_tok, b_forget, w_t, _query_decay_ones(fox_heads), B, S, fox_heads, SB_KEY_TILE)
    ka_p = _fox_insert_decay(ka_p, lf_p, B, S, fox_heads)
    qa, ka, va, qb, kb, vb, ka16, va16, kb16, vb16, lf = _ab_proj(xs, w_ab, b_forget, hw)
    dbn = lambda a: a.reshape(DB, n, -1)
    lf_s = dbn(lf)[:, :, :fox_heads]
    cq_s, ck_past, ck_new = _fox_sample_cum(lf_s, jnp.transpose(cache_fox_logf[0], (0, 2, 1)))
    cache_t = lambda c: jnp.transpose(c[0], (0, 2, 3, 1))
    hm = lambda a, t=False: _heads_major(dbn(a), fox_heads, t)
    o_fox_s = _fox_sample(hm(qa), hm(ka16, True), hm(va16, True), cache_t(cache_fox_k), cache_t(cache_fox_v),
                          cq_s, ck_past, ck_new, tk)
    o_sb_s = _sb_sample(hm(qb), hm(kb16, True), hm(vb16, True), cache_t(cache_sb_k), cache_t(cache_sb_v),
                        tk, SB_SUB_TILE)
    tokens_major = lambda a: jnp.transpose(a, (0, 2, 1, 3)).reshape(TS, hw)
    o_tail = jnp.concatenate([tokens_major(o_fox_s), tokens_major(o_sb_s)], axis=-1)
    o_fox_p = _flash_prompt(qa_p, ka_p, vat16, B, S, FLASH_Q_TILE, fox_heads, 1, "fox_prompt")
    o_sb_p = _sb_prompt(qb_p, kb_p, vbt16, B, S, SB_Q_TILE)
    xp, xs = ffn((o_fox_p, o_sb_p), o_tail, ab_w_out[0], (xp, xs), 0)

    q_lora, kv_lora = mla_g_q.shape[1], mla_g_kv.shape[1]
    heads = mla_w_uq.shape[2] // (QK_NOPE + QK_ROPE)
    wdn, wq, wqr, wk, wv, wuk_t, wuv = _prep_mla_weights(mla_w_down[0], mla_w_uq[0], mla_w_ukv[0], heads, q_lora, kv_lora)
    tm = TOKEN_TILE
    pos = jnp.concatenate([jnp.arange(S, dtype=jnp.int32), P + jnp.arange(tm, dtype=jnp.int32) % n])
    cos_t, sin_t = _rope_tables(pos)
    blocks_per_seq, prompt_blocks = S // tm, TP // tm
    table_block = lambda i: jnp.where(i < prompt_blocks, i % blocks_per_seq, blocks_per_seq)
    assert tm == FLASH_KEY_TILE
    ckv, kr, qcat, kcat, vt = _mla_proj(xp, xs, wdn, mla_g_q[0][None], mla_g_kv[0][None], wq, wqr, wk, wv,
                                        cos_t, sin_t, table_block, tm)
    qn, qr = _mla_sample_queries(sample3(qcat), heads)
    o_s = _mla_sample(qn, qr, wuk_t, wuv, cache_mla_ckv[0], jnp.transpose(cache_mla_krope[0], (0, 2, 1)),
                      sample3(ckv).astype(BF16), jnp.transpose(sample3(kr), (0, 2, 1)).astype(BF16), n, tk)
    o_p = _flash_prompt(qcat, kcat, vt, B, S, FLASH_Q_TILE, 8, CHUNK, "mla_prompt")
    xp, xs = ffn((o_p,), o_s.reshape(TS, -1), mla_w_out[0], (xp, xs), 1)

    rows_p = lambda a: jnp.transpose(a, (0, 3, 1, 2))[None]
    rows_s = lambda a: a.reshape(1, DB, n, fox_heads, hw // fox_heads)
    pr, sr = slice(0, TP), slice(TP, TP + TS)
    return (xp.reshape(B, S, D), xs.reshape(DB, n, D),
            rows_p(kat), rows_p(vat), lf_p[:, :fox_heads].reshape(1, B, S, fox_heads), rows_p(kbt), rows_p(vbt),
            ckv[pr].reshape(1, B, S, kv_lora), kr[pr].reshape(1, B, S, QK_ROPE),
            rows_s(ka), rows_s(va), lf[:, :fox_heads].reshape(1, DB, n, fox_heads), rows_s(kb), rows_s(vb),
            ckv[sr].reshape(1, DB, n, kv_lora), kr[sr].reshape(1, DB, n, QK_ROPE))
```

```python
import functools

import jax
import jax.numpy as jnp
from jax import lax
from jax.experimental import pallas as pl
from jax.experimental.pallas import tpu as pltpu

F32 = jnp.float32
BF16 = jnp.bfloat16
NEG_INF = -1e30
LOG2E = 1.4426950408889634

LANES = 128
HEAD_DIM = 64
PAIR_W = 2 * HEAD_DIM
CHUNK = 64
LN_EPS = 1e-5
RMS_EPS = 1e-6
ROPE_BASE = 10000.0
VMEM_LIMIT = 56 * 1024 * 1024


def _cparams(*sem):
    return pltpu.CompilerParams(dimension_semantics=sem, vmem_limit_bytes=VMEM_LIMIT)


def _dot(a, b):
    return jnp.dot(a, b, preferred_element_type=F32)


def _dot_nt(a, b):
    return lax.dot_general(a, b, (((1,), (1,)), ((), ())), preferred_element_type=F32)


def _split_bf16(x, parts):
    out = []
    r = x
    for _ in range(parts):
        h = r.astype(BF16)
        out.append(h)
        r = r - h.astype(F32)
    return out


def _split_bf16_trunc(x, parts):
    out = []
    r = x
    for _ in range(parts):
        bits = lax.bitcast_convert_type(r, jnp.uint32) & jnp.uint32(0xFFFF0000)
        h = lax.bitcast_convert_type(bits, F32)
        out.append(h.astype(BF16))
        r = r - h
    return out


def _log_sigmoid(x):
    return jnp.minimum(x, 0.0) - jnp.log(1.0 + jnp.exp(-jnp.abs(x)))


def _pick_tile(n, pref, mult=8):
    t = min(pref, n)
    while n % t or t % mult:
        t -= 1
    return t


def _ab_proj_kernel(x_ref, w_ref, bf_ref, qa_ref, ka_ref, va_ref, qb_ref, kb_ref, vb_ref,
                    ka16_ref, va16_ref, kb16_ref, vb16_ref, lf_ref, *, hw, qscale):
    xb = x_ref[...].astype(BF16)

    def seg(j):
        return _dot(xb, w_ref[:, j * hw:(j + 1) * hw])

    qa_ref[...] = (seg(0) * qscale).astype(BF16)
    z = seg(1)
    ka_ref[...] = z
    ka16_ref[...] = z.astype(BF16)
    z = seg(2)
    va_ref[...] = z
    va16_ref[...] = z.astype(BF16)
    qb_ref[...] = (seg(3) * qscale).astype(BF16)
    z = seg(4)
    kb_ref[...] = z
    kb16_ref[...] = z.astype(BF16)
    z = seg(5)
    vb_ref[...] = z
    vb16_ref[...] = z.astype(BF16)
    f = _dot(xb, w_ref[:, 6 * hw:6 * hw + LANES]) + bf_ref[...]
    lf_ref[...] = _log_sigmoid(f)


def _ab_proj(x, w, bf, hw):
    T, D = x.shape
    tm = _pick_tile(T, 256)
    row = lambda w_: pl.BlockSpec((tm, w_), lambda i: (i, 0))
    f32o = jax.ShapeDtypeStruct((T, hw), F32)
    b16o = jax.ShapeDtypeStruct((T, hw), BF16)
    return pl.pallas_call(
        functools.partial(_ab_proj_kernel, hw=hw, qscale=HEAD_DIM ** -0.5 * LOG2E),
        grid=(T // tm,),
        in_specs=[row(D), pl.BlockSpec(w.shape, lambda i: (0, 0)), pl.BlockSpec(bf.shape, lambda i: (0, 0))],
        out_specs=[row(hw)] * 10 + [row(LANES)],
        out_shape=[b16o, f32o, f32o, b16o, f32o, f32o, b16o, b16o, b16o, b16o,
                   jax.ShapeDtypeStruct((T, LANES), F32)],
        compiler_params=_cparams("parallel"),
        name="ab_proj",
    )(x, w, bf)


def _ab_proj_prompt_kernel(x_ref, w_ref, bf_ref, wt_ref, qone_ref, qa_ref, ka_ref, qb_ref, kb_ref, lf_ref,
                           kat_ref, vat_ref, kbt_ref, vbt_ref, vat16_ref, vbt16_ref, *, hw, heads, qscale, sb_tk):
    xb = x_ref[...].astype(BF16)
    tm = xb.shape[0]
    lo = lax.broadcasted_iota(jnp.int32, (tm, LANES), 1) < HEAD_DIM
    zero = jnp.zeros((tm, LANES), F32)

    def seg(j):
        z = _dot(xb, w_ref[:, j * hw:(j + 1) * hw])
        odd = pltpu.roll(z, hw - HEAD_DIM, axis=1)
        cols = []
        for c in range(hw // LANES):
            g = slice(c * LANES, (c + 1) * LANES)
            cols += [jnp.where(lo, z[:, g], zero), jnp.where(lo, odd[:, g], zero)]
        return jnp.concatenate(cols, axis=1)

    def seg_t(j):
        return _dot_nt(wt_ref[j * hw:(j + 1) * hw, :], xb).reshape(heads, hw // heads, tm)

    qa_ref[...] = (seg(0) * qscale + qone_ref[...]).astype(BF16)
    ka_ref[...] = seg(1).astype(BF16)
    qb_ref[...] = (seg(2) * qscale).astype(BF16)
    kb_ref[...] = seg(3).astype(BF16)
    lf_ref[...] = _log_sigmoid(_dot(xb, w_ref[:, 4 * hw:4 * hw + LANES]) + bf_ref[...])
    kat_ref[0] = seg_t(0)
    z = seg_t(1)
    vat_ref[0] = z
    vat16_ref[0, :, :HEAD_DIM, :] = z.astype(BF16)
    vat16_ref[0, :, HEAD_DIM:, :] = jnp.ones((heads, VT_ROWS - HEAD_DIM, tm), BF16)
    kbt_ref[0] = seg_t(2)
    z = seg_t(3)
    vbt_ref[0] = z
    for c in range(tm // sb_tk):
        vbt16_ref[c] = z[:, :, c * sb_tk:(c + 1) * sb_tk].astype(BF16)


def _ab_proj_prompt(x, w_tok, bf, w_t, q_ones, B, S, heads, sb_tk):
    TP, D = x.shape
    hw = w_t.shape[0] // 4
    wide = heads * HEAD_W
    tm = FLASH_KEY_TILE
    nj = S // tm
    const = lambda a: pl.BlockSpec(a.shape, lambda b, j: (0,) * a.ndim)
    row = lambda w_: pl.BlockSpec((tm, w_), lambda b, j: (b * nj + j, 0))
    t_spec = pl.BlockSpec((1, heads, hw // heads, tm), lambda b, j: (b, 0, 0, j))
    b16 = jax.ShapeDtypeStruct((TP, wide), BF16)
    t32 = jax.ShapeDtypeStruct((B, heads, hw // heads, S), F32)
    return pl.pallas_call(
        functools.partial(_ab_proj_prompt_kernel, hw=hw, heads=heads, qscale=HEAD_DIM ** -0.5 * LOG2E, sb_tk=sb_tk),
        grid=(B, nj),
        in_specs=[row(D), const(w_tok), const(bf), const(w_t), const(q_ones)],
        out_specs=[row(wide)] * 4 + [row(LANES)] + [t_spec] * 4 + [
            pl.BlockSpec((1, heads, VT_ROWS, tm), lambda b, j: (b * nj + j, 0, 0, 0)),
            pl.BlockSpec((tm // sb_tk, heads, hw // heads, sb_tk), lambda b, j: (b * nj + j, 0, 0, 0))],
        out_shape=[b16] * 4 + [jax.ShapeDtypeStruct((TP, LANES), F32)] + [t32] * 4 + [
            jax.ShapeDtypeStruct((TP // tm, heads, VT_ROWS, tm), BF16),
            jax.ShapeDtypeStruct((TP // sb_tk, heads, hw // heads, sb_tk), BF16)],
        compiler_params=_cparams("parallel", "parallel"),
        name="ab_proj_prompt",
    )(x, w_tok, bf, w_t, q_ones)


def _cumsum_kernel(x_ref, o_ref, carry_ref, *, tl):
    @pl.when(pl.program_id(0) == 0)
    def _():
        carry_ref[...] = jnp.zeros_like(carry_ref)

    x = x_ref[...]
    rows = x.shape[0]
    r = lax.broadcasted_iota(jnp.int32, (tl, tl), 0)
    c = lax.broadcasted_iota(jnp.int32, (tl, tl), 1)
    upper = (r <= c).astype(BF16)
    parts = jnp.concatenate(_split_bf16(x, 4), axis=0)
    y = _dot(parts, upper)
    cum = (y[0:rows] + y[rows:2 * rows]) + (y[2 * rows:3 * rows] + y[3 * rows:]) + carry_ref[:, 0:1]
    o_ref[...] = cum
    carry_ref[...] = jnp.broadcast_to(cum[:, tl - 1:tl], carry_ref.shape)


def _cumsum_rows(x):
    B, H, L = x.shape
    tl = _pick_tile(L, 512, LANES)
    out = pl.pallas_call(
        functools.partial(_cumsum_kernel, tl=tl),
        grid=(L // tl,),
        in_specs=[pl.BlockSpec((B * H, tl), lambda j: (0, j))],
        out_specs=pl.BlockSpec((B * H, tl), lambda j: (0, j)),
        out_shape=jax.ShapeDtypeStruct((B * H, L), F32),
        scratch_shapes=[pltpu.VMEM((B * H, LANES), F32)],
        compiler_params=_cparams("arbitrary"),
        name="cumsum_rows",
    )(x.reshape(B * H, L))
    return out.reshape(B, H, L)


def _stack_pair(qp):
    lo = lax.broadcasted_iota(jnp.int32, qp.shape, 1) < HEAD_DIM
    zero = jnp.zeros_like(qp)
    return jnp.concatenate([jnp.where(lo, qp, zero), jnp.where(lo, zero, qp)], axis=0)


def _unstack_pair(o):
    lo = lax.broadcasted_iota(jnp.int32, o.shape[1:], 1) < HEAD_DIM
    return jnp.where(lo, o[0], o[1])


def _softmax_step(s, vb, carry):
    m, l, acc = carry
    two, tq, tk = s.shape
    m_new = jnp.maximum(m, jnp.max(s, axis=-1, keepdims=True))
    alpha = jnp.exp2(m - m_new)
    p = jnp.exp2(s - m_new)
    l = alpha * l + jnp.sum(p, axis=-1, keepdims=True)
    pv = _dot(p.reshape(two * tq, tk).astype(BF16), vb).reshape(two, tq, vb.shape[-1])
    return m_new, l, alpha * acc + pv


def _log2_sigmoid_pair(z2):
    l1 = jnp.log2(1.0 + jnp.exp2(-jnp.abs(z2)))
    return jnp.minimum(z2, 0.0) - l1, jnp.minimum(-z2, 0.0) - l1


def _softmax_init(tq, width):
    return (jnp.full((2, tq, 1), NEG_INF, F32), jnp.zeros((2, tq, 1), F32), jnp.zeros((2, tq, width), F32))


def _sb_step(z, vb, strict_upper, carry, mask=None):
    run, acc = carry
    log_beta, log_rest = _log2_sigmoid_pair(z)
    if mask is not None:
        log_beta = jnp.where(mask, log_beta, NEG_INF)
        log_rest = jnp.where(mask, log_rest, 0.0)
    hi, lo = _split_bf16(log_rest, 2)
    later = _dot(hi, strict_upper) + _dot(lo, strict_upper)
    a = jnp.exp2(log_beta + later + run)
    acc = acc + _dot(a.astype(BF16), vb)
    run = run + jnp.sum(log_rest, axis=-1, keepdims=True)
    return run, acc


def _strict_upper(tk):
    r = lax.broadcasted_iota(jnp.int32, (tk, tk), 0)
    c = lax.broadcasted_iota(jnp.int32, (tk, tk), 1)
    return (r > c).astype(BF16)


HEAD_W = LANES
VT_ROWS = HEAD_DIM + 16
SB_DEAD_LOG2 = -160.0


def _pipeline_ahead(stage, first, count, cur, nxt):
    if first < count:
        stage(first, cur)
    elif nxt is not None:
        stage(first - count, nxt)


def _flash_prompt_kernel(q_ref, k_ref, vt_ref, o_ref, s_ref, m_ref, acc_ref, *, tq, tk, heads, chunk):
    i = pl.program_id(2)
    sub = tq // tk
    key = lax.broadcasted_iota(jnp.int32, (tk, tq), 0)
    query = lax.broadcasted_iota(jnp.int32, (tk, tq), 1)
    m_ref[...] = jnp.full(m_ref.shape, NEG_INF, F32)
    acc_ref[...] = jnp.zeros_like(acc_ref)

    def scores(h, kt):
        rows = pl.ds(pl.multiple_of(kt * tk, tk), tk)
        lanes = slice(h * HEAD_W, (h + 1) * HEAD_W)
        s_ref[h] = _dot_nt(k_ref[rows, lanes], q_ref[:, lanes])

    def absorb(h, kt, visible):
        s_t = s_ref[h]
        if visible is not None:
            s_t = jnp.where(visible, s_t, NEG_INF)
        m = m_ref[h]
        m_new = jnp.maximum(m, jnp.max(s_t, axis=0, keepdims=True))
        p_t = jnp.exp2(s_t - m_new).astype(BF16)
        pv = _dot(vt_ref[kt, h], p_t)
        acc_ref[h] = jnp.exp2(m - m_new) * acc_ref[h] + pv
        m_ref[h] = m_new

    scores(0, 0)
    scores(1, 0)

    def body(kt, carry):
        for h in range(heads):
            _pipeline_ahead(scores, h + 2, heads, kt, kt + 1)
            absorb(h, kt, None)
        return carry

    lax.fori_loop(0, sub * i, body, 0)
    for s in range(sub):
        kt = sub * i + s
        visible = ((s * tk + key) // chunk) <= (query // chunk)
        for h in range(heads):
            _pipeline_ahead(scores, h + 2, heads, kt, kt + 1 if s + 1 < sub else None)
            absorb(h, kt, visible)
    for j in range(heads // 2):
        a0, a1 = acc_ref[2 * j], acc_ref[2 * j + 1]
        o_t = jnp.concatenate([a0[:HEAD_DIM] / a0[HEAD_DIM:HEAD_DIM + 1],
                               a1[:HEAD_DIM] / a1[HEAD_DIM:HEAD_DIM + 1]], axis=0)
        o_ref[:, j * PAIR_W:(j + 1) * PAIR_W] = jnp.transpose(o_t).astype(o_ref.dtype)


def _flash_prompt(qx, kx, vt4, B, S, tq, heads_per_step, chunk, name):
    W = qx.shape[1]
    nq = S // tq
    hs = heads_per_step
    tk = vt4.shape[3]
    nk = S // tk
    return pl.pallas_call(
        functools.partial(_flash_prompt_kernel, tq=tq, tk=tk, heads=hs, chunk=chunk),
        grid=(B, W // (HEAD_W * hs), S // tq),
        in_specs=[pl.BlockSpec((tq, HEAD_W * hs), lambda b, g, i: (b * nq + i, g)),
                  pl.BlockSpec((S, HEAD_W * hs), lambda b, g, i: (b, g)),
                  pl.BlockSpec((nk, hs, VT_ROWS, tk), lambda b, g, i: (b, g, 0, 0))],
        out_specs=pl.BlockSpec((tq, HEAD_DIM * hs), lambda b, g, i: (b * nq + i, g)),
        out_shape=jax.ShapeDtypeStruct((B * S, W // HEAD_W * HEAD_DIM), BF16),
        scratch_shapes=[pltpu.VMEM((hs, tk, tq), F32), pltpu.VMEM((hs, 1, tq), F32),
                        pltpu.VMEM((hs, VT_ROWS, tq), F32)],
        compiler_params=_cparams("parallel", "parallel", "arbitrary"),
        name=name,
    )(qx, kx, vt4)


def _sb_prompt_kernel(q_ref, k_ref, vt_ref, o_ref, z_ref, lw_ref, tot_ref, run_ref, acc_ref, *, tq, tk, heads):
    i = pl.program_id(1)
    sub = tq // tk
    key = lax.broadcasted_iota(jnp.int32, (tk, tq), 0)
    query = lax.broadcasted_iota(jnp.int32, (tk, tq), 1)
    r = lax.broadcasted_iota(jnp.int32, (tk, tk), 0)
    c = lax.broadcasted_iota(jnp.int32, (tk, tk), 1)
    after = (c > r).astype(BF16)
    run_ref[...] = jnp.zeros_like(run_ref)
    acc_ref[...] = jnp.zeros_like(acc_ref)

    def logits(h, kt):
        rows = pl.ds(pl.multiple_of(kt * tk, tk), tk)
        lanes = slice(h * HEAD_W, (h + 1) * HEAD_W)
        z_ref[h] = _dot_nt(k_ref[rows, lanes], q_ref[:, lanes])

    def log_weights(before, h, kt):
        log_beta, log_rest = _log2_sigmoid_pair(z_ref[h])
        if before is not None:
            log_beta = jnp.where(before, log_beta, NEG_INF)
            log_rest = jnp.where(before, log_rest, 0.0)
        hi, lo = _split_bf16(log_rest, 2)
        later = _dot(after, hi) + _dot(after, lo)
        lw_ref[h] = log_beta + later
        tot_ref[h] = later[0:1] + log_rest[0:1]

    def accumulate(h, kt):
        run = run_ref[h]
        a_t = jnp.exp2(lw_ref[h] + run).astype(BF16)
        acc_ref[h] += _dot(vt_ref[kt, h], a_t)
        run_ref[h] = run + tot_ref[h]

    def step(kt, masked, nxt, nxt_masked):
        for h in range(heads):
            _pipeline_ahead(logits, h + 2, heads, kt, nxt)
            if h + 1 < heads:
                log_weights(masked, h + 1, kt)
            elif nxt is not None:
                log_weights(nxt_masked, 0, nxt)
            accumulate(h, kt)

    unmasked = sub * i
    first = unmasked + sub - 1
    masks = [((sub - 1 - s) * tk + key) < query for s in range(sub)]
    logits(0, first)
    logits(1, first)
    log_weights(masks[0], 0, first)
    for s in range(sub):
        kt = first - s
        if s + 1 < sub:
            step(kt, masks[s], kt - 1, masks[s + 1])
        else:
            step(kt, masks[s], jnp.maximum(kt - 1, 0), None)

    def alive():
        return (jnp.max(run_ref[...]) > SB_DEAD_LOG2).astype(jnp.int32)

    def body(carry):
        kt, _ = carry
        step(kt, None, jnp.maximum(kt - 1, 0), None)
        return kt - 1, alive()

    lax.while_loop(lambda c: jnp.logical_and(c[0] >= 0, c[1] > 0), body, (unmasked - 1, alive()))

    for j in range(heads // 2):
        o_t = jnp.concatenate([acc_ref[2 * j], acc_ref[2 * j + 1]], axis=0)
        o_ref[:, j * PAIR_W:(j + 1) * PAIR_W] = jnp.transpose(o_t).astype(o_ref.dtype)


def _sb_prompt(qx, kx, vt4, B, S, tq):
    W = qx.shape[1]
    heads = W // HEAD_W
    tk = vt4.shape[3]
    nq = S // tq
    return pl.pallas_call(
        functools.partial(_sb_prompt_kernel, tq=tq, tk=tk, heads=heads),
        grid=(B, nq),
        in_specs=[pl.BlockSpec((tq, W), lambda b, i: (b * nq + i, 0)),
                  pl.BlockSpec((S, W), lambda b, i: (b, 0)),
                  pl.BlockSpec((S // tk, heads, HEAD_DIM, tk), lambda b, i: (b, 0, 0, 0))],
        out_specs=pl.BlockSpec((tq, heads * HEAD_DIM), lambda b, i: (b * nq + i, 0)),
        out_shape=jax.ShapeDtypeStruct((B * S, heads * HEAD_DIM), BF16),
        scratch_shapes=[pltpu.VMEM((heads, tk, tq), F32), pltpu.VMEM((heads, tk, tq), F32),
                        pltpu.VMEM((heads, 1, tq), F32), pltpu.VMEM((heads, 1, tq), F32),
                        pltpu.VMEM((heads, HEAD_DIM, tq), F32)],
        compiler_params=_cparams("parallel", "arbitrary"),
        name="sb_prompt",
    )(qx, kx, vt4)


def _prep_ab_weights(w_in, b_f):
    D = w_in.shape[0]
    H = b_f.shape[0]
    hw = (w_in.shape[1] - H) // 6
    main = jnp.concatenate([w_in[:, :3 * hw], w_in[:, 3 * hw + H:]], axis=1)
    wf = jnp.zeros((D, LANES), w_in.dtype).at[:, :H].set(w_in[:, 3 * hw:3 * hw + H])
    bf = jnp.zeros((1, LANES), F32).at[0, :H].set(b_f.astype(F32))
    return jnp.concatenate([main, wf], axis=1).astype(BF16), bf


def _prompt_ab_weights(w_ab, hw, heads):
    seg = lambda j: w_ab[:, j * hw:(j + 1) * hw]
    w_tok = jnp.concatenate([seg(0), seg(1), seg(3), seg(4), w_ab[:, 6 * hw:]], axis=1)
    w_t = jnp.transpose(jnp.concatenate([seg(1), seg(2), seg(4), seg(5)], axis=1))
    return w_tok, w_t


DECAY_TERMS = 3


def _insert_decay_kernel(k_ref, c_ref, sel_ref, o_ref):
    terms = sum(_dot(part, sel_ref[j]) for j, part in enumerate(_split_bf16(c_ref[...], DECAY_TERMS)))
    o_ref[...] = (k_ref[...].astype(F32) + terms).astype(BF16)


def _fox_insert_decay(kx, lf, B, S, H):
    lt = jnp.transpose(lf[:, :H].reshape(B, S, H), (0, 2, 1))
    cum = jnp.transpose(_cumsum_rows(lt), (0, 2, 1)).reshape(B * S, H) * (-LOG2E)
    c = jnp.pad(cum, ((0, 0), (0, LANES - H)))
    src = jnp.arange(LANES)[None, :, None]
    dst = jnp.arange(H * HEAD_W)[None, None, :]
    term = jnp.arange(DECAY_TERMS)[:, None, None]
    sel = jnp.logical_and(src < H, dst == src * HEAD_W + HEAD_DIM + term).astype(BF16)
    tm = _pick_tile(B * S, 512)
    return pl.pallas_call(
        _insert_decay_kernel,
        grid=(B * S // tm,),
        in_specs=[pl.BlockSpec((tm, H * HEAD_W), lambda i: (i, 0)), pl.BlockSpec((tm, LANES), lambda i: (i, 0)),
                  pl.BlockSpec(sel.shape, lambda i: (0, 0, 0))],
        out_specs=pl.BlockSpec((tm, H * HEAD_W), lambda i: (i, 0)),
        out_shape=jax.ShapeDtypeStruct(kx.shape, BF16),
        input_output_aliases={0: 0},
        compiler_params=_cparams("parallel"),
        name="fox_insert_decay",
    )(kx, c, sel)


def _query_decay_ones(H):
    lane = jnp.arange(H * HEAD_W) % HEAD_W
    return jnp.logical_and(lane >= HEAD_DIM, lane < HEAD_DIM + DECAY_TERMS).astype(F32)[None]


def _fox_sample_kernel(q_ref, knt_ref, vnt_ref, kct_ref, vct_ref, cq_ref, ckp_ref, ckn_ref, o_ref,
                       s_ref, m_ref, l_ref, acc_ref, *, n, heads):
    kt = pl.program_id(1)

    @pl.when(kt == 0)
    def _():
        m_ref[...] = jnp.full(m_ref.shape, NEG_INF, F32)
        l_ref[...] = jnp.zeros_like(l_ref)
        acc_ref[...] = jnp.zeros_like(acc_ref)

    def absorb(h, s, v_t, ck, mask):
        s = s + (cq_ref[0, h] - ck)
        if mask is not None:
            s = jnp.where(mask, s, NEG_INF)
        m = m_ref[h]
        m_new = jnp.maximum(m, jnp.max(s, axis=-1, keepdims=True))
        alpha = jnp.exp2(m - m_new)
        p = jnp.exp2(s - m_new)
        l_ref[h] = alpha * l_ref[h] + jnp.sum(p, axis=-1, keepdims=True)
        acc_ref[h] = alpha * acc_ref[h] + _dot_nt(p.astype(BF16), v_t)
        m_ref[h] = m_new

    def scores(h, _=None):
        s_ref[h] = _dot(q_ref[0, h], kct_ref[0, h].astype(BF16))

    scores(0)
    scores(1)
    for h in range(heads):
        _pipeline_ahead(scores, h + 2, heads, None, None)
        absorb(h, s_ref[h], vct_ref[0, h].astype(BF16), ckp_ref[0, h:h + 1, :], None)

    @pl.when(kt == pl.num_programs(1) - 1)
    def _():
        row = lax.broadcasted_iota(jnp.int32, (n, n), 0)
        col = lax.broadcasted_iota(jnp.int32, (n, n), 1)
        for h in range(heads):
            absorb(h, _dot(q_ref[0, h], knt_ref[0, h]), vnt_ref[0, h], ckn_ref[0, h:h + 1, :], col <= row)
            o_ref[0, h] = (acc_ref[h] / l_ref[h]).astype(o_ref.dtype)


def _fox_sample(q, knt, vnt, kct, vct, cq, ckp, ckn, tk):
    DB, H, n, dh = q.shape
    P = kct.shape[-1]
    per_b = lambda a: pl.BlockSpec((1,) + a.shape[1:], lambda b, j: (b,) + (0,) * (a.ndim - 1))
    cache = pl.BlockSpec((1, H, dh, tk), lambda b, j: (b, 0, 0, j))
    return pl.pallas_call(
        functools.partial(_fox_sample_kernel, n=n, heads=H),
        grid=(DB, P // tk),
        in_specs=[per_b(q), per_b(knt), per_b(vnt), cache, cache, per_b(cq),
                  pl.BlockSpec((1, H, tk), lambda b, j: (b, 0, j)), per_b(ckn)],
        out_specs=per_b(q),
        out_shape=jax.ShapeDtypeStruct(q.shape, BF16),
        scratch_shapes=[pltpu.VMEM((H, n, tk), F32), pltpu.VMEM((H, n, 1), F32), pltpu.VMEM((H, n, 1), F32),
                        pltpu.VMEM((H, n, dh), F32)],
        compiler_params=_cparams("parallel", "arbitrary"),
        name="fox_sample",
    )(q, knt, vnt, kct, vct, cq, ckp, ckn)


def _sb_sample_kernel(q_ref, knt_ref, vnt_ref, kct_ref, vct_ref, o_ref, z_ref, run_ref, acc_ref, *, n, heads, sub):
    kt = pl.program_id(1)
    upper = _strict_upper(sub)

    def absorb(h, z, v_t, upper_m, width, mask=None):
        log_beta, log_rest = _log2_sigmoid_pair(z)
        if mask is not None:
            log_beta = jnp.where(mask, log_beta, NEG_INF)
            log_rest = jnp.where(mask, log_rest, 0.0)
        hi, lo = _split_bf16(log_rest, 2)
        run = run_ref[h]
        parts = []
        for c in reversed(range(z.shape[1] // width)):
            keys = slice(c * width, (c + 1) * width)
            later = _dot(hi[:, keys], upper_m) + _dot(lo[:, keys], upper_m)
            parts.append(jnp.exp2(log_beta[:, keys] + later + run).astype(BF16))
            run = run + jnp.sum(log_rest[:, keys], axis=-1, keepdims=True)
        a = parts[0] if len(parts) == 1 else jnp.concatenate(parts[::-1], axis=1)
        acc_ref[h] += _dot_nt(a, v_t)
        run_ref[h] = run

    def logits(h, _=None):
        z_ref[h] = _dot(q_ref[0, h], kct_ref[0, h].astype(BF16))

    @pl.when(kt == 0)
    def _():
        row = lax.broadcasted_iota(jnp.int32, (n, n), 0)
        col = lax.broadcasted_iota(jnp.int32, (n, n), 1)
        upper_n = _strict_upper(n)
        run_ref[...] = jnp.zeros_like(run_ref)
        acc_ref[...] = jnp.zeros_like(acc_ref)
        for h in range(heads):
            absorb(h, _dot(q_ref[0, h], knt_ref[0, h]), vnt_ref[0, h], upper_n, n, col < row)

    @pl.when(jnp.max(run_ref[...]) > SB_DEAD_LOG2)
    def _():
        logits(0)
        logits(1)
        for h in range(heads):
            _pipeline_ahead(logits, h + 2, heads, None, None)
            absorb(h, z_ref[h], vct_ref[0, h].astype(BF16), upper, sub)

    @pl.when(kt == pl.num_programs(1) - 1)
    def _():
        o_ref[0] = acc_ref[...].astype(o_ref.dtype)


def _sb_sample(q, knt, vnt, kct, vct, tk, sub):
    DB, H, n, dh = q.shape
    P = kct.shape[-1]
    nk = P // tk
    per_b = lambda a: pl.BlockSpec((1,) + a.shape[1:], lambda b, j: (b,) + (0,) * (a.ndim - 1))
    cache = pl.BlockSpec((1, H, dh, tk), lambda b, j: (b, 0, 0, nk - 1 - j))
    return pl.pallas_call(
        functools.partial(_sb_sample_kernel, n=n, heads=H, sub=sub),
        grid=(DB, nk),
        in_specs=[per_b(q), per_b(knt), per_b(vnt), cache, cache],
        out_specs=per_b(q),
        out_shape=jax.ShapeDtypeStruct(q.shape, BF16),
        scratch_shapes=[pltpu.VMEM((H, n, tk), F32), pltpu.VMEM((H, n, 1), F32), pltpu.VMEM((H, n, dh), F32)],
        compiler_params=_cparams("parallel", "arbitrary"),
        name="sb_sample",
    )(q, knt, vnt, kct, vct)


def _fox_sample_cum(lf_new, lf_past_t):
    DB, H, P = lf_past_t.shape
    n = lf_new.shape[1]
    L = -(-(P + n) // LANES) * LANES
    both = jnp.concatenate([lf_past_t.astype(F32), jnp.transpose(lf_new, (0, 2, 1)),
                            jnp.zeros((DB, H, L - P - n), F32)], axis=2)
    cum_t = _cumsum_rows(both) * LOG2E
    ckn = cum_t[:, :, P:P + n]
    return ckn[..., None], cum_t[:, :, :P], ckn


def _heads_major(a, heads, transpose_rows):
    DB, n, _ = a.shape
    a4 = a.reshape(DB, n, heads, -1)
    return jnp.transpose(a4, (0, 2, 3, 1) if transpose_rows else (0, 2, 1, 3))


ROUTER_ROWS = 48


def _layer_norm(y, g, b):
    mu = jnp.mean(y, axis=-1, keepdims=True)
    yc = y - mu
    var = jnp.mean(yc * yc, axis=-1, keepdims=True)
    return yc * lax.rsqrt(var + LN_EPS) * g + b


def _first_argmax(v, ridx):
    vmax = jnp.max(v, axis=0, keepdims=True)
    idx = jnp.min(jnp.where(v == vmax, ridx, v.shape[0]), axis=0, keepdims=True)
    return vmax, idx


def _two_part_specs(tm, width, head_tiles):
    return (pl.BlockSpec((tm, width), lambda i: (jnp.minimum(i, head_tiles - 1), 0)),
            pl.BlockSpec((tm, width), lambda i: (jnp.maximum(i - head_tiles, 0), 0)))


def _mix_out_kernel(*refs, alpha, n_groups, head_tiles, n_parts):
    o_parts, ot_ref = refs[:n_parts], refs[n_parts]
    w_ref, xh_ref, xt_ref, g_ref, b_ref, wrh_ref, wrl_ref, rb_ref, h_ref, ids_ref, wts_ref = refs[n_parts + 1:]
    in_head = pl.program_id(0) < head_tiles
    x = jnp.where(in_head, xh_ref[...], xt_ref[...])
    o_head = o_parts[0][...] if n_parts == 1 else jnp.concatenate([r[...] for r in o_parts], axis=1)
    o = jnp.where(in_head, o_head, ot_ref[...])
    h = _layer_norm(alpha * x + _dot(o, w_ref[...]), g_ref[...], b_ref[...])
    h_ref[...] = h
    hh, hl = _split_bf16(h, 2)
    R = ROUTER_ROWS
    wr = _dot_nt(wrl_ref[...], hh)
    lg = wr[:R] + (_dot_nt(wrh_ref[...], hl) + (wr[R:2 * R] + wr[2 * R:])) + rb_ref[...]
    tm = lg.shape[1]
    ridx = lax.broadcasted_iota(jnp.int32, (8, tm), 0)
    g = jnp.where(ridx < n_groups, lg[0:8], NEG_INF)
    gmax, gidx = _first_argmax(g, ridx)
    gate = 1.0 / jnp.sum(jnp.exp(g - gmax), axis=0, keepdims=True)
    esel = lg[8:16]
    for gg in range(1, n_groups):
        esel = jnp.where(gidx == gg, lg[8 + 8 * gg:16 + 8 * gg], esel)
    v1, i1 = _first_argmax(esel, ridx)
    v2, i2 = _first_argmax(jnp.where(ridx == i1, NEG_INF, esel), ridx)
    t = jnp.exp(v2 - v1)
    w1 = 1.0 / (1.0 + t)
    ids_ref[...] = jnp.where(ridx == 0, gidx * 8 + i1, jnp.where(ridx == 1, gidx * 8 + i2, 0))
    wts_ref[...] = jnp.where(ridx == 0, gate * w1, jnp.where(ridx == 1, gate * (t * w1), 0.0))


def _mix_out(o_head_parts, o_tail, w, x_head, x_tail, g, b, wrh, wrl, rb, alpha, n_groups, tm):
    D = x_head.shape[1]
    T = x_head.shape[0] + x_tail.shape[0]
    head_tiles = x_head.shape[0] // tm
    const = lambda a: pl.BlockSpec(a.shape, lambda i: (0,) * a.ndim)
    rb_t = jnp.broadcast_to(rb, (ROUTER_ROWS, tm))
    head_spec = lambda width: _two_part_specs(tm, width, head_tiles)[0]
    return pl.pallas_call(
        functools.partial(_mix_out_kernel, alpha=alpha, n_groups=n_groups, head_tiles=head_tiles,
                          n_parts=len(o_head_parts)),
        grid=(T // tm,),
        in_specs=[*[head_spec(p.shape[1]) for p in o_head_parts], _two_part_specs(tm, o_tail.shape[1], head_tiles)[1],
                  const(w), *_two_part_specs(tm, D, head_tiles), const(g), const(b), const(wrh), const(wrl),
                  const(rb_t)],
        out_specs=[pl.BlockSpec((tm, D), lambda i: (i, 0)), pl.BlockSpec((8, tm), lambda i: (0, i)),
                   pl.BlockSpec((8, tm), lambda i: (0, i))],
        out_shape=[jax.ShapeDtypeStruct((T, D), F32), jax.ShapeDtypeStruct((8, T), jnp.int32),
                   jax.ShapeDtypeStruct((8, T), F32)],
        compiler_params=_cparams("parallel"),
        name="mix_out_ln_router",
    )(*o_head_parts, o_tail, w, x_head, x_tail, g, b, wrh, wrl, rb_t)


def _prep_router(w_group, b_group, w_router, b_router):
    D, G = w_group.shape
    E = w_router.shape[-1]
    wr = jnp.zeros((ROUTER_ROWS, D), F32)
    wr = wr.at[:G].set(w_group.T.astype(F32))
    wr = wr.at[8:8 + G * E].set(jnp.transpose(w_router, (0, 2, 1)).reshape(G * E, D).astype(F32))
    rb = jnp.zeros((ROUTER_ROWS, 1), F32)
    rb = rb.at[:G, 0].set(b_group.astype(F32)).at[8:8 + G * E, 0].set(b_router.reshape(-1).astype(F32))
    hi, mid, lo = _split_bf16_trunc(wr, 3)
    return hi, jnp.concatenate([hi, mid, lo], axis=0), rb


def _gather_rows(idx_ref, lo, hi, src_hbm, dst, sem):
    for r in range(lo, hi):
        pltpu.make_async_copy(src_hbm.at[pl.ds(idx_ref[0, 0, r], 1)], dst.at[pl.ds(r, 1)], sem).start()


def _wait_rows(n, src_hbm, dst, sem):
    pltpu.make_async_copy(src_hbm.at[pl.ds(0, n)], dst, sem).wait()


def _moe_experts_kernel(te_ref, tv_ref, src_ref, nxt_ref, x_hbm, wg_ref, wu_ref, wd_ref, y_ref, xbuf, sem, *, tm):
    i = pl.program_id(0)
    nt = pl.num_programs(0)
    slot = i % 2

    @pl.when(jnp.logical_and(i == 0, tv_ref[0] > 0))
    def _():
        _gather_rows(src_ref, 0, tm, x_hbm, xbuf.at[0], sem.at[0])

    @pl.when(jnp.logical_and(i + 1 < nt, tv_ref[jnp.minimum(i + 1, nt - 1)] > 0))
    def _():
        _gather_rows(nxt_ref, 0, tm, x_hbm, xbuf.at[1 - slot], sem.at[1 - slot])

    @pl.when(tv_ref[i] > 0)
    def _():
        _wait_rows(tm, x_hbm, xbuf.at[slot], sem.at[slot])
        xb = xbuf[slot].astype(BF16)
        a = _dot(xb, wg_ref[0].astype(BF16))
        u = _dot(xb, wu_ref[0].astype(BF16))
        hid = (a / (1.0 + jnp.exp(-a))) * u
        y_ref[...] = _dot(hid.astype(BF16), wd_ref[0].astype(BF16))

    @pl.when(tv_ref[i] == 0)
    def _():
        y_ref[...] = jnp.zeros_like(y_ref)


def _moe_experts(x, w_gate, w_up, w_down, tile_expert, tile_valid, src, tm):
    T, D = x.shape
    F = w_gate.shape[-1]
    NT = tile_expert.shape[0]
    grid_spec = pltpu.PrefetchScalarGridSpec(
        num_scalar_prefetch=2,
        grid=(NT,),
        in_specs=[pl.BlockSpec((1, 1, tm), lambda i, te, tv: (i, 0, 0), memory_space=pltpu.SMEM),
                  pl.BlockSpec((1, 1, tm), lambda i, te, tv: (i + 1, 0, 0), memory_space=pltpu.SMEM),
                  pl.BlockSpec(memory_space=pl.ANY),
                  pl.BlockSpec((1, D, F), lambda i, te, tv: (te[i], 0, 0)),
                  pl.BlockSpec((1, D, F), lambda i, te, tv: (te[i], 0, 0)),
                  pl.BlockSpec((1, F, D), lambda i, te, tv: (te[i], 0, 0))],
        out_specs=pl.BlockSpec((tm, D), lambda i, te, tv: (i, 0)),
        scratch_shapes=[pltpu.VMEM((2, tm, D), F32), pltpu.SemaphoreType.DMA((2,))],
    )
    return pl.pallas_call(
        functools.partial(_moe_experts_kernel, tm=tm),
        grid_spec=grid_spec,
        out_shape=jax.ShapeDtypeStruct((NT * tm, D), F32),
        compiler_params=_cparams("arbitrary"),
        name="moe_experts",
    )(tile_expert, tile_valid, src, src, x, w_gate, w_up, w_down)


def _moe_combine_kernel(pos_ref, nxt_ref, ys_hbm, h_ref, w_ref, g_ref, b_ref, oh_ref, ot_ref, buf, sem,
                        *, tm, alpha, head_tiles):
    i = pl.program_id(0)
    nt = pl.num_programs(0)
    slot = i % 2

    @pl.when(i == 0)
    def _():
        _gather_rows(pos_ref, 0, 2 * tm, ys_hbm, buf.at[0], sem.at[0])

    @pl.when(i + 1 < nt)
    def _():
        _gather_rows(nxt_ref, 0, 2 * tm, ys_hbm, buf.at[1 - slot], sem.at[1 - slot])

    _wait_rows(2 * tm, ys_hbm, buf.at[slot], sem.at[slot])
    w = w_ref[...]
    y = _layer_norm(alpha * h_ref[...] + (w[:, 0:1] * buf[slot, 0:tm] + w[:, 1:2] * buf[slot, tm:2 * tm]),
                    g_ref[...], b_ref[...])

    @pl.when(i < head_tiles)
    def _():
        oh_ref[...] = y

    @pl.when(i >= head_tiles)
    def _():
        ot_ref[...] = y


def _moe_combine(ys, h, pos, wts, g, b, alpha, tm, head_rows):
    T, D = h.shape
    nt = T // tm
    head_tiles = head_rows // tm
    const = lambda a: pl.BlockSpec(a.shape, lambda i: (0,) * a.ndim)
    return pl.pallas_call(
        functools.partial(_moe_combine_kernel, tm=tm, alpha=alpha, head_tiles=head_tiles),
        grid=(nt,),
        in_specs=[pl.BlockSpec((1, 1, 2 * tm), lambda i: (i, 0, 0), memory_space=pltpu.SMEM),
                  pl.BlockSpec((1, 1, 2 * tm), lambda i: (i + 1, 0, 0), memory_space=pltpu.SMEM),
                  pl.BlockSpec(memory_space=pl.ANY),
                  pl.BlockSpec((tm, D), lambda i: (i, 0)),
                  pl.BlockSpec((tm, 2), lambda i: (i, 0)), const(g), const(b)],
        out_specs=list(_two_part_specs(tm, D, head_tiles)),
        out_shape=[jax.ShapeDtypeStruct((head_rows, D), F32), jax.ShapeDtypeStruct((T - head_rows, D), F32)],
        scratch_shapes=[pltpu.VMEM((2, 2 * tm, D), F32), pltpu.SemaphoreType.DMA((2,))],
        compiler_params=_cparams("arbitrary"),
        name="moe_combine_ln",
    )(pos, pos, ys, h, wts, g, b)


def _route(ids, n_experts, tm):
    T = ids.shape[1]
    flat = ids.reshape(-1)
    iota = jnp.arange(2 * T, dtype=jnp.int32)
    sorted_e, order = lax.sort((flat, iota), num_keys=1, is_stable=True)
    _, inverse = lax.sort((order, iota), num_keys=1)
    experts = jnp.arange(n_experts, dtype=jnp.int32)
    counts = jnp.sum((flat[:, None] == experts[None, :]).astype(jnp.int32), axis=0)
    padded = (counts + tm - 1) // tm * tm
    ends = jnp.cumsum(padded)
    shift = (ends - padded) - (jnp.cumsum(counts) - counts)
    NT = (2 * T + n_experts * (tm - 1)) // tm
    tile_start = jnp.arange(NT, dtype=jnp.int32) * tm
    tile_expert = jnp.minimum(jnp.sum((tile_start[:, None] >= ends[None, :]).astype(jnp.int32), axis=1),
                              n_experts - 1)
    tile_valid = (tile_start < ends[-1]).astype(jnp.int32)
    pos = (inverse + shift[flat]).reshape(2, T)
    row = jnp.arange((NT + 1) * tm, dtype=jnp.int32)
    row_shift = jnp.repeat(shift[jnp.concatenate([tile_expert, tile_expert[-1:]])], tm)
    src = (order % T)[jnp.clip(row - row_shift, 0, 2 * T - 1)]
    return tile_expert, tile_valid, src.reshape(NT + 1, 1, tm), pos


MLA_PAIR_W = 2 * LANES
MLA_SAMPLE_CHUNK = 256
QK_NOPE = 64
QK_ROPE = 32


def _mla_proj_kernel(hh_ref, ht_ref, wdn_ref, gq_ref, gkv_ref, wq_ref, wqr_ref, wk_ref, wv_ref, cos_ref, sin_ref,
                     ckv_ref, kr_ref, qcat_ref, kcat_ref, vt_ref, *, q_lora, kv_lora, npairs, scale, head_tiles):
    h = jnp.where(pl.program_id(0) < head_tiles, hh_ref[...], ht_ref[...])
    z = _dot(h.astype(BF16), wdn_ref[...])
    cq = z[:, :q_lora]
    ckv = z[:, q_lora:q_lora + kv_lora]
    o = q_lora + kv_lora
    kr_raw = z[:, o:o + HEAD_W]
    kr_rot = z[:, o + HEAD_W:o + 2 * HEAD_W]
    cq = cq * lax.rsqrt(jnp.mean(cq * cq, axis=-1, keepdims=True) + RMS_EPS) * gq_ref[...]
    ckv = ckv * lax.rsqrt(jnp.mean(ckv * ckv, axis=-1, keepdims=True) + RMS_EPS) * gkv_ref[...]
    ckv_ref[...] = ckv
    cos = cos_ref[...]
    sin = sin_ref[...]
    kr_tile = kr_raw * cos + kr_rot * sin
    kr_ref[...] = kr_tile[:, QK_NOPE:QK_NOPE + QK_ROPE]
    cqb = cq.astype(BF16)
    ckb = ckv.astype(BF16)
    cos2 = jnp.concatenate([cos, cos], axis=1)
    sin2 = jnp.concatenate([sin, sin], axis=1)
    kr2 = jnp.concatenate([kr_tile, kr_tile], axis=1)
    for p in range(npairs):
        lanes = slice(p * MLA_PAIR_W, (p + 1) * MLA_PAIR_W)
        q = _dot(cqb, wq_ref[:, lanes]) * cos2 + _dot(cqb, wqr_ref[:, lanes]) * sin2
        qcat_ref[:, lanes] = (q * scale).astype(BF16)
        kcat_ref[:, lanes] = (_dot(ckb, wk_ref[:, lanes]) + kr2).astype(BF16)
    heads = 2 * npairs
    vt = _dot_nt(wv_ref[...], ckb).reshape(heads, HEAD_DIM, ckb.shape[0])
    vt_ref[0, :, :HEAD_DIM, :] = vt.astype(BF16)
    vt_ref[0, :, HEAD_DIM:, :] = jnp.ones((heads, VT_ROWS - HEAD_DIM, ckb.shape[0]), BF16)


def _mla_proj(h_head, h_tail, wdn, gq, gkv, wq, wqr, wk, wv, cos_t, sin_t, table_block, tm):
    D = h_head.shape[1]
    T = h_head.shape[0] + h_tail.shape[0]
    head_tiles = h_head.shape[0] // tm
    q_lora, kv_lora = gq.shape[1], gkv.shape[1]
    npairs = wq.shape[1] // MLA_PAIR_W
    const = lambda a: pl.BlockSpec(a.shape, lambda i: (0,) * a.ndim)
    row = lambda w_: pl.BlockSpec((tm, w_), lambda i: (i, 0))
    table = pl.BlockSpec((tm, HEAD_W), lambda i: (table_block(i), 0))
    return pl.pallas_call(
        functools.partial(_mla_proj_kernel, q_lora=q_lora, kv_lora=kv_lora, npairs=npairs,
                          scale=(QK_NOPE + QK_ROPE) ** -0.5 * LOG2E, head_tiles=head_tiles),
        grid=(T // tm,),
        in_specs=[*_two_part_specs(tm, D, head_tiles), const(wdn), const(gq), const(gkv), const(wq), const(wqr),
                  const(wk), const(wv), table, table],
        out_specs=[row(kv_lora), row(QK_ROPE), row(wq.shape[1]), row(wk.shape[1]),
                   pl.BlockSpec((1, 2 * npairs, VT_ROWS, tm), lambda i: (i, 0, 0, 0))],
        out_shape=[jax.ShapeDtypeStruct((T, kv_lora), F32), jax.ShapeDtypeStruct((T, QK_ROPE), F32),
                   jax.ShapeDtypeStruct((T, wq.shape[1]), BF16), jax.ShapeDtypeStruct((T, wk.shape[1]), BF16),
                   jax.ShapeDtypeStruct((T // tm, 2 * npairs, VT_ROWS, tm), BF16)],
        compiler_params=_cparams("parallel"),
        name="mla_proj",
    )(h_head, h_tail, wdn, gq, gkv, wq, wqr, wk, wv, cos_t, sin_t)


def _rot_half(w):
    half = w.shape[-1] // 2
    return jnp.concatenate([-w[..., half:], w[..., :half]], axis=-1)


def _prep_mla_weights(w_down, w_uq, w_ukv, heads, q_lora, kv_lora):
    D = w_down.shape[0]
    tail = HEAD_W - QK_NOPE - QK_ROPE
    w_kr = w_down[:, q_lora + kv_lora:]
    slot = lambda w: jnp.concatenate([jnp.zeros((D, QK_NOPE), w.dtype), w, jnp.zeros((D, tail), w.dtype)], axis=1)
    wdn = jnp.concatenate([w_down[:, :q_lora + kv_lora], slot(w_kr), slot(_rot_half(w_kr))], axis=1)
    wq3 = w_uq.reshape(q_lora, heads, QK_NOPE + QK_ROPE)
    nope, ropew = wq3[..., :QK_NOPE], wq3[..., QK_NOPE:]
    zpad = jnp.zeros((q_lora, heads, tail), w_uq.dtype)
    wq = jnp.concatenate([nope, ropew, zpad], axis=-1)
    wqr = jnp.concatenate([jnp.zeros_like(nope), _rot_half(ropew), zpad], axis=-1)
    wkv3 = w_ukv.reshape(kv_lora, heads, QK_NOPE + HEAD_DIM)
    w_uk, w_uv = wkv3[..., :QK_NOPE], wkv3[..., QK_NOPE:]
    wk = jnp.concatenate([w_uk, jnp.zeros((kv_lora, heads, HEAD_W - QK_NOPE), w_ukv.dtype)], axis=-1)
    wv = jnp.transpose(w_uv.reshape(kv_lora, heads * HEAD_DIM))
    b16 = lambda a: a.astype(BF16)
    return (b16(wdn), b16(wq.reshape(q_lora, -1)), b16(wqr.reshape(q_lora, -1)), b16(wk.reshape(kv_lora, -1)),
            b16(wv), b16(jnp.transpose(w_uk, (1, 2, 0))), b16(jnp.transpose(w_uv, (1, 0, 2))))


def _rope_tables(pos):
    half = QK_ROPE // 2
    inv_freq = ROPE_BASE ** (-jnp.arange(half, dtype=F32) / half)
    ang = pos.astype(F32)[:, None] * inv_freq[None, :]
    n = pos.shape[0]
    pad = jnp.zeros((n, HEAD_W - QK_NOPE - QK_ROPE), F32)
    cos = jnp.concatenate([jnp.ones((n, QK_NOPE), F32)] + [jnp.cos(ang)] * 2 + [pad], axis=1)
    sin = jnp.concatenate([jnp.zeros((n, QK_NOPE), F32)] + [jnp.sin(ang)] * 2 + [pad], axis=1)
    return cos, sin


def _mla_sample_queries(qcat, heads):
    DB, n, _ = qcat.shape
    q4 = qcat.reshape(DB, n, heads, HEAD_W)
    rows = lambda a: jnp.transpose(a, (0, 2, 1, 3)).reshape(DB, heads * n, a.shape[-1])
    return rows(q4[..., :QK_NOPE]), rows(q4[..., QK_NOPE:QK_NOPE + QK_ROPE])


def _mla_sample_kernel(qn_ref, qr_ref, wuk_ref, wuv_ref, cc_ref, rc_ref, cn_ref, rn_ref, o_ref,
                       qlat_ref, s_ref, m_ref, l_ref, acc_ref, *, n, heads):
    kt = pl.program_id(1)

    @pl.when(kt == 0)
    def _():
        for h in range(heads):
            rows = slice(h * n, (h + 1) * n)
            qlat_ref[rows, :] = _dot(qn_ref[0, rows, :], wuk_ref[h]).astype(BF16)
        m_ref[...] = jnp.full(m_ref.shape, NEG_INF, F32)
        l_ref[...] = jnp.zeros_like(l_ref)
        acc_ref[...] = jnp.zeros_like(acc_ref)

    def scores(rows, ckv, kr_t):
        return _dot_nt(qlat_ref[rows, :], ckv) + _dot(qr_ref[0, rows, :], kr_t)

    def absorb(rows, s, ckv):
        m, l, acc = _softmax_step(s[None], ckv, (m_ref[:, rows], l_ref[:, rows], acc_ref[:, rows]))
        m_ref[:, rows] = m
        l_ref[:, rows] = l
        acc_ref[:, rows] = acc

    def update(ckv, kr_t):
        absorb(slice(None), scores(slice(None), ckv, kr_t), ckv)

    ckv = cc_ref[0].astype(BF16)
    kr_t = rc_ref[0].astype(BF16)
    chunk_rows = s_ref.shape[1]
    chunks = [slice(c * chunk_rows, (c + 1) * chunk_rows) for c in range(s_ref.shape[0])]
    s_ref[0] = scores(chunks[0], ckv, kr_t)
    for c, rows in enumerate(chunks):
        if c + 1 < len(chunks):
            s_ref[c + 1] = scores(chunks[c + 1], ckv, kr_t)
        absorb(rows, s_ref[c], ckv)

    @pl.when(kt == pl.num_programs(1) - 1)
    def _():
        update(cn_ref[0], rn_ref[0])
        o_lat = (acc_ref[0] / l_ref[0]).astype(BF16)
        for h in range(heads):
            o_ref[0, :, h * HEAD_DIM:(h + 1) * HEAD_DIM] = _dot(o_lat[h * n:(h + 1) * n], wuv_ref[h]).astype(o_ref.dtype)


def _mla_sample(qn, qr, wuk, wuv, ckv_c, kr_c, ckv_n, kr_n, n, tk):
    DB, R, _ = qn.shape
    heads = R // n
    P, C = ckv_c.shape[1], ckv_c.shape[2]
    const = lambda a: pl.BlockSpec(a.shape, lambda b, j: (0,) * a.ndim)
    per_b = lambda a: pl.BlockSpec((1,) + a.shape[1:], lambda b, j: (b, 0, 0))
    return pl.pallas_call(
        functools.partial(_mla_sample_kernel, n=n, heads=heads),
        grid=(DB, P // tk),
        in_specs=[per_b(qn), per_b(qr), const(wuk), const(wuv),
                  pl.BlockSpec((1, tk, C), lambda b, j: (b, j, 0)),
                  pl.BlockSpec((1, QK_ROPE, tk), lambda b, j: (b, 0, j)),
                  per_b(ckv_n), per_b(kr_n)],
        out_specs=pl.BlockSpec((1, n, heads * HEAD_DIM), lambda b, j: (b, 0, 0)),
        out_shape=jax.ShapeDtypeStruct((DB, n, heads * HEAD_DIM), BF16),
        scratch_shapes=[pltpu.VMEM((R, C), BF16), pltpu.VMEM((R // MLA_SAMPLE_CHUNK, MLA_SAMPLE_CHUNK, tk), F32),
                        pltpu.VMEM((1, R, 1), F32), pltpu.VMEM((1, R, 1), F32), pltpu.VMEM((1, R, C), F32)],
        compiler_params=_cparams("parallel", "arbitrary"),
        name="mla_sample",
    )(qn, qr, wuk, wuv, ckv_c, kr_c, ckv_n, kr_n)


def _moe_layer(h, ids, wts, w_gate, w_up, w_down, layer, g, b, alpha, tm, head_rows):
    T, D = h.shape
    n_experts = w_gate.shape[1] * w_gate.shape[2]
    tile_expert, tile_valid, src, pos = _route(ids[:2], n_experts, tm)
    flat3 = lambda w: w.reshape((-1,) + w.shape[3:])
    ys = _moe_experts(h, flat3(w_gate), flat3(w_up), flat3(w_down), tile_expert + layer * n_experts, tile_valid,
                      src, tm)
    pos_t = jnp.transpose(pos.reshape(2, T // tm, tm), (1, 0, 2)).reshape(T // tm, 1, 2 * tm)
    pos_t = jnp.pad(pos_t, ((0, 1), (0, 0), (0, 0)))
    return _moe_combine(ys, h, pos_t, jnp.transpose(wts[:2]), g, b, alpha, tm, head_rows)


TOKEN_TILE = 256
FLASH_Q_TILE = 512
FLASH_KEY_TILE = 256
SB_Q_TILE = 256
SB_KEY_TILE = 128
CACHE_TILE = 2048
SB_SUB_TILE = 256


def kernel(x_prompt, x_sample, cache_fox_k, cache_fox_v, cache_fox_logf, cache_sb_k, cache_sb_v, cache_mla_ckv, cache_mla_krope, ab_w_in, ab_b_forget, ab_w_out, mla_w_down, mla_g_q, mla_g_kv, mla_w_uq, mla_w_ukv, mla_w_out, moe_w_group, moe_b_group, moe_w_router, moe_b_router, moe_w_gate, moe_w_up, moe_w_down, ln_g, ln_b):
    B, S, D = x_prompt.shape
    DB, n, _ = x_sample.shape
    P = cache_fox_k.shape[2]
    TP, TS = B * S, DB * n
    depth = ln_g.shape[0]
    n_groups = moe_w_group.shape[-1]
    assert depth == 2 and ab_w_in.shape[0] == 1 and mla_w_down.shape[0] == 1
    assert S % FLASH_Q_TILE == 0 and S % SB_Q_TILE == 0 and TP % TOKEN_TILE == 0 and TS % TOKEN_TILE == 0 and TOKEN_TILE % n == 0
    assert P % CACHE_TILE == 0 and P % CHUNK == 0 and n == CHUNK
    alpha = (2 * depth) ** 0.25
    tk = CACHE_TILE

    xp, xs = x_prompt.reshape(TP, D), x_sample.reshape(TS, D)
    sample3 = lambda a: a[TP:].reshape(DB, n, -1)

    def ffn(o_head_parts, o_tail, w_out, resid, layer):
        wrh, wrl, rb = _prep_router(moe_w_group[layer], moe_b_group[layer], moe_w_router[layer], moe_b_router[layer])
        h, ids, wts = _mix_out(o_head_parts, o_tail, w_out.astype(BF16), *resid, ln_g[layer, 0][None],
                               ln_b[layer, 0][None], wrh, wrl, rb, alpha, n_groups, TOKEN_TILE)
        return _moe_layer(h, ids, wts, moe_w_gate, moe_w_up, moe_w_down, layer,
                          ln_g[layer, 1][None], ln_b[layer, 1][None], alpha, TOKEN_TILE, TP)

    fox_heads = ab_b_forget.shape[1]
    hw = (ab_w_in.shape[2] - fox_heads) // 6
    w_ab, b_forget = _prep_ab_weights(ab_w_in[0], ab_b_forget[0])
    w_tok, w_t = _prompt_ab_weights(w_ab, hw, fox_heads)
    (qa_p, ka_p, qb_p, kb_p, lf_p, kat, vat, kbt, vbt, vat16, vbt16) = _ab_proj_prompt(
        xp, w_tok, b_forget, w_t, _query_decay_ones(fox_heads), B, S, fox_heads, SB_KEY_TILE)
    ka_p = _fox_insert_decay(ka_p, lf_p, B, S, fox_heads)
    qa, ka, va, qb, kb, vb, ka16, va16, kb16, vb16, lf = _ab_proj(xs, w_ab, b_forget, hw)
    dbn = lambda a: a.reshape(DB, n, -1)
    lf_s = dbn(lf)[:, :, :fox_heads]
    cq_s, ck_past, ck_new = _fox_sample_cum(lf_s, jnp.transpose(cache_fox_logf[0], (0, 2, 1)))
    cache_t = lambda c: jnp.transpose(c[0], (0, 2, 3, 1))
    hm = lambda a, t=False: _heads_major(dbn(a), fox_heads, t)
    o_fox_s = _fox_sample(hm(qa), hm(ka16, True), hm(va16, True), cache_t(cache_fox_k), cache_t(cache_fox_v),
                          cq_s, ck_past, ck_new, tk)
    o_sb_s = _sb_sample(hm(qb), hm(kb16, True), hm(vb16, True), cache_t(cache_sb_k), cache_t(cache_sb_v),
                        tk, SB_SUB_TILE)
    tokens_major = lambda a: jnp.transpose(a, (0, 2, 1, 3)).reshape(TS, hw)
    o_tail = jnp.concatenate([tokens_major(o_fox_s), tokens_major(o_sb_s)], axis=-1)
    o_fox_p = _flash_prompt(qa_p, ka_p, vat16, B, S, FLASH_Q_TILE, fox_heads, 1, "fox_prompt")
    o_sb_p = _sb_prompt(qb_p, kb_p, vbt16, B, S, SB_Q_TILE)
    xp, xs = ffn((o_fox_p, o_sb_p), o_tail, ab_w_out[0], (xp, xs), 0)

    q_lora, kv_lora = mla_g_q.shape[1], mla_g_kv.shape[1]
    heads = mla_w_uq.shape[2] // (QK_NOPE + QK_ROPE)
    wdn, wq, wqr, wk, wv, wuk_t, wuv = _prep_mla_weights(mla_w_down[0], mla_w_uq[0], mla_w_ukv[0], heads, q_lora, kv_lora)
    tm = TOKEN_TILE
    pos = jnp.concatenate([jnp.arange(S, dtype=jnp.int32), P + jnp.arange(tm, dtype=jnp.int32) % n])
    cos_t, sin_t = _rope_tables(pos)
    blocks_per_seq, prompt_blocks = S // tm, TP // tm
    table_block = lambda i: jnp.where(i < prompt_blocks, i % blocks_per_seq, blocks_per_seq)
    assert tm == FLASH_KEY_TILE
    ckv, kr, qcat, kcat, vt = _mla_proj(xp, xs, wdn, mla_g_q[0][None], mla_g_kv[0][None], wq, wqr, wk, wv,
                                        cos_t, sin_t, table_block, tm)
    qn, qr = _mla_sample_queries(sample3(qcat), heads)
    o_s = _mla_sample(qn, qr, wuk_t, wuv, cache_mla_ckv[0], jnp.transpose(cache_mla_krope[0], (0, 2, 1)),
                      sample3(ckv).astype(BF16), jnp.transpose(sample3(kr), (0, 2, 1)).astype(BF16), n, tk)
    o_p = _flash_prompt(qcat, kcat, vt, B, S, FLASH_Q_TILE, 8, CHUNK, "mla_prompt")
    xp, xs = ffn((o_p,), o_s.reshape(TS, -1), mla_w_out[0], (xp, xs), 1)

    rows_p = lambda a: jnp.transpose(a, (0, 3, 1, 2))[None]
    rows_s = lambda a: a.reshape(1, DB, n, fox_heads, hw // fox_heads)
    pr, sr = slice(0, TP), slice(TP, TP + TS)
    return (xp.reshape(B, S, D), xs.reshape(DB, n, D),
            rows_p(kat), rows_p(vat), lf_p[:, :fox_heads].reshape(1, B, S, fox_heads), rows_p(kbt), rows_p(vbt),
            ckv[pr].reshape(1, B, S, kv_lora), kr[pr].reshape(1, B, S, QK_ROPE),
            rows_s(ka), rows_s(va), lf[:, :fox_heads].reshape(1, DB, n, fox_heads), rows_s(kb), rows_s(vb),
            ckv[sr].reshape(1, DB, n, kv_lora), kr[sr].reshape(1, DB, n, QK_ROPE))
```

```python
import functools

import jax
import jax.numpy as jnp
from jax import lax
from jax.experimental import pallas as pl
from jax.experimental.pallas import tpu as pltpu

F32 = jnp.float32
BF16 = jnp.bfloat16
NEG_INF = -1e30
LOG2E = 1.4426950408889634

LANES = 128
HEAD_DIM = 64
PAIR_W = 2 * HEAD_DIM
CHUNK = 64
LN_EPS = 1e-5
RMS_EPS = 1e-6
ROPE_BASE = 10000.0
VMEM_LIMIT = 56 * 1024 * 1024


def _cparams(*sem):
    return pltpu.CompilerParams(dimension_semantics=sem, vmem_limit_bytes=VMEM_LIMIT)


def _dot(a, b):
    return jnp.dot(a, b, preferred_element_type=F32)


def _dot_nt(a, b):
    return lax.dot_general(a, b, (((1,), (1,)), ((), ())), preferred_element_type=F32)


def _split_bf16(x, parts):
    out = []
    r = x
    for _ in range(parts):
        h = r.astype(BF16)
        out.append(h)
        r = r - h.astype(F32)
    return out


def _split_bf16_trunc(x, parts):
    out = []
    r = x
    for _ in range(parts):
        bits = lax.bitcast_convert_type(r, jnp.uint32) & jnp.uint32(0xFFFF0000)
        h = lax.bitcast_convert_type(bits, F32)
        out.append(h.astype(BF16))
        r = r - h
    return out


def _log_sigmoid(x):
    return jnp.minimum(x, 0.0) - jnp.log(1.0 + jnp.exp(-jnp.abs(x)))


def _pick_tile(n, pref, mult=8):
    t = min(pref, n)
    while n % t or t % mult:
        t -= 1
    return t


def _ab_proj_kernel(x_ref, w_ref, bf_ref, qa_ref, ka_ref, va_ref, qb_ref, kb_ref, vb_ref,
                    ka16_ref, va16_ref, kb16_ref, vb16_ref, lf_ref, *, hw, qscale):
    xb = x_ref[...].astype(BF16)

    def seg(j):
        return _dot(xb, w_ref[:, j * hw:(j + 1) * hw])

    qa_ref[...] = (seg(0) * qscale).astype(BF16)
    z = seg(1)
    ka_ref[...] = z
    ka16_ref[...] = z.astype(BF16)
    z = seg(2)
    va_ref[...] = z
    va16_ref[...] = z.astype(BF16)
    qb_ref[...] = (seg(3) * qscale).astype(BF16)
    z = seg(4)
    kb_ref[...] = z
    kb16_ref[...] = z.astype(BF16)
    z = seg(5)
    vb_ref[...] = z
    vb16_ref[...] = z.astype(BF16)
    f = _dot(xb, w_ref[:, 6 * hw:6 * hw + LANES]) + bf_ref[...]
    lf_ref[...] = _log_sigmoid(f)


def _ab_proj(x, w, bf, hw):
    T, D = x.shape
    tm = _pick_tile(T, 256)
    row = lambda w_: pl.BlockSpec((tm, w_), lambda i: (i, 0))
    f32o = jax.ShapeDtypeStruct((T, hw), F32)
    b16o = jax.ShapeDtypeStruct((T, hw), BF16)
    return pl.pallas_call(
        functools.partial(_ab_proj_kernel, hw=hw, qscale=HEAD_DIM ** -0.5 * LOG2E),
        grid=(T // tm,),
        in_specs=[row(D), pl.BlockSpec(w.shape, lambda i: (0, 0)), pl.BlockSpec(bf.shape, lambda i: (0, 0))],
        out_specs=[row(hw)] * 10 + [row(LANES)],
        out_shape=[b16o, f32o, f32o, b16o, f32o, f32o, b16o, b16o, b16o, b16o,
                   jax.ShapeDtypeStruct((T, LANES), F32)],
        compiler_params=_cparams("parallel"),
        name="ab_proj",
    )(x, w, bf)


def _ab_proj_prompt_kernel(x_ref, w_ref, bf_ref, wt_ref, qone_ref, qa_ref, ka_ref, qb_ref, kb_ref, lf_ref,
                           kat_ref, vat_ref, kbt_ref, vbt_ref, vat16_ref, vbt16_ref, *, hw, heads, qscale, sb_tk):
    xb = x_ref[...].astype(BF16)
    tm = xb.shape[0]
    lo = lax.broadcasted_iota(jnp.int32, (tm, LANES), 1) < HEAD_DIM
    zero = jnp.zeros((tm, LANES), F32)

    def seg(j):
        z = _dot(xb, w_ref[:, j * hw:(j + 1) * hw])
        odd = pltpu.roll(z, hw - HEAD_DIM, axis=1)
        cols = []
        for c in range(hw // LANES):
            g = slice(c * LANES, (c + 1) * LANES)
            cols += [jnp.where(lo, z[:, g], zero), jnp.where(lo, odd[:, g], zero)]
        return jnp.concatenate(cols, axis=1)

    def seg_t(j):
        return _dot_nt(wt_ref[j * hw:(j + 1) * hw, :], xb).reshape(heads, hw // heads, tm)

    qa_ref[...] = (seg(0) * qscale + qone_ref[...]).astype(BF16)
    ka_ref[...] = seg(1).astype(BF16)
    qb_ref[...] = (seg(2) * qscale).astype(BF16)
    kb_ref[...] = seg(3).astype(BF16)
    lf_ref[...] = _log_sigmoid(_dot(xb, w_ref[:, 4 * hw:4 * hw + LANES]) + bf_ref[...])
    kat_ref[0] = seg_t(0)
    z = seg_t(1)
    vat_ref[0] = z
    vat16_ref[0, :, :HEAD_DIM, :] = z.astype(BF16)
    vat16_ref[0, :, HEAD_DIM:, :] = jnp.ones((heads, VT_ROWS - HEAD_DIM, tm), BF16)
    kbt_ref[0] = seg_t(2)
    z = seg_t(3)
    vbt_ref[0] = z
    for c in range(tm // sb_tk):
        vbt16_ref[c] = z[:, :, c * sb_tk:(c + 1) * sb_tk].astype(BF16)


def _ab_proj_prompt(x, w_tok, bf, w_t, q_ones, B, S, heads, sb_tk):
    TP, D = x.shape
    hw = w_t.shape[0] // 4
    wide = heads * HEAD_W
    tm = FLASH_KEY_TILE
    nj = S // tm
    const = lambda a: pl.BlockSpec(a.shape, lambda b, j: (0,) * a.ndim)
    row = lambda w_: pl.BlockSpec((tm, w_), lambda b, j: (b * nj + j, 0))
    t_spec = pl.BlockSpec((1, heads, hw // heads, tm), lambda b, j: (b, 0, 0, j))
    b16 = jax.ShapeDtypeStruct((TP, wide), BF16)
    t32 = jax.ShapeDtypeStruct((B, heads, hw // heads, S), F32)
    return pl.pallas_call(
        functools.partial(_ab_proj_prompt_kernel, hw=hw, heads=heads, qscale=HEAD_DIM ** -0.5 * LOG2E, sb_tk=sb_tk),
        grid=(B, nj),
        in_specs=[row(D), const(w_tok), const(bf), const(w_t), const(q_ones)],
        out_specs=[row(wide)] * 4 + [row(LANES)] + [t_spec] * 4 + [
            pl.BlockSpec((1, heads, VT_ROWS, tm), lambda b, j: (b * nj + j, 0, 0, 0)),
            pl.BlockSpec((tm // sb_tk, heads, hw // heads, sb_tk), lambda b, j: (b * nj + j, 0, 0, 0))],
        out_shape=[b16] * 4 + [jax.ShapeDtypeStruct((TP, LANES), F32)] + [t32] * 4 + [
            jax.ShapeDtypeStruct((TP // tm, heads, VT_ROWS, tm), BF16),
            jax.ShapeDtypeStruct((TP // sb_tk, heads, hw // heads, sb_tk), BF16)],
        compiler_params=_cparams("parallel", "parallel"),
        name="ab_proj_prompt",
    )(x, w_tok, bf, w_t, q_ones)


def _cumsum_kernel(x_ref, o_ref, carry_ref, *, tl):
    @pl.when(pl.program_id(0) == 0)
    def _():
        carry_ref[...] = jnp.zeros_like(carry_ref)

    x = x_ref[...]
    rows = x.shape[0]
    r = lax.broadcasted_iota(jnp.int32, (tl, tl), 0)
    c = lax.broadcasted_iota(jnp.int32, (tl, tl), 1)
    upper = (r <= c).astype(BF16)
    parts = jnp.concatenate(_split_bf16(x, 4), axis=0)
    y = _dot(parts, upper)
    cum = (y[0:rows] + y[rows:2 * rows]) + (y[2 * rows:3 * rows] + y[3 * rows:]) + carry_ref[:, 0:1]
    o_ref[...] = cum
    carry_ref[...] = jnp.broadcast_to(cum[:, tl - 1:tl], carry_ref.shape)


def _cumsum_rows(x):
    B, H, L = x.shape
    tl = _pick_tile(L, 512, LANES)
    out = pl.pallas_call(
        functools.partial(_cumsum_kernel, tl=tl),
        grid=(L // tl,),
        in_specs=[pl.BlockSpec((B * H, tl), lambda j: (0, j))],
        out_specs=pl.BlockSpec((B * H, tl), lambda j: (0, j)),
        out_shape=jax.ShapeDtypeStruct((B * H, L), F32),
        scratch_shapes=[pltpu.VMEM((B * H, LANES), F32)],
        compiler_params=_cparams("arbitrary"),
        name="cumsum_rows",
    )(x.reshape(B * H, L))
    return out.reshape(B, H, L)


def _softmax_step(s, vb, carry):
    m, l, acc = carry
    two, tq, tk = s.shape
    m_new = jnp.maximum(m, jnp.max(s, axis=-1, keepdims=True))
    alpha = jnp.exp2(m - m_new)
    p = jnp.exp2(s - m_new)
    l = alpha * l + jnp.sum(p, axis=-1, keepdims=True)
    pv = _dot(p.reshape(two * tq, tk).astype(BF16), vb).reshape(two, tq, vb.shape[-1])
    return m_new, l, alpha * acc + pv


def _log2_sigmoid_pair(z2):
    l1 = jnp.log2(1.0 + jnp.exp2(-jnp.abs(z2)))
    return jnp.minimum(z2, 0.0) - l1, jnp.minimum(-z2, 0.0) - l1


def _strict_upper(tk):
    r = lax.broadcasted_iota(jnp.int32, (tk, tk), 0)
    c = lax.broadcasted_iota(jnp.int32, (tk, tk), 1)
    return (r > c).astype(BF16)


HEAD_W = LANES
VT_ROWS = HEAD_DIM + 16
SB_DEAD_LOG2 = -160.0


def _pipeline_ahead(stage, first, count, cur, nxt):
    if first < count:
        stage(first, cur)
    elif nxt is not None:
        stage(first - count, nxt)


def _flash_prompt_kernel(q_ref, k_ref, vt_ref, o_ref, s_ref, m_ref, acc_ref, *, tq, tk, heads, chunk):
    i = pl.program_id(2)
    sub = tq // tk
    key = lax.broadcasted_iota(jnp.int32, (tk, tq), 0)
    query = lax.broadcasted_iota(jnp.int32, (tk, tq), 1)
    m_ref[...] = jnp.full(m_ref.shape, NEG_INF, F32)
    acc_ref[...] = jnp.zeros_like(acc_ref)

    def scores(h, kt):
        rows = pl.ds(pl.multiple_of(kt * tk, tk), tk)
        lanes = slice(h * HEAD_W, (h + 1) * HEAD_W)
        s_ref[h] = _dot_nt(k_ref[rows, lanes], q_ref[:, lanes])

    def absorb(h, kt, visible):
        s_t = s_ref[h]
        if visible is not None:
            s_t = jnp.where(visible, s_t, NEG_INF)
        m = m_ref[h]
        m_new = jnp.maximum(m, jnp.max(s_t, axis=0, keepdims=True))
        p_t = jnp.exp2(s_t - m_new).astype(BF16)
        pv = _dot(vt_ref[kt, h], p_t)
        acc_ref[h] = jnp.exp2(m - m_new) * acc_ref[h] + pv
        m_ref[h] = m_new

    scores(0, 0)
    scores(1, 0)

    def body(kt, carry):
        for h in range(heads):
            _pipeline_ahead(scores, h + 2, heads, kt, kt + 1)
            absorb(h, kt, None)
        return carry

    lax.fori_loop(0, sub * i, body, 0)
    for s in range(sub):
        kt = sub * i + s
        visible = ((s * tk + key) // chunk) <= (query // chunk)
        for h in range(heads):
            _pipeline_ahead(scores, h + 2, heads, kt, kt + 1 if s + 1 < sub else None)
            absorb(h, kt, visible)
    for j in range(heads // 2):
        a0, a1 = acc_ref[2 * j], acc_ref[2 * j + 1]
        o_t = jnp.concatenate([a0[:HEAD_DIM] / a0[HEAD_DIM:HEAD_DIM + 1],
                               a1[:HEAD_DIM] / a1[HEAD_DIM:HEAD_DIM + 1]], axis=0)
        o_ref[:, j * PAIR_W:(j + 1) * PAIR_W] = jnp.transpose(o_t).astype(o_ref.dtype)


def _flash_prompt(qx, kx, vt4, B, S, tq, heads_per_step, chunk, name):
    W = qx.shape[1]
    nq = S // tq
    hs = heads_per_step
    tk = vt4.shape[3]
    nk = S // tk
    return pl.pallas_call(
        functools.partial(_flash_prompt_kernel, tq=tq, tk=tk, heads=hs, chunk=chunk),
        grid=(B, W // (HEAD_W * hs), S // tq),
        in_specs=[pl.BlockSpec((tq, HEAD_W * hs), lambda b, g, i: (b * nq + i, g)),
                  pl.BlockSpec((S, HEAD_W * hs), lambda b, g, i: (b, g)),
                  pl.BlockSpec((nk, hs, VT_ROWS, tk), lambda b, g, i: (b, g, 0, 0))],
        out_specs=pl.BlockSpec((tq, HEAD_DIM * hs), lambda b, g, i: (b * nq + i, g)),
        out_shape=jax.ShapeDtypeStruct((B * S, W // HEAD_W * HEAD_DIM), BF16),
        scratch_shapes=[pltpu.VMEM((hs, tk, tq), F32), pltpu.VMEM((hs, 1, tq), F32),
                        pltpu.VMEM((hs, VT_ROWS, tq), F32)],
        compiler_params=_cparams("parallel", "parallel", "arbitrary"),
        name=name,
    )(qx, kx, vt4)


def _sb_prompt_kernel(q_ref, k_ref, vt_ref, o_ref, z_ref, lw_ref, tot_ref, run_ref, acc_ref, *, tq, tk, heads):
    i = pl.program_id(1)
    sub = tq // tk
    key = lax.broadcasted_iota(jnp.int32, (tk, tq), 0)
    query = lax.broadcasted_iota(jnp.int32, (tk, tq), 1)
    r = lax.broadcasted_iota(jnp.int32, (tk, tk), 0)
    c = lax.broadcasted_iota(jnp.int32, (tk, tk), 1)
    after = (c > r).astype(BF16)
    run_ref[...] = jnp.zeros_like(run_ref)
    acc_ref[...] = jnp.zeros_like(acc_ref)

    def logits(h, kt):
        rows = pl.ds(pl.multiple_of(kt * tk, tk), tk)
        lanes = slice(h * HEAD_W, (h + 1) * HEAD_W)
        z_ref[h] = _dot_nt(k_ref[rows, lanes], q_ref[:, lanes])

    def log_weights(before, h, kt):
        log_beta, log_rest = _log2_sigmoid_pair(z_ref[h])
        if before is not None:
            log_beta = jnp.where(before, log_beta, NEG_INF)
            log_rest = jnp.where(before, log_rest, 0.0)
        hi, lo = _split_bf16(log_rest, 2)
        later = _dot(after, hi) + _dot(after, lo)
        lw_ref[h] = log_beta + later
        tot_ref[h] = later[0:1] + log_rest[0:1]

    def accumulate(h, kt):
        run = run_ref[h]
        a_t = jnp.exp2(lw_ref[h] + run).astype(BF16)
        acc_ref[h] += _dot(vt_ref[kt, h], a_t)
        run_ref[h] = run + tot_ref[h]

    def step(kt, masked, nxt, nxt_masked):
        for h in range(heads):
            _pipeline_ahead(logits, h + 2, heads, kt, nxt)
            if h + 1 < heads:
                log_weights(masked, h + 1, kt)
            elif nxt is not None:
                log_weights(nxt_masked, 0, nxt)
            accumulate(h, kt)

    unmasked = sub * i
    first = unmasked + sub - 1
    masks = [((sub - 1 - s) * tk + key) < query for s in range(sub)]
    logits(0, first)
    logits(1, first)
    log_weights(masks[0], 0, first)
    for s in range(sub):
        kt = first - s
        if s + 1 < sub:
            step(kt, masks[s], kt - 1, masks[s + 1])
        else:
            step(kt, masks[s], jnp.maximum(kt - 1, 0), None)

    def alive():
        return (jnp.max(run_ref[...]) > SB_DEAD_LOG2).astype(jnp.int32)

    def body(carry):
        kt, _ = carry
        step(kt, None, jnp.maximum(kt - 1, 0), None)
        return kt - 1, alive()

    lax.while_loop(lambda c: jnp.logical_and(c[0] >= 0, c[1] > 0), body, (unmasked - 1, alive()))

    for j in range(heads // 2):
        o_t = jnp.concatenate([acc_ref[2 * j], acc_ref[2 * j + 1]], axis=0)
        o_ref[:, j * PAIR_W:(j + 1) * PAIR_W] = jnp.transpose(o_t).astype(o_ref.dtype)


def _sb_prompt(qx, kx, vt4, B, S, tq):
    W = qx.shape[1]
    heads = W // HEAD_W
    tk = vt4.shape[3]
    nq = S // tq
    return pl.pallas_call(
        functools.partial(_sb_prompt_kernel, tq=tq, tk=tk, heads=heads),
        grid=(B, nq),
        in_specs=[pl.BlockSpec((tq, W), lambda b, i: (b * nq + i, 0)),
                  pl.BlockSpec((S, W), lambda b, i: (b, 0)),
                  pl.BlockSpec((S // tk, heads, HEAD_DIM, tk), lambda b, i: (b, 0, 0, 0))],
        out_specs=pl.BlockSpec((tq, heads * HEAD_DIM), lambda b, i: (b * nq + i, 0)),
        out_shape=jax.ShapeDtypeStruct((B * S, heads * HEAD_DIM), BF16),
        scratch_shapes=[pltpu.VMEM((heads, tk, tq), F32), pltpu.VMEM((heads, tk, tq), F32),
                        pltpu.VMEM((heads, 1, tq), F32), pltpu.VMEM((heads, 1, tq), F32),
                        pltpu.VMEM((heads, HEAD_DIM, tq), F32)],
        compiler_params=_cparams("parallel", "arbitrary"),
        name="sb_prompt",
    )(qx, kx, vt4)


def _prep_ab_weights(w_in, b_f):
    D = w_in.shape[0]
    H = b_f.shape[0]
    hw = (w_in.shape[1] - H) // 6
    main = jnp.concatenate([w_in[:, :3 * hw], w_in[:, 3 * hw + H:]], axis=1)
    wf = jnp.zeros((D, LANES), w_in.dtype).at[:, :H].set(w_in[:, 3 * hw:3 * hw + H])
    bf = jnp.zeros((1, LANES), F32).at[0, :H].set(b_f.astype(F32))
    return jnp.concatenate([main, wf], axis=1).astype(BF16), bf


def _prompt_ab_weights(w_ab, hw, heads):
    seg = lambda j: w_ab[:, j * hw:(j + 1) * hw]
    w_tok = jnp.concatenate([seg(0), seg(1), seg(3), seg(4), w_ab[:, 6 * hw:]], axis=1)
    w_t = jnp.transpose(jnp.concatenate([seg(1), seg(2), seg(4), seg(5)], axis=1))
    return w_tok, w_t


DECAY_TERMS = 3


def _insert_decay_kernel(k_ref, c_ref, sel_ref, o_ref):
    terms = sum(_dot(part, sel_ref[j]) for j, part in enumerate(_split_bf16(c_ref[...], DECAY_TERMS)))
    o_ref[...] = (k_ref[...].astype(F32) + terms).astype(BF16)


def _fox_insert_decay(kx, lf, B, S, H):
    lt = jnp.transpose(lf[:, :H].reshape(B, S, H), (0, 2, 1))
    cum = jnp.transpose(_cumsum_rows(lt), (0, 2, 1)).reshape(B * S, H) * (-LOG2E)
    c = jnp.pad(cum, ((0, 0), (0, LANES - H)))
    src = jnp.arange(LANES)[None, :, None]
    dst = jnp.arange(H * HEAD_W)[None, None, :]
    term = jnp.arange(DECAY_TERMS)[:, None, None]
    sel = jnp.logical_and(src < H, dst == src * HEAD_W + HEAD_DIM + term).astype(BF16)
    tm = _pick_tile(B * S, 512)
    return pl.pallas_call(
        _insert_decay_kernel,
        grid=(B * S // tm,),
        in_specs=[pl.BlockSpec((tm, H * HEAD_W), lambda i: (i, 0)), pl.BlockSpec((tm, LANES), lambda i: (i, 0)),
                  pl.BlockSpec(sel.shape, lambda i: (0, 0, 0))],
        out_specs=pl.BlockSpec((tm, H * HEAD_W), lambda i: (i, 0)),
        out_shape=jax.ShapeDtypeStruct(kx.shape, BF16),
        input_output_aliases={0: 0},
        compiler_params=_cparams("parallel"),
        name="fox_insert_decay",
    )(kx, c, sel)


def _query_decay_ones(H):
    lane = jnp.arange(H * HEAD_W) % HEAD_W
    return jnp.logical_and(lane >= HEAD_DIM, lane < HEAD_DIM + DECAY_TERMS).astype(F32)[None]


def _fox_sample_kernel(q_ref, knt_ref, vnt_ref, kct_ref, vct_ref, cq_ref, ckp_ref, ckn_ref, o_ref,
                       s_ref, m_ref, l_ref, acc_ref, *, n, heads):
    kt = pl.program_id(1)

    @pl.when(kt == 0)
    def _():
        m_ref[...] = jnp.full(m_ref.shape, NEG_INF, F32)
        l_ref[...] = jnp.zeros_like(l_ref)
        acc_ref[...] = jnp.zeros_like(acc_ref)

    def absorb(h, s, v_t, ck, mask):
        s = s + (cq_ref[0, h] - ck)
        if mask is not None:
            s = jnp.where(mask, s, NEG_INF)
        m = m_ref[h]
        m_new = jnp.maximum(m, jnp.max(s, axis=-1, keepdims=True))
        alpha = jnp.exp2(m - m_new)
        p = jnp.exp2(s - m_new)
        l_ref[h] = alpha * l_ref[h] + jnp.sum(p, axis=-1, keepdims=True)
        acc_ref[h] = alpha * acc_ref[h] + _dot_nt(p.astype(BF16), v_t)
        m_ref[h] = m_new

    def scores(h, _=None):
        s_ref[h] = _dot(q_ref[0, h], kct_ref[0, h].astype(BF16))

    scores(0)
    scores(1)
    for h in range(heads):
        _pipeline_ahead(scores, h + 2, heads, None, None)
        absorb(h, s_ref[h], vct_ref[0, h].astype(BF16), ckp_ref[0, h:h + 1, :], None)

    @pl.when(kt == pl.num_programs(1) - 1)
    def _():
        row = lax.broadcasted_iota(jnp.int32, (n, n), 0)
        col = lax.broadcasted_iota(jnp.int32, (n, n), 1)
        for h in range(heads):
            absorb(h, _dot(q_ref[0, h], knt_ref[0, h]), vnt_ref[0, h], ckn_ref[0, h:h + 1, :], col <= row)
            o_ref[0, h] = (acc_ref[h] / l_ref[h]).astype(o_ref.dtype)


def _fox_sample(q, knt, vnt, kct, vct, cq, ckp, ckn, tk):
    DB, H, n, dh = q.shape
    P = kct.shape[-1]
    per_b = lambda a: pl.BlockSpec((1,) + a.shape[1:], lambda b, j: (b,) + (0,) * (a.ndim - 1))
    cache = pl.BlockSpec((1, H, dh, tk), lambda b, j: (b, 0, 0, j))
    return pl.pallas_call(
        functools.partial(_fox_sample_kernel, n=n, heads=H),
        grid=(DB, P // tk),
        in_specs=[per_b(q), per_b(knt), per_b(vnt), cache, cache, per_b(cq),
                  pl.BlockSpec((1, H, tk), lambda b, j: (b, 0, j)), per_b(ckn)],
        out_specs=per_b(q),
        out_shape=jax.ShapeDtypeStruct(q.shape, BF16),
        scratch_shapes=[pltpu.VMEM((H, n, tk), F32), pltpu.VMEM((H, n, 1), F32), pltpu.VMEM((H, n, 1), F32),
                        pltpu.VMEM((H, n, dh), F32)],
        compiler_params=_cparams("parallel", "arbitrary"),
        name="fox_sample",
    )(q, knt, vnt, kct, vct, cq, ckp, ckn)


def _sb_sample_kernel(q_ref, knt_ref, vnt_ref, kct_ref, vct_ref, o_ref, z_ref, run_ref, acc_ref, *, n, heads, sub):
    kt = pl.program_id(1)
    upper = _strict_upper(sub)

    def absorb(h, z, v_t, upper_m, width, mask=None):
        log_beta, log_rest = _log2_sigmoid_pair(z)
        if mask is not None:
            log_beta = jnp.where(mask, log_beta, NEG_INF)
            log_rest = jnp.where(mask, log_rest, 0.0)
        hi, lo = _split_bf16(log_rest, 2)
        run = run_ref[h]
        parts = []
        for c in reversed(range(z.shape[1] // width)):
            keys = slice(c * width, (c + 1) * width)
            later = _dot(hi[:, keys], upper_m) + _dot(lo[:, keys], upper_m)
            parts.append(jnp.exp2(log_beta[:, keys] + later + run).astype(BF16))
            run = run + jnp.sum(log_rest[:, keys], axis=-1, keepdims=True)
        a = parts[0] if len(parts) == 1 else jnp.concatenate(parts[::-1], axis=1)
        acc_ref[h] += _dot_nt(a, v_t)
        run_ref[h] = run

    def logits(h, _=None):
        z_ref[h] = _dot(q_ref[0, h], kct_ref[0, h].astype(BF16))

    @pl.when(kt == 0)
    def _():
        row = lax.broadcasted_iota(jnp.int32, (n, n), 0)
        col = lax.broadcasted_iota(jnp.int32, (n, n), 1)
        upper_n = _strict_upper(n)
        run_ref[...] = jnp.zeros_like(run_ref)
        acc_ref[...] = jnp.zeros_like(acc_ref)
        for h in range(heads):
            absorb(h, _dot(q_ref[0, h], knt_ref[0, h]), vnt_ref[0, h], upper_n, n, col < row)

    @pl.when(jnp.max(run_ref[...]) > SB_DEAD_LOG2)
    def _():
        logits(0)
        logits(1)
        for h in range(heads):
            _pipeline_ahead(logits, h + 2, heads, None, None)
            absorb(h, z_ref[h], vct_ref[0, h].astype(BF16), upper, sub)

    @pl.when(kt == pl.num_programs(1) - 1)
    def _():
        o_ref[0] = acc_ref[...].astype(o_ref.dtype)


def _sb_sample(q, knt, vnt, kct, vct, tk, sub):
    DB, H, n, dh = q.shape
    P = kct.shape[-1]
    nk = P // tk
    per_b = lambda a: pl.BlockSpec((1,) + a.shape[1:], lambda b, j: (b,) + (0,) * (a.ndim - 1))
    cache = pl.BlockSpec((1, H, dh, tk), lambda b, j: (b, 0, 0, nk - 1 - j))
    return pl.pallas_call(
        functools.partial(_sb_sample_kernel, n=n, heads=H, sub=sub),
        grid=(DB, nk),
        in_specs=[per_b(q), per_b(knt), per_b(vnt), cache, cache],
        out_specs=per_b(q),
        out_shape=jax.ShapeDtypeStruct(q.shape, BF16),
        scratch_shapes=[pltpu.VMEM((H, n, tk), F32), pltpu.VMEM((H, n, 1), F32), pltpu.VMEM((H, n, dh), F32)],
        compiler_params=_cparams("parallel", "arbitrary"),
        name="sb_sample",
    )(q, knt, vnt, kct, vct)


def _fox_sample_cum(lf_new, lf_past_t):
    DB, H, P = lf_past_t.shape
    n = lf_new.shape[1]
    L = -(-(P + n) // LANES) * LANES
    both = jnp.concatenate([lf_past_t.astype(F32), jnp.transpose(lf_new, (0, 2, 1)),
                            jnp.zeros((DB, H, L - P - n), F32)], axis=2)
    cum_t = _cumsum_rows(both) * LOG2E
    ckn = cum_t[:, :, P:P + n]
    return ckn[..., None], cum_t[:, :, :P], ckn


def _heads_major(a, heads, transpose_rows):
    DB, n, _ = a.shape
    a4 = a.reshape(DB, n, heads, -1)
    return jnp.transpose(a4, (0, 2, 3, 1) if transpose_rows else (0, 2, 1, 3))


ROUTER_ROWS = 48


def _layer_norm(y, g, b):
    mu = jnp.mean(y, axis=-1, keepdims=True)
    yc = y - mu
    var = jnp.mean(yc * yc, axis=-1, keepdims=True)
    return yc * lax.rsqrt(var + LN_EPS) * g + b


def _first_argmax(v, ridx):
    vmax = jnp.max(v, axis=0, keepdims=True)
    idx = jnp.min(jnp.where(v == vmax, ridx, v.shape[0]), axis=0, keepdims=True)
    return vmax, idx


def _two_part_specs(tm, width, head_tiles):
    return (pl.BlockSpec((tm, width), lambda i: (jnp.minimum(i, head_tiles - 1), 0)),
            pl.BlockSpec((tm, width), lambda i: (jnp.maximum(i - head_tiles, 0), 0)))


def _mix_out_kernel(*refs, alpha, n_groups, head_tiles, n_parts):
    o_parts, ot_ref = refs[:n_parts], refs[n_parts]
    w_ref, xh_ref, xt_ref, g_ref, b_ref, wrh_ref, wrl_ref, rb_ref, h_ref, ids_ref, wts_ref = refs[n_parts + 1:]
    in_head = pl.program_id(0) < head_tiles
    x = jnp.where(in_head, xh_ref[...], xt_ref[...])
    o_head = o_parts[0][...] if n_parts == 1 else jnp.concatenate([r[...] for r in o_parts], axis=1)
    o = jnp.where(in_head, o_head, ot_ref[...])
    h = _layer_norm(alpha * x + _dot(o, w_ref[...]), g_ref[...], b_ref[...])
    h_ref[...] = h
    hh, hl = _split_bf16(h, 2)
    R = ROUTER_ROWS
    wr = _dot_nt(wrl_ref[...], hh)
    lg = wr[:R] + (_dot_nt(wrh_ref[...], hl) + (wr[R:2 * R] + wr[2 * R:])) + rb_ref[...]
    tm = lg.shape[1]
    ridx = lax.broadcasted_iota(jnp.int32, (8, tm), 0)
    g = jnp.where(ridx < n_groups, lg[0:8], NEG_INF)
    gmax, gidx = _first_argmax(g, ridx)
    gate = 1.0 / jnp.sum(jnp.exp(g - gmax), axis=0, keepdims=True)
    esel = lg[8:16]
    for gg in range(1, n_groups):
        esel = jnp.where(gidx == gg, lg[8 + 8 * gg:16 + 8 * gg], esel)
    v1, i1 = _first_argmax(esel, ridx)
    v2, i2 = _first_argmax(jnp.where(ridx == i1, NEG_INF, esel), ridx)
    t = jnp.exp(v2 - v1)
    w1 = 1.0 / (1.0 + t)
    ids_ref[...] = jnp.where(ridx == 0, gidx * 8 + i1, jnp.where(ridx == 1, gidx * 8 + i2, 0))
    wts_ref[...] = jnp.where(ridx == 0, gate * w1, jnp.where(ridx == 1, gate * (t * w1), 0.0))


def _mix_out(o_head_parts, o_tail, w, x_head, x_tail, g, b, wrh, wrl, rb, alpha, n_groups, tm):
    D = x_head.shape[1]
    T = x_head.shape[0] + x_tail.shape[0]
    head_tiles = x_head.shape[0] // tm
    const = lambda a: pl.BlockSpec(a.shape, lambda i: (0,) * a.ndim)
    rb_t = jnp.broadcast_to(rb, (ROUTER_ROWS, tm))
    head_spec = lambda width: _two_part_specs(tm, width, head_tiles)[0]
    return pl.pallas_call(
        functools.partial(_mix_out_kernel, alpha=alpha, n_groups=n_groups, head_tiles=head_tiles,
                          n_parts=len(o_head_parts)),
        grid=(T // tm,),
        in_specs=[*[head_spec(p.shape[1]) for p in o_head_parts], _two_part_specs(tm, o_tail.shape[1], head_tiles)[1],
                  const(w), *_two_part_specs(tm, D, head_tiles), const(g), const(b), const(wrh), const(wrl),
                  const(rb_t)],
        out_specs=[pl.BlockSpec((tm, D), lambda i: (i, 0)), pl.BlockSpec((8, tm), lambda i: (0, i)),
                   pl.BlockSpec((8, tm), lambda i: (0, i))],
        out_shape=[jax.ShapeDtypeStruct((T, D), F32), jax.ShapeDtypeStruct((8, T), jnp.int32),
                   jax.ShapeDtypeStruct((8, T), F32)],
        compiler_params=_cparams("parallel"),
        name="mix_out_ln_router",
    )(*o_head_parts, o_tail, w, x_head, x_tail, g, b, wrh, wrl, rb_t)


def _prep_router(w_group, b_group, w_router, b_router):
    D, G = w_group.shape
    E = w_router.shape[-1]
    wr = jnp.zeros((ROUTER_ROWS, D), F32)
    wr = wr.at[:G].set(w_group.T.astype(F32))
    wr = wr.at[8:8 + G * E].set(jnp.transpose(w_router, (0, 2, 1)).reshape(G * E, D).astype(F32))
    rb = jnp.zeros((ROUTER_ROWS, 1), F32)
    rb = rb.at[:G, 0].set(b_group.astype(F32)).at[8:8 + G * E, 0].set(b_router.reshape(-1).astype(F32))
    hi, mid, lo = _split_bf16_trunc(wr, 3)
    return hi, jnp.concatenate([hi, mid, lo], axis=0), rb


def _gather_rows(idx_ref, lo, hi, src_hbm, dst, sem):
    for r in range(lo, hi):
        pltpu.make_async_copy(src_hbm.at[pl.ds(idx_ref[0, 0, r], 1)], dst.at[pl.ds(r, 1)], sem).start()


def _wait_rows(n, src_hbm, dst, sem):
    pltpu.make_async_copy(src_hbm.at[pl.ds(0, n)], dst, sem).wait()


def _moe_experts_kernel(te_ref, tv_ref, src_ref, nxt_ref, x_hbm, wg_ref, wu_ref, wd_ref, y_ref, xbuf, sem, *, tm):
    i = pl.program_id(0)
    nt = pl.num_programs(0)
    slot = i % 2

    @pl.when(jnp.logical_and(i == 0, tv_ref[0] > 0))
    def _():
        _gather_rows(src_ref, 0, tm, x_hbm, xbuf.at[0], sem.at[0])

    @pl.when(jnp.logical_and(i + 1 < nt, tv_ref[jnp.minimum(i + 1, nt - 1)] > 0))
    def _():
        _gather_rows(nxt_ref, 0, tm, x_hbm, xbuf.at[1 - slot], sem.at[1 - slot])

    @pl.when(tv_ref[i] > 0)
    def _():
        _wait_rows(tm, x_hbm, xbuf.at[slot], sem.at[slot])
        xb = xbuf[slot].astype(BF16)
        a = _dot(xb, wg_ref[0].astype(BF16))
        u = _dot(xb, wu_ref[0].astype(BF16))
        hid = (a / (1.0 + jnp.exp(-a))) * u
        y_ref[...] = _dot(hid.astype(BF16), wd_ref[0].astype(BF16))

    @pl.when(tv_ref[i] == 0)
    def _():
        y_ref[...] = jnp.zeros_like(y_ref)


def _moe_experts(x, w_gate, w_up, w_down, tile_expert, tile_valid, src, tm):
    T, D = x.shape
    F = w_gate.shape[-1]
    NT = tile_expert.shape[0]
    grid_spec = pltpu.PrefetchScalarGridSpec(
        num_scalar_prefetch=2,
        grid=(NT,),
        in_specs=[pl.BlockSpec((1, 1, tm), lambda i, te, tv: (i, 0, 0), memory_space=pltpu.SMEM),
                  pl.BlockSpec((1, 1, tm), lambda i, te, tv: (i + 1, 0, 0), memory_space=pltpu.SMEM),
                  pl.BlockSpec(memory_space=pl.ANY),
                  pl.BlockSpec((1, D, F), lambda i, te, tv: (te[i], 0, 0)),
                  pl.BlockSpec((1, D, F), lambda i, te, tv: (te[i], 0, 0)),
                  pl.BlockSpec((1, F, D), lambda i, te, tv: (te[i], 0, 0))],
        out_specs=pl.BlockSpec((tm, D), lambda i, te, tv: (i, 0)),
        scratch_shapes=[pltpu.VMEM((2, tm, D), F32), pltpu.SemaphoreType.DMA((2,))],
    )
    return pl.pallas_call(
        functools.partial(_moe_experts_kernel, tm=tm),
        grid_spec=grid_spec,
        out_shape=jax.ShapeDtypeStruct((NT * tm, D), F32),
        compiler_params=_cparams("arbitrary"),
        name="moe_experts",
    )(tile_expert, tile_valid, src, src, x, w_gate, w_up, w_down)


def _moe_combine_kernel(pos_ref, nxt_ref, ys_hbm, h_ref, w_ref, g_ref, b_ref, oh_ref, ot_ref, buf, sem,
                        *, tm, alpha, head_tiles):
    i = pl.program_id(0)
    nt = pl.num_programs(0)
    slot = i % 2

    @pl.when(i == 0)
    def _():
        _gather_rows(pos_ref, 0, 2 * tm, ys_hbm, buf.at[0], sem.at[0])

    @pl.when(i + 1 < nt)
    def _():
        _gather_rows(nxt_ref, 0, 2 * tm, ys_hbm, buf.at[1 - slot], sem.at[1 - slot])

    _wait_rows(2 * tm, ys_hbm, buf.at[slot], sem.at[slot])
    w = w_ref[...]
    y = _layer_norm(alpha * h_ref[...] + (w[:, 0:1] * buf[slot, 0:tm] + w[:, 1:2] * buf[slot, tm:2 * tm]),
                    g_ref[...], b_ref[...])

    @pl.when(i < head_tiles)
    def _():
        oh_ref[...] = y

    @pl.when(i >= head_tiles)
    def _():
        ot_ref[...] = y


def _moe_combine(ys, h, pos, wts, g, b, alpha, tm, head_rows):
    T, D = h.shape
    nt = T // tm
    head_tiles = head_rows // tm
    const = lambda a: pl.BlockSpec(a.shape, lambda i: (0,) * a.ndim)
    return pl.pallas_call(
        functools.partial(_moe_combine_kernel, tm=tm, alpha=alpha, head_tiles=head_tiles),
        grid=(nt,),
        in_specs=[pl.BlockSpec((1, 1, 2 * tm), lambda i: (i, 0, 0), memory_space=pltpu.SMEM),
                  pl.BlockSpec((1, 1, 2 * tm), lambda i: (i + 1, 0, 0), memory_space=pltpu.SMEM),
                  pl.BlockSpec(memory_space=pl.ANY),
                  pl.BlockSpec((tm, D), lambda i: (i, 0)),
                  pl.BlockSpec((tm, 2), lambda i: (i, 0)), const(g), const(b)],
        out_specs=list(_two_part_specs(tm, D, head_tiles)),
        out_shape=[jax.ShapeDtypeStruct((head_rows, D), F32), jax.ShapeDtypeStruct((T - head_rows, D), F32)],
        scratch_shapes=[pltpu.VMEM((2, 2 * tm, D), F32), pltpu.SemaphoreType.DMA((2,))],
        compiler_params=_cparams("arbitrary"),
        name="moe_combine_ln",
    )(pos, pos, ys, h, wts, g, b)


def _route(ids, n_experts, tm):
    T = ids.shape[1]
    flat = ids.reshape(-1)
    iota = jnp.arange(2 * T, dtype=jnp.int32)
    sorted_e, order = lax.sort((flat, iota), num_keys=1, is_stable=True)
    _, inverse = lax.sort((order, iota), num_keys=1)
    experts = jnp.arange(n_experts, dtype=jnp.int32)
    counts = jnp.sum((flat[:, None] == experts[None, :]).astype(jnp.int32), axis=0)
    padded = (counts + tm - 1) // tm * tm
    ends = jnp.cumsum(padded)
    shift = (ends - padded) - (jnp.cumsum(counts) - counts)
    NT = (2 * T + n_experts * (tm - 1)) // tm
    tile_start = jnp.arange(NT, dtype=jnp.int32) * tm
    tile_expert = jnp.minimum(jnp.sum((tile_start[:, None] >= ends[None, :]).astype(jnp.int32), axis=1),
                              n_experts - 1)
    tile_valid = (tile_start < ends[-1]).astype(jnp.int32)
    pos = (inverse + shift[flat]).reshape(2, T)
    row = jnp.arange((NT + 1) * tm, dtype=jnp.int32)
    row_shift = jnp.repeat(shift[jnp.concatenate([tile_expert, tile_expert[-1:]])], tm)
    src = (order % T)[jnp.clip(row - row_shift, 0, 2 * T - 1)]
    return tile_expert, tile_valid, src.reshape(NT + 1, 1, tm), pos


MLA_PAIR_W = 2 * LANES
MLA_SAMPLE_CHUNK = 256
QK_NOPE = 64
QK_ROPE = 32


def _mla_proj_kernel(hh_ref, ht_ref, wdn_ref, gq_ref, gkv_ref, wq_ref, wqr_ref, wk_ref, wv_ref, cos_ref, sin_ref,
                     ckv_ref, kr_ref, qcat_ref, kcat_ref, vt_ref, *, q_lora, kv_lora, npairs, scale, head_tiles):
    h = jnp.where(pl.program_id(0) < head_tiles, hh_ref[...], ht_ref[...])
    z = _dot(h.astype(BF16), wdn_ref[...])
    cq = z[:, :q_lora]
    ckv = z[:, q_lora:q_lora + kv_lora]
    o = q_lora + kv_lora
    kr_raw = z[:, o:o + HEAD_W]
    kr_rot = z[:, o + HEAD_W:o + 2 * HEAD_W]
    cq = cq * lax.rsqrt(jnp.mean(cq * cq, axis=-1, keepdims=True) + RMS_EPS) * gq_ref[...]
    ckv = ckv * lax.rsqrt(jnp.mean(ckv * ckv, axis=-1, keepdims=True) + RMS_EPS) * gkv_ref[...]
    ckv_ref[...] = ckv
    cos = cos_ref[...]
    sin = sin_ref[...]
    kr_tile = kr_raw * cos + kr_rot * sin
    kr_ref[...] = kr_tile[:, QK_NOPE:QK_NOPE + QK_ROPE]
    cqb = cq.astype(BF16)
    ckb = ckv.astype(BF16)
    cos2 = jnp.concatenate([cos, cos], axis=1)
    sin2 = jnp.concatenate([sin, sin], axis=1)
    kr2 = jnp.concatenate([kr_tile, kr_tile], axis=1)
    for p in range(npairs):
        lanes = slice(p * MLA_PAIR_W, (p + 1) * MLA_PAIR_W)
        q = _dot(cqb, wq_ref[:, lanes]) * cos2 + _dot(cqb, wqr_ref[:, lanes]) * sin2
        qcat_ref[:, lanes] = (q * scale).astype(BF16)
        kcat_ref[:, lanes] = (_dot(ckb, wk_ref[:, lanes]) + kr2).astype(BF16)
    heads = 2 * npairs
    vt = _dot_nt(wv_ref[...], ckb).reshape(heads, HEAD_DIM, ckb.shape[0])
    vt_ref[0, :, :HEAD_DIM, :] = vt.astype(BF16)
    vt_ref[0, :, HEAD_DIM:, :] = jnp.ones((heads, VT_ROWS - HEAD_DIM, ckb.shape[0]), BF16)


def _mla_proj(h_head, h_tail, wdn, gq, gkv, wq, wqr, wk, wv, cos_t, sin_t, table_block, tm):
    D = h_head.shape[1]
    T = h_head.shape[0] + h_tail.shape[0]
    head_tiles = h_head.shape[0] // tm
    q_lora, kv_lora = gq.shape[1], gkv.shape[1]
    npairs = wq.shape[1] // MLA_PAIR_W
    const = lambda a: pl.BlockSpec(a.shape, lambda i: (0,) * a.ndim)
    row = lambda w_: pl.BlockSpec((tm, w_), lambda i: (i, 0))
    table = pl.BlockSpec((tm, HEAD_W), lambda i: (table_block(i), 0))
    return pl.pallas_call(
        functools.partial(_mla_proj_kernel, q_lora=q_lora, kv_lora=kv_lora, npairs=npairs,
                          scale=(QK_NOPE + QK_ROPE) ** -0.5 * LOG2E, head_tiles=head_tiles),
        grid=(T // tm,),
        in_specs=[*_two_part_specs(tm, D, head_tiles), const(wdn), const(gq), const(gkv), const(wq), const(wqr),
                  const(wk), const(wv), table, table],
        out_specs=[row(kv_lora), row(QK_ROPE), row(wq.shape[1]), row(wk.shape[1]),
                   pl.BlockSpec((1, 2 * npairs, VT_ROWS, tm), lambda i: (i, 0, 0, 0))],
        out_shape=[jax.ShapeDtypeStruct((T, kv_lora), F32), jax.ShapeDtypeStruct((T, QK_ROPE), F32),
                   jax.ShapeDtypeStruct((T, wq.shape[1]), BF16), jax.ShapeDtypeStruct((T, wk.shape[1]), BF16),
                   jax.ShapeDtypeStruct((T // tm, 2 * npairs, VT_ROWS, tm), BF16)],
        compiler_params=_cparams("parallel"),
        name="mla_proj",
    )(h_head, h_tail, wdn, gq, gkv, wq, wqr, wk, wv, cos_t, sin_t)


def _rot_half(w):
    half = w.shape[-1] // 2
    return jnp.concatenate([-w[..., half:], w[..., :half]], axis=-1)


def _prep_mla_weights(w_down, w_uq, w_ukv, heads, q_lora, kv_lora):
    D = w_down.shape[0]
    tail = HEAD_W - QK_NOPE - QK_ROPE
    w_kr = w_down[:, q_lora + kv_lora:]
    slot = lambda w: jnp.concatenate([jnp.zeros((D, QK_NOPE), w.dtype), w, jnp.zeros((D, tail), w.dtype)], axis=1)
    wdn = jnp.concatenate([w_down[:, :q_lora + kv_lora], slot(w_kr), slot(_rot_half(w_kr))], axis=1)
    wq3 = w_uq.reshape(q_lora, heads, QK_NOPE + QK_ROPE)
    nope, ropew = wq3[..., :QK_NOPE], wq3[..., QK_NOPE:]
    zpad = jnp.zeros((q_lora, heads, tail), w_uq.dtype)
    wq = jnp.concatenate([nope, ropew, zpad], axis=-1)
    wqr = jnp.concatenate([jnp.zeros_like(nope), _rot_half(ropew), zpad], axis=-1)
    wkv3 = w_ukv.reshape(kv_lora, heads, QK_NOPE + HEAD_DIM)
    w_uk, w_uv = wkv3[..., :QK_NOPE], wkv3[..., QK_NOPE:]
    wk = jnp.concatenate([w_uk, jnp.zeros((kv_lora, heads, HEAD_W - QK_NOPE), w_ukv.dtype)], axis=-1)
    wv = jnp.transpose(w_uv.reshape(kv_lora, heads * HEAD_DIM))
    b16 = lambda a: a.astype(BF16)
    return (b16(wdn), b16(wq.reshape(q_lora, -1)), b16(wqr.reshape(q_lora, -1)), b16(wk.reshape(kv_lora, -1)),
            b16(wv), b16(jnp.transpose(w_uk, (1, 2, 0))), b16(jnp.transpose(w_uv, (1, 0, 2))))


def _rope_tables(pos):
    half = QK_ROPE // 2
    inv_freq = ROPE_BASE ** (-jnp.arange(half, dtype=F32) / half)
    ang = pos.astype(F32)[:, None] * inv_freq[None, :]
    n = pos.shape[0]
    pad = jnp.zeros((n, HEAD_W - QK_NOPE - QK_ROPE), F32)
    cos = jnp.concatenate([jnp.ones((n, QK_NOPE), F32)] + [jnp.cos(ang)] * 2 + [pad], axis=1)
    sin = jnp.concatenate([jnp.zeros((n, QK_NOPE), F32)] + [jnp.sin(ang)] * 2 + [pad], axis=1)
    return cos, sin


def _mla_sample_queries(qcat, heads):
    DB, n, _ = qcat.shape
    q4 = qcat.reshape(DB, n, heads, HEAD_W)
    rows = lambda a: jnp.transpose(a, (0, 2, 1, 3)).reshape(DB, heads * n, a.shape[-1])
    return rows(q4[..., :QK_NOPE]), rows(q4[..., QK_NOPE:QK_NOPE + QK_ROPE])


def _mla_sample_kernel(qn_ref, qr_ref, wuk_ref, wuv_ref, cc_ref, rc_ref, cn_ref, rn_ref, o_ref,
                       qlat_ref, s_ref, m_ref, l_ref, acc_ref, *, n, heads):
    kt = pl.program_id(1)

    @pl.when(kt == 0)
    def _():
        for h in range(heads):
            rows = slice(h * n, (h + 1) * n)
            qlat_ref[rows, :] = _dot(qn_ref[0, rows, :], wuk_ref[h]).astype(BF16)
        m_ref[...] = jnp.full(m_ref.shape, NEG_INF, F32)
        l_ref[...] = jnp.zeros_like(l_ref)
        acc_ref[...] = jnp.zeros_like(acc_ref)

    def scores(rows, ckv, kr_t):
        return _dot_nt(qlat_ref[rows, :], ckv) + _dot(qr_ref[0, rows, :], kr_t)

    def absorb(rows, s, ckv):
        m, l, acc = _softmax_step(s[None], ckv, (m_ref[:, rows], l_ref[:, rows], acc_ref[:, rows]))
        m_ref[:, rows] = m
        l_ref[:, rows] = l
        acc_ref[:, rows] = acc

    def update(ckv, kr_t):
        absorb(slice(None), scores(slice(None), ckv, kr_t), ckv)

    ckv = cc_ref[0].astype(BF16)
    kr_t = rc_ref[0].astype(BF16)
    chunk_rows = s_ref.shape[1]
    chunks = [slice(c * chunk_rows, (c + 1) * chunk_rows) for c in range(s_ref.shape[0])]
    s_ref[0] = scores(chunks[0], ckv, kr_t)
    for c, rows in enumerate(chunks):
        if c + 1 < len(chunks):
            s_ref[c + 1] = scores(chunks[c + 1], ckv, kr_t)
        absorb(rows, s_ref[c], ckv)

    @pl.when(kt == pl.num_programs(1) - 1)
    def _():
        update(cn_ref[0], rn_ref[0])
        o_lat = (acc_ref[0] / l_ref[0]).astype(BF16)
        for h in range(heads):
            o_ref[0, :, h * HEAD_DIM:(h + 1) * HEAD_DIM] = _dot(o_lat[h * n:(h + 1) * n], wuv_ref[h]).astype(o_ref.dtype)


def _mla_sample(qn, qr, wuk, wuv, ckv_c, kr_c, ckv_n, kr_n, n, tk):
    DB, R, _ = qn.shape
    heads = R // n
    P, C = ckv_c.shape[1], ckv_c.shape[2]
    const = lambda a: pl.BlockSpec(a.shape, lambda b, j: (0,) * a.ndim)
    per_b = lambda a: pl.BlockSpec((1,) + a.shape[1:], lambda b, j: (b, 0, 0))
    return pl.pallas_call(
        functools.partial(_mla_sample_kernel, n=n, heads=heads),
        grid=(DB, P // tk),
        in_specs=[per_b(qn), per_b(qr), const(wuk), const(wuv),
                  pl.BlockSpec((1, tk, C), lambda b, j: (b, j, 0)),
                  pl.BlockSpec((1, QK_ROPE, tk), lambda b, j: (b, 0, j)),
                  per_b(ckv_n), per_b(kr_n)],
        out_specs=pl.BlockSpec((1, n, heads * HEAD_DIM), lambda b, j: (b, 0, 0)),
        out_shape=jax.ShapeDtypeStruct((DB, n, heads * HEAD_DIM), BF16),
        scratch_shapes=[pltpu.VMEM((R, C), BF16), pltpu.VMEM((R // MLA_SAMPLE_CHUNK, MLA_SAMPLE_CHUNK, tk), F32),
                        pltpu.VMEM((1, R, 1), F32), pltpu.VMEM((1, R, 1), F32), pltpu.VMEM((1, R, C), F32)],
        compiler_params=_cparams("parallel", "arbitrary"),
        name="mla_sample",
    )(qn, qr, wuk, wuv, ckv_c, kr_c, ckv_n, kr_n)


def _moe_layer(h, ids, wts, w_gate, w_up, w_down, layer, g, b, alpha, tm, head_rows):
    T, D = h.shape
    n_experts = w_gate.shape[1] * w_gate.shape[2]
    tile_expert, tile_valid, src, pos = _route(ids[:2], n_experts, tm)
    flat3 = lambda w: w.reshape((-1,) + w.shape[3:])
    ys = _moe_experts(h, flat3(w_gate), flat3(w_up), flat3(w_down), tile_expert + layer * n_experts, tile_valid,
                      src, tm)
    pos_t = jnp.transpose(pos.reshape(2, T // tm, tm), (1, 0, 2)).reshape(T // tm, 1, 2 * tm)
    pos_t = jnp.pad(pos_t, ((0, 1), (0, 0), (0, 0)))
    return _moe_combine(ys, h, pos_t, jnp.transpose(wts[:2]), g, b, alpha, tm, head_rows)


TOKEN_TILE = 256
FLASH_Q_TILE = 512
FLASH_KEY_TILE = 256
SB_Q_TILE = 256
SB_KEY_TILE = 128
CACHE_TILE = 2048
SB_CACHE_TILE = 1024
SB_SUB_TILE = 256


def kernel(x_prompt, x_sample, cache_fox_k, cache_fox_v, cache_fox_logf, cache_sb_k, cache_sb_v, cache_mla_ckv, cache_mla_krope, ab_w_in, ab_b_forget, ab_w_out, mla_w_down, mla_g_q, mla_g_kv, mla_w_uq, mla_w_ukv, mla_w_out, moe_w_group, moe_b_group, moe_w_router, moe_b_router, moe_w_gate, moe_w_up, moe_w_down, ln_g, ln_b):
    B, S, D = x_prompt.shape
    DB, n, _ = x_sample.shape
    P = cache_fox_k.shape[2]
    TP, TS = B * S, DB * n
    depth = ln_g.shape[0]
    n_groups = moe_w_group.shape[-1]
    assert depth == 2 and ab_w_in.shape[0] == 1 and mla_w_down.shape[0] == 1
    assert S % FLASH_Q_TILE == 0 and S % SB_Q_TILE == 0 and TP % TOKEN_TILE == 0 and TS % TOKEN_TILE == 0 and TOKEN_TILE % n == 0
    assert P % CACHE_TILE == 0 and P % SB_CACHE_TILE == 0 and P % CHUNK == 0 and n == CHUNK
    alpha = (2 * depth) ** 0.25
    tk = CACHE_TILE

    xp, xs = x_prompt.reshape(TP, D), x_sample.reshape(TS, D)
    sample3 = lambda a: a[TP:].reshape(DB, n, -1)

    def ffn(o_head_parts, o_tail, w_out, resid, layer):
        wrh, wrl, rb = _prep_router(moe_w_group[layer], moe_b_group[layer], moe_w_router[layer], moe_b_router[layer])
        h, ids, wts = _mix_out(o_head_parts, o_tail, w_out.astype(BF16), *resid, ln_g[layer, 0][None],
                               ln_b[layer, 0][None], wrh, wrl, rb, alpha, n_groups, TOKEN_TILE)
        return _moe_layer(h, ids, wts, moe_w_gate, moe_w_up, moe_w_down, layer,
                          ln_g[layer, 1][None], ln_b[layer, 1][None], alpha, TOKEN_TILE, TP)

    fox_heads = ab_b_forget.shape[1]
    hw = (ab_w_in.shape[2] - fox_heads) // 6
    w_ab, b_forget = _prep_ab_weights(ab_w_in[0], ab_b_forget[0])
    w_tok, w_t = _prompt_ab_weights(w_ab, hw, fox_heads)
    (qa_p, ka_p, qb_p, kb_p, lf_p, kat, vat, kbt, vbt, vat16, vbt16) = _ab_proj_prompt(
        xp, w_tok, b_forget, w_t, _query_decay_ones(fox_heads), B, S, fox_heads, SB_KEY_TILE)
    ka_p = _fox_insert_decay(ka_p, lf_p, B, S, fox_heads)
    qa, ka, va, qb, kb, vb, ka16, va16, kb16, vb16, lf = _ab_proj(xs, w_ab, b_forget, hw)
    dbn = lambda a: a.reshape(DB, n, -1)
    lf_s = dbn(lf)[:, :, :fox_heads]
    cq_s, ck_past, ck_new = _fox_sample_cum(lf_s, jnp.transpose(cache_fox_logf[0], (0, 2, 1)))
    cache_t = lambda c: jnp.transpose(c[0], (0, 2, 3, 1))
    hm = lambda a, t=False: _heads_major(dbn(a), fox_heads, t)
    o_fox_s = _fox_sample(hm(qa), hm(ka16, True), hm(va16, True), cache_t(cache_fox_k), cache_t(cache_fox_v),
                          cq_s, ck_past, ck_new, tk)
    o_sb_s = _sb_sample(hm(qb), hm(kb16, True), hm(vb16, True), cache_t(cache_sb_k), cache_t(cache_sb_v),
                        SB_CACHE_TILE, SB_SUB_TILE)
    tokens_major = lambda a: jnp.transpose(a, (0, 2, 1, 3)).reshape(TS, hw)
    o_tail = jnp.concatenate([tokens_major(o_fox_s), tokens_major(o_sb_s)], axis=-1)
    o_fox_p = _flash_prompt(qa_p, ka_p, vat16, B, S, FLASH_Q_TILE, fox_heads, 1, "fox_prompt")
    o_sb_p = _sb_prompt(qb_p, kb_p, vbt16, B, S, SB_Q_TILE)
    xp, xs = ffn((o_fox_p, o_sb_p), o_tail, ab_w_out[0], (xp, xs), 0)

    q_lora, kv_lora = mla_g_q.shape[1], mla_g_kv.shape[1]
    heads = mla_w_uq.shape[2] // (QK_NOPE + QK_ROPE)
    wdn, wq, wqr, wk, wv, wuk_t, wuv = _prep_mla_weights(mla_w_down[0], mla_w_uq[0], mla_w_ukv[0], heads, q_lora, kv_lora)
    tm = TOKEN_TILE
    pos = jnp.concatenate([jnp.arange(S, dtype=jnp.int32), P + jnp.arange(tm, dtype=jnp.int32) % n])
    cos_t, sin_t = _rope_tables(pos)
    blocks_per_seq, prompt_blocks = S // tm, TP // tm
    table_block = lambda i: jnp.where(i < prompt_blocks, i % blocks_per_seq, blocks_per_seq)
    assert tm == FLASH_KEY_TILE
    ckv, kr, qcat, kcat, vt = _mla_proj(xp, xs, wdn, mla_g_q[0][None], mla_g_kv[0][None], wq, wqr, wk, wv,
                                        cos_t, sin_t, table_block, tm)
    qn, qr = _mla_sample_queries(sample3(qcat), heads)
    o_s = _mla_sample(qn, qr, wuk_t, wuv, cache_mla_ckv[0], jnp.transpose(cache_mla_krope[0], (0, 2, 1)),
                      sample3(ckv).astype(BF16), jnp.transpose(sample3(kr), (0, 2, 1)).astype(BF16), n, tk)
    o_p = _flash_prompt(qcat, kcat, vt, B, S, FLASH_Q_TILE, 8, CHUNK, "mla_prompt")
    xp, xs = ffn((o_p,), o_s.reshape(TS, -1), mla_w_out[0], (xp, xs), 1)

    rows_p = lambda a: jnp.transpose(a, (0, 3, 1, 2))[None]
    rows_s = lambda a: a.reshape(1, DB, n, fox_heads, hw // fox_heads)
    pr, sr = slice(0, TP), slice(TP, TP + TS)
    return (xp.reshape(B, S, D), xs.reshape(DB, n, D),
            rows_p(kat), rows_p(vat), lf_p[:, :fox_heads].reshape(1, B, S, fox_heads), rows_p(kbt), rows_p(vbt),
            ckv[pr].reshape(1, B, S, kv_lora), kr[pr].reshape(1, B, S, QK_ROPE),
            rows_s(ka), rows_s(va), lf[:, :fox_heads].reshape(1, DB, n, fox_heads), rows_s(kb), rows_s(vb),
            ckv[sr].reshape(1, DB, n, kv_lora), kr[sr].reshape(1, DB, n, QK_ROPE))
```

```python
import functools

import jax
import jax.numpy as jnp
from jax import lax
from jax.experimental import pallas as pl
from jax.experimental.pallas import tpu as pltpu

F32 = jnp.float32
BF16 = jnp.bfloat16
NEG_INF = -1e30
LOG2E = 1.4426950408889634

LANES = 128
HEAD_DIM = 64
PAIR_W = 2 * HEAD_DIM
CHUNK = 64
LN_EPS = 1e-5
RMS_EPS = 1e-6
ROPE_BASE = 10000.0
VMEM_LIMIT = 56 * 1024 * 1024


def _cparams(*sem):
    return pltpu.CompilerParams(dimension_semantics=sem, vmem_limit_bytes=VMEM_LIMIT)


def _dot(a, b):
    return jnp.dot(a, b, preferred_element_type=F32)


def _dot_nt(a, b):
    return lax.dot_general(a, b, (((1,), (1,)), ((), ())), preferred_element_type=F32)


def _split_bf16(x, parts):
    out = []
    r = x
    for _ in range(parts):
        h = r.astype(BF16)
        out.append(h)
        r = r - h.astype(F32)
    return out


def _split_bf16_trunc(x, parts):
    out = []
    r = x
    for _ in range(parts):
        bits = lax.bitcast_convert_type(r, jnp.uint32) & jnp.uint32(0xFFFF0000)
        h = lax.bitcast_convert_type(bits, F32)
        out.append(h.astype(BF16))
        r = r - h
    return out


def _log_sigmoid(x):
    return jnp.minimum(x, 0.0) - jnp.log(1.0 + jnp.exp(-jnp.abs(x)))


def _pick_tile(n, pref, mult=8):
    t = min(pref, n)
    while n % t or t % mult:
        t -= 1
    return t


def _ab_proj_kernel(x_ref, w_ref, bf_ref, qa_ref, ka_ref, va_ref, qb_ref, kb_ref, vb_ref,
                    ka16_ref, va16_ref, kb16_ref, vb16_ref, lf_ref, *, hw, qscale):
    xb = x_ref[...].astype(BF16)

    def seg(j):
        return _dot(xb, w_ref[:, j * hw:(j + 1) * hw])

    qa_ref[...] = (seg(0) * qscale).astype(BF16)
    z = seg(1)
    ka_ref[...] = z
    ka16_ref[...] = z.astype(BF16)
    z = seg(2)
    va_ref[...] = z
    va16_ref[...] = z.astype(BF16)
    qb_ref[...] = (seg(3) * qscale).astype(BF16)
    z = seg(4)
    kb_ref[...] = z
    kb16_ref[...] = z.astype(BF16)
    z = seg(5)
    vb_ref[...] = z
    vb16_ref[...] = z.astype(BF16)
    f = _dot(xb, w_ref[:, 6 * hw:6 * hw + LANES]) + bf_ref[...]
    lf_ref[...] = _log_sigmoid(f)


def _ab_proj(x, w, bf, hw):
    T, D = x.shape
    tm = _pick_tile(T, 256)
    row = lambda w_: pl.BlockSpec((tm, w_), lambda i: (i, 0))
    f32o = jax.ShapeDtypeStruct((T, hw), F32)
    b16o = jax.ShapeDtypeStruct((T, hw), BF16)
    return pl.pallas_call(
        functools.partial(_ab_proj_kernel, hw=hw, qscale=HEAD_DIM ** -0.5 * LOG2E),
        grid=(T // tm,),
        in_specs=[row(D), pl.BlockSpec(w.shape, lambda i: (0, 0)), pl.BlockSpec(bf.shape, lambda i: (0, 0))],
        out_specs=[row(hw)] * 10 + [row(LANES)],
        out_shape=[b16o, f32o, f32o, b16o, f32o, f32o, b16o, b16o, b16o, b16o,
                   jax.ShapeDtypeStruct((T, LANES), F32)],
        compiler_params=_cparams("parallel"),
        name="ab_proj",
    )(x, w, bf)


def _ab_proj_prompt_kernel(x_ref, w_ref, bf_ref, wt_ref, qone_ref, qa_ref, ka_ref, qb_ref, kb_ref, lf_ref,
                           kat_ref, vat_ref, kbt_ref, vbt_ref, vat16_ref, vbt16_ref, *, hw, heads, qscale, sb_tk):
    xb = x_ref[...].astype(BF16)
    tm = xb.shape[0]
    lo = lax.broadcasted_iota(jnp.int32, (tm, LANES), 1) < HEAD_DIM
    zero = jnp.zeros((tm, LANES), F32)

    def seg(j):
        z = _dot(xb, w_ref[:, j * hw:(j + 1) * hw])
        odd = pltpu.roll(z, hw - HEAD_DIM, axis=1)
        cols = []
        for c in range(hw // LANES):
            g = slice(c * LANES, (c + 1) * LANES)
            cols += [jnp.where(lo, z[:, g], zero), jnp.where(lo, odd[:, g], zero)]
        return jnp.concatenate(cols, axis=1)

    def seg_t(j):
        return _dot_nt(wt_ref[j * hw:(j + 1) * hw, :], xb).reshape(heads, hw // heads, tm)

    qa_ref[...] = (seg(0) * qscale + qone_ref[...]).astype(BF16)
    ka_ref[...] = seg(1).astype(BF16)
    qb_ref[...] = (seg(2) * qscale).astype(BF16)
    kb_ref[...] = seg(3).astype(BF16)
    lf_ref[...] = _log_sigmoid(_dot(xb, w_ref[:, 4 * hw:4 * hw + LANES]) + bf_ref[...])
    kat_ref[0] = seg_t(0)
    z = seg_t(1)
    vat_ref[0] = z
    vat16_ref[0, :, :HEAD_DIM, :] = z.astype(BF16)
    vat16_ref[0, :, HEAD_DIM:, :] = jnp.ones((heads, VT_ROWS - HEAD_DIM, tm), BF16)
    kbt_ref[0] = seg_t(2)
    z = seg_t(3)
    vbt_ref[0] = z
    for c in range(tm // sb_tk):
        vbt16_ref[c] = z[:, :, c * sb_tk:(c + 1) * sb_tk].astype(BF16)


def _ab_proj_prompt(x, w_tok, bf, w_t, q_ones, B, S, heads, sb_tk):
    TP, D = x.shape
    hw = w_t.shape[0] // 4
    wide = heads * HEAD_W
    tm = FLASH_KEY_TILE
    nj = S // tm
    const = lambda a: pl.BlockSpec(a.shape, lambda b, j: (0,) * a.ndim)
    row = lambda w_: pl.BlockSpec((tm, w_), lambda b, j: (b * nj + j, 0))
    t_spec = pl.BlockSpec((1, heads, hw // heads, tm), lambda b, j: (b, 0, 0, j))
    b16 = jax.ShapeDtypeStruct((TP, wide), BF16)
    t32 = jax.ShapeDtypeStruct((B, heads, hw // heads, S), F32)
    return pl.pallas_call(
        functools.partial(_ab_proj_prompt_kernel, hw=hw, heads=heads, qscale=HEAD_DIM ** -0.5 * LOG2E, sb_tk=sb_tk),
        grid=(B, nj),
        in_specs=[row(D), const(w_tok), const(bf), const(w_t), const(q_ones)],
        out_specs=[row(wide)] * 4 + [row(LANES)] + [t_spec] * 4 + [
            pl.BlockSpec((1, heads, VT_ROWS, tm), lambda b, j: (b * nj + j, 0, 0, 0)),
            pl.BlockSpec((tm // sb_tk, heads, hw // heads, sb_tk), lambda b, j: (b * nj + j, 0, 0, 0))],
        out_shape=[b16] * 4 + [jax.ShapeDtypeStruct((TP, LANES), F32)] + [t32] * 4 + [
            jax.ShapeDtypeStruct((TP // tm, heads, VT_ROWS, tm), BF16),
            jax.ShapeDtypeStruct((TP // sb_tk, heads, hw // heads, sb_tk), BF16)],
        compiler_params=_cparams("parallel", "parallel"),
        name="ab_proj_prompt",
    )(x, w_tok, bf, w_t, q_ones)


def _cumsum_kernel(x_ref, o_ref, carry_ref, *, tl):
    @pl.when(pl.program_id(0) == 0)
    def _():
        carry_ref[...] = jnp.zeros_like(carry_ref)

    x = x_ref[...]
    rows = x.shape[0]
    r = lax.broadcasted_iota(jnp.int32, (tl, tl), 0)
    c = lax.broadcasted_iota(jnp.int32, (tl, tl), 1)
    upper = (r <= c).astype(BF16)
    parts = jnp.concatenate(_split_bf16(x, 4), axis=0)
    y = _dot(parts, upper)
    cum = (y[0:rows] + y[rows:2 * rows]) + (y[2 * rows:3 * rows] + y[3 * rows:]) + carry_ref[:, 0:1]
    o_ref[...] = cum
    carry_ref[...] = jnp.broadcast_to(cum[:, tl - 1:tl], carry_ref.shape)


def _cumsum_rows(x):
    B, H, L = x.shape
    tl = _pick_tile(L, 512, LANES)
    out = pl.pallas_call(
        functools.partial(_cumsum_kernel, tl=tl),
        grid=(L // tl,),
        in_specs=[pl.BlockSpec((B * H, tl), lambda j: (0, j))],
        out_specs=pl.BlockSpec((B * H, tl), lambda j: (0, j)),
        out_shape=jax.ShapeDtypeStruct((B * H, L), F32),
        scratch_shapes=[pltpu.VMEM((B * H, LANES), F32)],
        compiler_params=_cparams("arbitrary"),
        name="cumsum_rows",
    )(x.reshape(B * H, L))
    return out.reshape(B, H, L)


def _softmax_step(s, vb, carry):
    m, l, acc = carry
    two, tq, tk = s.shape
    m_new = jnp.maximum(m, jnp.max(s, axis=-1, keepdims=True))
    alpha = jnp.exp2(m - m_new)
    p = jnp.exp2(s - m_new)
    l = alpha * l + jnp.sum(p, axis=-1, keepdims=True)
    pv = _dot(p.reshape(two * tq, tk).astype(BF16), vb).reshape(two, tq, vb.shape[-1])
    return m_new, l, alpha * acc + pv


def _log2_sigmoid_pair(z2):
    l1 = jnp.log2(1.0 + jnp.exp2(-jnp.abs(z2)))
    return jnp.minimum(z2, 0.0) - l1, jnp.minimum(-z2, 0.0) - l1


def _strict_upper(tk):
    r = lax.broadcasted_iota(jnp.int32, (tk, tk), 0)
    c = lax.broadcasted_iota(jnp.int32, (tk, tk), 1)
    return (r > c).astype(BF16)


HEAD_W = LANES
VT_ROWS = HEAD_DIM + 16
SB_DEAD_LOG2 = -160.0


def _pipeline_ahead(stage, first, count, cur, nxt):
    if first < count:
        stage(first, cur)
    elif nxt is not None:
        stage(first - count, nxt)


def _flash_prompt_kernel(q_ref, k_ref, vt_ref, o_ref, s_ref, m_ref, acc_ref, *, tq, tk, heads, chunk):
    i = pl.program_id(2)
    sub = tq // tk
    key = lax.broadcasted_iota(jnp.int32, (tk, tq), 0)
    query = lax.broadcasted_iota(jnp.int32, (tk, tq), 1)
    m_ref[...] = jnp.full(m_ref.shape, NEG_INF, F32)
    acc_ref[...] = jnp.zeros_like(acc_ref)

    def scores(h, kt):
        rows = pl.ds(pl.multiple_of(kt * tk, tk), tk)
        lanes = slice(h * HEAD_W, (h + 1) * HEAD_W)
        s_ref[h] = _dot_nt(k_ref[rows, lanes], q_ref[:, lanes])

    def absorb(h, kt, visible):
        s_t = s_ref[h]
        if visible is not None:
            s_t = jnp.where(visible, s_t, NEG_INF)
        m = m_ref[h]
        m_new = jnp.maximum(m, jnp.max(s_t, axis=0, keepdims=True))
        p_t = jnp.exp2(s_t - m_new).astype(BF16)
        pv = _dot(vt_ref[kt, h], p_t)
        acc_ref[h] = jnp.exp2(m - m_new) * acc_ref[h] + pv
        m_ref[h] = m_new

    scores(0, 0)
    scores(1, 0)

    def body(kt, carry):
        for h in range(heads):
            _pipeline_ahead(scores, h + 2, heads, kt, kt + 1)
            absorb(h, kt, None)
        return carry

    lax.fori_loop(0, sub * i, body, 0)
    for s in range(sub):
        kt = sub * i + s
        visible = ((s * tk + key) // chunk) <= (query // chunk)
        for h in range(heads):
            _pipeline_ahead(scores, h + 2, heads, kt, kt + 1 if s + 1 < sub else None)
            absorb(h, kt, visible)
    for j in range(heads // 2):
        a0, a1 = acc_ref[2 * j], acc_ref[2 * j + 1]
        o_t = jnp.concatenate([a0[:HEAD_DIM] / a0[HEAD_DIM:HEAD_DIM + 1],
                               a1[:HEAD_DIM] / a1[HEAD_DIM:HEAD_DIM + 1]], axis=0)
        o_ref[:, j * PAIR_W:(j + 1) * PAIR_W] = jnp.transpose(o_t).astype(o_ref.dtype)


def _flash_prompt(qx, kx, vt4, B, S, tq, heads_per_step, chunk, name):
    W = qx.shape[1]
    nq = S // tq
    hs = heads_per_step
    tk = vt4.shape[3]
    nk = S // tk
    return pl.pallas_call(
        functools.partial(_flash_prompt_kernel, tq=tq, tk=tk, heads=hs, chunk=chunk),
        grid=(B, W // (HEAD_W * hs), S // tq),
        in_specs=[pl.BlockSpec((tq, HEAD_W * hs), lambda b, g, i: (b * nq + i, g)),
                  pl.BlockSpec((S, HEAD_W * hs), lambda b, g, i: (b, g)),
                  pl.BlockSpec((nk, hs, VT_ROWS, tk), lambda b, g, i: (b, g, 0, 0))],
        out_specs=pl.BlockSpec((tq, HEAD_DIM * hs), lambda b, g, i: (b * nq + i, g)),
        out_shape=jax.ShapeDtypeStruct((B * S, W // HEAD_W * HEAD_DIM), BF16),
        scratch_shapes=[pltpu.VMEM((hs, tk, tq), F32), pltpu.VMEM((hs, 1, tq), F32),
                        pltpu.VMEM((hs, VT_ROWS, tq), F32)],
        compiler_params=_cparams("parallel", "parallel", "arbitrary"),
        name=name,
    )(qx, kx, vt4)


def _sb_prompt_kernel(q_ref, k_ref, vt_ref, o_ref, z_ref, lw_ref, tot_ref, run_ref, acc_ref, *, tq, tk, heads):
    i = pl.program_id(1)
    sub = tq // tk
    key = lax.broadcasted_iota(jnp.int32, (tk, tq), 0)
    query = lax.broadcasted_iota(jnp.int32, (tk, tq), 1)
    r = lax.broadcasted_iota(jnp.int32, (tk, tk), 0)
    c = lax.broadcasted_iota(jnp.int32, (tk, tk), 1)
    after = (c > r).astype(BF16)
    run_ref[...] = jnp.zeros_like(run_ref)
    acc_ref[...] = jnp.zeros_like(acc_ref)

    def logits(h, kt):
        rows = pl.ds(pl.multiple_of(kt * tk, tk), tk)
        lanes = slice(h * HEAD_W, (h + 1) * HEAD_W)
        z_ref[h] = _dot_nt(k_ref[rows, lanes], q_ref[:, lanes])

    def log_weights(before, h, kt):
        log_beta, log_rest = _log2_sigmoid_pair(z_ref[h])
        if before is not None:
            log_beta = jnp.where(before, log_beta, NEG_INF)
            log_rest = jnp.where(before, log_rest, 0.0)
        hi, lo = _split_bf16(log_rest, 2)
        later = _dot(after, hi) + _dot(after, lo)
        lw_ref[h] = log_beta + later
        tot_ref[h] = later[0:1] + log_rest[0:1]

    def accumulate(h, kt):
        run = run_ref[h]
        a_t = jnp.exp2(lw_ref[h] + run).astype(BF16)
        acc_ref[h] += _dot(vt_ref[kt, h], a_t)
        run_ref[h] = run + tot_ref[h]

    def step(kt, masked, nxt, nxt_masked):
        for h in range(heads):
            _pipeline_ahead(logits, h + 2, heads, kt, nxt)
            if h + 1 < heads:
                log_weights(masked, h + 1, kt)
            elif nxt is not None:
                log_weights(nxt_masked, 0, nxt)
            accumulate(h, kt)

    unmasked = sub * i
    first = unmasked + sub - 1
    masks = [((sub - 1 - s) * tk + key) < query for s in range(sub)]
    logits(0, first)
    logits(1, first)
    log_weights(masks[0], 0, first)
    for s in range(sub):
        kt = first - s
        if s + 1 < sub:
            step(kt, masks[s], kt - 1, masks[s + 1])
        else:
            step(kt, masks[s], jnp.maximum(kt - 1, 0), None)

    def alive():
        return (jnp.max(run_ref[...]) > SB_DEAD_LOG2).astype(jnp.int32)

    def body(carry):
        kt, _ = carry
        step(kt, None, jnp.maximum(kt - 1, 0), None)
        return kt - 1, alive()

    lax.while_loop(lambda c: jnp.logical_and(c[0] >= 0, c[1] > 0), body, (unmasked - 1, alive()))

    for j in range(heads // 2):
        o_t = jnp.concatenate([acc_ref[2 * j], acc_ref[2 * j + 1]], axis=0)
        o_ref[:, j * PAIR_W:(j + 1) * PAIR_W] = jnp.transpose(o_t).astype(o_ref.dtype)


def _sb_prompt(qx, kx, vt4, B, S, tq):
    W = qx.shape[1]
    heads = W // HEAD_W
    tk = vt4.shape[3]
    nq = S // tq
    return pl.pallas_call(
        functools.partial(_sb_prompt_kernel, tq=tq, tk=tk, heads=heads),
        grid=(B, nq),
        in_specs=[pl.BlockSpec((tq, W), lambda b, i: (b * nq + i, 0)),
                  pl.BlockSpec((S, W), lambda b, i: (b, 0)),
                  pl.BlockSpec((S // tk, heads, HEAD_DIM, tk), lambda b, i: (b, 0, 0, 0))],
        out_specs=pl.BlockSpec((tq, heads * HEAD_DIM), lambda b, i: (b * nq + i, 0)),
        out_shape=jax.ShapeDtypeStruct((B * S, heads * HEAD_DIM), BF16),
        scratch_shapes=[pltpu.VMEM((heads, tk, tq), F32), pltpu.VMEM((heads, tk, tq), F32),
                        pltpu.VMEM((heads, 1, tq), F32), pltpu.VMEM((heads, 1, tq), F32),
                        pltpu.VMEM((heads, HEAD_DIM, tq), F32)],
        compiler_params=_cparams("parallel", "arbitrary"),
        name="sb_prompt",
    )(qx, kx, vt4)


def _prep_ab_weights(w_in, b_f):
    D = w_in.shape[0]
    H = b_f.shape[0]
    hw = (w_in.shape[1] - H) // 6
    main = jnp.concatenate([w_in[:, :3 * hw], w_in[:, 3 * hw + H:]], axis=1)
    wf = jnp.zeros((D, LANES), w_in.dtype).at[:, :H].set(w_in[:, 3 * hw:3 * hw + H])
    bf = jnp.zeros((1, LANES), F32).at[0, :H].set(b_f.astype(F32))
    return jnp.concatenate([main, wf], axis=1).astype(BF16), bf


def _prompt_ab_weights(w_ab, hw, heads):
    seg = lambda j: w_ab[:, j * hw:(j + 1) * hw]
    w_tok = jnp.concatenate([seg(0), seg(1), seg(3), seg(4), w_ab[:, 6 * hw:]], axis=1)
    w_t = jnp.transpose(jnp.concatenate([seg(1), seg(2), seg(4), seg(5)], axis=1))
    return w_tok, w_t


DECAY_TERMS = 3


def _insert_decay_kernel(k_ref, c_ref, sel_ref, o_ref):
    terms = sum(_dot(part, sel_ref[j]) for j, part in enumerate(_split_bf16(c_ref[...], DECAY_TERMS)))
    o_ref[...] = (k_ref[...].astype(F32) + terms).astype(BF16)


def _fox_insert_decay(kx, lf, B, S, H):
    lt = jnp.transpose(lf[:, :H].reshape(B, S, H), (0, 2, 1))
    cum = jnp.transpose(_cumsum_rows(lt), (0, 2, 1)).reshape(B * S, H) * (-LOG2E)
    c = jnp.pad(cum, ((0, 0), (0, LANES - H)))
    src = jnp.arange(LANES)[None, :, None]
    dst = jnp.arange(H * HEAD_W)[None, None, :]
    term = jnp.arange(DECAY_TERMS)[:, None, None]
    sel = jnp.logical_and(src < H, dst == src * HEAD_W + HEAD_DIM + term).astype(BF16)
    tm = _pick_tile(B * S, 512)
    return pl.pallas_call(
        _insert_decay_kernel,
        grid=(B * S // tm,),
        in_specs=[pl.BlockSpec((tm, H * HEAD_W), lambda i: (i, 0)), pl.BlockSpec((tm, LANES), lambda i: (i, 0)),
                  pl.BlockSpec(sel.shape, lambda i: (0, 0, 0))],
        out_specs=pl.BlockSpec((tm, H * HEAD_W), lambda i: (i, 0)),
        out_shape=jax.ShapeDtypeStruct(kx.shape, BF16),
        input_output_aliases={0: 0},
        compiler_params=_cparams("parallel"),
        name="fox_insert_decay",
    )(kx, c, sel)


def _query_decay_ones(H):
    lane = jnp.arange(H * HEAD_W) % HEAD_W
    return jnp.logical_and(lane >= HEAD_DIM, lane < HEAD_DIM + DECAY_TERMS).astype(F32)[None]


def _fox_sample_kernel(q_ref, knt_ref, vnt_ref, kct_ref, vct_ref, cq_ref, ckp_ref, ckn_ref, o_ref,
                       s_ref, m_ref, l_ref, acc_ref, *, n, heads):
    kt = pl.program_id(1)

    @pl.when(kt == 0)
    def _():
        m_ref[...] = jnp.full(m_ref.shape, NEG_INF, F32)
        l_ref[...] = jnp.zeros_like(l_ref)
        acc_ref[...] = jnp.zeros_like(acc_ref)

    def absorb(h, s, v_t, ck, mask):
        s = s + (cq_ref[0, h] - ck)
        if mask is not None:
            s = jnp.where(mask, s, NEG_INF)
        m = m_ref[h]
        m_new = jnp.maximum(m, jnp.max(s, axis=-1, keepdims=True))
        alpha = jnp.exp2(m - m_new)
        p = jnp.exp2(s - m_new)
        l_ref[h] = alpha * l_ref[h] + jnp.sum(p, axis=-1, keepdims=True)
        acc_ref[h] = alpha * acc_ref[h] + _dot_nt(p.astype(BF16), v_t)
        m_ref[h] = m_new

    def scores(h, _=None):
        s_ref[h] = _dot(q_ref[0, h], kct_ref[0, h].astype(BF16))

    scores(0)
    scores(1)
    for h in range(heads):
        _pipeline_ahead(scores, h + 2, heads, None, None)
        absorb(h, s_ref[h], vct_ref[0, h].astype(BF16), ckp_ref[0, h:h + 1, :], None)

    @pl.when(kt == pl.num_programs(1) - 1)
    def _():
        row = lax.broadcasted_iota(jnp.int32, (n, n), 0)
        col = lax.broadcasted_iota(jnp.int32, (n, n), 1)
        for h in range(heads):
            absorb(h, _dot(q_ref[0, h], knt_ref[0, h]), vnt_ref[0, h], ckn_ref[0, h:h + 1, :], col <= row)
            o_ref[0, h] = (acc_ref[h] / l_ref[h]).astype(o_ref.dtype)


def _fox_sample(q, knt, vnt, kct, vct, cq, ckp, ckn, tk):
    DB, H, n, dh = q.shape
    P = kct.shape[-1]
    per_b = lambda a: pl.BlockSpec((1,) + a.shape[1:], lambda b, j: (b,) + (0,) * (a.ndim - 1))
    cache = pl.BlockSpec((1, H, dh, tk), lambda b, j: (b, 0, 0, j))
    return pl.pallas_call(
        functools.partial(_fox_sample_kernel, n=n, heads=H),
        grid=(DB, P // tk),
        in_specs=[per_b(q), per_b(knt), per_b(vnt), cache, cache, per_b(cq),
                  pl.BlockSpec((1, H, tk), lambda b, j: (b, 0, j)), per_b(ckn)],
        out_specs=per_b(q),
        out_shape=jax.ShapeDtypeStruct(q.shape, BF16),
        scratch_shapes=[pltpu.VMEM((H, n, tk), F32), pltpu.VMEM((H, n, 1), F32), pltpu.VMEM((H, n, 1), F32),
                        pltpu.VMEM((H, n, dh), F32)],
        compiler_params=_cparams("parallel", "arbitrary"),
        name="fox_sample",
    )(q, knt, vnt, kct, vct, cq, ckp, ckn)


def _sb_sample_kernel(q_ref, knt_ref, vnt_ref, kct_ref, vct_ref, o_ref, z_ref, run_ref, acc_ref, *, n, heads, sub):
    kt = pl.program_id(1)
    upper = _strict_upper(sub)

    def absorb(h, z, v_t, upper_m, width, mask=None):
        log_beta, log_rest = _log2_sigmoid_pair(z)
        if mask is not None:
            log_beta = jnp.where(mask, log_beta, NEG_INF)
            log_rest = jnp.where(mask, log_rest, 0.0)
        hi, lo = _split_bf16(log_rest, 2)
        run = run_ref[h]
        parts = []
        for c in reversed(range(z.shape[1] // width)):
            keys = slice(c * width, (c + 1) * width)
            later = _dot(hi[:, keys], upper_m) + _dot(lo[:, keys], upper_m)
            parts.append(jnp.exp2(log_beta[:, keys] + later + run).astype(BF16))
            run = run + jnp.sum(log_rest[:, keys], axis=-1, keepdims=True)
        a = parts[0] if len(parts) == 1 else jnp.concatenate(parts[::-1], axis=1)
        acc_ref[h] += _dot_nt(a, v_t)
        run_ref[h] = run

    def logits(h, _=None):
        z_ref[h] = _dot(q_ref[0, h], kct_ref[0, h].astype(BF16))

    @pl.when(kt == 0)
    def _():
        row = lax.broadcasted_iota(jnp.int32, (n, n), 0)
        col = lax.broadcasted_iota(jnp.int32, (n, n), 1)
        upper_n = _strict_upper(n)
        run_ref[...] = jnp.zeros_like(run_ref)
        acc_ref[...] = jnp.zeros_like(acc_ref)
        for h in range(heads):
            absorb(h, _dot(q_ref[0, h], knt_ref[0, h]), vnt_ref[0, h], upper_n, n, col < row)

    @pl.when(jnp.max(run_ref[...]) > SB_DEAD_LOG2)
    def _():
        logits(0)
        logits(1)
        for h in range(heads):
            _pipeline_ahead(logits, h + 2, heads, None, None)
            absorb(h, z_ref[h], vct_ref[0, h].astype(BF16), upper, sub)

    @pl.when(kt == pl.num_programs(1) - 1)
    def _():
        o_ref[0] = acc_ref[...].astype(o_ref.dtype)


def _sb_sample(q, knt, vnt, kct, vct, tk, sub):
    DB, H, n, dh = q.shape
    P = kct.shape[-1]
    nk = P // tk
    per_b = lambda a: pl.BlockSpec((1,) + a.shape[1:], lambda b, j: (b,) + (0,) * (a.ndim - 1))
    cache = pl.BlockSpec((1, H, dh, tk), lambda b, j: (b, 0, 0, nk - 1 - j))
    return pl.pallas_call(
        functools.partial(_sb_sample_kernel, n=n, heads=H, sub=sub),
        grid=(DB, nk),
        in_specs=[per_b(q), per_b(knt), per_b(vnt), cache, cache],
        out_specs=per_b(q),
        out_shape=jax.ShapeDtypeStruct(q.shape, BF16),
        scratch_shapes=[pltpu.VMEM((H, n, tk), F32), pltpu.VMEM((H, n, 1), F32), pltpu.VMEM((H, n, dh), F32)],
        compiler_params=_cparams("parallel", "arbitrary"),
        name="sb_sample",
    )(q, knt, vnt, kct, vct)


def _fox_sample_cum(lf_new, lf_past_t):
    DB, H, P = lf_past_t.shape
    n = lf_new.shape[1]
    L = -(-(P + n) // LANES) * LANES
    both = jnp.concatenate([lf_past_t.astype(F32), jnp.transpose(lf_new, (0, 2, 1)),
                            jnp.zeros((DB, H, L - P - n), F32)], axis=2)
    cum_t = _cumsum_rows(both) * LOG2E
    ckn = cum_t[:, :, P:P + n]
    return ckn[..., None], cum_t[:, :, :P], ckn


def _heads_major(a, heads, transpose_rows):
    DB, n, _ = a.shape
    a4 = a.reshape(DB, n, heads, -1)
    return jnp.transpose(a4, (0, 2, 3, 1) if transpose_rows else (0, 2, 1, 3))


ROUTER_ROWS = 48


def _layer_norm(y, g, b):
    mu = jnp.mean(y, axis=-1, keepdims=True)
    yc = y - mu
    var = jnp.mean(yc * yc, axis=-1, keepdims=True)
    return yc * lax.rsqrt(var + LN_EPS) * g + b


def _first_argmax(v, ridx):
    vmax = jnp.max(v, axis=0, keepdims=True)
    idx = jnp.min(jnp.where(v == vmax, ridx, v.shape[0]), axis=0, keepdims=True)
    return vmax, idx


def _two_part_specs(tm, width, head_tiles):
    return (pl.BlockSpec((tm, width), lambda i: (jnp.minimum(i, head_tiles - 1), 0)),
            pl.BlockSpec((tm, width), lambda i: (jnp.maximum(i - head_tiles, 0), 0)))


def _mix_out_kernel(*refs, alpha, n_groups, head_tiles, n_parts):
    o_parts, ot_ref = refs[:n_parts], refs[n_parts]
    w_ref, xh_ref, xt_ref, g_ref, b_ref, wrh_ref, wrl_ref, rb_ref, h_ref, ids_ref, wts_ref = refs[n_parts + 1:]
    in_head = pl.program_id(0) < head_tiles
    x = jnp.where(in_head, xh_ref[...], xt_ref[...])
    o_head = o_parts[0][...] if n_parts == 1 else jnp.concatenate([r[...] for r in o_parts], axis=1)
    o = jnp.where(in_head, o_head, ot_ref[...])
    h = _layer_norm(alpha * x + _dot(o, w_ref[...]), g_ref[...], b_ref[...])
    h_ref[...] = h
    hh, hl = _split_bf16(h, 2)
    R = ROUTER_ROWS
    wr = _dot_nt(wrl_ref[...], hh)
    lg = wr[:R] + (_dot_nt(wrh_ref[...], hl) + (wr[R:2 * R] + wr[2 * R:])) + rb_ref[...]
    tm = lg.shape[1]
    ridx = lax.broadcasted_iota(jnp.int32, (8, tm), 0)
    g = jnp.where(ridx < n_groups, lg[0:8], NEG_INF)
    gmax, gidx = _first_argmax(g, ridx)
    gate = 1.0 / jnp.sum(jnp.exp(g - gmax), axis=0, keepdims=True)
    esel = lg[8:16]
    for gg in range(1, n_groups):
        esel = jnp.where(gidx == gg, lg[8 + 8 * gg:16 + 8 * gg], esel)
    v1, i1 = _first_argmax(esel, ridx)
    v2, i2 = _first_argmax(jnp.where(ridx == i1, NEG_INF, esel), ridx)
    t = jnp.exp(v2 - v1)
    w1 = 1.0 / (1.0 + t)
    ids_ref[...] = jnp.where(ridx == 0, gidx * 8 + i1, jnp.where(ridx == 1, gidx * 8 + i2, 0))
    wts_ref[...] = jnp.where(ridx == 0, gate * w1, jnp.where(ridx == 1, gate * (t * w1), 0.0))


def _mix_out(o_head_parts, o_tail, w, x_head, x_tail, g, b, wrh, wrl, rb, alpha, n_groups, tm):
    D = x_head.shape[1]
    T = x_head.shape[0] + x_tail.shape[0]
    head_tiles = x_head.shape[0] // tm
    const = lambda a: pl.BlockSpec(a.shape, lambda i: (0,) * a.ndim)
    rb_t = jnp.broadcast_to(rb, (ROUTER_ROWS, tm))
    head_spec = lambda width: _two_part_specs(tm, width, head_tiles)[0]
    return pl.pallas_call(
        functools.partial(_mix_out_kernel, alpha=alpha, n_groups=n_groups, head_tiles=head_tiles,
                          n_parts=len(o_head_parts)),
        grid=(T // tm,),
        in_specs=[*[head_spec(p.shape[1]) for p in o_head_parts], _two_part_specs(tm, o_tail.shape[1], head_tiles)[1],
                  const(w), *_two_part_specs(tm, D, head_tiles), const(g), const(b), const(wrh), const(wrl),
                  const(rb_t)],
        out_specs=[pl.BlockSpec((tm, D), lambda i: (i, 0)), pl.BlockSpec((8, tm), lambda i: (0, i)),
                   pl.BlockSpec((8, tm), lambda i: (0, i))],
        out_shape=[jax.ShapeDtypeStruct((T, D), F32), jax.ShapeDtypeStruct((8, T), jnp.int32),
                   jax.ShapeDtypeStruct((8, T), F32)],
        compiler_params=_cparams("parallel"),
        name="mix_out_ln_router",
    )(*o_head_parts, o_tail, w, x_head, x_tail, g, b, wrh, wrl, rb_t)


def _prep_router(w_group, b_group, w_router, b_router):
    D, G = w_group.shape
    E = w_router.shape[-1]
    wr = jnp.zeros((ROUTER_ROWS, D), F32)
    wr = wr.at[:G].set(w_group.T.astype(F32))
    wr = wr.at[8:8 + G * E].set(jnp.transpose(w_router, (0, 2, 1)).reshape(G * E, D).astype(F32))
    rb = jnp.zeros((ROUTER_ROWS, 1), F32)
    rb = rb.at[:G, 0].set(b_group.astype(F32)).at[8:8 + G * E, 0].set(b_router.reshape(-1).astype(F32))
    hi, mid, lo = _split_bf16_trunc(wr, 3)
    return hi, jnp.concatenate([hi, mid, lo], axis=0), rb


def _gather_rows(idx_ref, lo, hi, src_hbm, dst, sem):
    for r in range(lo, hi):
        pltpu.make_async_copy(src_hbm.at[pl.ds(idx_ref[0, 0, r], 1)], dst.at[pl.ds(r, 1)], sem).start()


def _wait_rows(n, src_hbm, dst, sem):
    pltpu.make_async_copy(src_hbm.at[pl.ds(0, n)], dst, sem).wait()


def _moe_experts_kernel(te_ref, tv_ref, src_ref, nxt_ref, x_hbm, wg_ref, wu_ref, wd_ref, y_ref, xbuf, sem, *, tm):
    i = pl.program_id(0)
    nt = pl.num_programs(0)
    slot = i % 2

    @pl.when(jnp.logical_and(i == 0, tv_ref[0] > 0))
    def _():
        _gather_rows(src_ref, 0, tm, x_hbm, xbuf.at[0], sem.at[0])

    @pl.when(jnp.logical_and(i + 1 < nt, tv_ref[jnp.minimum(i + 1, nt - 1)] > 0))
    def _():
        _gather_rows(nxt_ref, 0, tm, x_hbm, xbuf.at[1 - slot], sem.at[1 - slot])

    @pl.when(tv_ref[i] > 0)
    def _():
        _wait_rows(tm, x_hbm, xbuf.at[slot], sem.at[slot])
        xb = xbuf[slot].astype(BF16)
        a = _dot(xb, wg_ref[0].astype(BF16))
        u = _dot(xb, wu_ref[0].astype(BF16))
        hid = (a / (1.0 + jnp.exp(-a))) * u
        y_ref[...] = _dot(hid.astype(BF16), wd_ref[0].astype(BF16))

    @pl.when(tv_ref[i] == 0)
    def _():
        y_ref[...] = jnp.zeros_like(y_ref)


def _moe_experts(x, w_gate, w_up, w_down, tile_expert, tile_valid, src, tm):
    T, D = x.shape
    F = w_gate.shape[-1]
    NT = tile_expert.shape[0]
    grid_spec = pltpu.PrefetchScalarGridSpec(
        num_scalar_prefetch=2,
        grid=(NT,),
        in_specs=[pl.BlockSpec((1, 1, tm), lambda i, te, tv: (i, 0, 0), memory_space=pltpu.SMEM),
                  pl.BlockSpec((1, 1, tm), lambda i, te, tv: (i + 1, 0, 0), memory_space=pltpu.SMEM),
                  pl.BlockSpec(memory_space=pl.ANY),
                  pl.BlockSpec((1, D, F), lambda i, te, tv: (te[i], 0, 0)),
                  pl.BlockSpec((1, D, F), lambda i, te, tv: (te[i], 0, 0)),
                  pl.BlockSpec((1, F, D), lambda i, te, tv: (te[i], 0, 0))],
        out_specs=pl.BlockSpec((tm, D), lambda i, te, tv: (i, 0)),
        scratch_shapes=[pltpu.VMEM((2, tm, D), F32), pltpu.SemaphoreType.DMA((2,))],
    )
    return pl.pallas_call(
        functools.partial(_moe_experts_kernel, tm=tm),
        grid_spec=grid_spec,
        out_shape=jax.ShapeDtypeStruct((NT * tm, D), F32),
        compiler_params=_cparams("arbitrary"),
        name="moe_experts",
    )(tile_expert, tile_valid, src, src, x, w_gate, w_up, w_down)


def _moe_combine_kernel(pos_ref, nxt_ref, ys_hbm, h_ref, w_ref, g_ref, b_ref, oh_ref, ot_ref, buf, sem,
                        *, tm, alpha, head_tiles):
    i = pl.program_id(0)
    nt = pl.num_programs(0)
    slot = i % 2

    @pl.when(i == 0)
    def _():
        _gather_rows(pos_ref, 0, 2 * tm, ys_hbm, buf.at[0], sem.at[0])

    @pl.when(i + 1 < nt)
    def _():
        _gather_rows(nxt_ref, 0, 2 * tm, ys_hbm, buf.at[1 - slot], sem.at[1 - slot])

    _wait_rows(2 * tm, ys_hbm, buf.at[slot], sem.at[slot])
    w = w_ref[...]
    y = _layer_norm(alpha * h_ref[...] + (w[:, 0:1] * buf[slot, 0:tm] + w[:, 1:2] * buf[slot, tm:2 * tm]),
                    g_ref[...], b_ref[...])

    @pl.when(i < head_tiles)
    def _():
        oh_ref[...] = y

    @pl.when(i >= head_tiles)
    def _():
        ot_ref[...] = y


def _moe_combine(ys, h, pos, wts, g, b, alpha, tm, head_rows):
    T, D = h.shape
    nt = T // tm
    head_tiles = head_rows // tm
    const = lambda a: pl.BlockSpec(a.shape, lambda i: (0,) * a.ndim)
    return pl.pallas_call(
        functools.partial(_moe_combine_kernel, tm=tm, alpha=alpha, head_tiles=head_tiles),
        grid=(nt,),
        in_specs=[pl.BlockSpec((1, 1, 2 * tm), lambda i: (i, 0, 0), memory_space=pltpu.SMEM),
                  pl.BlockSpec((1, 1, 2 * tm), lambda i: (i + 1, 0, 0), memory_space=pltpu.SMEM),
                  pl.BlockSpec(memory_space=pl.ANY),
                  pl.BlockSpec((tm, D), lambda i: (i, 0)),
                  pl.BlockSpec((tm, 2), lambda i: (i, 0)), const(g), const(b)],
        out_specs=list(_two_part_specs(tm, D, head_tiles)),
        out_shape=[jax.ShapeDtypeStruct((head_rows, D), F32), jax.ShapeDtypeStruct((T - head_rows, D), F32)],
        scratch_shapes=[pltpu.VMEM((2, 2 * tm, D), F32), pltpu.SemaphoreType.DMA((2,))],
        compiler_params=_cparams("arbitrary"),
        name="moe_combine_ln",
    )(pos, pos, ys, h, wts, g, b)


def _route(ids, n_experts, tm):
    T = ids.shape[1]
    flat = ids.reshape(-1)
    iota = jnp.arange(2 * T, dtype=jnp.int32)
    sorted_e, order = lax.sort((flat, iota), num_keys=1, is_stable=True)
    _, inverse = lax.sort((order, iota), num_keys=1)
    experts = jnp.arange(n_experts, dtype=jnp.int32)
    counts = jnp.sum((flat[:, None] == experts[None, :]).astype(jnp.int32), axis=0)
    padded = (counts + tm - 1) // tm * tm
    ends = jnp.cumsum(padded)
    shift = (ends - padded) - (jnp.cumsum(counts) - counts)
    NT = (2 * T + n_experts * (tm - 1)) // tm
    tile_start = jnp.arange(NT, dtype=jnp.int32) * tm
    tile_expert = jnp.minimum(jnp.sum((tile_start[:, None] >= ends[None, :]).astype(jnp.int32), axis=1),
                              n_experts - 1)
    tile_valid = (tile_start < ends[-1]).astype(jnp.int32)
    pos = (inverse + shift[flat]).reshape(2, T)
    row = jnp.arange((NT + 1) * tm, dtype=jnp.int32)
    row_shift = jnp.repeat(shift[jnp.concatenate([tile_expert, tile_expert[-1:]])], tm)
    src = (order % T)[jnp.clip(row - row_shift, 0, 2 * T - 1)]
    return tile_expert, tile_valid, src.reshape(NT + 1, 1, tm), pos


MLA_PAIR_W = 2 * LANES
MLA_SAMPLE_CHUNK = 256
QK_NOPE = 64
QK_ROPE = 32


def _mla_proj_kernel(hh_ref, ht_ref, wdn_ref, gq_ref, gkv_ref, wq_ref, wqr_ref, wk_ref, wv_ref, cos_ref, sin_ref,
                     ckv_ref, kr_ref, qcat_ref, kcat_ref, vt_ref, *, q_lora, kv_lora, npairs, scale, head_tiles):
    h = jnp.where(pl.program_id(0) < head_tiles, hh_ref[...], ht_ref[...])
    z = _dot(h.astype(BF16), wdn_ref[...])
    cq = z[:, :q_lora]
    ckv = z[:, q_lora:q_lora + kv_lora]
    o = q_lora + kv_lora
    kr_raw = z[:, o:o + HEAD_W]
    kr_rot = z[:, o + HEAD_W:o + 2 * HEAD_W]
    cq = cq * lax.rsqrt(jnp.mean(cq * cq, axis=-1, keepdims=True) + RMS_EPS) * gq_ref[...]
    ckv = ckv * lax.rsqrt(jnp.mean(ckv * ckv, axis=-1, keepdims=True) + RMS_EPS) * gkv_ref[...]
    ckv_ref[...] = ckv
    cos = cos_ref[...]
    sin = sin_ref[...]
    kr_tile = kr_raw * cos + kr_rot * sin
    kr_ref[...] = kr_tile[:, QK_NOPE:QK_NOPE + QK_ROPE]
    cqb = cq.astype(BF16)
    ckb = ckv.astype(BF16)
    cos2 = jnp.concatenate([cos, cos], axis=1)
    sin2 = jnp.concatenate([sin, sin], axis=1)
    kr2 = jnp.concatenate([kr_tile, kr_tile], axis=1)
    for p in range(npairs):
        lanes = slice(p * MLA_PAIR_W, (p + 1) * MLA_PAIR_W)
        q = _dot(cqb, wq_ref[:, lanes]) * cos2 + _dot(cqb, wqr_ref[:, lanes]) * sin2
        qcat_ref[:, lanes] = (q * scale).astype(BF16)
        kcat_ref[:, lanes] = (_dot(ckb, wk_ref[:, lanes]) + kr2).astype(BF16)
    heads = 2 * npairs
    vt = _dot_nt(wv_ref[...], ckb).reshape(heads, HEAD_DIM, ckb.shape[0])
    vt_ref[0, :, :HEAD_DIM, :] = vt.astype(BF16)
    vt_ref[0, :, HEAD_DIM:, :] = jnp.ones((heads, VT_ROWS - HEAD_DIM, ckb.shape[0]), BF16)


def _mla_proj(h_head, h_tail, wdn, gq, gkv, wq, wqr, wk, wv, cos_t, sin_t, table_block, tm):
    D = h_head.shape[1]
    T = h_head.shape[0] + h_tail.shape[0]
    head_tiles = h_head.shape[0] // tm
    q_lora, kv_lora = gq.shape[1], gkv.shape[1]
    npairs = wq.shape[1] // MLA_PAIR_W
    const = lambda a: pl.BlockSpec(a.shape, lambda i: (0,) * a.ndim)
    row = lambda w_: pl.BlockSpec((tm, w_), lambda i: (i, 0))
    table = pl.BlockSpec((tm, HEAD_W), lambda i: (table_block(i), 0))
    return pl.pallas_call(
        functools.partial(_mla_proj_kernel, q_lora=q_lora, kv_lora=kv_lora, npairs=npairs,
                          scale=(QK_NOPE + QK_ROPE) ** -0.5 * LOG2E, head_tiles=head_tiles),
        grid=(T // tm,),
        in_specs=[*_two_part_specs(tm, D, head_tiles), const(wdn), const(gq), const(gkv), const(wq), const(wqr),
                  const(wk), const(wv), table, table],
        out_specs=[row(kv_lora), row(QK_ROPE), row(wq.shape[1]), row(wk.shape[1]),
                   pl.BlockSpec((1, 2 * npairs, VT_ROWS, tm), lambda i: (i, 0, 0, 0))],
        out_shape=[jax.ShapeDtypeStruct((T, kv_lora), F32), jax.ShapeDtypeStruct((T, QK_ROPE), F32),
                   jax.ShapeDtypeStruct((T, wq.shape[1]), BF16), jax.ShapeDtypeStruct((T, wk.shape[1]), BF16),
                   jax.ShapeDtypeStruct((T // tm, 2 * npairs, VT_ROWS, tm), BF16)],
        compiler_params=_cparams("parallel"),
        name="mla_proj",
    )(h_head, h_tail, wdn, gq, gkv, wq, wqr, wk, wv, cos_t, sin_t)


def _rot_half(w):
    half = w.shape[-1] // 2
    return jnp.concatenate([-w[..., half:], w[..., :half]], axis=-1)


def _prep_mla_weights(w_down, w_uq, w_ukv, heads, q_lora, kv_lora):
    D = w_down.shape[0]
    tail = HEAD_W - QK_NOPE - QK_ROPE
    w_kr = w_down[:, q_lora + kv_lora:]
    slot = lambda w: jnp.concatenate([jnp.zeros((D, QK_NOPE), w.dtype), w, jnp.zeros((D, tail), w.dtype)], axis=1)
    wdn = jnp.concatenate([w_down[:, :q_lora + kv_lora], slot(w_kr), slot(_rot_half(w_kr))], axis=1)
    wq3 = w_uq.reshape(q_lora, heads, QK_NOPE + QK_ROPE)
    nope, ropew = wq3[..., :QK_NOPE], wq3[..., QK_NOPE:]
    zpad = jnp.zeros((q_lora, heads, tail), w_uq.dtype)
    wq = jnp.concatenate([nope, ropew, zpad], axis=-1)
    wqr = jnp.concatenate([jnp.zeros_like(nope), _rot_half(ropew), zpad], axis=-1)
    wkv3 = w_ukv.reshape(kv_lora, heads, QK_NOPE + HEAD_DIM)
    w_uk, w_uv = wkv3[..., :QK_NOPE], wkv3[..., QK_NOPE:]
    wk = jnp.concatenate([w_uk, jnp.zeros((kv_lora, heads, HEAD_W - QK_NOPE), w_ukv.dtype)], axis=-1)
    wv = jnp.transpose(w_uv.reshape(kv_lora, heads * HEAD_DIM))
    b16 = lambda a: a.astype(BF16)
    return (b16(wdn), b16(wq.reshape(q_lora, -1)), b16(wqr.reshape(q_lora, -1)), b16(wk.reshape(kv_lora, -1)),
            b16(wv), b16(jnp.transpose(w_uk, (1, 2, 0))), b16(jnp.transpose(w_uv, (1, 0, 2))))


def _rope_tables(pos):
    half = QK_ROPE // 2
    inv_freq = ROPE_BASE ** (-jnp.arange(half, dtype=F32) / half)
    ang = pos.astype(F32)[:, None] * inv_freq[None, :]
    n = pos.shape[0]
    pad = jnp.zeros((n, HEAD_W - QK_NOPE - QK_ROPE), F32)
    cos = jnp.concatenate([jnp.ones((n, QK_NOPE), F32)] + [jnp.cos(ang)] * 2 + [pad], axis=1)
    sin = jnp.concatenate([jnp.zeros((n, QK_NOPE), F32)] + [jnp.sin(ang)] * 2 + [pad], axis=1)
    return cos, sin


def _mla_sample_queries(qcat, heads):
    DB, n, _ = qcat.shape
    q4 = qcat.reshape(DB, n, heads, HEAD_W)
    rows = lambda a: jnp.transpose(a, (0, 2, 1, 3)).reshape(DB, heads * n, a.shape[-1])
    return rows(q4[..., :QK_NOPE]), rows(q4[..., QK_NOPE:QK_NOPE + QK_ROPE])


def _mla_sample_kernel(qn_ref, qr_ref, wuk_ref, wuv_ref, cc_ref, rc_ref, cn_ref, rn_ref, o_ref,
                       qlat_ref, s_ref, m_ref, l_ref, acc_ref, *, n, heads):
    kt = pl.program_id(1)

    @pl.when(kt == 0)
    def _():
        for h in range(heads):
            rows = slice(h * n, (h + 1) * n)
            qlat_ref[rows, :] = _dot(qn_ref[0, rows, :], wuk_ref[h]).astype(BF16)
        m_ref[...] = jnp.full(m_ref.shape, NEG_INF, F32)
        l_ref[...] = jnp.zeros_like(l_ref)
        acc_ref[...] = jnp.zeros_like(acc_ref)

    def scores(rows, ckv, kr_t):
        return _dot_nt(qlat_ref[rows, :], ckv) + _dot(qr_ref[0, rows, :], kr_t)

    def absorb(rows, s, ckv):
        m, l, acc = _softmax_step(s[None], ckv, (m_ref[:, rows], l_ref[:, rows], acc_ref[:, rows]))
        m_ref[:, rows] = m
        l_ref[:, rows] = l
        acc_ref[:, rows] = acc

    def update(ckv, kr_t):
        absorb(slice(None), scores(slice(None), ckv, kr_t), ckv)

    ckv = cc_ref[0].astype(BF16)
    kr_t = rc_ref[0].astype(BF16)
    chunk_rows = s_ref.shape[1]
    chunks = [slice(c * chunk_rows, (c + 1) * chunk_rows) for c in range(s_ref.shape[0])]
    s_ref[0] = scores(chunks[0], ckv, kr_t)
    for c, rows in enumerate(chunks):
        if c + 1 < len(chunks):
            s_ref[c + 1] = scores(chunks[c + 1], ckv, kr_t)
        absorb(rows, s_ref[c], ckv)

    @pl.when(kt == pl.num_programs(1) - 1)
    def _():
        update(cn_ref[0], rn_ref[0])
        o_lat = (acc_ref[0] / l_ref[0]).astype(BF16)
        for h in range(heads):
            o_ref[0, :, h * HEAD_DIM:(h + 1) * HEAD_DIM] = _dot(o_lat[h * n:(h + 1) * n], wuv_ref[h]).astype(o_ref.dtype)


def _mla_sample(qn, qr, wuk, wuv, ckv_c, kr_c, ckv_n, kr_n, n, tk):
    DB, R, _ = qn.shape
    heads = R // n
    P, C = ckv_c.shape[1], ckv_c.shape[2]
    const = lambda a: pl.BlockSpec(a.shape, lambda b, j: (0,) * a.ndim)
    per_b = lambda a: pl.BlockSpec((1,) + a.shape[1:], lambda b, j: (b, 0, 0))
    return pl.pallas_call(
        functools.partial(_mla_sample_kernel, n=n, heads=heads),
        grid=(DB, P // tk),
        in_specs=[per_b(qn), per_b(qr), const(wuk), const(wuv),
                  pl.BlockSpec((1, tk, C), lambda b, j: (b, j, 0)),
                  pl.BlockSpec((1, QK_ROPE, tk), lambda b, j: (b, 0, j)),
                  per_b(ckv_n), per_b(kr_n)],
        out_specs=pl.BlockSpec((1, n, heads * HEAD_DIM), lambda b, j: (b, 0, 0)),
        out_shape=jax.ShapeDtypeStruct((DB, n, heads * HEAD_DIM), BF16),
        scratch_shapes=[pltpu.VMEM((R, C), BF16), pltpu.VMEM((R // MLA_SAMPLE_CHUNK, MLA_SAMPLE_CHUNK, tk), F32),
                        pltpu.VMEM((1, R, 1), F32), pltpu.VMEM((1, R, 1), F32), pltpu.VMEM((1, R, C), F32)],
        compiler_params=_cparams("parallel", "arbitrary"),
        name="mla_sample",
    )(qn, qr, wuk, wuv, ckv_c, kr_c, ckv_n, kr_n)


def _moe_layer(h, ids, wts, w_gate, w_up, w_down, layer, g, b, alpha, tm, head_rows):
    T, D = h.shape
    n_experts = w_gate.shape[1] * w_gate.shape[2]
    tile_expert, tile_valid, src, pos = _route(ids[:2], n_experts, tm)
    flat3 = lambda w: w.reshape((-1,) + w.shape[3:])
    ys = _moe_experts(h, flat3(w_gate), flat3(w_up), flat3(w_down), tile_expert + layer * n_experts, tile_valid,
                      src, tm)
    pos_t = jnp.transpose(pos.reshape(2, T // tm, tm), (1, 0, 2)).reshape(T // tm, 1, 2 * tm)
    pos_t = jnp.pad(pos_t, ((0, 1), (0, 0), (0, 0)))
    return _moe_combine(ys, h, pos_t, jnp.transpose(wts[:2]), g, b, alpha, tm, head_rows)


TOKEN_TILE = 256
FLASH_Q_TILE = 512
FLASH_KEY_TILE = 256
SB_Q_TILE = 512
SB_KEY_TILE = 128
CACHE_TILE = 2048
SB_CACHE_TILE = 1024
SB_SUB_TILE = 256


def kernel(x_prompt, x_sample, cache_fox_k, cache_fox_v, cache_fox_logf, cache_sb_k, cache_sb_v, cache_mla_ckv, cache_mla_krope, ab_w_in, ab_b_forget, ab_w_out, mla_w_down, mla_g_q, mla_g_kv, mla_w_uq, mla_w_ukv, mla_w_out, moe_w_group, moe_b_group, moe_w_router, moe_b_router, moe_w_gate, moe_w_up, moe_w_down, ln_g, ln_b):
    B, S, D = x_prompt.shape
    DB, n, _ = x_sample.shape
    P = cache_fox_k.shape[2]
    TP, TS = B * S, DB * n
    depth = ln_g.shape[0]
    n_groups = moe_w_group.shape[-1]
    assert depth == 2 and ab_w_in.shape[0] == 1 and mla_w_down.shape[0] == 1
    assert S % FLASH_Q_TILE == 0 and S % SB_Q_TILE == 0 and TP % TOKEN_TILE == 0 and TS % TOKEN_TILE == 0 and TOKEN_TILE % n == 0
    assert P % CACHE_TILE == 0 and P % SB_CACHE_TILE == 0 and P % CHUNK == 0 and n == CHUNK
    alpha = (2 * depth) ** 0.25
    tk = CACHE_TILE

    xp, xs = x_prompt.reshape(TP, D), x_sample.reshape(TS, D)
    sample3 = lambda a: a[TP:].reshape(DB, n, -1)

    def ffn(o_head_parts, o_tail, w_out, resid, layer):
        wrh, wrl, rb = _prep_router(moe_w_group[layer], moe_b_group[layer], moe_w_router[layer], moe_b_router[layer])
        h, ids, wts = _mix_out(o_head_parts, o_tail, w_out.astype(BF16), *resid, ln_g[layer, 0][None],
                               ln_b[layer, 0][None], wrh, wrl, rb, alpha, n_groups, TOKEN_TILE)
        return _moe_layer(h, ids, wts, moe_w_gate, moe_w_up, moe_w_down, layer,
                          ln_g[layer, 1][None], ln_b[layer, 1][None], alpha, TOKEN_TILE, TP)

    fox_heads = ab_b_forget.shape[1]
    hw = (ab_w_in.shape[2] - fox_heads) // 6
    w_ab, b_forget = _prep_ab_weights(ab_w_in[0], ab_b_forget[0])
    w_tok, w_t = _prompt_ab_weights(w_ab, hw, fox_heads)
    (qa_p, ka_p, qb_p, kb_p, lf_p, kat, vat, kbt, vbt, vat16, vbt16) = _ab_proj_prompt(
        xp, w_tok, b_forget, w_t, _query_decay_ones(fox_heads), B, S, fox_heads, SB_KEY_TILE)
    ka_p = _fox_insert_decay(ka_p, lf_p, B, S, fox_heads)
    qa, ka, va, qb, kb, vb, ka16, va16, kb16, vb16, lf = _ab_proj(xs, w_ab, b_forget, hw)
    dbn = lambda a: a.reshape(DB, n, -1)
    lf_s = dbn(lf)[:, :, :fox_heads]
    cq_s, ck_past, ck_new = _fox_sample_cum(lf_s, jnp.transpose(cache_fox_logf[0], (0, 2, 1)))
    cache_t = lambda c: jnp.transpose(c[0], (0, 2, 3, 1))
    hm = lambda a, t=False: _heads_major(dbn(a), fox_heads, t)
    o_fox_s = _fox_sample(hm(qa), hm(ka16, True), hm(va16, True), cache_t(cache_fox_k), cache_t(cache_fox_v),
                          cq_s, ck_past, ck_new, tk)
    o_sb_s = _sb_sample(hm(qb), hm(kb16, True), hm(vb16, True), cache_t(cache_sb_k), cache_t(cache_sb_v),
                        SB_CACHE_TILE, SB_SUB_TILE)
    tokens_major = lambda a: jnp.transpose(a, (0, 2, 1, 3)).reshape(TS, hw)
    o_tail = jnp.concatenate([tokens_major(o_fox_s), tokens_major(o_sb_s)], axis=-1)
    o_fox_p = _flash_prompt(qa_p, ka_p, vat16, B, S, FLASH_Q_TILE, fox_heads, 1, "fox_prompt")
    o_sb_p = _sb_prompt(qb_p, kb_p, vbt16, B, S, SB_Q_TILE)
    xp, xs = ffn((o_fox_p, o_sb_p), o_tail, ab_w_out[0], (xp, xs), 0)

    q_lora, kv_lora = mla_g_q.shape[1], mla_g_kv.shape[1]
    heads = mla_w_uq.shape[2] // (QK_NOPE + QK_ROPE)
    wdn, wq, wqr, wk, wv, wuk_t, wuv = _prep_mla_weights(mla_w_down[0], mla_w_uq[0], mla_w_ukv[0], heads, q_lora, kv_lora)
    tm = TOKEN_TILE
    pos = jnp.concatenate([jnp.arange(S, dtype=jnp.int32), P + jnp.arange(tm, dtype=jnp.int32) % n])
    cos_t, sin_t = _rope_tables(pos)
    blocks_per_seq, prompt_blocks = S // tm, TP // tm
    table_block = lambda i: jnp.where(i < prompt_blocks, i % blocks_per_seq, blocks_per_seq)
    assert tm == FLASH_KEY_TILE
    ckv, kr, qcat, kcat, vt = _mla_proj(xp, xs, wdn, mla_g_q[0][None], mla_g_kv[0][None], wq, wqr, wk, wv,
                                        cos_t, sin_t, table_block, tm)
    qn, qr = _mla_sample_queries(sample3(qcat), heads)
    o_s = _mla_sample(qn, qr, wuk_t, wuv, cache_mla_ckv[0], jnp.transpose(cache_mla_krope[0], (0, 2, 1)),
                      sample3(ckv).astype(BF16), jnp.transpose(sample3(kr), (0, 2, 1)).astype(BF16), n, tk)
    o_p = _flash_prompt(qcat, kcat, vt, B, S, FLASH_Q_TILE, 8, CHUNK, "mla_prompt")
    xp, xs = ffn((o_p,), o_s.reshape(TS, -1), mla_w_out[0], (xp, xs), 1)

    rows_p = lambda a: jnp.transpose(a, (0, 3, 1, 2))[None]
    rows_s = lambda a: a.reshape(1, DB, n, fox_heads, hw // fox_heads)
    pr, sr = slice(0, TP), slice(TP, TP + TS)
    return (xp.reshape(B, S, D), xs.reshape(DB, n, D),
            rows_p(kat), rows_p(vat), lf_p[:, :fox_heads].reshape(1, B, S, fox_heads), rows_p(kbt), rows_p(vbt),
            ckv[pr].reshape(1, B, S, kv_lora), kr[pr].reshape(1, B, S, QK_ROPE),
            rows_s(ka), rows_s(va), lf[:, :fox_heads].reshape(1, DB, n, fox_heads), rows_s(kb), rows_s(vb),
            ckv[sr].reshape(1, DB, n, kv_lora), kr[sr].reshape(1, DB, n, QK_ROPE))
```

```python
import functools

import jax
import jax.numpy as jnp
from jax import lax
from jax.experimental import pallas as pl
from jax.experimental.pallas import tpu as pltpu

F32 = jnp.float32
BF16 = jnp.bfloat16
NEG_INF = -1e30
LOG2E = 1.4426950408889634

LANES = 128
HEAD_DIM = 64
PAIR_W = 2 * HEAD_DIM
CHUNK = 64
LN_EPS = 1e-5
RMS_EPS = 1e-6
ROPE_BASE = 10000.0
VMEM_LIMIT = 56 * 1024 * 1024


def _cparams(*sem):
    return pltpu.CompilerParams(dimension_semantics=sem, vmem_limit_bytes=VMEM_LIMIT)


def _dot(a, b):
    return jnp.dot(a, b, preferred_element_type=F32)


def _dot_nt(a, b):
    return lax.dot_general(a, b, (((1,), (1,)), ((), ())), preferred_element_type=F32)


def _split_bf16(x, parts):
    out = []
    r = x
    for _ in range(parts):
        h = r.astype(BF16)
        out.append(h)
        r = r - h.astype(F32)
    return out


def _split_bf16_trunc(x, parts):
    out = []
    r = x
    for _ in range(parts):
        bits = lax.bitcast_convert_type(r, jnp.uint32) & jnp.uint32(0xFFFF0000)
        h = lax.bitcast_convert_type(bits, F32)
        out.append(h.astype(BF16))
        r = r - h
    return out


def _log_sigmoid(x):
    return jnp.minimum(x, 0.0) - jnp.log(1.0 + jnp.exp(-jnp.abs(x)))


def _pick_tile(n, pref, mult=8):
    t = min(pref, n)
    while n % t or t % mult:
        t -= 1
    return t


def _ab_proj_kernel(x_ref, w_ref, bf_ref, qa_ref, ka_ref, va_ref, qb_ref, kb_ref, vb_ref,
                    ka16_ref, va16_ref, kb16_ref, vb16_ref, lf_ref, *, hw, qscale):
    xb = x_ref[...].astype(BF16)

    def seg(j):
        return _dot(xb, w_ref[:, j * hw:(j + 1) * hw])

    qa_ref[...] = (seg(0) * qscale).astype(BF16)
    z = seg(1)
    ka_ref[...] = z
    ka16_ref[...] = z.astype(BF16)
    z = seg(2)
    va_ref[...] = z
    va16_ref[...] = z.astype(BF16)
    qb_ref[...] = (seg(3) * qscale).astype(BF16)
    z = seg(4)
    kb_ref[...] = z
    kb16_ref[...] = z.astype(BF16)
    z = seg(5)
    vb_ref[...] = z
    vb16_ref[...] = z.astype(BF16)
    f = _dot(xb, w_ref[:, 6 * hw:6 * hw + LANES]) + bf_ref[...]
    lf_ref[...] = _log_sigmoid(f)


def _ab_proj(x, w, bf, hw):
    T, D = x.shape
    tm = _pick_tile(T, 256)
    row = lambda w_: pl.BlockSpec((tm, w_), lambda i: (i, 0))
    f32o = jax.ShapeDtypeStruct((T, hw), F32)
    b16o = jax.ShapeDtypeStruct((T, hw), BF16)
    return pl.pallas_call(
        functools.partial(_ab_proj_kernel, hw=hw, qscale=HEAD_DIM ** -0.5 * LOG2E),
        grid=(T // tm,),
        in_specs=[row(D), pl.BlockSpec(w.shape, lambda i: (0, 0)), pl.BlockSpec(bf.shape, lambda i: (0, 0))],
        out_specs=[row(hw)] * 10 + [row(LANES)],
        out_shape=[b16o, f32o, f32o, b16o, f32o, f32o, b16o, b16o, b16o, b16o,
                   jax.ShapeDtypeStruct((T, LANES), F32)],
        compiler_params=_cparams("parallel"),
        name="ab_proj",
    )(x, w, bf)


def _ab_proj_prompt_kernel(x_ref, w_ref, bf_ref, wt_ref, qone_ref, qa_ref, ka_ref, qb_ref, kb_ref, lf_ref,
                           kat_ref, vat_ref, kbt_ref, vbt_ref, vat16_ref, vbt16_ref, *, hw, heads, qscale, sb_tk):
    xb = x_ref[...].astype(BF16)
    tm = xb.shape[0]
    lo = lax.broadcasted_iota(jnp.int32, (tm, LANES), 1) < HEAD_DIM
    zero = jnp.zeros((tm, LANES), F32)

    def seg(j):
        z = _dot(xb, w_ref[:, j * hw:(j + 1) * hw])
        odd = pltpu.roll(z, hw - HEAD_DIM, axis=1)
        cols = []
        for c in range(hw // LANES):
            g = slice(c * LANES, (c + 1) * LANES)
            cols += [jnp.where(lo, z[:, g], zero), jnp.where(lo, odd[:, g], zero)]
        return jnp.concatenate(cols, axis=1)

    def seg_t(j):
        return _dot_nt(wt_ref[j * hw:(j + 1) * hw, :], xb).reshape(heads, hw // heads, tm)

    qa_ref[...] = (seg(0) * qscale + qone_ref[...]).astype(BF16)
    ka_ref[...] = seg(1).astype(BF16)
    qb_ref[...] = (seg(2) * qscale).astype(BF16)
    kb_ref[...] = seg(3).astype(BF16)
    lf_ref[...] = _log_sigmoid(_dot(xb, w_ref[:, 4 * hw:4 * hw + LANES]) + bf_ref[...])
    kat_ref[0] = seg_t(0)
    z = seg_t(1)
    vat_ref[0] = z
    vat16_ref[0, :, :HEAD_DIM, :] = z.astype(BF16)
    vat16_ref[0, :, HEAD_DIM:, :] = jnp.ones((heads, VT_ROWS - HEAD_DIM, tm), BF16)
    kbt_ref[0] = seg_t(2)
    z = seg_t(3)
    vbt_ref[0] = z
    for c in range(tm // sb_tk):
        vbt16_ref[c] = z[:, :, c * sb_tk:(c + 1) * sb_tk].astype(BF16)


def _ab_proj_prompt(x, w_tok, bf, w_t, q_ones, B, S, heads, sb_tk):
    TP, D = x.shape
    hw = w_t.shape[0] // 4
    wide = heads * HEAD_W
    tm = FLASH_KEY_TILE
    nj = S // tm
    const = lambda a: pl.BlockSpec(a.shape, lambda b, j: (0,) * a.ndim)
    row = lambda w_: pl.BlockSpec((tm, w_), lambda b, j: (b * nj + j, 0))
    t_spec = pl.BlockSpec((1, heads, hw // heads, tm), lambda b, j: (b, 0, 0, j))
    b16 = jax.ShapeDtypeStruct((TP, wide), BF16)
    t32 = jax.ShapeDtypeStruct((B, heads, hw // heads, S), F32)
    return pl.pallas_call(
        functools.partial(_ab_proj_prompt_kernel, hw=hw, heads=heads, qscale=HEAD_DIM ** -0.5 * LOG2E, sb_tk=sb_tk),
        grid=(B, nj),
        in_specs=[row(D), const(w_tok), const(bf), const(w_t), const(q_ones)],
        out_specs=[row(wide)] * 4 + [row(LANES)] + [t_spec] * 4 + [
            pl.BlockSpec((1, heads, VT_ROWS, tm), lambda b, j: (b * nj + j, 0, 0, 0)),
            pl.BlockSpec((tm // sb_tk, heads, hw // heads, sb_tk), lambda b, j: (b * nj + j, 0, 0, 0))],
        out_shape=[b16] * 4 + [jax.ShapeDtypeStruct((TP, LANES), F32)] + [t32] * 4 + [
            jax.ShapeDtypeStruct((TP // tm, heads, VT_ROWS, tm), BF16),
            jax.ShapeDtypeStruct((TP // sb_tk, heads, hw // heads, sb_tk), BF16)],
        compiler_params=_cparams("parallel", "parallel"),
        name="ab_proj_prompt",
    )(x, w_tok, bf, w_t, q_ones)


def _cumsum_kernel(x_ref, o_ref, carry_ref, *, tl):
    @pl.when(pl.program_id(0) == 0)
    def _():
        carry_ref[...] = jnp.zeros_like(carry_ref)

    x = x_ref[...]
    rows = x.shape[0]
    r = lax.broadcasted_iota(jnp.int32, (tl, tl), 0)
    c = lax.broadcasted_iota(jnp.int32, (tl, tl), 1)
    upper = (r <= c).astype(BF16)
    parts = jnp.concatenate(_split_bf16(x, 4), axis=0)
    y = _dot(parts, upper)
    cum = (y[0:rows] + y[rows:2 * rows]) + (y[2 * rows:3 * rows] + y[3 * rows:]) + carry_ref[:, 0:1]
    o_ref[...] = cum
    carry_ref[...] = jnp.broadcast_to(cum[:, tl - 1:tl], carry_ref.shape)


def _cumsum_rows(x):
    B, H, L = x.shape
    tl = _pick_tile(L, 512, LANES)
    out = pl.pallas_call(
        functools.partial(_cumsum_kernel, tl=tl),
        grid=(L // tl,),
        in_specs=[pl.BlockSpec((B * H, tl), lambda j: (0, j))],
        out_specs=pl.BlockSpec((B * H, tl), lambda j: (0, j)),
        out_shape=jax.ShapeDtypeStruct((B * H, L), F32),
        scratch_shapes=[pltpu.VMEM((B * H, LANES), F32)],
        compiler_params=_cparams("arbitrary"),
        name="cumsum_rows",
    )(x.reshape(B * H, L))
    return out.reshape(B, H, L)


def _softmax_step(s, vb, carry):
    m, l, acc = carry
    two, tq, tk = s.shape
    m_new = jnp.maximum(m, jnp.max(s, axis=-1, keepdims=True))
    alpha = jnp.exp2(m - m_new)
    p = jnp.exp2(s - m_new)
    l = alpha * l + jnp.sum(p, axis=-1, keepdims=True)
    pv = _dot(p.reshape(two * tq, tk).astype(BF16), vb).reshape(two, tq, vb.shape[-1])
    return m_new, l, alpha * acc + pv


def _log2_sigmoid_pair(z2):
    l1 = jnp.log2(1.0 + jnp.exp2(-jnp.abs(z2)))
    return jnp.minimum(z2, 0.0) - l1, jnp.minimum(-z2, 0.0) - l1


def _strict_upper(tk):
    r = lax.broadcasted_iota(jnp.int32, (tk, tk), 0)
    c = lax.broadcasted_iota(jnp.int32, (tk, tk), 1)
    return (r > c).astype(BF16)


HEAD_W = LANES
VT_ROWS = HEAD_DIM + 16
SB_DEAD_LOG2 = -160.0


def _pipeline_ahead(stage, first, count, cur, nxt):
    if first < count:
        stage(first, cur)
    elif nxt is not None:
        stage(first - count, nxt)


def _flash_prompt_kernel(q_ref, k_ref, vt_ref, o_ref, s_ref, m_ref, acc_ref, *, tq, tk, heads, chunk):
    i = pl.program_id(2)
    sub = tq // tk
    key = lax.broadcasted_iota(jnp.int32, (tk, tq), 0)
    query = lax.broadcasted_iota(jnp.int32, (tk, tq), 1)
    m_ref[...] = jnp.full(m_ref.shape, NEG_INF, F32)
    acc_ref[...] = jnp.zeros_like(acc_ref)

    def scores(h, kt):
        rows = pl.ds(pl.multiple_of(kt * tk, tk), tk)
        lanes = slice(h * HEAD_W, (h + 1) * HEAD_W)
        s_ref[h] = _dot_nt(k_ref[rows, lanes], q_ref[:, lanes])

    def absorb(h, kt, visible):
        s_t = s_ref[h]
        if visible is not None:
            s_t = jnp.where(visible, s_t, NEG_INF)
        m = m_ref[h]
        m_new = jnp.maximum(m, jnp.max(s_t, axis=0, keepdims=True))
        p_t = jnp.exp2(s_t - m_new).astype(BF16)
        pv = _dot(vt_ref[kt, h], p_t)
        acc_ref[h] = jnp.exp2(m - m_new) * acc_ref[h] + pv
        m_ref[h] = m_new

    scores(0, 0)
    scores(1, 0)

    def body(kt, carry):
        for h in range(heads):
            _pipeline_ahead(scores, h + 2, heads, kt, kt + 1)
            absorb(h, kt, None)
        return carry

    lax.fori_loop(0, sub * i, body, 0)
    for s in range(sub):
        kt = sub * i + s
        visible = ((s * tk + key) // chunk) <= (query // chunk)
        for h in range(heads):
            _pipeline_ahead(scores, h + 2, heads, kt, kt + 1 if s + 1 < sub else None)
            absorb(h, kt, visible)
    for j in range(heads // 2):
        a0, a1 = acc_ref[2 * j], acc_ref[2 * j + 1]
        o_t = jnp.concatenate([a0[:HEAD_DIM] / a0[HEAD_DIM:HEAD_DIM + 1],
                               a1[:HEAD_DIM] / a1[HEAD_DIM:HEAD_DIM + 1]], axis=0)
        o_ref[:, j * PAIR_W:(j + 1) * PAIR_W] = jnp.transpose(o_t).astype(o_ref.dtype)


def _flash_prompt(qx, kx, vt4, B, S, tq, heads_per_step, chunk, name):
    W = qx.shape[1]
    nq = S // tq
    hs = heads_per_step
    tk = vt4.shape[3]
    nk = S // tk
    return pl.pallas_call(
        functools.partial(_flash_prompt_kernel, tq=tq, tk=tk, heads=hs, chunk=chunk),
        grid=(B, W // (HEAD_W * hs), S // tq),
        in_specs=[pl.BlockSpec((tq, HEAD_W * hs), lambda b, g, i: (b * nq + i, g)),
                  pl.BlockSpec((S, HEAD_W * hs), lambda b, g, i: (b, g)),
                  pl.BlockSpec((nk, hs, VT_ROWS, tk), lambda b, g, i: (b, g, 0, 0))],
        out_specs=pl.BlockSpec((tq, HEAD_DIM * hs), lambda b, g, i: (b * nq + i, g)),
        out_shape=jax.ShapeDtypeStruct((B * S, W // HEAD_W * HEAD_DIM), BF16),
        scratch_shapes=[pltpu.VMEM((hs, tk, tq), F32), pltpu.VMEM((hs, 1, tq), F32),
                        pltpu.VMEM((hs, VT_ROWS, tq), F32)],
        compiler_params=_cparams("parallel", "parallel", "arbitrary"),
        name=name,
    )(qx, kx, vt4)


def _sb_prompt_kernel(q_ref, k_ref, vt_ref, o_ref, z_ref, lw_ref, tot_ref, run_ref, acc_ref, *, tq, tk, heads):
    i = pl.program_id(1)
    sub = tq // tk
    key = lax.broadcasted_iota(jnp.int32, (tk, tq), 0)
    query = lax.broadcasted_iota(jnp.int32, (tk, tq), 1)
    r = lax.broadcasted_iota(jnp.int32, (tk, tk), 0)
    c = lax.broadcasted_iota(jnp.int32, (tk, tk), 1)
    after = (c > r).astype(BF16)
    run_ref[...] = jnp.zeros_like(run_ref)
    acc_ref[...] = jnp.zeros_like(acc_ref)

    def logits(h, kt):
        rows = pl.ds(pl.multiple_of(kt * tk, tk), tk)
        lanes = slice(h * HEAD_W, (h + 1) * HEAD_W)
        z_ref[h] = _dot_nt(k_ref[rows, lanes], q_ref[:, lanes])

    def log_weights(before, h, kt):
        log_beta, log_rest = _log2_sigmoid_pair(z_ref[h])
        if before is not None:
            log_beta = jnp.where(before, log_beta, NEG_INF)
            log_rest = jnp.where(before, log_rest, 0.0)
        hi, lo = _split_bf16(log_rest, 2)
        later = _dot(after, hi) + _dot(after, lo)
        lw_ref[h] = log_beta + later
        tot_ref[h] = later[0:1] + log_rest[0:1]

    def accumulate(h, kt):
        run = run_ref[h]
        a_t = jnp.exp2(lw_ref[h] + run).astype(BF16)
        acc_ref[h] += _dot(vt_ref[kt, h], a_t)
        run_ref[h] = run + tot_ref[h]

    def step(kt, masked, nxt, nxt_masked):
        for h in range(heads):
            _pipeline_ahead(logits, h + 2, heads, kt, nxt)
            if h + 1 < heads:
                log_weights(masked, h + 1, kt)
            elif nxt is not None:
                log_weights(nxt_masked, 0, nxt)
            accumulate(h, kt)

    unmasked = sub * i
    first = unmasked + sub - 1
    masks = [((sub - 1 - s) * tk + key) < query for s in range(sub)]
    logits(0, first)
    logits(1, first)
    log_weights(masks[0], 0, first)
    for s in range(sub):
        kt = first - s
        if s + 1 < sub:
            step(kt, masks[s], kt - 1, masks[s + 1])
        else:
            step(kt, masks[s], jnp.maximum(kt - 1, 0), None)

    def alive():
        return (jnp.max(run_ref[...]) > SB_DEAD_LOG2).astype(jnp.int32)

    def body(carry):
        kt, _ = carry
        step(kt, None, jnp.maximum(kt - 1, 0), None)
        return kt - 1, alive()

    lax.while_loop(lambda c: jnp.logical_and(c[0] >= 0, c[1] > 0), body, (unmasked - 1, alive()))

    for j in range(heads // 2):
        o_t = jnp.concatenate([acc_ref[2 * j], acc_ref[2 * j + 1]], axis=0)
        o_ref[:, j * PAIR_W:(j + 1) * PAIR_W] = jnp.transpose(o_t).astype(o_ref.dtype)


def _sb_prompt(qx, kx, vt4, B, S, tq):
    W = qx.shape[1]
    heads = W // HEAD_W
    tk = vt4.shape[3]
    nq = S // tq
    return pl.pallas_call(
        functools.partial(_sb_prompt_kernel, tq=tq, tk=tk, heads=heads),
        grid=(B, nq),
        in_specs=[pl.BlockSpec((tq, W), lambda b, i: (b * nq + i, 0)),
                  pl.BlockSpec((S, W), lambda b, i: (b, 0)),
                  pl.BlockSpec((S // tk, heads, HEAD_DIM, tk), lambda b, i: (b, 0, 0, 0))],
        out_specs=pl.BlockSpec((tq, heads * HEAD_DIM), lambda b, i: (b * nq + i, 0)),
        out_shape=jax.ShapeDtypeStruct((B * S, heads * HEAD_DIM), BF16),
        scratch_shapes=[pltpu.VMEM((heads, tk, tq), F32), pltpu.VMEM((heads, tk, tq), F32),
                        pltpu.VMEM((heads, 1, tq), F32), pltpu.VMEM((heads, 1, tq), F32),
                        pltpu.VMEM((heads, HEAD_DIM, tq), F32)],
        compiler_params=_cparams("parallel", "arbitrary"),
        name="sb_prompt",
    )(qx, kx, vt4)


def _prep_ab_weights(w_in, b_f):
    D = w_in.shape[0]
    H = b_f.shape[0]
    hw = (w_in.shape[1] - H) // 6
    main = jnp.concatenate([w_in[:, :3 * hw], w_in[:, 3 * hw + H:]], axis=1)
    wf = jnp.zeros((D, LANES), w_in.dtype).at[:, :H].set(w_in[:, 3 * hw:3 * hw + H])
    bf = jnp.zeros((1, LANES), F32).at[0, :H].set(b_f.astype(F32))
    return jnp.concatenate([main, wf], axis=1).astype(BF16), bf


def _prompt_ab_weights(w_ab, hw, heads):
    seg = lambda j: w_ab[:, j * hw:(j + 1) * hw]
    w_tok = jnp.concatenate([seg(0), seg(1), seg(3), seg(4), w_ab[:, 6 * hw:]], axis=1)
    w_t = jnp.transpose(jnp.concatenate([seg(1), seg(2), seg(4), seg(5)], axis=1))
    return w_tok, w_t


DECAY_TERMS = 3


def _insert_decay_kernel(k_ref, c_ref, sel_ref, o_ref):
    terms = sum(_dot(part, sel_ref[j]) for j, part in enumerate(_split_bf16(c_ref[...], DECAY_TERMS)))
    o_ref[...] = (k_ref[...].astype(F32) + terms).astype(BF16)


def _fox_insert_decay(kx, lf, B, S, H):
    lt = jnp.transpose(lf[:, :H].reshape(B, S, H), (0, 2, 1))
    cum = jnp.transpose(_cumsum_rows(lt), (0, 2, 1)).reshape(B * S, H) * (-LOG2E)
    c = jnp.pad(cum, ((0, 0), (0, LANES - H)))
    src = jnp.arange(LANES)[None, :, None]
    dst = jnp.arange(H * HEAD_W)[None, None, :]
    term = jnp.arange(DECAY_TERMS)[:, None, None]
    sel = jnp.logical_and(src < H, dst == src * HEAD_W + HEAD_DIM + term).astype(BF16)
    tm = _pick_tile(B * S, 512)
    return pl.pallas_call(
        _insert_decay_kernel,
        grid=(B * S // tm,),
        in_specs=[pl.BlockSpec((tm, H * HEAD_W), lambda i: (i, 0)), pl.BlockSpec((tm, LANES), lambda i: (i, 0)),
                  pl.BlockSpec(sel.shape, lambda i: (0, 0, 0))],
        out_specs=pl.BlockSpec((tm, H * HEAD_W), lambda i: (i, 0)),
        out_shape=jax.ShapeDtypeStruct(kx.shape, BF16),
        input_output_aliases={0: 0},
        compiler_params=_cparams("parallel"),
        name="fox_insert_decay",
    )(kx, c, sel)


def _query_decay_ones(H):
    lane = jnp.arange(H * HEAD_W) % HEAD_W
    return jnp.logical_and(lane >= HEAD_DIM, lane < HEAD_DIM + DECAY_TERMS).astype(F32)[None]


def _fox_sample_kernel(q_ref, knt_ref, vnt_ref, kct_ref, vct_ref, cq_ref, ckp_ref, ckn_ref, o_ref,
                       s_ref, m_ref, l_ref, acc_ref, *, n, heads):
    kt = pl.program_id(1)

    @pl.when(kt == 0)
    def _():
        m_ref[...] = jnp.full(m_ref.shape, NEG_INF, F32)
        l_ref[...] = jnp.zeros_like(l_ref)
        acc_ref[...] = jnp.zeros_like(acc_ref)

    def absorb(h, s, v_t, ck, mask):
        s = s + (cq_ref[0, h] - ck)
        if mask is not None:
            s = jnp.where(mask, s, NEG_INF)
        m = m_ref[h]
        m_new = jnp.maximum(m, jnp.max(s, axis=-1, keepdims=True))
        alpha = jnp.exp2(m - m_new)
        p = jnp.exp2(s - m_new)
        l_ref[h] = alpha * l_ref[h] + jnp.sum(p, axis=-1, keepdims=True)
        acc_ref[h] = alpha * acc_ref[h] + _dot_nt(p.astype(BF16), v_t)
        m_ref[h] = m_new

    def scores(h, _=None):
        s_ref[h] = _dot(q_ref[0, h], kct_ref[0, h].astype(BF16))

    scores(0)
    scores(1)
    for h in range(heads):
        _pipeline_ahead(scores, h + 2, heads, None, None)
        absorb(h, s_ref[h], vct_ref[0, h].astype(BF16), ckp_ref[0, h:h + 1, :], None)

    @pl.when(kt == pl.num_programs(1) - 1)
    def _():
        row = lax.broadcasted_iota(jnp.int32, (n, n), 0)
        col = lax.broadcasted_iota(jnp.int32, (n, n), 1)
        for h in range(heads):
            absorb(h, _dot(q_ref[0, h], knt_ref[0, h]), vnt_ref[0, h], ckn_ref[0, h:h + 1, :], col <= row)
            o_ref[0, h] = (acc_ref[h] / l_ref[h]).astype(o_ref.dtype)


def _fox_sample(q, knt, vnt, kct, vct, cq, ckp, ckn, tk):
    DB, H, n, dh = q.shape
    P = kct.shape[-1]
    per_b = lambda a: pl.BlockSpec((1,) + a.shape[1:], lambda b, j: (b,) + (0,) * (a.ndim - 1))
    cache = pl.BlockSpec((1, H, dh, tk), lambda b, j: (b, 0, 0, j))
    return pl.pallas_call(
        functools.partial(_fox_sample_kernel, n=n, heads=H),
        grid=(DB, P // tk),
        in_specs=[per_b(q), per_b(knt), per_b(vnt), cache, cache, per_b(cq),
                  pl.BlockSpec((1, H, tk), lambda b, j: (b, 0, j)), per_b(ckn)],
        out_specs=per_b(q),
        out_shape=jax.ShapeDtypeStruct(q.shape, BF16),
        scratch_shapes=[pltpu.VMEM((H, n, tk), F32), pltpu.VMEM((H, n, 1), F32), pltpu.VMEM((H, n, 1), F32),
                        pltpu.VMEM((H, n, dh), F32)],
        compiler_params=_cparams("parallel", "arbitrary"),
        name="fox_sample",
    )(q, knt, vnt, kct, vct, cq, ckp, ckn)


def _sb_sample_kernel(q_ref, knt_ref, vnt_ref, kct_ref, vct_ref, o_ref, z_ref, run_ref, acc_ref, *, n, heads, sub):
    kt = pl.program_id(1)
    upper = _strict_upper(sub)

    def absorb(h, z, v_t, upper_m, width, mask=None):
        log_beta, log_rest = _log2_sigmoid_pair(z)
        if mask is not None:
            log_beta = jnp.where(mask, log_beta, NEG_INF)
            log_rest = jnp.where(mask, log_rest, 0.0)
        hi, lo = _split_bf16(log_rest, 2)
        run = run_ref[h]
        parts = []
        for c in reversed(range(z.shape[1] // width)):
            keys = slice(c * width, (c + 1) * width)
            later = _dot(hi[:, keys], upper_m) + _dot(lo[:, keys], upper_m)
            parts.append(jnp.exp2(log_beta[:, keys] + later + run).astype(BF16))
            run = run + jnp.sum(log_rest[:, keys], axis=-1, keepdims=True)
        a = parts[0] if len(parts) == 1 else jnp.concatenate(parts[::-1], axis=1)
        acc_ref[h] += _dot_nt(a, v_t)
        run_ref[h] = run

    def logits(h, _=None):
        z_ref[h] = _dot(q_ref[0, h], kct_ref[0, h].astype(BF16))

    @pl.when(kt == 0)
    def _():
        row = lax.broadcasted_iota(jnp.int32, (n, n), 0)
        col = lax.broadcasted_iota(jnp.int32, (n, n), 1)
        upper_n = _strict_upper(n)
        run_ref[...] = jnp.zeros_like(run_ref)
        acc_ref[...] = jnp.zeros_like(acc_ref)
        for h in range(heads):
            absorb(h, _dot(q_ref[0, h], knt_ref[0, h]), vnt_ref[0, h], upper_n, n, col < row)

    @pl.when(jnp.max(run_ref[...]) > SB_DEAD_LOG2)
    def _():
        logits(0)
        logits(1)
        for h in range(heads):
            _pipeline_ahead(logits, h + 2, heads, None, None)
            absorb(h, z_ref[h], vct_ref[0, h].astype(BF16), upper, sub)

    @pl.when(kt == pl.num_programs(1) - 1)
    def _():
        o_ref[0] = acc_ref[...].astype(o_ref.dtype)


def _sb_sample(q, knt, vnt, kct, vct, tk, sub):
    DB, H, n, dh = q.shape
    P = kct.shape[-1]
    nk = P // tk
    per_b = lambda a: pl.BlockSpec((1,) + a.shape[1:], lambda b, j: (b,) + (0,) * (a.ndim - 1))
    cache = pl.BlockSpec((1, H, dh, tk), lambda b, j: (b, 0, 0, nk - 1 - j))
    return pl.pallas_call(
        functools.partial(_sb_sample_kernel, n=n, heads=H, sub=sub),
        grid=(DB, nk),
        in_specs=[per_b(q), per_b(knt), per_b(vnt), cache, cache],
        out_specs=per_b(q),
        out_shape=jax.ShapeDtypeStruct(q.shape, BF16),
        scratch_shapes=[pltpu.VMEM((H, n, tk), F32), pltpu.VMEM((H, n, 1), F32), pltpu.VMEM((H, n, dh), F32)],
        compiler_params=_cparams("parallel", "arbitrary"),
        name="sb_sample",
    )(q, knt, vnt, kct, vct)


def _fox_sample_cum(lf_new, lf_past_t):
    DB, H, P = lf_past_t.shape
    n = lf_new.shape[1]
    L = -(-(P + n) // LANES) * LANES
    both = jnp.concatenate([lf_past_t.astype(F32), jnp.transpose(lf_new, (0, 2, 1)),
                            jnp.zeros((DB, H, L - P - n), F32)], axis=2)
    cum_t = _cumsum_rows(both) * LOG2E
    ckn = cum_t[:, :, P:P + n]
    return ckn[..., None], cum_t[:, :, :P], ckn


def _heads_major(a, heads, transpose_rows):
    DB, n, _ = a.shape
    a4 = a.reshape(DB, n, heads, -1)
    return jnp.transpose(a4, (0, 2, 3, 1) if transpose_rows else (0, 2, 1, 3))


ROUTER_ROWS = 48


def _layer_norm(y, g, b):
    mu = jnp.mean(y, axis=-1, keepdims=True)
    yc = y - mu
    var = jnp.mean(yc * yc, axis=-1, keepdims=True)
    return yc * lax.rsqrt(var + LN_EPS) * g + b


def _first_argmax(v, ridx):
    vmax = jnp.max(v, axis=0, keepdims=True)
    idx = jnp.min(jnp.where(v == vmax, ridx, v.shape[0]), axis=0, keepdims=True)
    return vmax, idx


def _two_part_specs(tm, width, head_tiles):
    return (pl.BlockSpec((tm, width), lambda i: (jnp.minimum(i, head_tiles - 1), 0)),
            pl.BlockSpec((tm, width), lambda i: (jnp.maximum(i - head_tiles, 0), 0)))


def _mix_out_kernel(*refs, alpha, n_groups, head_tiles, n_parts):
    o_parts, ot_ref = refs[:n_parts], refs[n_parts]
    w_ref, xh_ref, xt_ref, g_ref, b_ref, wrh_ref, wrl_ref, rb_ref, h_ref, ids_ref, wts_ref = refs[n_parts + 1:]
    in_head = pl.program_id(0) < head_tiles
    x = jnp.where(in_head, xh_ref[...], xt_ref[...])
    o_head = o_parts[0][...] if n_parts == 1 else jnp.concatenate([r[...] for r in o_parts], axis=1)
    o = jnp.where(in_head, o_head, ot_ref[...])
    h = _layer_norm(alpha * x + _dot(o, w_ref[...]), g_ref[...], b_ref[...])
    h_ref[...] = h
    hh, hl = _split_bf16(h, 2)
    R = ROUTER_ROWS
    wr = _dot_nt(wrl_ref[...], hh)
    lg = wr[:R] + (_dot_nt(wrh_ref[...], hl) + (wr[R:2 * R] + wr[2 * R:])) + rb_ref[...]
    tm = lg.shape[1]
    ridx = lax.broadcasted_iota(jnp.int32, (8, tm), 0)
    g = jnp.where(ridx < n_groups, lg[0:8], NEG_INF)
    gmax, gidx = _first_argmax(g, ridx)
    gate = 1.0 / jnp.sum(jnp.exp(g - gmax), axis=0, keepdims=True)
    esel = lg[8:16]
    for gg in range(1, n_groups):
        esel = jnp.where(gidx == gg, lg[8 + 8 * gg:16 + 8 * gg], esel)
    v1, i1 = _first_argmax(esel, ridx)
    v2, i2 = _first_argmax(jnp.where(ridx == i1, NEG_INF, esel), ridx)
    t = jnp.exp(v2 - v1)
    w1 = 1.0 / (1.0 + t)
    ids_ref[...] = jnp.where(ridx == 0, gidx * 8 + i1, jnp.where(ridx == 1, gidx * 8 + i2, 0))
    wts_ref[...] = jnp.where(ridx == 0, gate * w1, jnp.where(ridx == 1, gate * (t * w1), 0.0))


def _mix_out(o_head_parts, o_tail, w, x_head, x_tail, g, b, wrh, wrl, rb, alpha, n_groups, tm):
    D = x_head.shape[1]
    T = x_head.shape[0] + x_tail.shape[0]
    head_tiles = x_head.shape[0] // tm
    const = lambda a: pl.BlockSpec(a.shape, lambda i: (0,) * a.ndim)
    rb_t = jnp.broadcast_to(rb, (ROUTER_ROWS, tm))
    head_spec = lambda width: _two_part_specs(tm, width, head_tiles)[0]
    return pl.pallas_call(
        functools.partial(_mix_out_kernel, alpha=alpha, n_groups=n_groups, head_tiles=head_tiles,
                          n_parts=len(o_head_parts)),
        grid=(T // tm,),
        in_specs=[*[head_spec(p.shape[1]) for p in o_head_parts], _two_part_specs(tm, o_tail.shape[1], head_tiles)[1],
                  const(w), *_two_part_specs(tm, D, head_tiles), const(g), const(b), const(wrh), const(wrl),
                  const(rb_t)],
        out_specs=[pl.BlockSpec((tm, D), lambda i: (i, 0)), pl.BlockSpec((8, tm), lambda i: (0, i)),
                   pl.BlockSpec((8, tm), lambda i: (0, i))],
        out_shape=[jax.ShapeDtypeStruct((T, D), F32), jax.ShapeDtypeStruct((8, T), jnp.int32),
                   jax.ShapeDtypeStruct((8, T), F32)],
        compiler_params=_cparams("parallel"),
        name="mix_out_ln_router",
    )(*o_head_parts, o_tail, w, x_head, x_tail, g, b, wrh, wrl, rb_t)


def _prep_router(w_group, b_group, w_router, b_router):
    D, G = w_group.shape
    E = w_router.shape[-1]
    wr = jnp.zeros((ROUTER_ROWS, D), F32)
    wr = wr.at[:G].set(w_group.T.astype(F32))
    wr = wr.at[8:8 + G * E].set(jnp.transpose(w_router, (0, 2, 1)).reshape(G * E, D).astype(F32))
    rb = jnp.zeros((ROUTER_ROWS, 1), F32)
    rb = rb.at[:G, 0].set(b_group.astype(F32)).at[8:8 + G * E, 0].set(b_router.reshape(-1).astype(F32))
    hi, mid, lo = _split_bf16_trunc(wr, 3)
    return hi, jnp.concatenate([hi, mid, lo], axis=0), rb


def _gather_rows(idx_ref, lo, hi, src_hbm, dst, sem):
    for r in range(lo, hi):
        pltpu.make_async_copy(src_hbm.at[pl.ds(idx_ref[0, 0, r], 1)], dst.at[pl.ds(r, 1)], sem).start()


def _wait_rows(n, src_hbm, dst, sem):
    pltpu.make_async_copy(src_hbm.at[pl.ds(0, n)], dst, sem).wait()


def _moe_experts_kernel(te_ref, tv_ref, src_ref, nxt_ref, x_hbm, wg_ref, wu_ref, wd_ref, y_ref, xbuf, sem, *, tm):
    i = pl.program_id(0)
    nt = pl.num_programs(0)
    slot = i % 2

    @pl.when(jnp.logical_and(i == 0, tv_ref[0] > 0))
    def _():
        _gather_rows(src_ref, 0, tm, x_hbm, xbuf.at[0], sem.at[0])

    @pl.when(jnp.logical_and(i + 1 < nt, tv_ref[jnp.minimum(i + 1, nt - 1)] > 0))
    def _():
        _gather_rows(nxt_ref, 0, tm, x_hbm, xbuf.at[1 - slot], sem.at[1 - slot])

    @pl.when(tv_ref[i] > 0)
    def _():
        _wait_rows(tm, x_hbm, xbuf.at[slot], sem.at[slot])
        xb = xbuf[slot].astype(BF16)
        a = _dot(xb, wg_ref[0].astype(BF16))
        u = _dot(xb, wu_ref[0].astype(BF16))
        hid = (a / (1.0 + jnp.exp(-a))) * u
        y_ref[...] = _dot(hid.astype(BF16), wd_ref[0].astype(BF16))

    @pl.when(tv_ref[i] == 0)
    def _():
        y_ref[...] = jnp.zeros_like(y_ref)


def _moe_experts(x, w_gate, w_up, w_down, tile_expert, tile_valid, src, tm):
    T, D = x.shape
    F = w_gate.shape[-1]
    NT = tile_expert.shape[0]
    grid_spec = pltpu.PrefetchScalarGridSpec(
        num_scalar_prefetch=2,
        grid=(NT,),
        in_specs=[pl.BlockSpec((1, 1, tm), lambda i, te, tv: (i, 0, 0), memory_space=pltpu.SMEM),
                  pl.BlockSpec((1, 1, tm), lambda i, te, tv: (i + 1, 0, 0), memory_space=pltpu.SMEM),
                  pl.BlockSpec(memory_space=pl.ANY),
                  pl.BlockSpec((1, D, F), lambda i, te, tv: (te[i], 0, 0)),
                  pl.BlockSpec((1, D, F), lambda i, te, tv: (te[i], 0, 0)),
                  pl.BlockSpec((1, F, D), lambda i, te, tv: (te[i], 0, 0))],
        out_specs=pl.BlockSpec((tm, D), lambda i, te, tv: (i, 0)),
        scratch_shapes=[pltpu.VMEM((2, tm, D), F32), pltpu.SemaphoreType.DMA((2,))],
    )
    return pl.pallas_call(
        functools.partial(_moe_experts_kernel, tm=tm),
        grid_spec=grid_spec,
        out_shape=jax.ShapeDtypeStruct((NT * tm, D), F32),
        compiler_params=_cparams("arbitrary"),
        name="moe_experts",
    )(tile_expert, tile_valid, src, src, x, w_gate, w_up, w_down)


def _moe_combine_kernel(pos_ref, nxt_ref, ys_hbm, h_ref, w_ref, g_ref, b_ref, oh_ref, ot_ref, buf, sem,
                        *, tm, alpha, head_tiles):
    i = pl.program_id(0)
    nt = pl.num_programs(0)
    slot = i % 2

    @pl.when(i == 0)
    def _():
        _gather_rows(pos_ref, 0, 2 * tm, ys_hbm, buf.at[0], sem.at[0])

    @pl.when(i + 1 < nt)
    def _():
        _gather_rows(nxt_ref, 0, 2 * tm, ys_hbm, buf.at[1 - slot], sem.at[1 - slot])

    _wait_rows(2 * tm, ys_hbm, buf.at[slot], sem.at[slot])
    w = w_ref[...]
    y = _layer_norm(alpha * h_ref[...] + (w[:, 0:1] * buf[slot, 0:tm] + w[:, 1:2] * buf[slot, tm:2 * tm]),
                    g_ref[...], b_ref[...])

    @pl.when(i < head_tiles)
    def _():
        oh_ref[...] = y

    @pl.when(i >= head_tiles)
    def _():
        ot_ref[...] = y


def _moe_combine(ys, h, pos, wts, g, b, alpha, tm, head_rows):
    T, D = h.shape
    nt = T // tm
    head_tiles = head_rows // tm
    const = lambda a: pl.BlockSpec(a.shape, lambda i: (0,) * a.ndim)
    return pl.pallas_call(
        functools.partial(_moe_combine_kernel, tm=tm, alpha=alpha, head_tiles=head_tiles),
        grid=(nt,),
        in_specs=[pl.BlockSpec((1, 1, 2 * tm), lambda i: (i, 0, 0), memory_space=pltpu.SMEM),
                  pl.BlockSpec((1, 1, 2 * tm), lambda i: (i + 1, 0, 0), memory_space=pltpu.SMEM),
                  pl.BlockSpec(memory_space=pl.ANY),
                  pl.BlockSpec((tm, D), lambda i: (i, 0)),
                  pl.BlockSpec((tm, 2), lambda i: (i, 0)), const(g), const(b)],
        out_specs=list(_two_part_specs(tm, D, head_tiles)),
        out_shape=[jax.ShapeDtypeStruct((head_rows, D), F32), jax.ShapeDtypeStruct((T - head_rows, D), F32)],
        scratch_shapes=[pltpu.VMEM((2, 2 * tm, D), F32), pltpu.SemaphoreType.DMA((2,))],
        compiler_params=_cparams("arbitrary"),
        name="moe_combine_ln",
    )(pos, pos, ys, h, wts, g, b)


def _route(ids, n_experts, tm):
    T = ids.shape[1]
    flat = ids.reshape(-1)
    iota = jnp.arange(2 * T, dtype=jnp.int32)
    sorted_e, order = lax.sort((flat, iota), num_keys=1, is_stable=True)
    _, inverse = lax.sort((order, iota), num_keys=1)
    experts = jnp.arange(n_experts, dtype=jnp.int32)
    counts = jnp.sum((flat[:, None] == experts[None, :]).astype(jnp.int32), axis=0)
    padded = (counts + tm - 1) // tm * tm
    ends = jnp.cumsum(padded)
    shift = (ends - padded) - (jnp.cumsum(counts) - counts)
    NT = (2 * T + n_experts * (tm - 1)) // tm
    tile_start = jnp.arange(NT, dtype=jnp.int32) * tm
    tile_expert = jnp.minimum(jnp.sum((tile_start[:, None] >= ends[None, :]).astype(jnp.int32), axis=1),
                              n_experts - 1)
    tile_valid = (tile_start < ends[-1]).astype(jnp.int32)
    pos = (inverse + shift[flat]).reshape(2, T)
    row = jnp.arange((NT + 1) * tm, dtype=jnp.int32)
    row_shift = jnp.repeat(shift[jnp.concatenate([tile_expert, tile_expert[-1:]])], tm)
    src = (order % T)[jnp.clip(row - row_shift, 0, 2 * T - 1)]
    return tile_expert, tile_valid, src.reshape(NT + 1, 1, tm), pos


MLA_PAIR_W = 2 * LANES
MLA_SAMPLE_CHUNK = 256
QK_NOPE = 64
QK_ROPE = 32


def _mla_proj_kernel(hh_ref, ht_ref, wdn_ref, gq_ref, gkv_ref, wq_ref, wqr_ref, wk_ref, wv_ref, cos_ref, sin_ref,
                     ckv_ref, kr_ref, qcat_ref, kcat_ref, vt_ref, *, q_lora, kv_lora, npairs, scale, head_tiles):
    h = jnp.where(pl.program_id(0) < head_tiles, hh_ref[...], ht_ref[...])
    z = _dot(h.astype(BF16), wdn_ref[...])
    cq = z[:, :q_lora]
    ckv = z[:, q_lora:q_lora + kv_lora]
    o = q_lora + kv_lora
    kr_raw = z[:, o:o + HEAD_W]
    kr_rot = z[:, o + HEAD_W:o + 2 * HEAD_W]
    cq = cq * lax.rsqrt(jnp.mean(cq * cq, axis=-1, keepdims=True) + RMS_EPS) * gq_ref[...]
    ckv = ckv * lax.rsqrt(jnp.mean(ckv * ckv, axis=-1, keepdims=True) + RMS_EPS) * gkv_ref[...]
    ckv_ref[...] = ckv
    cos = cos_ref[...]
    sin = sin_ref[...]
    kr_tile = kr_raw * cos + kr_rot * sin
    kr_ref[...] = kr_tile[:, QK_NOPE:QK_NOPE + QK_ROPE]
    cqb = cq.astype(BF16)
    ckb = ckv.astype(BF16)
    cos2 = jnp.concatenate([cos, cos], axis=1)
    sin2 = jnp.concatenate([sin, sin], axis=1)
    kr2 = jnp.concatenate([kr_tile, kr_tile], axis=1)
    for p in range(npairs):
        lanes = slice(p * MLA_PAIR_W, (p + 1) * MLA_PAIR_W)
        q = _dot(cqb, wq_ref[:, lanes]) * cos2 + _dot(cqb, wqr_ref[:, lanes]) * sin2
        qcat_ref[:, lanes] = (q * scale).astype(BF16)
        kcat_ref[:, lanes] = (_dot(ckb, wk_ref[:, lanes]) + kr2).astype(BF16)
    heads = 2 * npairs
    vt = _dot_nt(wv_ref[...], ckb).reshape(heads, HEAD_DIM, ckb.shape[0])
    vt_ref[0, :, :HEAD_DIM, :] = vt.astype(BF16)
    vt_ref[0, :, HEAD_DIM:, :] = jnp.ones((heads, VT_ROWS - HEAD_DIM, ckb.shape[0]), BF16)


def _mla_proj(h_head, h_tail, wdn, gq, gkv, wq, wqr, wk, wv, cos_t, sin_t, table_block, tm):
    D = h_head.shape[1]
    T = h_head.shape[0] + h_tail.shape[0]
    head_tiles = h_head.shape[0] // tm
    q_lora, kv_lora = gq.shape[1], gkv.shape[1]
    npairs = wq.shape[1] // MLA_PAIR_W
    const = lambda a: pl.BlockSpec(a.shape, lambda i: (0,) * a.ndim)
    row = lambda w_: pl.BlockSpec((tm, w_), lambda i: (i, 0))
    table = pl.BlockSpec((tm, HEAD_W), lambda i: (table_block(i), 0))
    return pl.pallas_call(
        functools.partial(_mla_proj_kernel, q_lora=q_lora, kv_lora=kv_lora, npairs=npairs,
                          scale=(QK_NOPE + QK_ROPE) ** -0.5 * LOG2E, head_tiles=head_tiles),
        grid=(T // tm,),
        in_specs=[*_two_part_specs(tm, D, head_tiles), const(wdn), const(gq), const(gkv), const(wq), const(wqr),
                  const(wk), const(wv), table, table],
        out_specs=[row(kv_lora), row(QK_ROPE), row(wq.shape[1]), row(wk.shape[1]),
                   pl.BlockSpec((1, 2 * npairs, VT_ROWS, tm), lambda i: (i, 0, 0, 0))],
        out_shape=[jax.ShapeDtypeStruct((T, kv_lora), F32), jax.ShapeDtypeStruct((T, QK_ROPE), F32),
                   jax.ShapeDtypeStruct((T, wq.shape[1]), BF16), jax.ShapeDtypeStruct((T, wk.shape[1]), BF16),
                   jax.ShapeDtypeStruct((T // tm, 2 * npairs, VT_ROWS, tm), BF16)],
        compiler_params=_cparams("parallel"),
        name="mla_proj",
    )(h_head, h_tail, wdn, gq, gkv, wq, wqr, wk, wv, cos_t, sin_t)


def _rot_half(w):
    half = w.shape[-1] // 2
    return jnp.concatenate([-w[..., half:], w[..., :half]], axis=-1)


def _prep_mla_weights(w_down, w_uq, w_ukv, heads, q_lora, kv_lora):
    D = w_down.shape[0]
    tail = HEAD_W - QK_NOPE - QK_ROPE
    w_kr = w_down[:, q_lora + kv_lora:]
    slot = lambda w: jnp.concatenate([jnp.zeros((D, QK_NOPE), w.dtype), w, jnp.zeros((D, tail), w.dtype)], axis=1)
    wdn = jnp.concatenate([w_down[:, :q_lora + kv_lora], slot(w_kr), slot(_rot_half(w_kr))], axis=1)
    wq3 = w_uq.reshape(q_lora, heads, QK_NOPE + QK_ROPE)
    nope, ropew = wq3[..., :QK_NOPE], wq3[..., QK_NOPE:]
    zpad = jnp.zeros((q_lora, heads, tail), w_uq.dtype)
    wq = jnp.concatenate([nope, ropew, zpad], axis=-1)
    wqr = jnp.concatenate([jnp.zeros_like(nope), _rot_half(ropew), zpad], axis=-1)
    wkv3 = w_ukv.reshape(kv_lora, heads, QK_NOPE + HEAD_DIM)
    w_uk, w_uv = wkv3[..., :QK_NOPE], wkv3[..., QK_NOPE:]
    wk = jnp.concatenate([w_uk, jnp.zeros((kv_lora, heads, HEAD_W - QK_NOPE), w_ukv.dtype)], axis=-1)
    wv = jnp.transpose(w_uv.reshape(kv_lora, heads * HEAD_DIM))
    b16 = lambda a: a.astype(BF16)
    return (b16(wdn), b16(wq.reshape(q_lora, -1)), b16(wqr.reshape(q_lora, -1)), b16(wk.reshape(kv_lora, -1)),
            b16(wv), b16(jnp.transpose(w_uk, (1, 2, 0))), b16(jnp.transpose(w_uv, (1, 0, 2))))


def _rope_tables(pos):
    half = QK_ROPE // 2
    inv_freq = ROPE_BASE ** (-jnp.arange(half, dtype=F32) / half)
    ang = pos.astype(F32)[:, None] * inv_freq[None, :]
    n = pos.shape[0]
    pad = jnp.zeros((n, HEAD_W - QK_NOPE - QK_ROPE), F32)
    cos = jnp.concatenate([jnp.ones((n, QK_NOPE), F32)] + [jnp.cos(ang)] * 2 + [pad], axis=1)
    sin = jnp.concatenate([jnp.zeros((n, QK_NOPE), F32)] + [jnp.sin(ang)] * 2 + [pad], axis=1)
    return cos, sin


def _mla_sample_queries(qcat, heads):
    DB, n, _ = qcat.shape
    q4 = qcat.reshape(DB, n, heads, HEAD_W)
    rows = lambda a: jnp.transpose(a, (0, 2, 1, 3)).reshape(DB, heads * n, a.shape[-1])
    return rows(q4[..., :QK_NOPE]), rows(q4[..., QK_NOPE:QK_NOPE + QK_ROPE])


def _mla_sample_kernel(qn_ref, qr_ref, wuk_ref, wuv_ref, cc_ref, rc_ref, cn_ref, rn_ref, o_ref,
                       qlat_ref, s_ref, m_ref, l_ref, acc_ref, *, n, heads):
    kt = pl.program_id(1)

    @pl.when(kt == 0)
    def _():
        for h in range(heads):
            rows = slice(h * n, (h + 1) * n)
            qlat_ref[rows, :] = _dot(qn_ref[0, rows, :], wuk_ref[h]).astype(BF16)
        m_ref[...] = jnp.full(m_ref.shape, NEG_INF, F32)
        l_ref[...] = jnp.zeros_like(l_ref)
        acc_ref[...] = jnp.zeros_like(acc_ref)

    def scores(rows, ckv, kr_t):
        return _dot_nt(qlat_ref[rows, :], ckv) + _dot(qr_ref[0, rows, :], kr_t)

    def absorb(rows, s, ckv):
        m, l, acc = _softmax_step(s[None], ckv, (m_ref[:, rows], l_ref[:, rows], acc_ref[:, rows]))
        m_ref[:, rows] = m
        l_ref[:, rows] = l
        acc_ref[:, rows] = acc

    def update(ckv, kr_t):
        absorb(slice(None), scores(slice(None), ckv, kr_t), ckv)

    ckv = cc_ref[0].astype(BF16)
    kr_t = rc_ref[0].astype(BF16)
    chunk_rows = s_ref.shape[1]
    chunks = [slice(c * chunk_rows, (c + 1) * chunk_rows) for c in range(s_ref.shape[0])]
    s_ref[0] = scores(chunks[0], ckv, kr_t)
    for c, rows in enumerate(chunks):
        if c + 1 < len(chunks):
            s_ref[c + 1] = scores(chunks[c + 1], ckv, kr_t)
        absorb(rows, s_ref[c], ckv)

    @pl.when(kt == pl.num_programs(1) - 1)
    def _():
        update(cn_ref[0], rn_ref[0])
        o_lat = (acc_ref[0] / l_ref[0]).astype(BF16)
        for h in range(heads):
            o_ref[0, :, h * HEAD_DIM:(h + 1) * HEAD_DIM] = _dot(o_lat[h * n:(h + 1) * n], wuv_ref[h]).astype(o_ref.dtype)


def _mla_sample(qn, qr, wuk, wuv, ckv_c, kr_c, ckv_n, kr_n, n, tk):
    DB, R, _ = qn.shape
    heads = R // n
    P, C = ckv_c.shape[1], ckv_c.shape[2]
    const = lambda a: pl.BlockSpec(a.shape, lambda b, j: (0,) * a.ndim)
    per_b = lambda a: pl.BlockSpec((1,) + a.shape[1:], lambda b, j: (b, 0, 0))
    return pl.pallas_call(
        functools.partial(_mla_sample_kernel, n=n, heads=heads),
        grid=(DB, P // tk),
        in_specs=[per_b(qn), per_b(qr), const(wuk), const(wuv),
                  pl.BlockSpec((1, tk, C), lambda b, j: (b, j, 0)),
                  pl.BlockSpec((1, QK_ROPE, tk), lambda b, j: (b, 0, j)),
                  per_b(ckv_n), per_b(kr_n)],
        out_specs=pl.BlockSpec((1, n, heads * HEAD_DIM), lambda b, j: (b, 0, 0)),
        out_shape=jax.ShapeDtypeStruct((DB, n, heads * HEAD_DIM), BF16),
        scratch_shapes=[pltpu.VMEM((R, C), BF16), pltpu.VMEM((R // MLA_SAMPLE_CHUNK, MLA_SAMPLE_CHUNK, tk), F32),
                        pltpu.VMEM((1, R, 1), F32), pltpu.VMEM((1, R, 1), F32), pltpu.VMEM((1, R, C), F32)],
        compiler_params=_cparams("parallel", "arbitrary"),
        name="mla_sample",
    )(qn, qr, wuk, wuv, ckv_c, kr_c, ckv_n, kr_n)


def _moe_layer(h, ids, wts, w_gate, w_up, w_down, layer, g, b, alpha, tm, head_rows):
    T, D = h.shape
    n_experts = w_gate.shape[1] * w_gate.shape[2]
    tile_expert, tile_valid, src, pos = _route(ids[:2], n_experts, tm)
    flat3 = lambda w: w.reshape((-1,) + w.shape[3:])
    ys = _moe_experts(h, flat3(w_gate), flat3(w_up), flat3(w_down), tile_expert + layer * n_experts, tile_valid,
                      src, tm)
    pos_t = jnp.transpose(pos.reshape(2, T // tm, tm), (1, 0, 2)).reshape(T // tm, 1, 2 * tm)
    pos_t = jnp.pad(pos_t, ((0, 1), (0, 0), (0, 0)))
    return _moe_combine(ys, h, pos_t, jnp.transpose(wts[:2]), g, b, alpha, tm, head_rows)


TOKEN_TILE = 256
MIX_TILE = 512
FLASH_Q_TILE = 512
FLASH_KEY_TILE = 256
SB_Q_TILE = 512
SB_KEY_TILE = 128
CACHE_TILE = 2048
SB_CACHE_TILE = 1024
SB_SUB_TILE = 256


def kernel(x_prompt, x_sample, cache_fox_k, cache_fox_v, cache_fox_logf, cache_sb_k, cache_sb_v, cache_mla_ckv, cache_mla_krope, ab_w_in, ab_b_forget, ab_w_out, mla_w_down, mla_g_q, mla_g_kv, mla_w_uq, mla_w_ukv, mla_w_out, moe_w_group, moe_b_group, moe_w_router, moe_b_router, moe_w_gate, moe_w_up, moe_w_down, ln_g, ln_b):
    B, S, D = x_prompt.shape
    DB, n, _ = x_sample.shape
    P = cache_fox_k.shape[2]
    TP, TS = B * S, DB * n
    depth = ln_g.shape[0]
    n_groups = moe_w_group.shape[-1]
    assert depth == 2 and ab_w_in.shape[0] == 1 and mla_w_down.shape[0] == 1
    assert S % FLASH_Q_TILE == 0 and S % SB_Q_TILE == 0 and TP % MIX_TILE == 0 and TS % MIX_TILE == 0 and MIX_TILE % TOKEN_TILE == 0 and TOKEN_TILE % n == 0
    assert P % CACHE_TILE == 0 and P % SB_CACHE_TILE == 0 and P % CHUNK == 0 and n == CHUNK
    alpha = (2 * depth) ** 0.25
    tk = CACHE_TILE

    xp, xs = x_prompt.reshape(TP, D), x_sample.reshape(TS, D)
    sample3 = lambda a: a[TP:].reshape(DB, n, -1)

    def ffn(o_head_parts, o_tail, w_out, resid, layer):
        wrh, wrl, rb = _prep_router(moe_w_group[layer], moe_b_group[layer], moe_w_router[layer], moe_b_router[layer])
        h, ids, wts = _mix_out(o_head_parts, o_tail, w_out.astype(BF16), *resid, ln_g[layer, 0][None],
                               ln_b[layer, 0][None], wrh, wrl, rb, alpha, n_groups, MIX_TILE)
        return _moe_layer(h, ids, wts, moe_w_gate, moe_w_up, moe_w_down, layer,
                          ln_g[layer, 1][None], ln_b[layer, 1][None], alpha, TOKEN_TILE, TP)

    fox_heads = ab_b_forget.shape[1]
    hw = (ab_w_in.shape[2] - fox_heads) // 6
    w_ab, b_forget = _prep_ab_weights(ab_w_in[0], ab_b_forget[0])
    w_tok, w_t = _prompt_ab_weights(w_ab, hw, fox_heads)
    (qa_p, ka_p, qb_p, kb_p, lf_p, kat, vat, kbt, vbt, vat16, vbt16) = _ab_proj_prompt(
        xp, w_tok, b_forget, w_t, _query_decay_ones(fox_heads), B, S, fox_heads, SB_KEY_TILE)
    ka_p = _fox_insert_decay(ka_p, lf_p, B, S, fox_heads)
    qa, ka, va, qb, kb, vb, ka16, va16, kb16, vb16, lf = _ab_proj(xs, w_ab, b_forget, hw)
    dbn = lambda a: a.reshape(DB, n, -1)
    lf_s = dbn(lf)[:, :, :fox_heads]
    cq_s, ck_past, ck_new = _fox_sample_cum(lf_s, jnp.transpose(cache_fox_logf[0], (0, 2, 1)))
    cache_t = lambda c: jnp.transpose(c[0], (0, 2, 3, 1))
    hm = lambda a, t=False: _heads_major(dbn(a), fox_heads, t)
    o_fox_s = _fox_sample(hm(qa), hm(ka16, True), hm(va16, True), cache_t(cache_fox_k), cache_t(cache_fox_v),
                          cq_s, ck_past, ck_new, tk)
    o_sb_s = _sb_sample(hm(qb), hm(kb16, True), hm(vb16, True), cache_t(cache_sb_k), cache_t(cache_sb_v),
                        SB_CACHE_TILE, SB_SUB_TILE)
    tokens_major = lambda a: jnp.transpose(a, (0, 2, 1, 3)).reshape(TS, hw)
    o_tail = jnp.concatenate([tokens_major(o_fox_s), tokens_major(o_sb_s)], axis=-1)
    o_fox_p = _flash_prompt(qa_p, ka_p, vat16, B, S, FLASH_Q_TILE, fox_heads, 1, "fox_prompt")
    o_sb_p = _sb_prompt(qb_p, kb_p, vbt16, B, S, SB_Q_TILE)
    xp, xs = ffn((o_fox_p, o_sb_p), o_tail, ab_w_out[0], (xp, xs), 0)

    q_lora, kv_lora = mla_g_q.shape[1], mla_g_kv.shape[1]
    heads = mla_w_uq.shape[2] // (QK_NOPE + QK_ROPE)
    wdn, wq, wqr, wk, wv, wuk_t, wuv = _prep_mla_weights(mla_w_down[0], mla_w_uq[0], mla_w_ukv[0], heads, q_lora, kv_lora)
    tm = TOKEN_TILE
    pos = jnp.concatenate([jnp.arange(S, dtype=jnp.int32), P + jnp.arange(tm, dtype=jnp.int32) % n])
    cos_t, sin_t = _rope_tables(pos)
    blocks_per_seq, prompt_blocks = S // tm, TP // tm
    table_block = lambda i: jnp.where(i < prompt_blocks, i % blocks_per_seq, blocks_per_seq)
    assert tm == FLASH_KEY_TILE
    ckv, kr, qcat, kcat, vt = _mla_proj(xp, xs, wdn, mla_g_q[0][None], mla_g_kv[0][None], wq, wqr, wk, wv,
                                        cos_t, sin_t, table_block, tm)
    qn, qr = _mla_sample_queries(sample3(qcat), heads)
    o_s = _mla_sample(qn, qr, wuk_t, wuv, cache_mla_ckv[0], jnp.transpose(cache_mla_krope[0], (0, 2, 1)),
                      sample3(ckv).astype(BF16), jnp.transpose(sample3(kr), (0, 2, 1)).astype(BF16), n, tk)
    o_p = _flash_prompt(qcat, kcat, vt, B, S, FLASH_Q_TILE, 8, CHUNK, "mla_prompt")
    xp, xs = ffn((o_p,), o_s.reshape(TS, -1), mla_w_out[0], (xp, xs), 1)

    rows_p = lambda a: jnp.transpose(a, (0, 3, 1, 2))[None]
    rows_s = lambda a: a.reshape(1, DB, n, fox_heads, hw // fox_heads)
    pr, sr = slice(0, TP), slice(TP, TP + TS)
    return (xp.reshape(B, S, D), xs.reshape(DB, n, D),
            rows_p(kat), rows_p(vat), lf_p[:, :fox_heads].reshape(1, B, S, fox_heads), rows_p(kbt), rows_p(vbt),
            ckv[pr].reshape(1, B, S, kv_lora), kr[pr].reshape(1, B, S, QK_ROPE),
            rows_s(ka), rows_s(va), lf[:, :fox_heads].reshape(1, DB, n, fox_heads), rows_s(kb), rows_s(vb),
            ckv[sr].reshape(1, DB, n, kv_lora), kr[sr].reshape(1, DB, n, QK_ROPE))
```

```python
import functools

import jax
import jax.numpy as jnp
from jax import lax
from jax.experimental import pallas as pl
from jax.experimental.pallas import tpu as pltpu

F32 = jnp.float32
BF16 = jnp.bfloat16
NEG_INF = -1e30
LOG2E = 1.4426950408889634

LANES = 128
HEAD_DIM = 64
PAIR_W = 2 * HEAD_DIM
CHUNK = 64
LN_EPS = 1e-5
RMS_EPS = 1e-6
ROPE_BASE = 10000.0
VMEM_LIMIT = 56 * 1024 * 1024


def _cparams(*sem):
    return pltpu.CompilerParams(dimension_semantics=sem, vmem_limit_bytes=VMEM_LIMIT)


def _dot(a, b):
    return jnp.dot(a, b, preferred_element_type=F32)


def _dot_nt(a, b):
    return lax.dot_general(a, b, (((1,), (1,)), ((), ())), preferred_element_type=F32)


def _split_bf16(x, parts):
    out = []
    r = x
    for _ in range(parts):
        h = r.astype(BF16)
        out.append(h)
        r = r - h.astype(F32)
    return out


def _split_bf16_trunc(x, parts):
    out = []
    r = x
    for _ in range(parts):
        bits = lax.bitcast_convert_type(r, jnp.uint32) & jnp.uint32(0xFFFF0000)
        h = lax.bitcast_convert_type(bits, F32)
        out.append(h.astype(BF16))
        r = r - h
    return out


def _log_sigmoid(x):
    return jnp.minimum(x, 0.0) - jnp.log(1.0 + jnp.exp(-jnp.abs(x)))


def _pick_tile(n, pref, mult=8):
    t = min(pref, n)
    while n % t or t % mult:
        t -= 1
    return t


def _ab_proj_kernel(x_ref, w_ref, bf_ref, qa_ref, ka_ref, va_ref, qb_ref, kb_ref, vb_ref,
                    ka16_ref, va16_ref, kb16_ref, vb16_ref, lf_ref, *, hw, qscale):
    xb = x_ref[...].astype(BF16)

    def seg(j):
        return _dot(xb, w_ref[:, j * hw:(j + 1) * hw])

    qa_ref[...] = (seg(0) * qscale).astype(BF16)
    z = seg(1)
    ka_ref[...] = z
    ka16_ref[...] = z.astype(BF16)
    z = seg(2)
    va_ref[...] = z
    va16_ref[...] = z.astype(BF16)
    qb_ref[...] = (seg(3) * qscale).astype(BF16)
    z = seg(4)
    kb_ref[...] = z
    kb16_ref[...] = z.astype(BF16)
    z = seg(5)
    vb_ref[...] = z
    vb16_ref[...] = z.astype(BF16)
    f = _dot(xb, w_ref[:, 6 * hw:6 * hw + LANES]) + bf_ref[...]
    lf_ref[...] = _log_sigmoid(f)


def _ab_proj(x, w, bf, hw):
    T, D = x.shape
    tm = _pick_tile(T, 256)
    row = lambda w_: pl.BlockSpec((tm, w_), lambda i: (i, 0))
    f32o = jax.ShapeDtypeStruct((T, hw), F32)
    b16o = jax.ShapeDtypeStruct((T, hw), BF16)
    return pl.pallas_call(
        functools.partial(_ab_proj_kernel, hw=hw, qscale=HEAD_DIM ** -0.5 * LOG2E),
        grid=(T // tm,),
        in_specs=[row(D), pl.BlockSpec(w.shape, lambda i: (0, 0)), pl.BlockSpec(bf.shape, lambda i: (0, 0))],
        out_specs=[row(hw)] * 10 + [row(LANES)],
        out_shape=[b16o, f32o, f32o, b16o, f32o, f32o, b16o, b16o, b16o, b16o,
                   jax.ShapeDtypeStruct((T, LANES), F32)],
        compiler_params=_cparams("parallel"),
        name="ab_proj",
    )(x, w, bf)


def _ab_proj_prompt_kernel(x_ref, w_ref, bf_ref, wt_ref, qone_ref, qa_ref, ka_ref, qb_ref, kb_ref, lf_ref,
                           kat_ref, vat_ref, kbt_ref, vbt_ref, vat16_ref, vbt16_ref, *, hw, heads, qscale, sb_tk):
    xb = x_ref[...].astype(BF16)
    tm = xb.shape[0]
    lo = lax.broadcasted_iota(jnp.int32, (tm, LANES), 1) < HEAD_DIM
    zero = jnp.zeros((tm, LANES), F32)

    def seg(j):
        z = _dot(xb, w_ref[:, j * hw:(j + 1) * hw])
        odd = pltpu.roll(z, hw - HEAD_DIM, axis=1)
        cols = []
        for c in range(hw // LANES):
            g = slice(c * LANES, (c + 1) * LANES)
            cols += [jnp.where(lo, z[:, g], zero), jnp.where(lo, odd[:, g], zero)]
        return jnp.concatenate(cols, axis=1)

    def seg_t(j):
        return _dot_nt(wt_ref[j * hw:(j + 1) * hw, :], xb).reshape(heads, hw // heads, tm)

    qa_ref[...] = (seg(0) * qscale + qone_ref[...]).astype(BF16)
    ka_ref[...] = seg(1).astype(BF16)
    qb_ref[...] = (seg(2) * qscale).astype(BF16)
    kb_ref[...] = seg(3).astype(BF16)
    lf_ref[...] = _log_sigmoid(_dot(xb, w_ref[:, 4 * hw:4 * hw + LANES]) + bf_ref[...])
    kat_ref[0] = seg_t(0)
    z = seg_t(1)
    vat_ref[0] = z
    vat16_ref[0, :, :HEAD_DIM, :] = z.astype(BF16)
    vat16_ref[0, :, HEAD_DIM:, :] = jnp.ones((heads, VT_ROWS - HEAD_DIM, tm), BF16)
    kbt_ref[0] = seg_t(2)
    z = seg_t(3)
    vbt_ref[0] = z
    for c in range(tm // sb_tk):
        vbt16_ref[c] = z[:, :, c * sb_tk:(c + 1) * sb_tk].astype(BF16)


def _ab_proj_prompt(x, w_tok, bf, w_t, q_ones, B, S, heads, sb_tk):
    TP, D = x.shape
    hw = w_t.shape[0] // 4
    wide = heads * HEAD_W
    tm = FLASH_KEY_TILE
    nj = S // tm
    const = lambda a: pl.BlockSpec(a.shape, lambda b, j: (0,) * a.ndim)
    row = lambda w_: pl.BlockSpec((tm, w_), lambda b, j: (b * nj + j, 0))
    t_spec = pl.BlockSpec((1, heads, hw // heads, tm), lambda b, j: (b, 0, 0, j))
    b16 = jax.ShapeDtypeStruct((TP, wide), BF16)
    t32 = jax.ShapeDtypeStruct((B, heads, hw // heads, S), F32)
    return pl.pallas_call(
        functools.partial(_ab_proj_prompt_kernel, hw=hw, heads=heads, qscale=HEAD_DIM ** -0.5 * LOG2E, sb_tk=sb_tk),
        grid=(B, nj),
        in_specs=[row(D), const(w_tok), const(bf), const(w_t), const(q_ones)],
        out_specs=[row(wide)] * 4 + [row(LANES)] + [t_spec] * 4 + [
            pl.BlockSpec((1, heads, VT_ROWS, tm), lambda b, j: (b * nj + j, 0, 0, 0)),
            pl.BlockSpec((tm // sb_tk, heads, hw // heads, sb_tk), lambda b, j: (b * nj + j, 0, 0, 0))],
        out_shape=[b16] * 4 + [jax.ShapeDtypeStruct((TP, LANES), F32)] + [t32] * 4 + [
            jax.ShapeDtypeStruct((TP // tm, heads, VT_ROWS, tm), BF16),
            jax.ShapeDtypeStruct((TP // sb_tk, heads, hw // heads, sb_tk), BF16)],
        compiler_params=_cparams("parallel", "parallel"),
        name="ab_proj_prompt",
    )(x, w_tok, bf, w_t, q_ones)


def _cumsum_kernel(x_ref, o_ref, carry_ref, *, tl):
    @pl.when(pl.program_id(0) == 0)
    def _():
        carry_ref[...] = jnp.zeros_like(carry_ref)

    x = x_ref[...]
    rows = x.shape[0]
    r = lax.broadcasted_iota(jnp.int32, (tl, tl), 0)
    c = lax.broadcasted_iota(jnp.int32, (tl, tl), 1)
    upper = (r <= c).astype(BF16)
    parts = jnp.concatenate(_split_bf16(x, 4), axis=0)
    y = _dot(parts, upper)
    cum = (y[0:rows] + y[rows:2 * rows]) + (y[2 * rows:3 * rows] + y[3 * rows:]) + carry_ref[:, 0:1]
    o_ref[...] = cum
    carry_ref[...] = jnp.broadcast_to(cum[:, tl - 1:tl], carry_ref.shape)


def _cumsum_rows(x):
    B, H, L = x.shape
    tl = _pick_tile(L, 512, LANES)
    out = pl.pallas_call(
        functools.partial(_cumsum_kernel, tl=tl),
        grid=(L // tl,),
        in_specs=[pl.BlockSpec((B * H, tl), lambda j: (0, j))],
        out_specs=pl.BlockSpec((B * H, tl), lambda j: (0, j)),
        out_shape=jax.ShapeDtypeStruct((B * H, L), F32),
        scratch_shapes=[pltpu.VMEM((B * H, LANES), F32)],
        compiler_params=_cparams("arbitrary"),
        name="cumsum_rows",
    )(x.reshape(B * H, L))
    return out.reshape(B, H, L)


def _softmax_step(s, vb, carry):
    m, l, acc = carry
    two, tq, tk = s.shape
    m_new = jnp.maximum(m, jnp.max(s, axis=-1, keepdims=True))
    alpha = jnp.exp2(m - m_new)
    p = jnp.exp2(s - m_new)
    l = alpha * l + jnp.sum(p, axis=-1, keepdims=True)
    pv = _dot(p.reshape(two * tq, tk).astype(BF16), vb).reshape(two, tq, vb.shape[-1])
    return m_new, l, alpha * acc + pv


def _log2_sigmoid_pair(z2):
    l1 = jnp.log2(1.0 + jnp.exp2(-jnp.abs(z2)))
    return jnp.minimum(z2, 0.0) - l1, jnp.minimum(-z2, 0.0) - l1


def _strict_upper(tk):
    r = lax.broadcasted_iota(jnp.int32, (tk, tk), 0)
    c = lax.broadcasted_iota(jnp.int32, (tk, tk), 1)
    return (r > c).astype(BF16)


HEAD_W = LANES
VT_ROWS = HEAD_DIM + 16
SB_DEAD_LOG2 = -160.0


def _pipeline_ahead(stage, first, count, cur, nxt):
    if first < count:
        stage(first, cur)
    elif nxt is not None:
        stage(first - count, nxt)


def _flash_prompt_kernel(q_ref, k_ref, vt_ref, o_ref, s_ref, m_ref, acc_ref, *, tq, tk, heads, chunk):
    i = pl.program_id(2)
    sub = tq // tk
    key = lax.broadcasted_iota(jnp.int32, (tk, tq), 0)
    query = lax.broadcasted_iota(jnp.int32, (tk, tq), 1)
    m_ref[...] = jnp.full(m_ref.shape, NEG_INF, F32)
    acc_ref[...] = jnp.zeros_like(acc_ref)

    def scores(h, kt):
        rows = pl.ds(pl.multiple_of(kt * tk, tk), tk)
        lanes = slice(h * HEAD_W, (h + 1) * HEAD_W)
        s_ref[h] = _dot_nt(k_ref[rows, lanes], q_ref[:, lanes])

    def absorb(h, kt, visible):
        s_t = s_ref[h]
        if visible is not None:
            s_t = jnp.where(visible, s_t, NEG_INF)
        m = m_ref[h]
        m_new = jnp.maximum(m, jnp.max(s_t, axis=0, keepdims=True))
        p_t = jnp.exp2(s_t - m_new).astype(BF16)
        pv = _dot(vt_ref[kt, h], p_t)
        acc_ref[h] = jnp.exp2(m - m_new) * acc_ref[h] + pv
        m_ref[h] = m_new

    scores(0, 0)
    scores(1, 0)

    def body(kt, carry):
        for h in range(heads):
            _pipeline_ahead(scores, h + 2, heads, kt, kt + 1)
            absorb(h, kt, None)
        return carry

    lax.fori_loop(0, sub * i, body, 0)
    for s in range(sub):
        kt = sub * i + s
        visible = ((s * tk + key) // chunk) <= (query // chunk)
        for h in range(heads):
            _pipeline_ahead(scores, h + 2, heads, kt, kt + 1 if s + 1 < sub else None)
            absorb(h, kt, visible)
    for j in range(heads // 2):
        a0, a1 = acc_ref[2 * j], acc_ref[2 * j + 1]
        o_t = jnp.concatenate([a0[:HEAD_DIM] / a0[HEAD_DIM:HEAD_DIM + 1],
                               a1[:HEAD_DIM] / a1[HEAD_DIM:HEAD_DIM + 1]], axis=0)
        o_ref[:, j * PAIR_W:(j + 1) * PAIR_W] = jnp.transpose(o_t).astype(o_ref.dtype)


def _flash_prompt(qx, kx, vt4, B, S, tq, heads_per_step, chunk, name):
    W = qx.shape[1]
    nq = S // tq
    hs = heads_per_step
    tk = vt4.shape[3]
    nk = S // tk
    return pl.pallas_call(
        functools.partial(_flash_prompt_kernel, tq=tq, tk=tk, heads=hs, chunk=chunk),
        grid=(B, W // (HEAD_W * hs), S // tq),
        in_specs=[pl.BlockSpec((tq, HEAD_W * hs), lambda b, g, i: (b * nq + i, g)),
                  pl.BlockSpec((S, HEAD_W * hs), lambda b, g, i: (b, g)),
                  pl.BlockSpec((nk, hs, VT_ROWS, tk), lambda b, g, i: (b, g, 0, 0))],
        out_specs=pl.BlockSpec((tq, HEAD_DIM * hs), lambda b, g, i: (b * nq + i, g)),
        out_shape=jax.ShapeDtypeStruct((B * S, W // HEAD_W * HEAD_DIM), BF16),
        scratch_shapes=[pltpu.VMEM((hs, tk, tq), F32), pltpu.VMEM((hs, 1, tq), F32),
                        pltpu.VMEM((hs, VT_ROWS, tq), F32)],
        compiler_params=_cparams("parallel", "parallel", "arbitrary"),
        name=name,
    )(qx, kx, vt4)


def _sb_prompt_kernel(q_ref, k_ref, vt_ref, o_ref, z_ref, lw_ref, tot_ref, run_ref, acc_ref, *, tq, tk, heads):
    i = pl.program_id(1)
    sub = tq // tk
    key = lax.broadcasted_iota(jnp.int32, (tk, tq), 0)
    query = lax.broadcasted_iota(jnp.int32, (tk, tq), 1)
    r = lax.broadcasted_iota(jnp.int32, (tk, tk), 0)
    c = lax.broadcasted_iota(jnp.int32, (tk, tk), 1)
    after = (c > r).astype(BF16)
    run_ref[...] = jnp.zeros_like(run_ref)
    acc_ref[...] = jnp.zeros_like(acc_ref)

    def logits(h, kt):
        rows = pl.ds(pl.multiple_of(kt * tk, tk), tk)
        lanes = slice(h * HEAD_W, (h + 1) * HEAD_W)
        z_ref[h] = _dot_nt(k_ref[rows, lanes], q_ref[:, lanes])

    def log_weights(before, h, kt):
        log_beta, log_rest = _log2_sigmoid_pair(z_ref[h])
        if before is not None:
            log_beta = jnp.where(before, log_beta, NEG_INF)
            log_rest = jnp.where(before, log_rest, 0.0)
        hi, lo = _split_bf16(log_rest, 2)
        later = _dot(after, hi) + _dot(after, lo)
        lw_ref[h] = log_beta + later
        tot_ref[h] = later[0:1] + log_rest[0:1]

    def accumulate(h, kt):
        run = run_ref[h]
        a_t = jnp.exp2(lw_ref[h] + run).astype(BF16)
        acc_ref[h] += _dot(vt_ref[kt, h], a_t)
        run_ref[h] = run + tot_ref[h]

    def step(kt, masked, nxt, nxt_masked):
        for h in range(heads):
            _pipeline_ahead(logits, h + 2, heads, kt, nxt)
            if h + 1 < heads:
                log_weights(masked, h + 1, kt)
            elif nxt is not None:
                log_weights(nxt_masked, 0, nxt)
            accumulate(h, kt)

    unmasked = sub * i
    first = unmasked + sub - 1
    masks = [((sub - 1 - s) * tk + key) < query for s in range(sub)]
    logits(0, first)
    logits(1, first)
    log_weights(masks[0], 0, first)
    for s in range(sub):
        kt = first - s
        if s + 1 < sub:
            step(kt, masks[s], kt - 1, masks[s + 1])
        else:
            step(kt, masks[s], jnp.maximum(kt - 1, 0), None)

    def alive():
        return (jnp.max(run_ref[...]) > SB_DEAD_LOG2).astype(jnp.int32)

    def body(carry):
        kt, _ = carry
        step(kt, None, jnp.maximum(kt - 1, 0), None)
        return kt - 1, alive()

    lax.while_loop(lambda c: jnp.logical_and(c[0] >= 0, c[1] > 0), body, (unmasked - 1, alive()))

    for j in range(heads // 2):
        o_t = jnp.concatenate([acc_ref[2 * j], acc_ref[2 * j + 1]], axis=0)
        o_ref[:, j * PAIR_W:(j + 1) * PAIR_W] = jnp.transpose(o_t).astype(o_ref.dtype)


def _sb_prompt(qx, kx, vt4, B, S, tq):
    W = qx.shape[1]
    heads = W // HEAD_W
    tk = vt4.shape[3]
    nq = S // tq
    return pl.pallas_call(
        functools.partial(_sb_prompt_kernel, tq=tq, tk=tk, heads=heads),
        grid=(B, nq),
        in_specs=[pl.BlockSpec((tq, W), lambda b, i: (b * nq + i, 0)),
                  pl.BlockSpec((S, W), lambda b, i: (b, 0)),
                  pl.BlockSpec((S // tk, heads, HEAD_DIM, tk), lambda b, i: (b, 0, 0, 0))],
        out_specs=pl.BlockSpec((tq, heads * HEAD_DIM), lambda b, i: (b * nq + i, 0)),
        out_shape=jax.ShapeDtypeStruct((B * S, heads * HEAD_DIM), BF16),
        scratch_shapes=[pltpu.VMEM((heads, tk, tq), F32), pltpu.VMEM((heads, tk, tq), F32),
                        pltpu.VMEM((heads, 1, tq), F32), pltpu.VMEM((heads, 1, tq), F32),
                        pltpu.VMEM((heads, HEAD_DIM, tq), F32)],
        compiler_params=_cparams("parallel", "arbitrary"),
        name="sb_prompt",
    )(qx, kx, vt4)


def _prep_ab_weights(w_in, b_f):
    D = w_in.shape[0]
    H = b_f.shape[0]
    hw = (w_in.shape[1] - H) // 6
    main = jnp.concatenate([w_in[:, :3 * hw], w_in[:, 3 * hw + H:]], axis=1)
    wf = jnp.zeros((D, LANES), w_in.dtype).at[:, :H].set(w_in[:, 3 * hw:3 * hw + H])
    bf = jnp.zeros((1, LANES), F32).at[0, :H].set(b_f.astype(F32))
    return jnp.concatenate([main, wf], axis=1).astype(BF16), bf


def _prompt_ab_weights(w_ab, hw, heads):
    seg = lambda j: w_ab[:, j * hw:(j + 1) * hw]
    w_tok = jnp.concatenate([seg(0), seg(1), seg(3), seg(4), w_ab[:, 6 * hw:]], axis=1)
    w_t = jnp.transpose(jnp.concatenate([seg(1), seg(2), seg(4), seg(5)], axis=1))
    return w_tok, w_t


DECAY_TERMS = 3


def _insert_decay_kernel(k_ref, c_ref, sel_ref, o_ref):
    terms = sum(_dot(part, sel_ref[j]) for j, part in enumerate(_split_bf16(c_ref[...], DECAY_TERMS)))
    o_ref[...] = (k_ref[...].astype(F32) + terms).astype(BF16)


def _fox_insert_decay(kx, lf, B, S, H):
    lt = jnp.transpose(lf[:, :H].reshape(B, S, H), (0, 2, 1))
    cum = jnp.transpose(_cumsum_rows(lt), (0, 2, 1)).reshape(B * S, H) * (-LOG2E)
    c = jnp.pad(cum, ((0, 0), (0, LANES - H)))
    src = jnp.arange(LANES)[None, :, None]
    dst = jnp.arange(H * HEAD_W)[None, None, :]
    term = jnp.arange(DECAY_TERMS)[:, None, None]
    sel = jnp.logical_and(src < H, dst == src * HEAD_W + HEAD_DIM + term).astype(BF16)
    tm = _pick_tile(B * S, 512)
    return pl.pallas_call(
        _insert_decay_kernel,
        grid=(B * S // tm,),
        in_specs=[pl.BlockSpec((tm, H * HEAD_W), lambda i: (i, 0)), pl.BlockSpec((tm, LANES), lambda i: (i, 0)),
                  pl.BlockSpec(sel.shape, lambda i: (0, 0, 0))],
        out_specs=pl.BlockSpec((tm, H * HEAD_W), lambda i: (i, 0)),
        out_shape=jax.ShapeDtypeStruct(kx.shape, BF16),
        input_output_aliases={0: 0},
        compiler_params=_cparams("parallel"),
        name="fox_insert_decay",
    )(kx, c, sel)


def _query_decay_ones(H):
    lane = jnp.arange(H * HEAD_W) % HEAD_W
    return jnp.logical_and(lane >= HEAD_DIM, lane < HEAD_DIM + DECAY_TERMS).astype(F32)[None]


def _fox_sample_kernel(q_ref, knt_ref, vnt_ref, kct_ref, vct_ref, cq_ref, ckp_ref, ckn_ref, o_ref,
                       s_ref, m_ref, l_ref, acc_ref, *, n, heads):
    kt = pl.program_id(1)

    @pl.when(kt == 0)
    def _():
        m_ref[...] = jnp.full(m_ref.shape, NEG_INF, F32)
        l_ref[...] = jnp.zeros_like(l_ref)
        acc_ref[...] = jnp.zeros_like(acc_ref)

    def absorb(h, s, v_t, ck, mask):
        s = s + (cq_ref[0, h] - ck)
        if mask is not None:
            s = jnp.where(mask, s, NEG_INF)
        m = m_ref[h]
        m_new = jnp.maximum(m, jnp.max(s, axis=-1, keepdims=True))
        alpha = jnp.exp2(m - m_new)
        p = jnp.exp2(s - m_new)
        l_ref[h] = alpha * l_ref[h] + jnp.sum(p, axis=-1, keepdims=True)
        acc_ref[h] = alpha * acc_ref[h] + _dot_nt(p.astype(BF16), v_t)
        m_ref[h] = m_new

    def scores(h, _=None):
        s_ref[h] = _dot(q_ref[0, h], kct_ref[0, h].astype(BF16))

    scores(0)
    scores(1)
    for h in range(heads):
        _pipeline_ahead(scores, h + 2, heads, None, None)
        absorb(h, s_ref[h], vct_ref[0, h].astype(BF16), ckp_ref[0, h:h + 1, :], None)

    @pl.when(kt == pl.num_programs(1) - 1)
    def _():
        row = lax.broadcasted_iota(jnp.int32, (n, n), 0)
        col = lax.broadcasted_iota(jnp.int32, (n, n), 1)
        for h in range(heads):
            absorb(h, _dot(q_ref[0, h], knt_ref[0, h]), vnt_ref[0, h], ckn_ref[0, h:h + 1, :], col <= row)
            o_ref[0, h] = (acc_ref[h] / l_ref[h]).astype(o_ref.dtype)


def _fox_sample(q, knt, vnt, kct, vct, cq, ckp, ckn, tk):
    DB, H, n, dh = q.shape
    P = kct.shape[-1]
    per_b = lambda a: pl.BlockSpec((1,) + a.shape[1:], lambda b, j: (b,) + (0,) * (a.ndim - 1))
    cache = pl.BlockSpec((1, H, dh, tk), lambda b, j: (b, 0, 0, j))
    return pl.pallas_call(
        functools.partial(_fox_sample_kernel, n=n, heads=H),
        grid=(DB, P // tk),
        in_specs=[per_b(q), per_b(knt), per_b(vnt), cache, cache, per_b(cq),
                  pl.BlockSpec((1, H, tk), lambda b, j: (b, 0, j)), per_b(ckn)],
        out_specs=per_b(q),
        out_shape=jax.ShapeDtypeStruct(q.shape, BF16),
        scratch_shapes=[pltpu.VMEM((H, n, tk), F32), pltpu.VMEM((H, n, 1), F32), pltpu.VMEM((H, n, 1), F32),
                        pltpu.VMEM((H, n, dh), F32)],
        compiler_params=_cparams("parallel", "arbitrary"),
        name="fox_sample",
    )(q, knt, vnt, kct, vct, cq, ckp, ckn)


def _sb_sample_kernel(q_ref, knt_ref, vnt_ref, kct_ref, vct_ref, o_ref, z_ref, run_ref, acc_ref, *, n, heads, sub):
    kt = pl.program_id(1)
    upper = _strict_upper(sub)

    def absorb(h, z, v_t, upper_m, width, mask=None):
        log_beta, log_rest = _log2_sigmoid_pair(z)
        if mask is not None:
            log_beta = jnp.where(mask, log_beta, NEG_INF)
            log_rest = jnp.where(mask, log_rest, 0.0)
        hi, lo = _split_bf16(log_rest, 2)
        run = run_ref[h]
        parts = []
        for c in reversed(range(z.shape[1] // width)):
            keys = slice(c * width, (c + 1) * width)
            later = _dot(hi[:, keys], upper_m) + _dot(lo[:, keys], upper_m)
            parts.append(jnp.exp2(log_beta[:, keys] + later + run).astype(BF16))
            run = run + jnp.sum(log_rest[:, keys], axis=-1, keepdims=True)
        a = parts[0] if len(parts) == 1 else jnp.concatenate(parts[::-1], axis=1)
        acc_ref[h] += _dot_nt(a, v_t)
        run_ref[h] = run

    def logits(h, _=None):
        z_ref[h] = _dot(q_ref[0, h], kct_ref[0, h].astype(BF16))

    @pl.when(kt == 0)
    def _():
        row = lax.broadcasted_iota(jnp.int32, (n, n), 0)
        col = lax.broadcasted_iota(jnp.int32, (n, n), 1)
        upper_n = _strict_upper(n)
        run_ref[...] = jnp.zeros_like(run_ref)
        acc_ref[...] = jnp.zeros_like(acc_ref)
        for h in range(heads):
            absorb(h, _dot(q_ref[0, h], knt_ref[0, h]), vnt_ref[0, h], upper_n, n, col < row)

    @pl.when(jnp.max(run_ref[...]) > SB_DEAD_LOG2)
    def _():
        logits(0)
        logits(1)
        for h in range(heads):
            _pipeline_ahead(logits, h + 2, heads, None, None)
            absorb(h, z_ref[h], vct_ref[0, h].astype(BF16), upper, sub)

    @pl.when(kt == pl.num_programs(1) - 1)
    def _():
        o_ref[0] = acc_ref[...].astype(o_ref.dtype)


def _sb_sample(q, knt, vnt, kct, vct, tk, sub):
    DB, H, n, dh = q.shape
    P = kct.shape[-1]
    nk = P // tk
    per_b = lambda a: pl.BlockSpec((1,) + a.shape[1:], lambda b, j: (b,) + (0,) * (a.ndim - 1))
    cache = pl.BlockSpec((1, H, dh, tk), lambda b, j: (b, 0, 0, nk - 1 - j))
    return pl.pallas_call(
        functools.partial(_sb_sample_kernel, n=n, heads=H, sub=sub),
        grid=(DB, nk),
        in_specs=[per_b(q), per_b(knt), per_b(vnt), cache, cache],
        out_specs=per_b(q),
        out_shape=jax.ShapeDtypeStruct(q.shape, BF16),
        scratch_shapes=[pltpu.VMEM((H, n, tk), F32), pltpu.VMEM((H, n, 1), F32), pltpu.VMEM((H, n, dh), F32)],
        compiler_params=_cparams("parallel", "arbitrary"),
        name="sb_sample",
    )(q, knt, vnt, kct, vct)


def _fox_sample_cum(lf_new, lf_past_t):
    DB, H, P = lf_past_t.shape
    n = lf_new.shape[1]
    L = -(-(P + n) // LANES) * LANES
    both = jnp.concatenate([lf_past_t.astype(F32), jnp.transpose(lf_new, (0, 2, 1)),
                            jnp.zeros((DB, H, L - P - n), F32)], axis=2)
    cum_t = _cumsum_rows(both) * LOG2E
    ckn = cum_t[:, :, P:P + n]
    return ckn[..., None], cum_t[:, :, :P], ckn


def _heads_major(a, heads, transpose_rows):
    DB, n, _ = a.shape
    a4 = a.reshape(DB, n, heads, -1)
    return jnp.transpose(a4, (0, 2, 3, 1) if transpose_rows else (0, 2, 1, 3))


ROUTER_ROWS = 48


def _layer_norm(y, g, b):
    mu = jnp.mean(y, axis=-1, keepdims=True)
    yc = y - mu
    var = jnp.mean(yc * yc, axis=-1, keepdims=True)
    return yc * lax.rsqrt(var + LN_EPS) * g + b


def _first_argmax(v, ridx):
    vmax = jnp.max(v, axis=0, keepdims=True)
    idx = jnp.min(jnp.where(v == vmax, ridx, v.shape[0]), axis=0, keepdims=True)
    return vmax, idx


def _two_part_specs(tm, width, head_tiles):
    return (pl.BlockSpec((tm, width), lambda i: (jnp.minimum(i, head_tiles - 1), 0)),
            pl.BlockSpec((tm, width), lambda i: (jnp.maximum(i - head_tiles, 0), 0)))


def _mix_out_kernel(*refs, alpha, n_groups, head_tiles, n_parts):
    o_parts, ot_ref = refs[:n_parts], refs[n_parts]
    w_ref, xh_ref, xt_ref, g_ref, b_ref, wrh_ref, wrl_ref, rb_ref, h_ref, ids_ref, wts_ref = refs[n_parts + 1:]
    in_head = pl.program_id(0) < head_tiles
    x = jnp.where(in_head, xh_ref[...], xt_ref[...])
    o_head = o_parts[0][...] if n_parts == 1 else jnp.concatenate([r[...] for r in o_parts], axis=1)
    o = jnp.where(in_head, o_head, ot_ref[...])
    h = _layer_norm(alpha * x + _dot(o, w_ref[...]), g_ref[...], b_ref[...])
    h_ref[...] = h
    hh, hl = _split_bf16(h, 2)
    R = ROUTER_ROWS
    wr = _dot_nt(wrl_ref[...], hh)
    lg = wr[:R] + (_dot_nt(wrh_ref[...], hl) + (wr[R:2 * R] + wr[2 * R:])) + rb_ref[...]
    tm = lg.shape[1]
    ridx = lax.broadcasted_iota(jnp.int32, (8, tm), 0)
    g = jnp.where(ridx < n_groups, lg[0:8], NEG_INF)
    gmax, gidx = _first_argmax(g, ridx)
    gate = 1.0 / jnp.sum(jnp.exp(g - gmax), axis=0, keepdims=True)
    esel = lg[8:16]
    for gg in range(1, n_groups):
        esel = jnp.where(gidx == gg, lg[8 + 8 * gg:16 + 8 * gg], esel)
    v1, i1 = _first_argmax(esel, ridx)
    v2, i2 = _first_argmax(jnp.where(ridx == i1, NEG_INF, esel), ridx)
    t = jnp.exp(v2 - v1)
    w1 = 1.0 / (1.0 + t)
    ids_ref[...] = jnp.where(ridx == 0, gidx * 8 + i1, jnp.where(ridx == 1, gidx * 8 + i2, 0))
    wts_ref[...] = jnp.where(ridx == 0, gate * w1, jnp.where(ridx == 1, gate * (t * w1), 0.0))


def _mix_out(o_head_parts, o_tail, w, x_head, x_tail, g, b, wrh, wrl, rb, alpha, n_groups, tm):
    D = x_head.shape[1]
    T = x_head.shape[0] + x_tail.shape[0]
    head_tiles = x_head.shape[0] // tm
    const = lambda a: pl.BlockSpec(a.shape, lambda i: (0,) * a.ndim)
    rb_t = jnp.broadcast_to(rb, (ROUTER_ROWS, tm))
    head_spec = lambda width: _two_part_specs(tm, width, head_tiles)[0]
    return pl.pallas_call(
        functools.partial(_mix_out_kernel, alpha=alpha, n_groups=n_groups, head_tiles=head_tiles,
                          n_parts=len(o_head_parts)),
        grid=(T // tm,),
        in_specs=[*[head_spec(p.shape[1]) for p in o_head_parts], _two_part_specs(tm, o_tail.shape[1], head_tiles)[1],
                  const(w), *_two_part_specs(tm, D, head_tiles), const(g), const(b), const(wrh), const(wrl),
                  const(rb_t)],
        out_specs=[pl.BlockSpec((tm, D), lambda i: (i, 0)), pl.BlockSpec((8, tm), lambda i: (0, i)),
                   pl.BlockSpec((8, tm), lambda i: (0, i))],
        out_shape=[jax.ShapeDtypeStruct((T, D), F32), jax.ShapeDtypeStruct((8, T), jnp.int32),
                   jax.ShapeDtypeStruct((8, T), F32)],
        compiler_params=_cparams("parallel"),
        name="mix_out_ln_router",
    )(*o_head_parts, o_tail, w, x_head, x_tail, g, b, wrh, wrl, rb_t)


def _prep_router(w_group, b_group, w_router, b_router):
    D, G = w_group.shape
    E = w_router.shape[-1]
    wr = jnp.zeros((ROUTER_ROWS, D), F32)
    wr = wr.at[:G].set(w_group.T.astype(F32))
    wr = wr.at[8:8 + G * E].set(jnp.transpose(w_router, (0, 2, 1)).reshape(G * E, D).astype(F32))
    rb = jnp.zeros((ROUTER_ROWS, 1), F32)
    rb = rb.at[:G, 0].set(b_group.astype(F32)).at[8:8 + G * E, 0].set(b_router.reshape(-1).astype(F32))
    hi, mid, lo = _split_bf16_trunc(wr, 3)
    return hi, jnp.concatenate([hi, mid, lo], axis=0), rb


def _gather_rows(idx_ref, lo, hi, src_hbm, dst, sem):
    for r in range(lo, hi):
        pltpu.make_async_copy(src_hbm.at[pl.ds(idx_ref[0, 0, r], 1)], dst.at[pl.ds(r, 1)], sem).start(priority=r % 2)


def _wait_rows(n, src_hbm, dst, sem):
    pltpu.make_async_copy(src_hbm.at[pl.ds(0, n)], dst, sem).wait()


def _moe_experts_kernel(te_ref, tv_ref, src_ref, nxt_ref, x_hbm, wg_ref, wu_ref, wd_ref, y_ref, xbuf, sem, *, tm):
    i = pl.program_id(0)
    nt = pl.num_programs(0)
    slot = i % 2

    @pl.when(jnp.logical_and(i == 0, tv_ref[0] > 0))
    def _():
        _gather_rows(src_ref, 0, tm, x_hbm, xbuf.at[0], sem.at[0])

    @pl.when(jnp.logical_and(i + 1 < nt, tv_ref[jnp.minimum(i + 1, nt - 1)] > 0))
    def _():
        _gather_rows(nxt_ref, 0, tm, x_hbm, xbuf.at[1 - slot], sem.at[1 - slot])

    @pl.when(tv_ref[i] > 0)
    def _():
        _wait_rows(tm, x_hbm, xbuf.at[slot], sem.at[slot])
        xb = xbuf[slot].astype(BF16)
        a = _dot(xb, wg_ref[0].astype(BF16))
        u = _dot(xb, wu_ref[0].astype(BF16))
        hid = (a / (1.0 + jnp.exp(-a))) * u
        y_ref[...] = _dot(hid.astype(BF16), wd_ref[0].astype(BF16))

    @pl.when(tv_ref[i] == 0)
    def _():
        y_ref[...] = jnp.zeros_like(y_ref)


def _moe_experts(x, w_gate, w_up, w_down, tile_expert, tile_valid, src, tm):
    T, D = x.shape
    F = w_gate.shape[-1]
    NT = tile_expert.shape[0]
    grid_spec = pltpu.PrefetchScalarGridSpec(
        num_scalar_prefetch=2,
        grid=(NT,),
        in_specs=[pl.BlockSpec((1, 1, tm), lambda i, te, tv: (i, 0, 0), memory_space=pltpu.SMEM),
                  pl.BlockSpec((1, 1, tm), lambda i, te, tv: (i + 1, 0, 0), memory_space=pltpu.SMEM),
                  pl.BlockSpec(memory_space=pl.ANY),
                  pl.BlockSpec((1, D, F), lambda i, te, tv: (te[i], 0, 0)),
                  pl.BlockSpec((1, D, F), lambda i, te, tv: (te[i], 0, 0)),
                  pl.BlockSpec((1, F, D), lambda i, te, tv: (te[i], 0, 0))],
        out_specs=pl.BlockSpec((tm, D), lambda i, te, tv: (i, 0)),
        scratch_shapes=[pltpu.VMEM((2, tm, D), F32), pltpu.SemaphoreType.DMA((2,))],
    )
    return pl.pallas_call(
        functools.partial(_moe_experts_kernel, tm=tm),
        grid_spec=grid_spec,
        out_shape=jax.ShapeDtypeStruct((NT * tm, D), F32),
        compiler_params=_cparams("arbitrary"),
        name="moe_experts",
    )(tile_expert, tile_valid, src, src, x, w_gate, w_up, w_down)


def _moe_combine_kernel(pos_ref, nxt_ref, ys_hbm, h_ref, w_ref, g_ref, b_ref, oh_ref, ot_ref, buf, sem,
                        *, tm, alpha, head_tiles):
    i = pl.program_id(0)
    nt = pl.num_programs(0)
    slot = i % 2

    @pl.when(i == 0)
    def _():
        _gather_rows(pos_ref, 0, 2 * tm, ys_hbm, buf.at[0], sem.at[0])

    @pl.when(i + 1 < nt)
    def _():
        _gather_rows(nxt_ref, 0, 2 * tm, ys_hbm, buf.at[1 - slot], sem.at[1 - slot])

    _wait_rows(2 * tm, ys_hbm, buf.at[slot], sem.at[slot])
    w = w_ref[...]
    y = _layer_norm(alpha * h_ref[...] + (w[:, 0:1] * buf[slot, 0:tm] + w[:, 1:2] * buf[slot, tm:2 * tm]),
                    g_ref[...], b_ref[...])

    @pl.when(i < head_tiles)
    def _():
        oh_ref[...] = y

    @pl.when(i >= head_tiles)
    def _():
        ot_ref[...] = y


def _moe_combine(ys, h, pos, wts, g, b, alpha, tm, head_rows):
    T, D = h.shape
    nt = T // tm
    head_tiles = head_rows // tm
    const = lambda a: pl.BlockSpec(a.shape, lambda i: (0,) * a.ndim)
    return pl.pallas_call(
        functools.partial(_moe_combine_kernel, tm=tm, alpha=alpha, head_tiles=head_tiles),
        grid=(nt,),
        in_specs=[pl.BlockSpec((1, 1, 2 * tm), lambda i: (i, 0, 0), memory_space=pltpu.SMEM),
                  pl.BlockSpec((1, 1, 2 * tm), lambda i: (i + 1, 0, 0), memory_space=pltpu.SMEM),
                  pl.BlockSpec(memory_space=pl.ANY),
                  pl.BlockSpec((tm, D), lambda i: (i, 0)),
                  pl.BlockSpec((tm, 2), lambda i: (i, 0)), const(g), const(b)],
        out_specs=list(_two_part_specs(tm, D, head_tiles)),
        out_shape=[jax.ShapeDtypeStruct((head_rows, D), F32), jax.ShapeDtypeStruct((T - head_rows, D), F32)],
        scratch_shapes=[pltpu.VMEM((2, 2 * tm, D), F32), pltpu.SemaphoreType.DMA((2,))],
        compiler_params=_cparams("arbitrary"),
        name="moe_combine_ln",
    )(pos, pos, ys, h, wts, g, b)


def _route(ids, n_experts, tm):
    T = ids.shape[1]
    flat = ids.reshape(-1)
    iota = jnp.arange(2 * T, dtype=jnp.int32)
    sorted_e, order = lax.sort((flat, iota), num_keys=1, is_stable=True)
    _, inverse = lax.sort((order, iota), num_keys=1)
    experts = jnp.arange(n_experts, dtype=jnp.int32)
    counts = jnp.sum((flat[:, None] == experts[None, :]).astype(jnp.int32), axis=0)
    padded = (counts + tm - 1) // tm * tm
    ends = jnp.cumsum(padded)
    shift = (ends - padded) - (jnp.cumsum(counts) - counts)
    NT = (2 * T + n_experts * (tm - 1)) // tm
    tile_start = jnp.arange(NT, dtype=jnp.int32) * tm
    tile_expert = jnp.minimum(jnp.sum((tile_start[:, None] >= ends[None, :]).astype(jnp.int32), axis=1),
                              n_experts - 1)
    tile_valid = (tile_start < ends[-1]).astype(jnp.int32)
    pos = (inverse + shift[flat]).reshape(2, T)
    row = jnp.arange((NT + 1) * tm, dtype=jnp.int32)
    row_shift = jnp.repeat(shift[jnp.concatenate([tile_expert, tile_expert[-1:]])], tm)
    src = (order % T)[jnp.clip(row - row_shift, 0, 2 * T - 1)]
    return tile_expert, tile_valid, src.reshape(NT + 1, 1, tm), pos


MLA_PAIR_W = 2 * LANES
MLA_SAMPLE_CHUNK = 256
QK_NOPE = 64
QK_ROPE = 32


def _mla_proj_kernel(hh_ref, ht_ref, wdn_ref, gq_ref, gkv_ref, wq_ref, wqr_ref, wk_ref, wv_ref, cos_ref, sin_ref,
                     ckv_ref, kr_ref, qcat_ref, kcat_ref, vt_ref, *, q_lora, kv_lora, npairs, scale, head_tiles):
    h = jnp.where(pl.program_id(0) < head_tiles, hh_ref[...], ht_ref[...])
    z = _dot(h.astype(BF16), wdn_ref[...])
    cq = z[:, :q_lora]
    ckv = z[:, q_lora:q_lora + kv_lora]
    o = q_lora + kv_lora
    kr_raw = z[:, o:o + HEAD_W]
    kr_rot = z[:, o + HEAD_W:o + 2 * HEAD_W]
    cq = cq * lax.rsqrt(jnp.mean(cq * cq, axis=-1, keepdims=True) + RMS_EPS) * gq_ref[...]
    ckv = ckv * lax.rsqrt(jnp.mean(ckv * ckv, axis=-1, keepdims=True) + RMS_EPS) * gkv_ref[...]
    ckv_ref[...] = ckv
    cos = cos_ref[...]
    sin = sin_ref[...]
    kr_tile = kr_raw * cos + kr_rot * sin
    kr_ref[...] = kr_tile[:, QK_NOPE:QK_NOPE + QK_ROPE]
    cqb = cq.astype(BF16)
    ckb = ckv.astype(BF16)
    cos2 = jnp.concatenate([cos, cos], axis=1)
    sin2 = jnp.concatenate([sin, sin], axis=1)
    kr2 = jnp.concatenate([kr_tile, kr_tile], axis=1)
    for p in range(npairs):
        lanes = slice(p * MLA_PAIR_W, (p + 1) * MLA_PAIR_W)
        q = _dot(cqb, wq_ref[:, lanes]) * cos2 + _dot(cqb, wqr_ref[:, lanes]) * sin2
        qcat_ref[:, lanes] = (q * scale).astype(BF16)
        kcat_ref[:, lanes] = (_dot(ckb, wk_ref[:, lanes]) + kr2).astype(BF16)
    heads = 2 * npairs
    vt = _dot_nt(wv_ref[...], ckb).reshape(heads, HEAD_DIM, ckb.shape[0])
    vt_ref[0, :, :HEAD_DIM, :] = vt.astype(BF16)
    vt_ref[0, :, HEAD_DIM:, :] = jnp.ones((heads, VT_ROWS - HEAD_DIM, ckb.shape[0]), BF16)


def _mla_proj(h_head, h_tail, wdn, gq, gkv, wq, wqr, wk, wv, cos_t, sin_t, table_block, tm):
    D = h_head.shape[1]
    T = h_head.shape[0] + h_tail.shape[0]
    head_tiles = h_head.shape[0] // tm
    q_lora, kv_lora = gq.shape[1], gkv.shape[1]
    npairs = wq.shape[1] // MLA_PAIR_W
    const = lambda a: pl.BlockSpec(a.shape, lambda i: (0,) * a.ndim)
    row = lambda w_: pl.BlockSpec((tm, w_), lambda i: (i, 0))
    table = pl.BlockSpec((tm, HEAD_W), lambda i: (table_block(i), 0))
    return pl.pallas_call(
        functools.partial(_mla_proj_kernel, q_lora=q_lora, kv_lora=kv_lora, npairs=npairs,
                          scale=(QK_NOPE + QK_ROPE) ** -0.5 * LOG2E, head_tiles=head_tiles),
        grid=(T // tm,),
        in_specs=[*_two_part_specs(tm, D, head_tiles), const(wdn), const(gq), const(gkv), const(wq), const(wqr),
                  const(wk), const(wv), table, table],
        out_specs=[row(kv_lora), row(QK_ROPE), row(wq.shape[1]), row(wk.shape[1]),
                   pl.BlockSpec((1, 2 * npairs, VT_ROWS, tm), lambda i: (i, 0, 0, 0))],
        out_shape=[jax.ShapeDtypeStruct((T, kv_lora), F32), jax.ShapeDtypeStruct((T, QK_ROPE), F32),
                   jax.ShapeDtypeStruct((T, wq.shape[1]), BF16), jax.ShapeDtypeStruct((T, wk.shape[1]), BF16),
                   jax.ShapeDtypeStruct((T // tm, 2 * npairs, VT_ROWS, tm), BF16)],
        compiler_params=_cparams("parallel"),
        name="mla_proj",
    )(h_head, h_tail, wdn, gq, gkv, wq, wqr, wk, wv, cos_t, sin_t)


def _rot_half(w):
    half = w.shape[-1] // 2
    return jnp.concatenate([-w[..., half:], w[..., :half]], axis=-1)


def _prep_mla_weights(w_down, w_uq, w_ukv, heads, q_lora, kv_lora):
    D = w_down.shape[0]
    tail = HEAD_W - QK_NOPE - QK_ROPE
    w_kr = w_down[:, q_lora + kv_lora:]
    slot = lambda w: jnp.concatenate([jnp.zeros((D, QK_NOPE), w.dtype), w, jnp.zeros((D, tail), w.dtype)], axis=1)
    wdn = jnp.concatenate([w_down[:, :q_lora + kv_lora], slot(w_kr), slot(_rot_half(w_kr))], axis=1)
    wq3 = w_uq.reshape(q_lora, heads, QK_NOPE + QK_ROPE)
    nope, ropew = wq3[..., :QK_NOPE], wq3[..., QK_NOPE:]
    zpad = jnp.zeros((q_lora, heads, tail), w_uq.dtype)
    wq = jnp.concatenate([nope, ropew, zpad], axis=-1)
    wqr = jnp.concatenate([jnp.zeros_like(nope), _rot_half(ropew), zpad], axis=-1)
    wkv3 = w_ukv.reshape(kv_lora, heads, QK_NOPE + HEAD_DIM)
    w_uk, w_uv = wkv3[..., :QK_NOPE], wkv3[..., QK_NOPE:]
    wk = jnp.concatenate([w_uk, jnp.zeros((kv_lora, heads, HEAD_W - QK_NOPE), w_ukv.dtype)], axis=-1)
    wv = jnp.transpose(w_uv.reshape(kv_lora, heads * HEAD_DIM))
    b16 = lambda a: a.astype(BF16)
    return (b16(wdn), b16(wq.reshape(q_lora, -1)), b16(wqr.reshape(q_lora, -1)), b16(wk.reshape(kv_lora, -1)),
            b16(wv), b16(jnp.transpose(w_uk, (1, 2, 0))), b16(jnp.transpose(w_uv, (1, 0, 2))))


def _rope_tables(pos):
    half = QK_ROPE // 2
    inv_freq = ROPE_BASE ** (-jnp.arange(half, dtype=F32) / half)
    ang = pos.astype(F32)[:, None] * inv_freq[None, :]
    n = pos.shape[0]
    pad = jnp.zeros((n, HEAD_W - QK_NOPE - QK_ROPE), F32)
    cos = jnp.concatenate([jnp.ones((n, QK_NOPE), F32)] + [jnp.cos(ang)] * 2 + [pad], axis=1)
    sin = jnp.concatenate([jnp.zeros((n, QK_NOPE), F32)] + [jnp.sin(ang)] * 2 + [pad], axis=1)
    return cos, sin


def _mla_sample_queries(qcat, heads):
    DB, n, _ = qcat.shape
    q4 = qcat.reshape(DB, n, heads, HEAD_W)
    rows = lambda a: jnp.transpose(a, (0, 2, 1, 3)).reshape(DB, heads * n, a.shape[-1])
    return rows(q4[..., :QK_NOPE]), rows(q4[..., QK_NOPE:QK_NOPE + QK_ROPE])


def _mla_sample_kernel(qn_ref, qr_ref, wuk_ref, wuv_ref, cc_ref, rc_ref, cn_ref, rn_ref, o_ref,
                       qlat_ref, s_ref, m_ref, l_ref, acc_ref, *, n, heads):
    kt = pl.program_id(1)

    @pl.when(kt == 0)
    def _():
        for h in range(heads):
            rows = slice(h * n, (h + 1) * n)
            qlat_ref[rows, :] = _dot(qn_ref[0, rows, :], wuk_ref[h]).astype(BF16)
        m_ref[...] = jnp.full(m_ref.shape, NEG_INF, F32)
        l_ref[...] = jnp.zeros_like(l_ref)
        acc_ref[...] = jnp.zeros_like(acc_ref)

    def scores(rows, ckv, kr_t):
        return _dot_nt(qlat_ref[rows, :], ckv) + _dot(qr_ref[0, rows, :], kr_t)

    def absorb(rows, s, ckv):
        m, l, acc = _softmax_step(s[None], ckv, (m_ref[:, rows], l_ref[:, rows], acc_ref[:, rows]))
        m_ref[:, rows] = m
        l_ref[:, rows] = l
        acc_ref[:, rows] = acc

    def update(ckv, kr_t):
        absorb(slice(None), scores(slice(None), ckv, kr_t), ckv)

    ckv = cc_ref[0].astype(BF16)
    kr_t = rc_ref[0].astype(BF16)
    chunk_rows = s_ref.shape[1]
    chunks = [slice(c * chunk_rows, (c + 1) * chunk_rows) for c in range(s_ref.shape[0])]
    s_ref[0] = scores(chunks[0], ckv, kr_t)
    for c, rows in enumerate(chunks):
        if c + 1 < len(chunks):
            s_ref[c + 1] = scores(chunks[c + 1], ckv, kr_t)
        absorb(rows, s_ref[c], ckv)

    @pl.when(kt == pl.num_programs(1) - 1)
    def _():
        update(cn_ref[0], rn_ref[0])
        o_lat = (acc_ref[0] / l_ref[0]).astype(BF16)
        for h in range(heads):
            o_ref[0, :, h * HEAD_DIM:(h + 1) * HEAD_DIM] = _dot(o_lat[h * n:(h + 1) * n], wuv_ref[h]).astype(o_ref.dtype)


def _mla_sample(qn, qr, wuk, wuv, ckv_c, kr_c, ckv_n, kr_n, n, tk):
    DB, R, _ = qn.shape
    heads = R // n
    P, C = ckv_c.shape[1], ckv_c.shape[2]
    const = lambda a: pl.BlockSpec(a.shape, lambda b, j: (0,) * a.ndim)
    per_b = lambda a: pl.BlockSpec((1,) + a.shape[1:], lambda b, j: (b, 0, 0))
    return pl.pallas_call(
        functools.partial(_mla_sample_kernel, n=n, heads=heads),
        grid=(DB, P // tk),
        in_specs=[per_b(qn), per_b(qr), const(wuk), const(wuv),
                  pl.BlockSpec((1, tk, C), lambda b, j: (b, j, 0)),
                  pl.BlockSpec((1, QK_ROPE, tk), lambda b, j: (b, 0, j)),
                  per_b(ckv_n), per_b(kr_n)],
        out_specs=pl.BlockSpec((1, n, heads * HEAD_DIM), lambda b, j: (b, 0, 0)),
        out_shape=jax.ShapeDtypeStruct((DB, n, heads * HEAD_DIM), BF16),
        scratch_shapes=[pltpu.VMEM((R, C), BF16), pltpu.VMEM((R // MLA_SAMPLE_CHUNK, MLA_SAMPLE_CHUNK, tk), F32),
                        pltpu.VMEM((1, R, 1), F32), pltpu.VMEM((1, R, 1), F32), pltpu.VMEM((1, R, C), F32)],
        compiler_params=_cparams("parallel", "arbitrary"),
        name="mla_sample",
    )(qn, qr, wuk, wuv, ckv_c, kr_c, ckv_n, kr_n)


def _moe_layer(h, ids, wts, w_gate, w_up, w_down, layer, g, b, alpha, tm, head_rows):
    T, D = h.shape
    n_experts = w_gate.shape[1] * w_gate.shape[2]
    tile_expert, tile_valid, src, pos = _route(ids[:2], n_experts, tm)
    flat3 = lambda w: w.reshape((-1,) + w.shape[3:])
    ys = _moe_experts(h, flat3(w_gate), flat3(w_up), flat3(w_down), tile_expert + layer * n_experts, tile_valid,
                      src, tm)
    pos_t = jnp.transpose(pos.reshape(2, T // tm, tm), (1, 0, 2)).reshape(T // tm, 1, 2 * tm)
    pos_t = jnp.pad(pos_t, ((0, 1), (0, 0), (0, 0)))
    return _moe_combine(ys, h, pos_t, jnp.transpose(wts[:2]), g, b, alpha, tm, head_rows)


TOKEN_TILE = 256
MIX_TILE = 512
FLASH_Q_TILE = 512
FLASH_KEY_TILE = 256
SB_Q_TILE = 512
SB_KEY_TILE = 128
CACHE_TILE = 2048
SB_CACHE_TILE = 1024
SB_SUB_TILE = 256


def kernel(x_prompt, x_sample, cache_fox_k, cache_fox_v, cache_fox_logf, cache_sb_k, cache_sb_v, cache_mla_ckv, cache_mla_krope, ab_w_in, ab_b_forget, ab_w_out, mla_w_down, mla_g_q, mla_g_kv, mla_w_uq, mla_w_ukv, mla_w_out, moe_w_group, moe_b_group, moe_w_router, moe_b_router, moe_w_gate, moe_w_up, moe_w_down, ln_g, ln_b):
    B, S, D = x_prompt.shape
    DB, n, _ = x_sample.shape
    P = cache_fox_k.shape[2]
    TP, TS = B * S, DB * n
    depth = ln_g.shape[0]
    n_groups = moe_w_group.shape[-1]
    assert depth == 2 and ab_w_in.shape[0] == 1 and mla_w_down.shape[0] == 1
    assert S % FLASH_Q_TILE == 0 and S % SB_Q_TILE == 0 and TP % MIX_TILE == 0 and TS % MIX_TILE == 0 and MIX_TILE % TOKEN_TILE == 0 and TOKEN_TILE % n == 0
    assert P % CACHE_TILE == 0 and P % SB_CACHE_TILE == 0 and P % CHUNK == 0 and n == CHUNK
    alpha = (2 * depth) ** 0.25
    tk = CACHE_TILE

    xp, xs = x_prompt.reshape(TP, D), x_sample.reshape(TS, D)
    sample3 = lambda a: a[TP:].reshape(DB, n, -1)

    def ffn(o_head_parts, o_tail, w_out, resid, layer):
        wrh, wrl, rb = _prep_router(moe_w_group[layer], moe_b_group[layer], moe_w_router[layer], moe_b_router[layer])
        h, ids, wts = _mix_out(o_head_parts, o_tail, w_out.astype(BF16), *resid, ln_g[layer, 0][None],
                               ln_b[layer, 0][None], wrh, wrl, rb, alpha, n_groups, MIX_TILE)
        return _moe_layer(h, ids, wts, moe_w_gate, moe_w_up, moe_w_down, layer,
                          ln_g[layer, 1][None], ln_b[layer, 1][None], alpha, TOKEN_TILE, TP)

    fox_heads = ab_b_forget.shape[1]
    hw = (ab_w_in.shape[2] - fox_heads) // 6
    w_ab, b_forget = _prep_ab_weights(ab_w_in[0], ab_b_forget[0])
    w_tok, w_t = _prompt_ab_weights(w_ab, hw, fox_heads)
    (qa_p, ka_p, qb_p, kb_p, lf_p, kat, vat, kbt, vbt, vat16, vbt16) = _ab_proj_prompt(
        xp, w_tok, b_forget, w_t, _query_decay_ones(fox_heads), B, S, fox_heads, SB_KEY_TILE)
    ka_p = _fox_insert_decay(ka_p, lf_p, B, S, fox_heads)
    qa, ka, va, qb, kb, vb, ka16, va16, kb16, vb16, lf = _ab_proj(xs, w_ab, b_forget, hw)
    dbn = lambda a: a.reshape(DB, n, -1)
    lf_s = dbn(lf)[:, :, :fox_heads]
    cq_s, ck_past, ck_new = _fox_sample_cum(lf_s, jnp.transpose(cache_fox_logf[0], (0, 2, 1)))
    cache_t = lambda c: jnp.transpose(c[0], (0, 2, 3, 1))
    hm = lambda a, t=False: _heads_major(dbn(a), fox_heads, t)
    o_fox_s = _fox_sample(hm(qa), hm(ka16, True), hm(va16, True), cache_t(cache_fox_k), cache_t(cache_fox_v),
                          cq_s, ck_past, ck_new, tk)
    o_sb_s = _sb_sample(hm(qb), hm(kb16, True), hm(vb16, True), cache_t(cache_sb_k), cache_t(cache_sb_v),
                        SB_CACHE_TILE, SB_SUB_TILE)
    tokens_major = lambda a: jnp.transpose(a, (0, 2, 1, 3)).reshape(TS, hw)
    o_tail = jnp.concatenate([tokens_major(o_fox_s), tokens_major(o_sb_s)], axis=-1)
    o_fox_p = _flash_prompt(qa_p, ka_p, vat16, B, S, FLASH_Q_TILE, fox_heads, 1, "fox_prompt")
    o_sb_p = _sb_prompt(qb_p, kb_p, vbt16, B, S, SB_Q_TILE)
    xp, xs = ffn((o_fox_p, o_sb_p), o_tail, ab_w_out[0], (xp, xs), 0)

    q_lora, kv_lora = mla_g_q.shape[1], mla_g_kv.shape[1]
    heads = mla_w_uq.shape[2] // (QK_NOPE + QK_ROPE)
    wdn, wq, wqr, wk, wv, wuk_t, wuv = _prep_mla_weights(mla_w_down[0], mla_w_uq[0], mla_w_ukv[0], heads, q_lora, kv_lora)
    tm = TOKEN_TILE
    pos = jnp.concatenate([jnp.arange(S, dtype=jnp.int32), P + jnp.arange(tm, dtype=jnp.int32) % n])
    cos_t, sin_t = _rope_tables(pos)
    blocks_per_seq, prompt_blocks = S // tm, TP // tm
    table_block = lambda i: jnp.where(i < prompt_blocks, i % blocks_per_seq, blocks_per_seq)
    assert tm == FLASH_KEY_TILE
    ckv, kr, qcat, kcat, vt = _mla_proj(xp, xs, wdn, mla_g_q[0][None], mla_g_kv[0][None], wq, wqr, wk, wv,
                                        cos_t, sin_t, table_block, tm)
    qn, qr = _mla_sample_queries(sample3(qcat), heads)
    o_s = _mla_sample(qn, qr, wuk_t, wuv, cache_mla_ckv[0], jnp.transpose(cache_mla_krope[0], (0, 2, 1)),
                      sample3(ckv).astype(BF16), jnp.transpose(sample3(kr), (0, 2, 1)).astype(BF16), n, tk)
    o_p = _flash_prompt(qcat, kcat, vt, B, S, FLASH_Q_TILE, 8, CHUNK, "mla_prompt")
    xp, xs = ffn((o_p,), o_s.reshape(TS, -1), mla_w_out[0], (xp, xs), 1)

    rows_p = lambda a: jnp.transpose(a, (0, 3, 1, 2))[None]
    rows_s = lambda a: a.reshape(1, DB, n, fox_heads, hw // fox_heads)
    pr, sr = slice(0, TP), slice(TP, TP + TS)
    return (xp.reshape(B, S, D), xs.reshape(DB, n, D),
            rows_p(kat), rows_p(vat), lf_p[:, :fox_heads].reshape(1, B, S, fox_heads), rows_p(kbt), rows_p(vbt),
            ckv[pr].reshape(1, B, S, kv_lora), kr[pr].reshape(1, B, S, QK_ROPE),
            rows_s(ka), rows_s(va), lf[:, :fox_heads].reshape(1, DB, n, fox_heads), rows_s(kb), rows_s(vb),
            ckv[sr].reshape(1, DB, n, kv_lora), kr[sr].reshape(1, DB, n, QK_ROPE))
```
